```python
import jax, jax.numpy as jnp
from jax import lax
import numpy as np

D_MODEL = 1024
BATCH = 8
SEQ = 2048
DEPTH = 4
DEC_BATCH = 128
DEC_SEQ = 1
PAST_LEN = 16384
PAGE_SIZE = 128

MIX_WIDTH = D_MODEL
HG_WIDTH = MIX_WIDTH // 2
HG_HEADS = 4
HG_DK = HG_WIDTH // HG_HEADS
HG_DV = HG_DK
POOL_WIDTH = MIX_WIDTH - HG_WIDTH
POOL_WINDOWS = (2, 4, 8, 16)
POOL_GROUPS = len(POOL_WINDOWS)
POOL_GROUP_W = POOL_WIDTH // POOL_GROUPS
POOL_BUF = max(POOL_WINDOWS) - 1
IN_WIDTH = 4 * HG_WIDTH + POOL_WIDTH
CHUNK = 64
D_FF = 2816
N_EXPERTS = 8
TOP_K = 2
D_FF_E = 1408
N_DENSE = (DEPTH + 1) // 2
N_MOE = DEPTH // 2
EPS = 1e-6

kernel_name = "hgrn2_pool_hybrid_decoder_step"


def rmsnorm(x, g):
    xf = x.astype(jnp.float32)
    y = xf * lax.rsqrt(jnp.mean(xf * xf, axis=-1, keepdims=True) + EPS)
    return (y * g.astype(jnp.float32)).astype(x.dtype)


def lower_bounds(lb_logits):
    p = jax.nn.softmax(lb_logits.astype(jnp.float32), axis=0)
    return jnp.cumsum(p, axis=0) - p[0:1]


def hgrn2_scan(q, logf, k, v, S0):
    B, H, L, _ = q.shape
    C = CHUNK if L >= CHUNK else L
    pad = (-L) % C
    if pad:
        pw = ((0, 0), (0, 0), (0, pad), (0, 0))
        q, logf, k, v = jnp.pad(q, pw), jnp.pad(logf, pw), jnp.pad(k, pw), jnp.pad(v, pw)
    n = (L + pad) // C

    def to_chunks(a):
        return jnp.moveaxis(a.reshape(B, H, n, C, a.shape[-1]), 2, 0)

    causal = jnp.tril(jnp.ones((C, C), dtype=bool))[:, :, None]

    def step(S, xs):
        qc, lfc, kc, vc = xs
        b = jnp.cumsum(lfc, axis=2)
        diff = b[:, :, :, None, :] - b[:, :, None, :, :]
        decay = jnp.exp(jnp.where(causal, diff, -jnp.inf))
        scores = jnp.einsum('bhtk,bhtsk,bhsk->bhts', qc, decay, kc)
        o = (jnp.einsum('bhts,bhsv->bhtv', scores, vc)
             + jnp.einsum('bhtk,bhkv->bhtv', qc * jnp.exp(b), S))
        bC = b[:, :, -1:, :]
        S_new = (jnp.exp(bC[:, :, 0, :])[..., None] * S
                 + jnp.einsum('bhsk,bhsv->bhkv', kc * jnp.exp(bC - b), vc))
        return S_new, o

    S, o = lax.scan(step, S0, (to_chunks(q), to_chunks(logf), to_chunks(k), to_chunks(v)))
    o = jnp.moveaxis(o, 0, 2).reshape(B, H, n * C, -1)[:, :, :L]
    return o, S


def multiscale_pool(u, buf, start_pos, w_pool, scale):
    B, L, _ = u.shape
    z = jnp.concatenate([buf.astype(jnp.float32), u.astype(jnp.float32)], axis=1)
    cs = jnp.concatenate([jnp.zeros((B, 1, POOL_WIDTH), jnp.float32), jnp.cumsum(z, axis=1)], axis=1)
    end = cs[:, POOL_BUF + 1:POOL_BUF + 1 + L]
    pos = start_pos + jnp.arange(L)
    cur = z[:, POOL_BUF:]
    outs = []
    for gi, w in enumerate(POOL_WINDOWS):
        sl = slice(gi * POOL_GROUP_W, (gi + 1) * POOL_GROUP_W)
        begin = cs[:, POOL_BUF + 1 - w:POOL_BUF + 1 - w + L, sl]
        cnt = jnp.minimum(pos + 1, w).astype(jnp.float32)[None, :, None]
        outs.append((end[..., sl] - begin) / cnt - cur[..., sl])
    d = jnp.stack(outs, axis=2)
    y = jnp.einsum('blgc,gcd->blgd', d, w_pool.astype(jnp.float32)).reshape(B, L, POOL_WIDTH)
    y = y * scale.astype(jnp.float32)
    new_buf = z[:, -POOL_BUF:]
    return y.astype(u.dtype), new_buf.astype(buf.dtype)


def mixer(h, S0, buf, start_pos, lb, w_in, w_o, hg_norm, w_pool, pool_scale):
    B, L, _ = h.shape
    proj = h @ w_in
    q, fx, ix, g, u = jnp.split(proj, [HG_WIDTH, 2 * HG_WIDTH, 3 * HG_WIDTH, 4 * HG_WIDTH], axis=-1)

    def heads(a):
        return a.reshape(B, L, HG_HEADS, -1).transpose(0, 2, 1, 3).astype(jnp.float32)

    f = lb + (1.0 - lb) * jax.nn.sigmoid(fx.astype(jnp.float32))
    o, S = hgrn2_scan(heads(jax.nn.silu(q.astype(jnp.float32))), heads(jnp.log(f)), heads(1.0 - f),
                      heads(ix), S0.astype(jnp.float32))
    o = o.transpose(0, 2, 1, 3)
    o = o * lax.rsqrt(jnp.mean(o * o, axis=-1, keepdims=True) + EPS)
    o = o * hg_norm.astype(jnp.float32).reshape(HG_HEADS, HG_DV)
    o = o.reshape(B, L, HG_WIDTH) * jax.nn.silu(g.astype(jnp.float32))
    yp, new_buf = multiscale_pool(u, buf, start_pos, w_pool, pool_scale)
    y = jnp.concatenate([o.astype(h.dtype), yp], axis=-1) @ w_o
    return y, S.astype(S0.dtype), new_buf


def swiglu(h, wg, wu, wd):
    return (jax.nn.silu(h @ wg) * (h @ wu)) @ wd


def moe(h, router, wg, wu, wd):
    logits = (h @ router).astype(jnp.float32)
    top_v, top_i = lax.top_k(logits, TOP_K)
    gates = jax.nn.softmax(top_v, axis=-1)
    combine = jnp.sum(jax.nn.one_hot(top_i, N_EXPERTS, dtype=jnp.float32) * gates[..., None], axis=-2)
    y = jnp.zeros_like(h)
    for e in range(N_EXPERTS):
        y = y + combine[..., e:e + 1].astype(h.dtype) * swiglu(h, wg[e], wu[e], wd[e])
    return y


def trunk(x, S_in, buf_in, start_pos, lb_logits, norm_mix, w_in, w_o, hg_norm, pool_w, pool_scale,
          norm_ffn, ffn_w_gate, ffn_w_up, ffn_w_down, router, moe_w_gate, moe_w_up, moe_w_down, norm_final):
    lbs = lower_bounds(lb_logits)
    S_out, buf_out = [], []
    for l in range(DEPTH):
        h = rmsnorm(x, norm_mix[l])
        y, S, b = mixer(h, S_in[l], buf_in[l], start_pos, lbs[l], w_in[l], w_o[l], hg_norm[l], pool_w[l], pool_scale[l])
        S_out.append(S)
        buf_out.append(b)
        x = x + y
        h = rmsnorm(x, norm_ffn[l])
        if l % 2 == 0:
            j = l // 2
            x = x + swiglu(h, ffn_w_gate[j], ffn_w_up[j], ffn_w_down[j])
        else:
            j = l // 2
            x = x + moe(h, router[j], moe_w_gate[j], moe_w_up[j], moe_w_down[j])
    return rmsnorm(x, norm_final), jnp.stack(S_out), jnp.stack(buf_out)


def setup_inputs(seed: int = 0) -> dict:
    key = jax.random.key(seed)
    ks = jax.random.split(key, 24)
    f32 = jnp.float32

    def nrm(k, shape, scale):
        return jax.random.normal(k, shape, f32) * scale

    return {
        "x_prompt": nrm(ks[0], (BATCH, SEQ, D_MODEL), 1.0),
        "x_sample": nrm(ks[1], (DEC_BATCH, DEC_SEQ, D_MODEL), 1.0),
        "state_hgrn": nrm(ks[2], (DEPTH, DEC_BATCH, HG_HEADS, HG_DK, HG_DV), 0.5),
        "state_pool": nrm(ks[3], (DEPTH, DEC_BATCH, POOL_BUF, POOL_WIDTH), 1.0),
        "lb_logits": nrm(ks[4], (DEPTH, HG_WIDTH), 0.5),
        "norm_mix": 1.0 + nrm(ks[5], (DEPTH, D_MODEL), 0.02),
        "w_in": nrm(ks[6], (DEPTH, D_MODEL, IN_WIDTH), D_MODEL ** -0.5),
        "w_o": nrm(ks[7], (DEPTH, MIX_WIDTH, D_MODEL), MIX_WIDTH ** -0.5),
        "hg_norm": 1.0 + nrm(ks[8], (DEPTH, HG_WIDTH), 0.02),
        "pool_w": nrm(ks[9], (DEPTH, POOL_GROUPS, POOL_GROUP_W, POOL_GROUP_W), POOL_GROUP_W ** -0.5),
        "pool_scale": 1.0 + nrm(ks[10], (DEPTH, POOL_WIDTH), 0.1),
        "norm_ffn": 1.0 + nrm(ks[11], (DEPTH, D_MODEL), 0.02),
        "ffn_w_gate": nrm(ks[12], (N_DENSE, D_MODEL, D_FF), D_MODEL ** -0.5),
        "ffn_w_up": nrm(ks[13], (N_DENSE, D_MODEL, D_FF), D_MODEL ** -0.5),
        "ffn_w_down": nrm(ks[14], (N_DENSE, D_FF, D_MODEL), D_FF ** -0.5),
        "router": nrm(ks[15], (N_MOE, D_MODEL, N_EXPERTS), D_MODEL ** -0.5),
        "moe_w_gate": nrm(ks[16], (N_MOE, N_EXPERTS, D_MODEL, D_FF_E), D_MODEL ** -0.5),
        "moe_w_up": nrm(ks[17], (N_MOE, N_EXPERTS, D_MODEL, D_FF_E), D_MODEL ** -0.5),
        "moe_w_down": nrm(ks[18], (N_MOE, N_EXPERTS, D_FF_E, D_MODEL), D_FF_E ** -0.5),
        "norm_final": 1.0 + nrm(ks[19], (D_MODEL,), 0.02),
    }


def reference(x_prompt, x_sample, state_hgrn, state_pool, lb_logits, norm_mix, w_in, w_o, hg_norm, pool_w,
              pool_scale, norm_ffn, ffn_w_gate, ffn_w_up, ffn_w_down, router, moe_w_gate, moe_w_up, moe_w_down,
              norm_final):
    B = x_prompt.shape[0]
    S_zero = jnp.zeros((DEPTH, B, HG_HEADS, HG_DK, HG_DV), state_hgrn.dtype)
    buf_zero = jnp.zeros((DEPTH, B, POOL_BUF, POOL_WIDTH), state_pool.dtype)
    y_prompt, new_hgrn_prompt, new_pool_prompt = trunk(
        x_prompt, S_zero, buf_zero, 0, lb_logits, norm_mix, w_in, w_o, hg_norm, pool_w, pool_scale,
        norm_ffn, ffn_w_gate, ffn_w_up, ffn_w_down, router, moe_w_gate, moe_w_up, moe_w_down, norm_final)
    y_sample, new_hgrn_sample, new_pool_sample = trunk(
        x_sample, state_hgrn, state_pool, PAST_LEN, lb_logits, norm_mix, w_in, w_o, hg_norm, pool_w, pool_scale,
        norm_ffn, ffn_w_gate, ffn_w_up, ffn_w_down, router, moe_w_gate, moe_w_up, moe_w_down, norm_final)
    return (y_prompt, y_sample, new_hgrn_prompt, new_hgrn_sample, new_pool_prompt, new_pool_sample)
```

```python
import functools

import jax
import jax.numpy as jnp
from jax import lax
from jax.experimental import pallas as pl
from jax.experimental.pallas import tpu as pltpu

F32 = jnp.float32
BF16 = jnp.bfloat16

D_MODEL = 1024
DEPTH = 4
HG_WIDTH = 512
HG_HEADS = 4
HG_D = 128
POOL_WIDTH = 512
POOL_WINDOWS = (2, 4, 8, 16)
POOL_GROUP_W = 128
POOL_BUF = 15
IN_WIDTH = 4 * HG_WIDTH + POOL_WIDTH
ACT_WIDTH = 5 * HG_WIDTH + POOL_WIDTH
N_EXPERTS = 8
EPS = 1e-6
LANES = 128
SUBLANES = 8
VMEM_LIMIT = 56 * 1024 * 1024
HGRN_CHUNK = 128


def _params(*sem):
    return pltpu.CompilerParams(dimension_semantics=sem, vmem_limit_bytes=VMEM_LIMIT)


def _rmsnorm(x, g):
    return x * lax.rsqrt(jnp.mean(x * x, axis=-1, keepdims=True) + EPS) * g


def _silu(x):
    return x * jax.nn.sigmoid(x)


def _mixer_in_kernel(layer, x_ref, g_ref, w_ref, lbl_ref, o_ref):
    h = _rmsnorm(x_ref[...], g_ref[...])
    p = jnp.dot(h.astype(BF16), w_ref[...], preferred_element_type=F32)
    lg = lbl_ref[...]
    e = jnp.exp(lg - jnp.max(lg, axis=0, keepdims=True))
    pr = e / jnp.sum(e, axis=0, keepdims=True)
    cum = pr[0:1]
    for j in range(1, layer + 1):
        cum = cum + pr[j:j + 1]
    lb = cum - pr[0:1]
    w = HG_WIDTH
    q, fx, ix, g, u = p[:, :w], p[:, w:2 * w], p[:, 2 * w:3 * w], p[:, 3 * w:4 * w], p[:, 4 * w:]
    f = lb + (1.0 - lb) * jax.nn.sigmoid(fx)
    o_ref[:, 0:w] = _silu(q)
    o_ref[:, w:2 * w] = jnp.log(f)
    o_ref[:, 2 * w:3 * w] = 1.0 - f
    o_ref[:, 3 * w:4 * w] = ix
    o_ref[:, 4 * w:5 * w] = _silu(g)
    o_ref[:, 5 * w:] = u


def _mixer_in(x, g, w_in, lb_logits, layer, tm):
    t = x.shape[0]
    return pl.pallas_call(
        functools.partial(_mixer_in_kernel, layer),
        grid=(t // tm,),
        in_specs=[
            pl.BlockSpec((tm, D_MODEL), lambda i: (i, 0)),
            pl.BlockSpec((1, D_MODEL), lambda i: (0, 0)),
            pl.BlockSpec((D_MODEL, IN_WIDTH), lambda i: (0, 0)),
            pl.BlockSpec((DEPTH, HG_WIDTH), lambda i: (0, 0)),
        ],
        out_specs=pl.BlockSpec((tm, ACT_WIDTH), lambda i: (i, 0)),
        out_shape=jax.ShapeDtypeStruct((t, ACT_WIDTH), F32),
        compiler_params=_params("parallel"),
        name="mixer_in",
    )(x, g, w_in, lb_logits)


def _level_reference(b, n, row):
    c = b.shape[0]
    half = n // 2
    if half >= SUBLANES:
        b3 = b.reshape(c // n, n, HG_D)
        mid = b3[:, half - 1:half, :]
        return jnp.broadcast_to(mid, (c // n, n, HG_D)).reshape(c, HG_D)
    rmod = row & (n - 1)
    bm = b
    for j in range(half):
        bm = jnp.where(rmod == half + j, pltpu.roll(b, j + 1, 0), bm)
    for j in range(half - 1):
        bm = jnp.where(rmod == j, pltpu.roll(b, c - (half - 1 - j), 0), bm)
    return bm


def _hgrn_prompt_kernel(seq, q_ref, lf_ref, k_ref, v_ref, g_ref, nrm_ref, o_ref, s_ref, st_ref):
    c = HGRN_CHUNK
    st_ref[...] = jnp.zeros_like(st_ref)
    row = lax.broadcasted_iota(jnp.int32, (c, HG_D), 0)
    ti = lax.broadcasted_iota(jnp.int32, (c, c), 0)
    si = lax.broadcasted_iota(jnp.int32, (c, c), 1)
    tri = (si <= ti).astype(F32)
    nt = (((1,), (1,)), ((), ()))
    tn = (((0,), (0,)), ((), ()))

    def chunk(ci, carry):
        r0 = pl.multiple_of(ci * c, c)
        q = q_ref[pl.ds(r0, c), :]
        lf = lf_ref[pl.ds(r0, c), :]
        k = k_ref[pl.ds(r0, c), :]
        v = v_ref[pl.ds(r0, c), :]
        b = jnp.dot(tri, lf, precision=lax.Precision.HIGHEST, preferred_element_type=F32)
        vb = v.astype(BF16)
        scores = jnp.zeros((c, c), F32)
        n = 2
        while n <= c:
            half = n // 2
            bm = _level_reference(b, n, row)
            right = (row & half) != 0
            e = jnp.exp(jnp.minimum(jnp.where(right, b - bm, bm - b), 0.0))
            qt = jnp.where(right, q * e, 0.0).astype(BF16)
            kt = jnp.where(right, 0.0, k * e).astype(BF16)
            s_n = lax.dot_general(qt, kt, nt, preferred_element_type=F32)
            same = (ti & (-n)) == (si & (-n))
            scores = scores + jnp.where(same, s_n, 0.0)
            n *= 2
        st = st_ref[...]
        qs = (q * jnp.exp(b)).astype(BF16)
        o = jnp.dot(scores.astype(BF16), vb, preferred_element_type=F32)
        o = o + lax.dot_general(qs, st.astype(BF16), nt, preferred_element_type=F32)
        o = o + jnp.sum(q * k, axis=-1, keepdims=True) * v
        bc = b[c - 1:c, :]
        kd = (k * jnp.exp(bc - b)).astype(BF16)
        st_ref[...] = st * jnp.exp(bc) + lax.dot_general(vb, kd, tn, preferred_element_type=F32)
        o = o * lax.rsqrt(jnp.mean(o * o, axis=-1, keepdims=True) + EPS) * nrm_ref[...]
        o_ref[pl.ds(r0, c), :] = (o * g_ref[pl.ds(r0, c), :]).astype(o_ref.dtype)
        return carry

    lax.fori_loop(0, seq // c, chunk, 0)
    s_ref[0, 0] = st_ref[...].T


def _hgrn_prompt(act, hg_norm, batch, seq):
    col = lambda off: (lambda b, h: (b, off + h))
    blk = lambda off: pl.BlockSpec((seq, HG_D), col(off))
    return pl.pallas_call(
        functools.partial(_hgrn_prompt_kernel, seq),
        grid=(batch, HG_HEADS),
        in_specs=[blk(0), blk(4), blk(8), blk(12), blk(16),
                  pl.BlockSpec((1, HG_D), lambda b, h: (0, h))],
        out_specs=[pl.BlockSpec((seq, HG_D), lambda b, h: (b, h)),
                   pl.BlockSpec((1, 1, HG_D, HG_D), lambda b, h: (b, h, 0, 0))],
        out_shape=[jax.ShapeDtypeStruct((batch * seq, HG_WIDTH), BF16),
                   jax.ShapeDtypeStruct((batch, HG_HEADS, HG_D, HG_D), F32)],
        scratch_shapes=[pltpu.VMEM((HG_D, HG_D), F32)],
        compiler_params=_params("parallel", "parallel"),
        name="hgrn_prompt",
    )(act, act, act, act, act, hg_norm)


HGRN_STEP_BT = 16


def _hgrn_step_kernel(q_ref, lf_ref, k_ref, v_ref, g_ref, nrm_ref, s_ref, o_ref, so_ref):
    bt = HGRN_STEP_BT
    pad = jnp.zeros((HG_D - bt, HG_D), F32)
    col = lambda a: jnp.concatenate([a, pad], axis=0).T
    q = q_ref[...]
    v = v_ref[...]
    ft, kt, qt = col(jnp.exp(lf_ref[...])), col(k_ref[...]), col(q)
    rows = []
    for s in range(bt):
        sn = ft[:, s:s + 1] * s_ref[s, 0] + kt[:, s:s + 1] * v[s:s + 1, :]
        so_ref[s, 0] = sn
        rows.append(jnp.sum(sn * qt[:, s:s + 1], axis=0, keepdims=True))
    o = jnp.concatenate(rows, axis=0)
    o = o * lax.rsqrt(jnp.mean(o * o, axis=-1, keepdims=True) + EPS) * nrm_ref[...]
    o_ref[...] = (o * g_ref[...]).astype(o_ref.dtype)


def _hgrn_step(act, hg_norm, state):
    batch = act.shape[0]
    bt = HGRN_STEP_BT
    blk = lambda off: pl.BlockSpec((bt, HG_D), lambda i, h: (i, off + h))
    sblk = pl.BlockSpec((bt, 1, HG_D, HG_D), lambda i, h: (i, h, 0, 0))
    return pl.pallas_call(
        _hgrn_step_kernel,
        grid=(batch // bt, HG_HEADS),
        in_specs=[blk(0), blk(4), blk(8), blk(12), blk(16),
                  pl.BlockSpec((1, HG_D), lambda i, h: (0, h)), sblk],
        out_specs=[pl.BlockSpec((bt, HG_D), lambda i, h: (i, h)), sblk],
        out_shape=[jax.ShapeDtypeStruct((batch, HG_WIDTH), BF16),
                   jax.ShapeDtypeStruct(state.shape, F32)],
        compiler_params=_params("parallel", "parallel"),
        name="hgrn_step",
    )(act, act, act, act, act, hg_norm, state)


def _pool_map(d_groups, wp_ref, scale_ref, o_ref):
    for gi, d in enumerate(d_groups):
        sl = slice(gi * POOL_GROUP_W, (gi + 1) * POOL_GROUP_W)
        y = jnp.dot(d.astype(BF16), wp_ref[gi], preferred_element_type=F32)
        o_ref[:, sl] = (y * scale_ref[:, sl]).astype(o_ref.dtype)


POOL_HIST = 16


def _pool_prompt_kernel(tl, u_ref, wp_ref, scale_ref, o_ref, nb_ref, carry_ref):
    i = pl.program_id(1)

    @pl.when(i == 0)
    def _():
        carry_ref[...] = jnp.zeros_like(carry_ref)

    z = u_ref[...]
    ext = jnp.concatenate([carry_ref[...], z], axis=0)
    sums = {1: ext}
    w = 1
    while w < max(POOL_WINDOWS):
        sums[2 * w] = sums[w] + pltpu.roll(sums[w], w, 0)
        w *= 2
    pos = i * tl + lax.broadcasted_iota(jnp.int32, (tl, POOL_GROUP_W), 0)
    ds = []
    for gi, w in enumerate(POOL_WINDOWS):
        sl = slice(gi * POOL_GROUP_W, (gi + 1) * POOL_GROUP_W)
        cnt = jnp.minimum(pos + 1, w).astype(F32)
        ds.append(sums[w][POOL_HIST:, sl] / cnt - z[:, sl])
    _pool_map(ds, wp_ref, scale_ref, o_ref)
    carry_ref[...] = z[tl - POOL_HIST:, :]

    @pl.when(i == pl.num_programs(1) - 1)
    def _():
        nb_ref[0] = z[tl - POOL_BUF:, :]


def _pool_prompt(act, pool_w, pool_scale, batch, seq, tl=512):
    nl = seq // tl
    return pl.pallas_call(
        functools.partial(_pool_prompt_kernel, tl),
        grid=(batch, nl),
        in_specs=[pl.BlockSpec((tl, POOL_WIDTH), lambda b, i: (b * nl + i, 5)),
                  pl.BlockSpec((len(POOL_WINDOWS), POOL_GROUP_W, POOL_GROUP_W), lambda b, i: (0, 0, 0)),
                  pl.BlockSpec((1, POOL_WIDTH), lambda b, i: (0, 0))],
        out_specs=[pl.BlockSpec((tl, POOL_WIDTH), lambda b, i: (b * nl + i, 0)),
                   pl.BlockSpec((1, POOL_BUF, POOL_WIDTH), lambda b, i: (b, 0, 0))],
        out_shape=[jax.ShapeDtypeStruct((batch * seq, POOL_WIDTH), BF16),
                   jax.ShapeDtypeStruct((batch, POOL_BUF, POOL_WIDTH), F32)],
        scratch_shapes=[pltpu.VMEM((POOL_HIST, POOL_WIDTH), F32)],
        compiler_params=_params("parallel", "arbitrary"),
        name="pool_prompt",
    )(act, pool_w, pool_scale)


POOL_STEP_BT = 16


def _pool_step_kernel(start_pos, u_ref, buf_ref, wp_ref, scale_ref, o_ref, nb_ref):
    u = u_ref[...]
    buf = buf_ref[...]
    ds = []
    for gi, w in enumerate(POOL_WINDOWS):
        sl = slice(gi * POOL_GROUP_W, (gi + 1) * POOL_GROUP_W)
        tot = u[:, sl] + jnp.sum(buf[:, POOL_BUF - (w - 1):, sl], axis=1)
        ds.append(tot / float(min(start_pos + 1, w)) - u[:, sl])
    _pool_map(ds, wp_ref, scale_ref, o_ref)
    nb_ref[:, 0:POOL_BUF - 1, :] = buf[:, 1:, :]
    nb_ref[:, POOL_BUF - 1:, :] = u[:, None, :]


def _pool_step(act, buf, pool_w, pool_scale, start_pos):
    batch = act.shape[0]
    bt = POOL_STEP_BT
    bblk = pl.BlockSpec((bt, POOL_BUF, POOL_WIDTH), lambda i: (i, 0, 0))
    return pl.pallas_call(
        functools.partial(_pool_step_kernel, start_pos),
        grid=(batch // bt,),
        in_specs=[pl.BlockSpec((bt, POOL_WIDTH), lambda i: (i, 5)), bblk,
                  pl.BlockSpec((len(POOL_WINDOWS), POOL_GROUP_W, POOL_GROUP_W), lambda i: (0, 0, 0)),
                  pl.BlockSpec((1, POOL_WIDTH), lambda i: (0, 0))],
        out_specs=[pl.BlockSpec((bt, POOL_WIDTH), lambda i: (i, 0)), bblk],
        out_shape=[jax.ShapeDtypeStruct((batch, POOL_WIDTH), BF16),
                   jax.ShapeDtypeStruct(buf.shape, F32)],
        compiler_params=_params("parallel"),
        name="pool_step",
    )(act, buf, pool_w, pool_scale)


def _mixer_out_kernel(x_ref, o_ref, p_ref, w_ref, y_ref):
    y = jnp.dot(o_ref[...], w_ref[0:HG_WIDTH, :], preferred_element_type=F32)
    y = y + jnp.dot(p_ref[...], w_ref[HG_WIDTH:, :], preferred_element_type=F32)
    y_ref[...] = x_ref[...] + y


def _mixer_out(x, og, yp, w_o, tm):
    t = x.shape[0]
    return pl.pallas_call(
        _mixer_out_kernel,
        grid=(t // tm,),
        in_specs=[pl.BlockSpec((tm, D_MODEL), lambda i: (i, 0)),
                  pl.BlockSpec((tm, HG_WIDTH), lambda i: (i, 0)),
                  pl.BlockSpec((tm, POOL_WIDTH), lambda i: (i, 0)),
                  pl.BlockSpec((D_MODEL, D_MODEL), lambda i: (0, 0))],
        out_specs=pl.BlockSpec((tm, D_MODEL), lambda i: (i, 0)),
        out_shape=jax.ShapeDtypeStruct((t, D_MODEL), F32),
        compiler_params=_params("parallel"),
        name="mixer_out",
    )(x, og, yp, w_o)


def _top2_combine(logits):
    lane = lax.broadcasted_iota(jnp.int32, logits.shape, 1).astype(F32)
    neg = jnp.float32(-jnp.inf)
    lg = jnp.where(lane < N_EXPERTS, logits, neg)
    m1 = jnp.max(lg, axis=-1, keepdims=True)
    i1 = jnp.min(jnp.where(lg == m1, lane, float(LANES)), axis=-1, keepdims=True)
    lg2 = jnp.where(lane == i1, neg, lg)
    m2 = jnp.max(lg2, axis=-1, keepdims=True)
    i2 = jnp.min(jnp.where(lg2 == m2, lane, float(LANES)), axis=-1, keepdims=True)
    e2 = jnp.exp(m2 - m1)
    g1 = 1.0 / (1.0 + e2)
    g2 = e2 / (1.0 + e2)
    return jnp.where(lane == i1, g1, 0.0) + jnp.where(lane == i2, g2, 0.0)


def _ffn_kernel(moe, final, x_ref, g_ref, *rest):
    if moe:
        r_ref, rest = rest[0], rest[1:]
    if final:
        gf_ref, rest = rest[0], rest[1:]
    wg_ref, wu_ref, wd_ref, y_ref, h_ref, acc_ref = rest[:6]
    e, j = pl.program_id(1), pl.program_id(2)
    first = (e == 0) & (j == 0)
    last = (e == pl.num_programs(1) - 1) & (j == pl.num_programs(2) - 1)

    @pl.when(first)
    def _():
        h = _rmsnorm(x_ref[...], g_ref[...])
        h_ref[...] = h.astype(BF16)
        acc_ref[...] = jnp.zeros_like(acc_ref)
        if moe:
            logits = jnp.dot(h, r_ref[...], precision=lax.Precision.HIGHEST, preferred_element_type=F32)
            rest[6][...] = _top2_combine(logits)

    h = h_ref[...]
    gate = jnp.dot(h, wg_ref[0], preferred_element_type=F32)
    up = jnp.dot(h, wu_ref[0], preferred_element_type=F32)
    y = jnp.dot((_silu(gate) * up).astype(BF16), wd_ref[0], preferred_element_type=F32)
    if moe:
        comb = rest[6][...]
        lane = lax.broadcasted_iota(jnp.int32, comb.shape, 1)
        y = jnp.sum(jnp.where(lane == e, comb, 0.0), axis=-1, keepdims=True) * y
    acc_ref[...] += y

    @pl.when(last)
    def _():
        out = x_ref[...] + acc_ref[...]
        if final:
            out = _rmsnorm(out, gf_ref[...])
        y_ref[...] = out


def _ffn(x, g, wg, wu, wd, tm, tf, router=None, final_g=None):
    t = x.shape[0]
    ne, _, f = wg.shape
    moe, final = router is not None, final_g is not None
    vec = pl.BlockSpec((1, D_MODEL), lambda i, e, j: (0, 0))
    in_specs = [pl.BlockSpec((tm, D_MODEL), lambda i, e, j: (i, 0)), vec]
    args = [x, g]
    scratch = [pltpu.VMEM((tm, D_MODEL), BF16), pltpu.VMEM((tm, D_MODEL), F32)]
    if moe:
        in_specs.append(pl.BlockSpec((D_MODEL, LANES), lambda i, e, j: (0, 0)))
        args.append(router)
        scratch.append(pltpu.VMEM((tm, LANES), F32))
    if final:
        in_specs.append(vec)
        args.append(final_g)
    in_specs += [pl.BlockSpec((1, D_MODEL, tf), lambda i, e, j: (e, 0, j)),
                 pl.BlockSpec((1, D_MODEL, tf), lambda i, e, j: (e, 0, j)),
                 pl.BlockSpec((1, tf, D_MODEL), lambda i, e, j: (e, j, 0))]
    args += [wg, wu, wd]
    return pl.pallas_call(
        functools.partial(_ffn_kernel, moe, final),
        grid=(t // tm, ne, f // tf),
        in_specs=in_specs,
        out_specs=pl.BlockSpec((tm, D_MODEL), lambda i, e, j: (i, 0)),
        out_shape=jax.ShapeDtypeStruct((t, D_MODEL), F32),
        scratch_shapes=scratch,
        compiler_params=_params("parallel", "arbitrary", "arbitrary"),
        name="ffn_moe" if moe else "ffn_dense",
    )(*args)


def _trunk(x, state_hgrn, state_pool, start_pos, w, seq, tm):
    batch = x.shape[0] // seq
    s_out, b_out = [], []
    for l in range(DEPTH):
        act = _mixer_in(x, w["norm_mix"][l], w["w_in"][l], w["lb_logits"], l, tm)
        if seq > 1:
            og, s_new = _hgrn_prompt(act, w["hg_norm"][l], batch, seq)
            yp, b_new = _pool_prompt(act, w["pool_w"][l], w["pool_scale"][l], batch, seq)
        else:
            og, s_new = _hgrn_step(act, w["hg_norm"][l], state_hgrn[l])
            yp, b_new = _pool_step(act, state_pool[l], w["pool_w"][l], w["pool_scale"][l], start_pos)
        s_out.append(s_new)
        b_out.append(b_new)
        x = _mixer_out(x, og, yp, w["w_o"][l], tm)
        j = l // 2
        final_g = w["norm_final"] if l == DEPTH - 1 else None
        if l % 2 == 0:
            x = _ffn(x, w["norm_ffn"][l], w["ffn_w_gate"][j:j + 1], w["ffn_w_up"][j:j + 1],
                     w["ffn_w_down"][j:j + 1], tm, 1408, final_g=final_g)
        else:
            x = _ffn(x, w["norm_ffn"][l], w["moe_w_gate"][j], w["moe_w_up"][j], w["moe_w_down"][j],
                     tm, 1408, router=w["router"][j], final_g=final_g)
    return x, jnp.stack(s_out), jnp.stack(b_out)


def kernel(x_prompt, x_sample, state_hgrn, state_pool, lb_logits, norm_mix, w_in, w_o, hg_norm, pool_w,
           pool_scale, norm_ffn, ffn_w_gate, ffn_w_up, ffn_w_down, router, moe_w_gate, moe_w_up, moe_w_down,
           norm_final):
    batch, seq, _ = x_prompt.shape
    dec_batch, dec_seq, _ = x_sample.shape
    assert dec_seq == 1
    past_len = 16384
    w = dict(
        lb_logits=lb_logits,
        norm_mix=norm_mix.reshape(DEPTH, 1, D_MODEL),
        w_in=w_in.astype(BF16),
        w_o=w_o.astype(BF16),
        hg_norm=hg_norm.reshape(DEPTH, 1, HG_WIDTH),
        pool_w=pool_w.astype(BF16),
        pool_scale=pool_scale.reshape(DEPTH, 1, POOL_WIDTH),
        norm_ffn=norm_ffn.reshape(DEPTH, 1, D_MODEL),
        ffn_w_gate=ffn_w_gate.astype(BF16),
        ffn_w_up=ffn_w_up.astype(BF16),
        ffn_w_down=ffn_w_down.astype(BF16),
        router=jnp.pad(router, ((0, 0), (0, 0), (0, LANES - N_EXPERTS))),
        moe_w_gate=moe_w_gate.astype(BF16),
        moe_w_up=moe_w_up.astype(BF16),
        moe_w_down=moe_w_down.astype(BF16),
        norm_final=norm_final.reshape(1, D_MODEL),
    )
    yp, sp, bp = _trunk(x_prompt.reshape(batch * seq, D_MODEL), None, None, 0, w, seq, 512)
    ys, ss, bs = _trunk(x_sample.reshape(dec_batch, D_MODEL), state_hgrn, state_pool, past_len, w, 1, 128)
    return (yp.reshape(batch, seq, D_MODEL), ys.reshape(dec_batch, 1, D_MODEL), sp, ss, bp, bs)
```

```python
import functools

import jax
import jax.numpy as jnp
from jax import lax
from jax.experimental import pallas as pl
from jax.experimental.pallas import tpu as pltpu

F32 = jnp.float32
BF16 = jnp.bfloat16

D_MODEL = 1024
DEPTH = 4
HG_WIDTH = 512
HG_HEADS = 4
HG_D = 128
POOL_WIDTH = 512
POOL_WINDOWS = (2, 4, 8, 16)
POOL_GROUP_W = 128
POOL_BUF = 15
IN_WIDTH = 4 * HG_WIDTH + POOL_WIDTH
ACT_WIDTH = 5 * HG_WIDTH + POOL_WIDTH
N_EXPERTS = 8
EPS = 1e-6
LANES = 128
SUBLANES = 8
VMEM_LIMIT = 56 * 1024 * 1024
HGRN_CHUNK = 128


def _params(*sem):
    return pltpu.CompilerParams(dimension_semantics=sem, vmem_limit_bytes=VMEM_LIMIT)


def _rmsnorm(x, g):
    return x * lax.rsqrt(jnp.mean(x * x, axis=-1, keepdims=True) + EPS) * g


def _silu(x):
    return x * jax.nn.sigmoid(x)


def _mixer_in_kernel(layer, x_ref, g_ref, w_ref, lbl_ref, o_ref):
    h = _rmsnorm(x_ref[...], g_ref[...])
    p = jnp.dot(h.astype(BF16), w_ref[...], preferred_element_type=F32)
    lg = lbl_ref[...]
    e = jnp.exp(lg - jnp.max(lg, axis=0, keepdims=True))
    pr = e / jnp.sum(e, axis=0, keepdims=True)
    cum = pr[0:1]
    for j in range(1, layer + 1):
        cum = cum + pr[j:j + 1]
    lb = cum - pr[0:1]
    w = HG_WIDTH
    q, fx, ix, g, u = p[:, :w], p[:, w:2 * w], p[:, 2 * w:3 * w], p[:, 3 * w:4 * w], p[:, 4 * w:]
    f = lb + (1.0 - lb) * jax.nn.sigmoid(fx)
    o_ref[:, 0:w] = _silu(q)
    o_ref[:, w:2 * w] = jnp.log(f)
    o_ref[:, 2 * w:3 * w] = 1.0 - f
    o_ref[:, 3 * w:4 * w] = ix
    o_ref[:, 4 * w:5 * w] = _silu(g)
    o_ref[:, 5 * w:] = u


def _mixer_in(x, g, w_in, lb_logits, layer, tm):
    t = x.shape[0]
    return pl.pallas_call(
        functools.partial(_mixer_in_kernel, layer),
        grid=(t // tm,),
        in_specs=[
            pl.BlockSpec((tm, D_MODEL), lambda i: (i, 0)),
            pl.BlockSpec((1, D_MODEL), lambda i: (0, 0)),
            pl.BlockSpec((D_MODEL, IN_WIDTH), lambda i: (0, 0)),
            pl.BlockSpec((DEPTH, HG_WIDTH), lambda i: (0, 0)),
        ],
        out_specs=pl.BlockSpec((tm, ACT_WIDTH), lambda i: (i, 0)),
        out_shape=jax.ShapeDtypeStruct((t, ACT_WIDTH), F32),
        compiler_params=_params("parallel"),
        name="mixer_in",
    )(x, g, w_in, lb_logits)


def _level_reference(b, n, row):
    c = b.shape[0]
    half = n // 2
    if half >= SUBLANES:
        b3 = b.reshape(c // n, n, HG_D)
        mid = b3[:, half - 1:half, :]
        return jnp.broadcast_to(mid, (c // n, n, HG_D)).reshape(c, HG_D)
    rmod = row & (n - 1)
    bm = b
    for j in range(half):
        bm = jnp.where(rmod == half + j, pltpu.roll(b, j + 1, 0), bm)
    for j in range(half - 1):
        bm = jnp.where(rmod == j, pltpu.roll(b, c - (half - 1 - j), 0), bm)
    return bm


def _hgrn_prompt_kernel(seq, q_ref, lf_ref, k_ref, v_ref, g_ref, nrm_ref, o_ref, s_ref, st_ref):
    c = HGRN_CHUNK
    st_ref[...] = jnp.zeros_like(st_ref)
    row = lax.broadcasted_iota(jnp.int32, (c, HG_D), 0)
    ti = lax.broadcasted_iota(jnp.int32, (c, c), 0)
    si = lax.broadcasted_iota(jnp.int32, (c, c), 1)
    tri = (si <= ti).astype(F32)
    nt = (((1,), (1,)), ((), ()))
    tn = (((0,), (0,)), ((), ()))

    def chunk(ci, carry):
        r0 = pl.multiple_of(ci * c, c)
        q = q_ref[pl.ds(r0, c), :]
        lf = lf_ref[pl.ds(r0, c), :]
        k = k_ref[pl.ds(r0, c), :]
        v = v_ref[pl.ds(r0, c), :]
        b = jnp.dot(tri, lf, precision=lax.Precision.HIGHEST, preferred_element_type=F32)
        vb = v.astype(BF16)
        scores = jnp.zeros((c, c), F32)
        n = 2
        while n <= c:
            half = n // 2
            bm = _level_reference(b, n, row)
            right = (row & half) != 0
            e = jnp.exp(jnp.minimum(jnp.where(right, b - bm, bm - b), 0.0))
            qt = jnp.where(right, q * e, 0.0).astype(BF16)
            kt = jnp.where(right, 0.0, k * e).astype(BF16)
            s_n = lax.dot_general(qt, kt, nt, preferred_element_type=F32)
            same = (ti & (-n)) == (si & (-n))
            scores = scores + jnp.where(same, s_n, 0.0)
            n *= 2
        st = st_ref[...]
        qs = (q * jnp.exp(b)).astype(BF16)
        o = jnp.dot(scores.astype(BF16), vb, preferred_element_type=F32)
        o = o + lax.dot_general(qs, st.astype(BF16), nt, preferred_element_type=F32)
        o = o + jnp.sum(q * k, axis=-1, keepdims=True) * v
        bc = b[c - 1:c, :]
        kd = (k * jnp.exp(bc - b)).astype(BF16)
        st_ref[...] = st * jnp.exp(bc) + lax.dot_general(vb, kd, tn, preferred_element_type=F32)
        o = o * lax.rsqrt(jnp.mean(o * o, axis=-1, keepdims=True) + EPS) * nrm_ref[...]
        o_ref[pl.ds(r0, c), :] = (o * g_ref[pl.ds(r0, c), :]).astype(o_ref.dtype)
        return carry

    lax.fori_loop(0, seq // c, chunk, 0)
    s_ref[0, 0] = st_ref[...].T


def _hgrn_prompt(act, hg_norm, batch, seq):
    col = lambda off: (lambda b, h: (b, off + h))
    blk = lambda off: pl.BlockSpec((seq, HG_D), col(off))
    return pl.pallas_call(
        functools.partial(_hgrn_prompt_kernel, seq),
        grid=(batch, HG_HEADS),
        in_specs=[blk(0), blk(4), blk(8), blk(12), blk(16),
                  pl.BlockSpec((1, HG_D), lambda b, h: (0, h))],
        out_specs=[pl.BlockSpec((seq, HG_D), lambda b, h: (b, h)),
                   pl.BlockSpec((1, 1, HG_D, HG_D), lambda b, h: (b, h, 0, 0))],
        out_shape=[jax.ShapeDtypeStruct((batch * seq, HG_WIDTH), BF16),
                   jax.ShapeDtypeStruct((batch, HG_HEADS, HG_D, HG_D), F32)],
        scratch_shapes=[pltpu.VMEM((HG_D, HG_D), F32)],
        compiler_params=_params("parallel", "parallel"),
        name="hgrn_prompt",
    )(act, act, act, act, act, hg_norm)


HGRN_STEP_BT = 16


def _hgrn_step_kernel(q_ref, lf_ref, k_ref, v_ref, g_ref, nrm_ref, s_ref, o_ref, so_ref):
    bt = HGRN_STEP_BT
    pad = jnp.zeros((HG_D - bt, HG_D), F32)
    col = lambda a: jnp.concatenate([a, pad], axis=0).T
    q = q_ref[...]
    v = v_ref[...]
    ft, kt, qt = col(jnp.exp(lf_ref[...])), col(k_ref[...]), col(q)
    rows = []
    for s in range(bt):
        sn = ft[:, s:s + 1] * s_ref[s, 0] + kt[:, s:s + 1] * v[s:s + 1, :]
        so_ref[s, 0] = sn
        rows.append(jnp.sum(sn * qt[:, s:s + 1], axis=0, keepdims=True))
    o = jnp.concatenate(rows, axis=0)
    o = o * lax.rsqrt(jnp.mean(o * o, axis=-1, keepdims=True) + EPS) * nrm_ref[...]
    o_ref[...] = (o * g_ref[...]).astype(o_ref.dtype)


def _hgrn_step(act, hg_norm, state):
    batch = act.shape[0]
    bt = HGRN_STEP_BT
    blk = lambda off: pl.BlockSpec((bt, HG_D), lambda i, h: (i, off + h))
    sblk = pl.BlockSpec((bt, 1, HG_D, HG_D), lambda i, h: (i, h, 0, 0))
    return pl.pallas_call(
        _hgrn_step_kernel,
        grid=(batch // bt, HG_HEADS),
        in_specs=[blk(0), blk(4), blk(8), blk(12), blk(16),
                  pl.BlockSpec((1, HG_D), lambda i, h: (0, h)), sblk],
        out_specs=[pl.BlockSpec((bt, HG_D), lambda i, h: (i, h)), sblk],
        out_shape=[jax.ShapeDtypeStruct((batch, HG_WIDTH), BF16),
                   jax.ShapeDtypeStruct(state.shape, F32)],
        compiler_params=_params("parallel", "parallel"),
        name="hgrn_step",
    )(act, act, act, act, act, hg_norm, state)


def _pool_map(d_groups, wp_ref, scale_ref, o_ref):
    for gi, d in enumerate(d_groups):
        sl = slice(gi * POOL_GROUP_W, (gi + 1) * POOL_GROUP_W)
        y = jnp.dot(d.astype(BF16), wp_ref[gi], preferred_element_type=F32)
        o_ref[:, sl] = (y * scale_ref[:, sl]).astype(o_ref.dtype)


POOL_HIST = 16


def _pool_prompt_kernel(tl, u_ref, wp_ref, scale_ref, o_ref, nb_ref, carry_ref):
    i = pl.program_id(1)

    @pl.when(i == 0)
    def _():
        carry_ref[...] = jnp.zeros_like(carry_ref)

    z = u_ref[...]
    ext = jnp.concatenate([carry_ref[...], z], axis=0)
    sums = {1: ext}
    w = 1
    while w < max(POOL_WINDOWS):
        sums[2 * w] = sums[w] + pltpu.roll(sums[w], w, 0)
        w *= 2
    pos = i * tl + lax.broadcasted_iota(jnp.int32, (tl, POOL_GROUP_W), 0)
    ds = []
    for gi, w in enumerate(POOL_WINDOWS):
        sl = slice(gi * POOL_GROUP_W, (gi + 1) * POOL_GROUP_W)
        cnt = jnp.minimum(pos + 1, w).astype(F32)
        ds.append(sums[w][POOL_HIST:, sl] / cnt - z[:, sl])
    _pool_map(ds, wp_ref, scale_ref, o_ref)
    carry_ref[...] = z[tl - POOL_HIST:, :]

    @pl.when(i == pl.num_programs(1) - 1)
    def _():
        nb_ref[0] = z[tl - POOL_BUF:, :]


def _pool_prompt(act, pool_w, pool_scale, batch, seq, tl=512):
    nl = seq // tl
    return pl.pallas_call(
        functools.partial(_pool_prompt_kernel, tl),
        grid=(batch, nl),
        in_specs=[pl.BlockSpec((tl, POOL_WIDTH), lambda b, i: (b * nl + i, 5)),
                  pl.BlockSpec((len(POOL_WINDOWS), POOL_GROUP_W, POOL_GROUP_W), lambda b, i: (0, 0, 0)),
                  pl.BlockSpec((1, POOL_WIDTH), lambda b, i: (0, 0))],
        out_specs=[pl.BlockSpec((tl, POOL_WIDTH), lambda b, i: (b * nl + i, 0)),
                   pl.BlockSpec((1, POOL_BUF, POOL_WIDTH), lambda b, i: (b, 0, 0))],
        out_shape=[jax.ShapeDtypeStruct((batch * seq, POOL_WIDTH), BF16),
                   jax.ShapeDtypeStruct((batch, POOL_BUF, POOL_WIDTH), F32)],
        scratch_shapes=[pltpu.VMEM((POOL_HIST, POOL_WIDTH), F32)],
        compiler_params=_params("parallel", "arbitrary"),
        name="pool_prompt",
    )(act, pool_w, pool_scale)


POOL_STEP_BT = 16


def _pool_step_kernel(start_pos, u_ref, buf_ref, wp_ref, scale_ref, o_ref, nb_ref):
    u = u_ref[...]
    buf = buf_ref[...]
    ds = []
    for gi, w in enumerate(POOL_WINDOWS):
        sl = slice(gi * POOL_GROUP_W, (gi + 1) * POOL_GROUP_W)
        tot = u[:, sl] + jnp.sum(buf[:, POOL_BUF - (w - 1):, sl], axis=1)
        ds.append(tot / float(min(start_pos + 1, w)) - u[:, sl])
    _pool_map(ds, wp_ref, scale_ref, o_ref)
    nb_ref[:, 0:POOL_BUF - 1, :] = buf[:, 1:, :]
    nb_ref[:, POOL_BUF - 1:, :] = u[:, None, :]


def _pool_step(act, buf, pool_w, pool_scale, start_pos):
    batch = act.shape[0]
    bt = POOL_STEP_BT
    bblk = pl.BlockSpec((bt, POOL_BUF, POOL_WIDTH), lambda i: (i, 0, 0))
    return pl.pallas_call(
        functools.partial(_pool_step_kernel, start_pos),
        grid=(batch // bt,),
        in_specs=[pl.BlockSpec((bt, POOL_WIDTH), lambda i: (i, 5)), bblk,
                  pl.BlockSpec((len(POOL_WINDOWS), POOL_GROUP_W, POOL_GROUP_W), lambda i: (0, 0, 0)),
                  pl.BlockSpec((1, POOL_WIDTH), lambda i: (0, 0))],
        out_specs=[pl.BlockSpec((bt, POOL_WIDTH), lambda i: (i, 0)), bblk],
        out_shape=[jax.ShapeDtypeStruct((batch, POOL_WIDTH), BF16),
                   jax.ShapeDtypeStruct(buf.shape, F32)],
        compiler_params=_params("parallel"),
        name="pool_step",
    )(act, buf, pool_w, pool_scale)


def _mixer_out_kernel(x_ref, o_ref, p_ref, w_ref, y_ref):
    y = jnp.dot(o_ref[...], w_ref[0:HG_WIDTH, :], preferred_element_type=F32)
    y = y + jnp.dot(p_ref[...], w_ref[HG_WIDTH:, :], preferred_element_type=F32)
    y_ref[...] = x_ref[...] + y


def _mixer_out(x, og, yp, w_o, tm):
    t = x.shape[0]
    return pl.pallas_call(
        _mixer_out_kernel,
        grid=(t // tm,),
        in_specs=[pl.BlockSpec((tm, D_MODEL), lambda i: (i, 0)),
                  pl.BlockSpec((tm, HG_WIDTH), lambda i: (i, 0)),
                  pl.BlockSpec((tm, POOL_WIDTH), lambda i: (i, 0)),
                  pl.BlockSpec((D_MODEL, D_MODEL), lambda i: (0, 0))],
        out_specs=pl.BlockSpec((tm, D_MODEL), lambda i: (i, 0)),
        out_shape=jax.ShapeDtypeStruct((t, D_MODEL), F32),
        compiler_params=_params("parallel"),
        name="mixer_out",
    )(x, og, yp, w_o)


def _top2_combine(logits):
    lane = lax.broadcasted_iota(jnp.int32, logits.shape, 1).astype(F32)
    neg = jnp.float32(-jnp.inf)
    lg = jnp.where(lane < N_EXPERTS, logits, neg)
    m1 = jnp.max(lg, axis=-1, keepdims=True)
    i1 = jnp.min(jnp.where(lg == m1, lane, float(LANES)), axis=-1, keepdims=True)
    lg2 = jnp.where(lane == i1, neg, lg)
    m2 = jnp.max(lg2, axis=-1, keepdims=True)
    i2 = jnp.min(jnp.where(lg2 == m2, lane, float(LANES)), axis=-1, keepdims=True)
    e2 = jnp.exp(m2 - m1)
    g1 = 1.0 / (1.0 + e2)
    g2 = e2 / (1.0 + e2)
    return jnp.where(lane == i1, g1, 0.0) + jnp.where(lane == i2, g2, 0.0)


def _ffn_kernel(moe, final, x_ref, g_ref, *rest):
    if moe:
        r_ref, rest = rest[0], rest[1:]
    if final:
        gf_ref, rest = rest[0], rest[1:]
    wg_ref, wu_ref, wd_ref, y_ref, h_ref, acc_ref = rest[:6]
    e, j = pl.program_id(1), pl.program_id(2)
    first = (e == 0) & (j == 0)
    last = (e == pl.num_programs(1) - 1) & (j == pl.num_programs(2) - 1)

    @pl.when(first)
    def _():
        h = _rmsnorm(x_ref[...], g_ref[...])
        h_ref[...] = h.astype(BF16)
        acc_ref[...] = jnp.zeros_like(acc_ref)
        if moe:
            logits = jnp.dot(h, r_ref[...], precision=lax.Precision.HIGHEST, preferred_element_type=F32)
            rest[6][...] = _top2_combine(logits)

    h = h_ref[...]
    gate = jnp.dot(h, wg_ref[0], preferred_element_type=F32)
    up = jnp.dot(h, wu_ref[0], preferred_element_type=F32)
    y = jnp.dot((_silu(gate) * up).astype(BF16), wd_ref[0], preferred_element_type=F32)
    if moe:
        comb = rest[6][...]
        lane = lax.broadcasted_iota(jnp.int32, comb.shape, 1)
        y = jnp.sum(jnp.where(lane == e, comb, 0.0), axis=-1, keepdims=True) * y
    acc_ref[...] += y

    @pl.when(last)
    def _():
        out = x_ref[...] + acc_ref[...]
        if final:
            out = _rmsnorm(out, gf_ref[...])
        y_ref[...] = out


def _ffn(x, g, wg, wu, wd, tm, tf, router=None, final_g=None):
    t = x.shape[0]
    ne, _, f = wg.shape
    moe, final = router is not None, final_g is not None
    vec = pl.BlockSpec((1, D_MODEL), lambda i, e, j: (0, 0))
    in_specs = [pl.BlockSpec((tm, D_MODEL), lambda i, e, j: (i, 0)), vec]
    args = [x, g]
    scratch = [pltpu.VMEM((tm, D_MODEL), BF16), pltpu.VMEM((tm, D_MODEL), F32)]
    if moe:
        in_specs.append(pl.BlockSpec((D_MODEL, LANES), lambda i, e, j: (0, 0)))
        args.append(router)
        scratch.append(pltpu.VMEM((tm, LANES), F32))
    if final:
        in_specs.append(vec)
        args.append(final_g)
    in_specs += [pl.BlockSpec((1, D_MODEL, tf), lambda i, e, j: (e, 0, j)),
                 pl.BlockSpec((1, D_MODEL, tf), lambda i, e, j: (e, 0, j)),
                 pl.BlockSpec((1, tf, D_MODEL), lambda i, e, j: (e, j, 0))]
    args += [wg, wu, wd]
    return pl.pallas_call(
        functools.partial(_ffn_kernel, moe, final),
        grid=(t // tm, ne, f // tf),
        in_specs=in_specs,
        out_specs=pl.BlockSpec((tm, D_MODEL), lambda i, e, j: (i, 0)),
        out_shape=jax.ShapeDtypeStruct((t, D_MODEL), F32),
        scratch_shapes=scratch,
        compiler_params=_params("parallel", "arbitrary", "arbitrary"),
        name="ffn_moe" if moe else "ffn_dense",
    )(*args)


MOE_TILE = 512
ROUTE_ROWS = 8
NT_DIMS = (((1,), (1,)), ((), ()))


def _moe_route_kernel(x_ref, g_ref, rt_ref, route_ref, gate_ref, cnt_ref, carry_ref):
    tm = x_ref.shape[0]

    @pl.when(pl.program_id(0) == 0)
    def _():
        carry_ref[...] = jnp.zeros_like(carry_ref)

    h = _rmsnorm(x_ref[...], g_ref[...])
    lt = lax.dot_general(rt_ref[...], h, NT_DIMS, precision=lax.Precision.HIGHEST,
                         preferred_element_type=F32)
    ex = lax.broadcasted_iota(jnp.int32, lt.shape, 0).astype(F32)
    neg = jnp.float32(-jnp.inf)
    m1 = jnp.max(lt, axis=0, keepdims=True)
    i1 = jnp.min(jnp.where(lt == m1, ex, float(N_EXPERTS)), axis=0, keepdims=True)
    l2 = jnp.where(ex == i1, neg, lt)
    m2 = jnp.max(l2, axis=0, keepdims=True)
    i2 = jnp.min(jnp.where(l2 == m2, ex, float(N_EXPERTS)), axis=0, keepdims=True)
    e2 = jnp.exp(m2 - m1)
    g1 = 1.0 / (1.0 + e2)
    g2 = e2 / (1.0 + e2)
    sel1, sel2 = ex == i1, ex == i2
    member = jnp.where(sel1 | sel2, 1.0, 0.0)
    t0 = lax.broadcasted_iota(jnp.int32, (tm, tm), 0)
    t1 = lax.broadcasted_iota(jnp.int32, (tm, tm), 1)
    earlier = (t0 < t1).astype(BF16)
    rank = jnp.dot(member.astype(BF16), earlier, preferred_element_type=F32) + carry_ref[:, 0:1]
    route_ref[...] = jnp.zeros_like(route_ref)
    route_ref[0:1, :] = i1.astype(jnp.int32)
    route_ref[1:2, :] = jnp.sum(jnp.where(sel1, rank, 0.0), axis=0, keepdims=True).astype(jnp.int32)
    route_ref[2:3, :] = i2.astype(jnp.int32)
    route_ref[3:4, :] = jnp.sum(jnp.where(sel2, rank, 0.0), axis=0, keepdims=True).astype(jnp.int32)
    carry_ref[...] += jnp.sum(member, axis=1, keepdims=True)
    cnt_ref[...] = carry_ref[...]
    row = lax.broadcasted_iota(jnp.int32, (LANES, tm), 0)
    gate_ref[...] = jnp.where(row == 0, g1, jnp.where(row == 1, g2, 0.0)).T


def _moe_route(x, g, router_t, tm=512):
    t = x.shape[0]
    return pl.pallas_call(
        _moe_route_kernel,
        grid=(t // tm,),
        in_specs=[pl.BlockSpec((tm, D_MODEL), lambda i: (i, 0)),
                  pl.BlockSpec((1, D_MODEL), lambda i: (0, 0)),
                  pl.BlockSpec((N_EXPERTS, D_MODEL), lambda i: (0, 0))],
        out_specs=[pl.BlockSpec((ROUTE_ROWS, tm), lambda i: (0, i)),
                   pl.BlockSpec((tm, LANES), lambda i: (i, 0)),
                   pl.BlockSpec((N_EXPERTS, LANES), lambda i: (0, 0))],
        out_shape=[jax.ShapeDtypeStruct((ROUTE_ROWS, t), jnp.int32),
                   jax.ShapeDtypeStruct((t, LANES), F32),
                   jax.ShapeDtypeStruct((N_EXPERTS, LANES), F32)],
        scratch_shapes=[pltpu.VMEM((N_EXPERTS, LANES), F32)],
        compiler_params=_params("arbitrary"),
        name="moe_route",
    )(x, g, router_t)


def _moe_slots_kernel(base_ref, route_ref, slot_ref):
    r = route_ref[...]
    slot_ref[...] = jnp.zeros_like(slot_ref)
    for k in range(2):
        e, rank = r[2 * k:2 * k + 1, :], r[2 * k + 1:2 * k + 2, :]
        start = jnp.zeros_like(e)
        for j in range(N_EXPERTS):
            start = jnp.where(e == j, base_ref[j], start)
        slot_ref[k:k + 1, :] = start + rank


def _moe_slots(route, base, tm=2048):
    t = route.shape[1]
    return pl.pallas_call(
        _moe_slots_kernel,
        grid_spec=pltpu.PrefetchScalarGridSpec(
            num_scalar_prefetch=1,
            grid=(t // tm,),
            in_specs=[pl.BlockSpec((ROUTE_ROWS, tm), lambda i, *_: (0, i))],
            out_specs=pl.BlockSpec((ROUTE_ROWS, tm), lambda i, *_: (0, i))),
        out_shape=jax.ShapeDtypeStruct((ROUTE_ROWS, t), jnp.int32),
        compiler_params=_params("parallel"),
        name="moe_slots",
    )(base, route)


DMA_UNROLL = 8
COMBINE_ROWS = 128


def _moe_dispatch_kernel(tm, fs_ref, fe_ref, slot_ref, x_hbm, xs_hbm, sem):
    i = pl.program_id(0)

    def row_copy(src_row, dst_row):
        return pltpu.make_async_copy(x_hbm.at[pl.ds(src_row, 1)], xs_hbm.at[pl.ds(dst_row, 1)], sem)

    def issue(t, c):
        for k in range(2):
            row_copy(i * tm + t, slot_ref[k, t]).start()
        return c

    lax.fori_loop(0, tm, issue, 0, unroll=DMA_UNROLL)
    pltpu.make_async_copy(x_hbm.at[pl.ds(0, 2 * tm)], xs_hbm.at[pl.ds(0, 2 * tm)], sem).wait()

    @pl.when(i == pl.num_programs(0) - 1)
    def _():
        for e in range(N_EXPERTS + 1):
            def fill(p, c):
                row_copy(0, p).start()
                return c

            def drain(p, c):
                row_copy(0, p).wait()
                return c

            lax.fori_loop(fs_ref[e], fe_ref[e], fill, 0)
            lax.fori_loop(fs_ref[e], fe_ref[e], drain, 0)


def _moe_dispatch(x, slots, fill_start, fill_end, n_slots, tm=1024):
    t = x.shape[0]
    return pl.pallas_call(
        functools.partial(_moe_dispatch_kernel, tm),
        grid_spec=pltpu.PrefetchScalarGridSpec(
            num_scalar_prefetch=2,
            grid=(t // tm,),
            in_specs=[pl.BlockSpec((ROUTE_ROWS, tm), lambda i, *_: (0, i), memory_space=pltpu.SMEM),
                      pl.BlockSpec(memory_space=pl.ANY)],
            out_specs=pl.BlockSpec(memory_space=pl.ANY),
            scratch_shapes=[pltpu.SemaphoreType.DMA(())]),
        out_shape=jax.ShapeDtypeStruct((n_slots, D_MODEL), F32),
        compiler_params=_params("arbitrary"),
        name="moe_dispatch",
    )(fill_start, fill_end, slots, x)


def _ffn_grouped_kernel(te_ref, nv_ref, x_ref, g_ref, wg_ref, wu_ref, wd_ref, y_ref):
    i = pl.program_id(0)

    @pl.when(i < nv_ref[0])
    def _():
        h = _rmsnorm(x_ref[...], g_ref[...]).astype(BF16)
        gate = jnp.dot(h, wg_ref[0], preferred_element_type=F32)
        up = jnp.dot(h, wu_ref[0], preferred_element_type=F32)
        y_ref[...] = jnp.dot((_silu(gate) * up).astype(BF16), wd_ref[0], preferred_element_type=F32)

    @pl.when(i >= nv_ref[0])
    def _():
        y_ref[...] = jnp.zeros_like(y_ref)


def _ffn_grouped(xs, g, wg, wu, wd, tile_expert, n_valid):
    n_slots = xs.shape[0]
    f = wg.shape[2]
    rows = pl.BlockSpec((MOE_TILE, D_MODEL), lambda i, te, nv: (i, 0))
    return pl.pallas_call(
        _ffn_grouped_kernel,
        grid_spec=pltpu.PrefetchScalarGridSpec(
            num_scalar_prefetch=2,
            grid=(n_slots // MOE_TILE,),
            in_specs=[rows,
                      pl.BlockSpec((1, D_MODEL), lambda i, te, nv: (0, 0)),
                      pl.BlockSpec((1, D_MODEL, f), lambda i, te, nv: (te[i], 0, 0)),
                      pl.BlockSpec((1, D_MODEL, f), lambda i, te, nv: (te[i], 0, 0)),
                      pl.BlockSpec((1, f, D_MODEL), lambda i, te, nv: (te[i], 0, 0))],
            out_specs=rows),
        out_shape=jax.ShapeDtypeStruct((n_slots, D_MODEL), F32),
        compiler_params=_params("arbitrary"),
        name="ffn_grouped",
    )(tile_expert, n_valid, xs, g, wg, wu, wd)


def _moe_combine_kernel(tm, final, slot_ref, x_ref, gate_ref, *rest):
    if final:
        gf_ref, rest = rest[0], rest[1:]
    ys_hbm, o_ref, y1_ref, y2_ref, sems = rest
    bufs = (y1_ref, y2_ref)

    def issue(t, c):
        for k in range(2):
            pltpu.make_async_copy(ys_hbm.at[pl.ds(slot_ref[k, t], 1)], bufs[k].at[pl.ds(t, 1)],
                                  sems.at[k]).start()
        return c

    lax.fori_loop(0, tm, issue, 0, unroll=DMA_UNROLL)
    for k in range(2):
        pltpu.make_async_copy(ys_hbm.at[pl.ds(0, tm)], bufs[k], sems.at[k]).wait()
    rc = COMBINE_ROWS

    def rows(ci, c):
        r = pl.ds(pl.multiple_of(ci * rc, rc), rc)
        g1 = jnp.broadcast_to(gate_ref[r, 0:1], (rc, LANES))
        g2 = jnp.broadcast_to(gate_ref[r, 1:2], (rc, LANES))
        ssq = jnp.zeros((rc, LANES), F32)
        for j in range(D_MODEL // LANES):
            cols = slice(j * LANES, (j + 1) * LANES)
            out = x_ref[r, cols] + g1 * y1_ref[r, cols] + g2 * y2_ref[r, cols]
            o_ref[r, cols] = out
            ssq = ssq + out * out
        if final:
            scale = lax.rsqrt(jnp.sum(ssq, axis=-1, keepdims=True) * (1.0 / D_MODEL) + EPS)
            for j in range(D_MODEL // LANES):
                cols = slice(j * LANES, (j + 1) * LANES)
                o_ref[r, cols] = o_ref[r, cols] * scale * gf_ref[:, cols]
        return c

    lax.fori_loop(0, tm // rc, rows, 0)


def _moe_combine(x, slots, gates, ys, final_g=None, tm=512):
    t = x.shape[0]
    final = final_g is not None
    in_specs = [pl.BlockSpec((ROUTE_ROWS, tm), lambda i: (0, i), memory_space=pltpu.SMEM),
                pl.BlockSpec((tm, D_MODEL), lambda i: (i, 0)),
                pl.BlockSpec((tm, LANES), lambda i: (i, 0))]
    args = [slots, x, gates]
    if final:
        in_specs.append(pl.BlockSpec((1, D_MODEL), lambda i: (0, 0)))
        args.append(final_g)
    in_specs.append(pl.BlockSpec(memory_space=pl.ANY))
    args.append(ys)
    return pl.pallas_call(
        functools.partial(_moe_combine_kernel, tm, final),
        grid=(t // tm,),
        in_specs=in_specs,
        out_specs=pl.BlockSpec((tm, D_MODEL), lambda i: (i, 0)),
        out_shape=jax.ShapeDtypeStruct((t, D_MODEL), F32),
        scratch_shapes=[pltpu.VMEM((tm, D_MODEL), F32), pltpu.VMEM((tm, D_MODEL), F32),
                        pltpu.SemaphoreType.DMA((2,))],
        compiler_params=_params("arbitrary"),
        name="moe_combine",
    )(*args)


def _moe_routed(x, g, router_t, wg, wu, wd, final_g=None):
    t = x.shape[0]
    n_tiles = 2 * t // MOE_TILE + N_EXPERTS
    route, gates, counts = _moe_route(x, g, router_t)
    cnt = counts[:, 0].astype(jnp.int32)
    caps = (cnt + MOE_TILE - 1) // MOE_TILE
    cum = jnp.cumsum(caps)
    base = (cum - caps) * MOE_TILE
    n_valid = cum[-1:]
    tile_expert = jnp.minimum(
        jnp.sum((cum[None, :] <= jnp.arange(n_tiles, dtype=jnp.int32)[:, None]).astype(jnp.int32), axis=1),
        N_EXPERTS - 1)
    fill_start = jnp.concatenate([base + cnt, n_valid * MOE_TILE])
    fill_end = jnp.concatenate([base + caps * MOE_TILE, jnp.full((1,), n_tiles * MOE_TILE, jnp.int32)])
    slots = _moe_slots(route, base)
    xs = _moe_dispatch(x, slots, fill_start, fill_end, n_tiles * MOE_TILE)
    ys = _ffn_grouped(xs, g, wg, wu, wd, tile_expert, n_valid)
    return _moe_combine(x, slots, gates, ys, final_g)


def _trunk(x, state_hgrn, state_pool, start_pos, w, seq, tm):
    batch = x.shape[0] // seq
    s_out, b_out = [], []
    for l in range(DEPTH):
        act = _mixer_in(x, w["norm_mix"][l], w["w_in"][l], w["lb_logits"], l, tm)
        if seq > 1:
            og, s_new = _hgrn_prompt(act, w["hg_norm"][l], batch, seq)
            yp, b_new = _pool_prompt(act, w["pool_w"][l], w["pool_scale"][l], batch, seq)
        else:
            og, s_new = _hgrn_step(act, w["hg_norm"][l], state_hgrn[l])
            yp, b_new = _pool_step(act, state_pool[l], w["pool_w"][l], w["pool_scale"][l], start_pos)
        s_out.append(s_new)
        b_out.append(b_new)
        routed = l % 2 == 1 and seq > 1
        x = _mixer_out(x, og, yp, w["w_o"][l], tm)
        j = l // 2
        final_g = w["norm_final"] if l == DEPTH - 1 else None
        if l % 2 == 0:
            x = _ffn(x, w["norm_ffn"][l], w["ffn_w_gate"][j:j + 1], w["ffn_w_up"][j:j + 1],
                     w["ffn_w_down"][j:j + 1], tm, 1408, final_g=final_g)
        elif routed:
            x = _moe_routed(x, w["norm_ffn"][l], w["router_t"][j], w["moe_w_gate"][j],
                            w["moe_w_up"][j], w["moe_w_down"][j], final_g=final_g)
        else:
            x = _ffn(x, w["norm_ffn"][l], w["moe_w_gate"][j], w["moe_w_up"][j], w["moe_w_down"][j],
                     tm, 1408, router=w["router"][j], final_g=final_g)
    return x, jnp.stack(s_out), jnp.stack(b_out)


def kernel(x_prompt, x_sample, state_hgrn, state_pool, lb_logits, norm_mix, w_in, w_o, hg_norm, pool_w,
           pool_scale, norm_ffn, ffn_w_gate, ffn_w_up, ffn_w_down, router, moe_w_gate, moe_w_up, moe_w_down,
           norm_final):
    batch, seq, _ = x_prompt.shape
    dec_batch, dec_seq, _ = x_sample.shape
    assert dec_seq == 1
    past_len = 16384
    w = dict(
        lb_logits=lb_logits,
        norm_mix=norm_mix.reshape(DEPTH, 1, D_MODEL),
        w_in=w_in.astype(BF16),
        w_o=w_o.astype(BF16),
        hg_norm=hg_norm.reshape(DEPTH, 1, HG_WIDTH),
        pool_w=pool_w.astype(BF16),
        pool_scale=pool_scale.reshape(DEPTH, 1, POOL_WIDTH),
        norm_ffn=norm_ffn.reshape(DEPTH, 1, D_MODEL),
        ffn_w_gate=ffn_w_gate.astype(BF16),
        ffn_w_up=ffn_w_up.astype(BF16),
        ffn_w_down=ffn_w_down.astype(BF16),
        router=jnp.pad(router, ((0, 0), (0, 0), (0, LANES - N_EXPERTS))),
        router_t=jnp.swapaxes(router, 1, 2),
        moe_w_gate=moe_w_gate.astype(BF16),
        moe_w_up=moe_w_up.astype(BF16),
        moe_w_down=moe_w_down.astype(BF16),
        norm_final=norm_final.reshape(1, D_MODEL),
    )
    yp, sp, bp = _trunk(x_prompt.reshape(batch * seq, D_MODEL), None, None, 0, w, seq, 512)
    ys, ss, bs = _trunk(x_sample.reshape(dec_batch, D_MODEL), state_hgrn, state_pool, past_len, w, 1, 128)
    return (yp.reshape(batch, seq, D_MODEL), ys.reshape(dec_batch, 1, D_MODEL), sp, ss, bp, bs)
```

```python
import functools

import jax
import jax.numpy as jnp
from jax import lax
from jax.experimental import pallas as pl
from jax.experimental.pallas import tpu as pltpu

F32 = jnp.float32
BF16 = jnp.bfloat16

D_MODEL = 1024
DEPTH = 4
HG_WIDTH = 512
HG_HEADS = 4
HG_D = 128
POOL_WIDTH = 512
POOL_WINDOWS = (2, 4, 8, 16)
POOL_GROUP_W = 128
POOL_BUF = 15
IN_WIDTH = 4 * HG_WIDTH + POOL_WIDTH
ACT_WIDTH = 5 * HG_WIDTH + POOL_WIDTH
N_EXPERTS = 8
EPS = 1e-6
LANES = 128
SUBLANES = 8
VMEM_LIMIT = 56 * 1024 * 1024
HGRN_CHUNK = 128


def _params(*sem):
    return pltpu.CompilerParams(dimension_semantics=sem, vmem_limit_bytes=VMEM_LIMIT)


def _rmsnorm(x, g):
    return x * lax.rsqrt(jnp.mean(x * x, axis=-1, keepdims=True) + EPS) * g


def _silu(x):
    return x * jax.nn.sigmoid(x)


def _mixer_in_kernel(layer, x_ref, g_ref, w_ref, lbl_ref, o_ref):
    h = _rmsnorm(x_ref[...], g_ref[...])
    p = jnp.dot(h.astype(BF16), w_ref[...], preferred_element_type=F32)
    lg = lbl_ref[...]
    e = jnp.exp(lg - jnp.max(lg, axis=0, keepdims=True))
    pr = e / jnp.sum(e, axis=0, keepdims=True)
    cum = pr[0:1]
    for j in range(1, layer + 1):
        cum = cum + pr[j:j + 1]
    lb = cum - pr[0:1]
    w = HG_WIDTH
    q, fx, ix, g, u = p[:, :w], p[:, w:2 * w], p[:, 2 * w:3 * w], p[:, 3 * w:4 * w], p[:, 4 * w:]
    f = lb + (1.0 - lb) * jax.nn.sigmoid(fx)
    o_ref[:, 0:w] = _silu(q)
    o_ref[:, w:2 * w] = jnp.log(f)
    o_ref[:, 2 * w:3 * w] = 1.0 - f
    o_ref[:, 3 * w:4 * w] = ix
    o_ref[:, 4 * w:5 * w] = _silu(g)
    o_ref[:, 5 * w:] = u


def _mixer_in(x, g, w_in, lb_logits, layer, tm):
    t = x.shape[0]
    return pl.pallas_call(
        functools.partial(_mixer_in_kernel, layer),
        grid=(t // tm,),
        in_specs=[
            pl.BlockSpec((tm, D_MODEL), lambda i: (i, 0)),
            pl.BlockSpec((1, D_MODEL), lambda i: (0, 0)),
            pl.BlockSpec((D_MODEL, IN_WIDTH), lambda i: (0, 0)),
            pl.BlockSpec((DEPTH, HG_WIDTH), lambda i: (0, 0)),
        ],
        out_specs=pl.BlockSpec((tm, ACT_WIDTH), lambda i: (i, 0)),
        out_shape=jax.ShapeDtypeStruct((t, ACT_WIDTH), F32),
        compiler_params=_params("parallel"),
        name="mixer_in",
    )(x, g, w_in, lb_logits)


def _level_factors(q, k, b, n, row):
    c = b.shape[0]
    half = n // 2
    if half >= SUBLANES:
        zero = jnp.zeros((half, HG_D), F32)
        qs, ks = [], []
        for r0 in range(0, c, n):
            mid = b[r0 + half - 1:r0 + half, :]
            ks += [k[r0:r0 + half] * jnp.exp(mid - b[r0:r0 + half]), zero]
            qs += [zero, q[r0 + half:r0 + n] * jnp.exp(b[r0 + half:r0 + n] - mid)]
        return jnp.concatenate(qs, axis=0), jnp.concatenate(ks, axis=0)
    rmod = row & (n - 1)
    bm = b
    for j in range(half):
        bm = jnp.where(rmod == half + j, pltpu.roll(b, j + 1, 0), bm)
    for j in range(half - 1):
        bm = jnp.where(rmod == j, pltpu.roll(b, c - (half - 1 - j), 0), bm)
    right = (row & half) != 0
    qk = jnp.where(right, q, k) * jnp.exp(jnp.minimum(jnp.where(right, b - bm, bm - b), 0.0))
    return jnp.where(right, qk, 0.0), jnp.where(right, 0.0, qk)


def _split3(x):
    hi = x.astype(BF16)
    r1 = x - hi.astype(F32)
    mid = r1.astype(BF16)
    lo = (r1 - mid.astype(F32)).astype(BF16)
    return jnp.concatenate([hi, mid, lo], axis=1)


def _hgrn_prompt_kernel(tl, q_ref, lf_ref, k_ref, v_ref, g_ref, nrm_ref, o_ref, s_ref, st_ref):
    c = HGRN_CHUNK
    li = pl.program_id(1)

    @pl.when(li == 0)
    def _():
        st_ref[...] = jnp.zeros_like(st_ref)

    row = lax.broadcasted_iota(jnp.int32, (c, HG_D), 0)
    ti = lax.broadcasted_iota(jnp.int32, (c, c), 0)
    si = lax.broadcasted_iota(jnp.int32, (c, c), 1)
    tri = (si <= ti).astype(BF16)
    apart = ti ^ si
    nt = (((1,), (1,)), ((), ()))
    tn = (((0,), (0,)), ((), ()))

    def chunk(ci, carry):
        rows = pl.ds(pl.multiple_of(ci * c, c), c)
        for hd in range(HG_HEADS):
            cols = slice(hd * HG_D, (hd + 1) * HG_D)
            q, lf, k, v = q_ref[rows, cols], lf_ref[rows, cols], k_ref[rows, cols], v_ref[rows, cols]
            b3 = jnp.dot(tri, _split3(lf), preferred_element_type=F32)
            b = b3[:, :HG_D] + b3[:, HG_D:2 * HG_D] + b3[:, 2 * HG_D:]
            vb = v.astype(BF16)
            scores = None
            n = c
            while n >= 2:
                qt, kt = _level_factors(q, k, b, n, row)
                s_n = lax.dot_general(qt.astype(BF16), kt.astype(BF16), nt, preferred_element_type=F32)
                scores = s_n if scores is None else jnp.where(apart < n, s_n, scores)
                n //= 2
            st = st_ref[hd]
            qs = (q * jnp.exp(b)).astype(BF16)
            o = jnp.dot(scores.astype(BF16), vb, preferred_element_type=F32)
            o = o + lax.dot_general(qs, st.astype(BF16), nt, preferred_element_type=F32)
            o = o + jnp.sum(q * k, axis=-1, keepdims=True) * v
            bc = b[c - 1:c, :]
            kd = (k * jnp.exp(bc - b)).astype(BF16)
            st_ref[hd] = st * jnp.exp(bc) + lax.dot_general(vb, kd, tn, preferred_element_type=F32)
            o = o * lax.rsqrt(jnp.mean(o * o, axis=-1, keepdims=True) + EPS) * nrm_ref[:, cols]
            o_ref[rows, cols] = (o * g_ref[rows, cols]).astype(o_ref.dtype)
        return carry

    lax.fori_loop(0, tl // c, chunk, 0)

    @pl.when(li == pl.num_programs(1) - 1)
    def _():
        for hd in range(HG_HEADS):
            s_ref[0, hd] = st_ref[hd].T


def _hgrn_prompt(act, hg_norm, batch, seq, tl=1024):
    nl = seq // tl
    blk = lambda part: pl.BlockSpec((tl, HG_WIDTH), lambda b, i: (b * nl + i, part))
    return pl.pallas_call(
        functools.partial(_hgrn_prompt_kernel, tl),
        grid=(batch, nl),
        in_specs=[blk(0), blk(1), blk(2), blk(3), blk(4),
                  pl.BlockSpec((1, HG_WIDTH), lambda b, i: (0, 0))],
        out_specs=[pl.BlockSpec((tl, HG_WIDTH), lambda b, i: (b * nl + i, 0)),
                   pl.BlockSpec((1, HG_HEADS, HG_D, HG_D), lambda b, i: (b, 0, 0, 0))],
        out_shape=[jax.ShapeDtypeStruct((batch * seq, HG_WIDTH), BF16),
                   jax.ShapeDtypeStruct((batch, HG_HEADS, HG_D, HG_D), F32)],
        scratch_shapes=[pltpu.VMEM((HG_HEADS, HG_D, HG_D), F32)],
        compiler_params=_params("parallel", "arbitrary"),
        name="hgrn_prompt",
    )(act, act, act, act, act, hg_norm)


HGRN_STEP_BT = 16


def _hgrn_step_kernel(q_ref, lf_ref, k_ref, v_ref, g_ref, nrm_ref, s_ref, o_ref, so_ref):
    bt = HGRN_STEP_BT
    pad = jnp.zeros((HG_D - bt, HG_D), F32)
    col = lambda a: jnp.concatenate([a, pad], axis=0).T
    q = q_ref[...]
    v = v_ref[...]
    ft, kt, qt = col(jnp.exp(lf_ref[...])), col(k_ref[...]), col(q)
    rows = []
    for s in range(bt):
        sn = ft[:, s:s + 1] * s_ref[s, 0] + kt[:, s:s + 1] * v[s:s + 1, :]
        so_ref[s, 0] = sn
        rows.append(jnp.sum(sn * qt[:, s:s + 1], axis=0, keepdims=True))
    o = jnp.concatenate(rows, axis=0)
    o = o * lax.rsqrt(jnp.mean(o * o, axis=-1, keepdims=True) + EPS) * nrm_ref[...]
    o_ref[...] = (o * g_ref[...]).astype(o_ref.dtype)


def _hgrn_step(act, hg_norm, state):
    batch = act.shape[0]
    bt = HGRN_STEP_BT
    blk = lambda off: pl.BlockSpec((bt, HG_D), lambda i, h: (i, off + h))
    sblk = pl.BlockSpec((bt, 1, HG_D, HG_D), lambda i, h: (i, h, 0, 0))
    return pl.pallas_call(
        _hgrn_step_kernel,
        grid=(batch // bt, HG_HEADS),
        in_specs=[blk(0), blk(4), blk(8), blk(12), blk(16),
                  pl.BlockSpec((1, HG_D), lambda i, h: (0, h)), sblk],
        out_specs=[pl.BlockSpec((bt, HG_D), lambda i, h: (i, h)), sblk],
        out_shape=[jax.ShapeDtypeStruct((batch, HG_WIDTH), BF16),
                   jax.ShapeDtypeStruct(state.shape, F32)],
        compiler_params=_params("parallel", "parallel"),
        name="hgrn_step",
    )(act, act, act, act, act, hg_norm, state)


def _pool_map(d_groups, wp_ref, scale_ref, o_ref):
    for gi, d in enumerate(d_groups):
        sl = slice(gi * POOL_GROUP_W, (gi + 1) * POOL_GROUP_W)
        y = jnp.dot(d.astype(BF16), wp_ref[gi], preferred_element_type=F32)
        o_ref[:, sl] = (y * scale_ref[:, sl]).astype(o_ref.dtype)


POOL_HIST = 16


def _pool_prompt_kernel(tl, u_ref, wp_ref, scale_ref, o_ref, nb_ref, carry_ref):
    i = pl.program_id(1)

    @pl.when(i == 0)
    def _():
        carry_ref[...] = jnp.zeros_like(carry_ref)

    z = u_ref[...]
    ext = jnp.concatenate([carry_ref[...], z], axis=0)
    sums = {1: ext}
    w = 1
    while w < max(POOL_WINDOWS):
        sums[2 * w] = sums[w] + pltpu.roll(sums[w], w, 0)
        w *= 2
    pos = i * tl + lax.broadcasted_iota(jnp.int32, (tl, POOL_GROUP_W), 0)
    ds = []
    for gi, w in enumerate(POOL_WINDOWS):
        sl = slice(gi * POOL_GROUP_W, (gi + 1) * POOL_GROUP_W)
        cnt = jnp.minimum(pos + 1, w).astype(F32)
        ds.append(sums[w][POOL_HIST:, sl] / cnt - z[:, sl])
    _pool_map(ds, wp_ref, scale_ref, o_ref)
    carry_ref[...] = z[tl - POOL_HIST:, :]

    @pl.when(i == pl.num_programs(1) - 1)
    def _():
        nb_ref[0] = z[tl - POOL_BUF:, :]


def _pool_prompt(act, pool_w, pool_scale, batch, seq, tl=512):
    nl = seq // tl
    return pl.pallas_call(
        functools.partial(_pool_prompt_kernel, tl),
        grid=(batch, nl),
        in_specs=[pl.BlockSpec((tl, POOL_WIDTH), lambda b, i: (b * nl + i, 5)),
                  pl.BlockSpec((len(POOL_WINDOWS), POOL_GROUP_W, POOL_GROUP_W), lambda b, i: (0, 0, 0)),
                  pl.BlockSpec((1, POOL_WIDTH), lambda b, i: (0, 0))],
        out_specs=[pl.BlockSpec((tl, POOL_WIDTH), lambda b, i: (b * nl + i, 0)),
                   pl.BlockSpec((1, POOL_BUF, POOL_WIDTH), lambda b, i: (b, 0, 0))],
        out_shape=[jax.ShapeDtypeStruct((batch * seq, POOL_WIDTH), BF16),
                   jax.ShapeDtypeStruct((batch, POOL_BUF, POOL_WIDTH), F32)],
        scratch_shapes=[pltpu.VMEM((POOL_HIST, POOL_WIDTH), F32)],
        compiler_params=_params("parallel", "arbitrary"),
        name="pool_prompt",
    )(act, pool_w, pool_scale)


POOL_STEP_BT = 16


def _pool_step_kernel(start_pos, u_ref, buf_ref, wp_ref, scale_ref, o_ref, nb_ref):
    u = u_ref[...]
    buf = buf_ref[...]
    ds = []
    for gi, w in enumerate(POOL_WINDOWS):
        sl = slice(gi * POOL_GROUP_W, (gi + 1) * POOL_GROUP_W)
        tot = u[:, sl] + jnp.sum(buf[:, POOL_BUF - (w - 1):, sl], axis=1)
        ds.append(tot / float(min(start_pos + 1, w)) - u[:, sl])
    _pool_map(ds, wp_ref, scale_ref, o_ref)
    nb_ref[:, 0:POOL_BUF - 1, :] = buf[:, 1:, :]
    nb_ref[:, POOL_BUF - 1:, :] = u[:, None, :]


def _pool_step(act, buf, pool_w, pool_scale, start_pos):
    batch = act.shape[0]
    bt = POOL_STEP_BT
    bblk = pl.BlockSpec((bt, POOL_BUF, POOL_WIDTH), lambda i: (i, 0, 0))
    return pl.pallas_call(
        functools.partial(_pool_step_kernel, start_pos),
        grid=(batch // bt,),
        in_specs=[pl.BlockSpec((bt, POOL_WIDTH), lambda i: (i, 5)), bblk,
                  pl.BlockSpec((len(POOL_WINDOWS), POOL_GROUP_W, POOL_GROUP_W), lambda i: (0, 0, 0)),
                  pl.BlockSpec((1, POOL_WIDTH), lambda i: (0, 0))],
        out_specs=[pl.BlockSpec((bt, POOL_WIDTH), lambda i: (i, 0)), bblk],
        out_shape=[jax.ShapeDtypeStruct((batch, POOL_WIDTH), BF16),
                   jax.ShapeDtypeStruct(buf.shape, F32)],
        compiler_params=_params("parallel"),
        name="pool_step",
    )(act, buf, pool_w, pool_scale)


def _mixer_out_kernel(x_ref, o_ref, p_ref, w_ref, y_ref):
    y = jnp.dot(o_ref[...], w_ref[0:HG_WIDTH, :], preferred_element_type=F32)
    y = y + jnp.dot(p_ref[...], w_ref[HG_WIDTH:, :], preferred_element_type=F32)
    y_ref[...] = x_ref[...] + y


def _mixer_out(x, og, yp, w_o, tm):
    t = x.shape[0]
    return pl.pallas_call(
        _mixer_out_kernel,
        grid=(t // tm,),
        in_specs=[pl.BlockSpec((tm, D_MODEL), lambda i: (i, 0)),
                  pl.BlockSpec((tm, HG_WIDTH), lambda i: (i, 0)),
                  pl.BlockSpec((tm, POOL_WIDTH), lambda i: (i, 0)),
                  pl.BlockSpec((D_MODEL, D_MODEL), lambda i: (0, 0))],
        out_specs=pl.BlockSpec((tm, D_MODEL), lambda i: (i, 0)),
        out_shape=jax.ShapeDtypeStruct((t, D_MODEL), F32),
        compiler_params=_params("parallel"),
        name="mixer_out",
    )(x, og, yp, w_o)


def _top2_combine(logits):
    lane = lax.broadcasted_iota(jnp.int32, logits.shape, 1).astype(F32)
    neg = jnp.float32(-jnp.inf)
    lg = jnp.where(lane < N_EXPERTS, logits, neg)
    m1 = jnp.max(lg, axis=-1, keepdims=True)
    i1 = jnp.min(jnp.where(lg == m1, lane, float(LANES)), axis=-1, keepdims=True)
    lg2 = jnp.where(lane == i1, neg, lg)
    m2 = jnp.max(lg2, axis=-1, keepdims=True)
    i2 = jnp.min(jnp.where(lg2 == m2, lane, float(LANES)), axis=-1, keepdims=True)
    e2 = jnp.exp(m2 - m1)
    g1 = 1.0 / (1.0 + e2)
    g2 = e2 / (1.0 + e2)
    return jnp.where(lane == i1, g1, 0.0) + jnp.where(lane == i2, g2, 0.0)


def _ffn_kernel(moe, final, x_ref, g_ref, *rest):
    if moe:
        r_ref, rest = rest[0], rest[1:]
    if final:
        gf_ref, rest = rest[0], rest[1:]
    wg_ref, wu_ref, wd_ref, y_ref, h_ref, acc_ref = rest[:6]
    e, j = pl.program_id(1), pl.program_id(2)
    first = (e == 0) & (j == 0)
    last = (e == pl.num_programs(1) - 1) & (j == pl.num_programs(2) - 1)

    @pl.when(first)
    def _():
        h = _rmsnorm(x_ref[...], g_ref[...])
        h_ref[...] = h.astype(BF16)
        acc_ref[...] = jnp.zeros_like(acc_ref)
        if moe:
            logits = jnp.dot(h, r_ref[...], precision=lax.Precision.HIGHEST, preferred_element_type=F32)
            rest[6][...] = _top2_combine(logits)

    h = h_ref[...]
    gate = jnp.dot(h, wg_ref[0], preferred_element_type=F32)
    up = jnp.dot(h, wu_ref[0], preferred_element_type=F32)
    y = jnp.dot((_silu(gate) * up).astype(BF16), wd_ref[0], preferred_element_type=F32)
    if moe:
        comb = rest[6][...]
        lane = lax.broadcasted_iota(jnp.int32, comb.shape, 1)
        y = jnp.sum(jnp.where(lane == e, comb, 0.0), axis=-1, keepdims=True) * y
    acc_ref[...] += y

    @pl.when(last)
    def _():
        out = x_ref[...] + acc_ref[...]
        if final:
            out = _rmsnorm(out, gf_ref[...])
        y_ref[...] = out


def _ffn(x, g, wg, wu, wd, tm, tf, router=None, final_g=None):
    t = x.shape[0]
    ne, _, f = wg.shape
    moe, final = router is not None, final_g is not None
    vec = pl.BlockSpec((1, D_MODEL), lambda i, e, j: (0, 0))
    in_specs = [pl.BlockSpec((tm, D_MODEL), lambda i, e, j: (i, 0)), vec]
    args = [x, g]
    scratch = [pltpu.VMEM((tm, D_MODEL), BF16), pltpu.VMEM((tm, D_MODEL), F32)]
    if moe:
        in_specs.append(pl.BlockSpec((D_MODEL, LANES), lambda i, e, j: (0, 0)))
        args.append(router)
        scratch.append(pltpu.VMEM((tm, LANES), F32))
    if final:
        in_specs.append(vec)
        args.append(final_g)
    in_specs += [pl.BlockSpec((1, D_MODEL, tf), lambda i, e, j: (e, 0, j)),
                 pl.BlockSpec((1, D_MODEL, tf), lambda i, e, j: (e, 0, j)),
                 pl.BlockSpec((1, tf, D_MODEL), lambda i, e, j: (e, j, 0))]
    args += [wg, wu, wd]
    return pl.pallas_call(
        functools.partial(_ffn_kernel, moe, final),
        grid=(t // tm, ne, f // tf),
        in_specs=in_specs,
        out_specs=pl.BlockSpec((tm, D_MODEL), lambda i, e, j: (i, 0)),
        out_shape=jax.ShapeDtypeStruct((t, D_MODEL), F32),
        scratch_shapes=scratch,
        compiler_params=_params("parallel", "arbitrary", "arbitrary"),
        name="ffn_moe" if moe else "ffn_dense",
    )(*args)


MOE_TILE = 512
ROUTE_ROWS = 8
NT_DIMS = (((1,), (1,)), ((), ()))


def _moe_route_kernel(x_ref, g_ref, rt_ref, route_ref, gate_ref, cnt_ref, carry_ref):
    tm = x_ref.shape[0]

    @pl.when(pl.program_id(0) == 0)
    def _():
        carry_ref[...] = jnp.zeros_like(carry_ref)

    h = _rmsnorm(x_ref[...], g_ref[...])
    lt = lax.dot_general(rt_ref[...], h, NT_DIMS, precision=lax.Precision.HIGHEST,
                         preferred_element_type=F32)
    ex = lax.broadcasted_iota(jnp.int32, lt.shape, 0).astype(F32)
    neg = jnp.float32(-jnp.inf)
    m1 = jnp.max(lt, axis=0, keepdims=True)
    i1 = jnp.min(jnp.where(lt == m1, ex, float(N_EXPERTS)), axis=0, keepdims=True)
    l2 = jnp.where(ex == i1, neg, lt)
    m2 = jnp.max(l2, axis=0, keepdims=True)
    i2 = jnp.min(jnp.where(l2 == m2, ex, float(N_EXPERTS)), axis=0, keepdims=True)
    e2 = jnp.exp(m2 - m1)
    g1 = 1.0 / (1.0 + e2)
    g2 = e2 / (1.0 + e2)
    sel1, sel2 = ex == i1, ex == i2
    member = jnp.where(sel1 | sel2, 1.0, 0.0)
    t0 = lax.broadcasted_iota(jnp.int32, (tm, tm), 0)
    t1 = lax.broadcasted_iota(jnp.int32, (tm, tm), 1)
    earlier = (t0 < t1).astype(BF16)
    rank = jnp.dot(member.astype(BF16), earlier, preferred_element_type=F32) + carry_ref[:, 0:1]
    route_ref[...] = jnp.zeros_like(route_ref)
    route_ref[0:1, :] = i1.astype(jnp.int32)
    route_ref[1:2, :] = jnp.sum(jnp.where(sel1, rank, 0.0), axis=0, keepdims=True).astype(jnp.int32)
    route_ref[2:3, :] = i2.astype(jnp.int32)
    route_ref[3:4, :] = jnp.sum(jnp.where(sel2, rank, 0.0), axis=0, keepdims=True).astype(jnp.int32)
    carry_ref[...] += jnp.sum(member, axis=1, keepdims=True)
    cnt_ref[...] = carry_ref[...]
    row = lax.broadcasted_iota(jnp.int32, (LANES, tm), 0)
    gate_ref[...] = jnp.where(row == 0, g1, jnp.where(row == 1, g2, 0.0)).T


def _moe_route(x, g, router_t, tm=512):
    t = x.shape[0]
    return pl.pallas_call(
        _moe_route_kernel,
        grid=(t // tm,),
        in_specs=[pl.BlockSpec((tm, D_MODEL), lambda i: (i, 0)),
                  pl.BlockSpec((1, D_MODEL), lambda i: (0, 0)),
                  pl.BlockSpec((N_EXPERTS, D_MODEL), lambda i: (0, 0))],
        out_specs=[pl.BlockSpec((ROUTE_ROWS, tm), lambda i: (0, i)),
                   pl.BlockSpec((tm, LANES), lambda i: (i, 0)),
                   pl.BlockSpec((N_EXPERTS, LANES), lambda i: (0, 0))],
        out_shape=[jax.ShapeDtypeStruct((ROUTE_ROWS, t), jnp.int32),
                   jax.ShapeDtypeStruct((t, LANES), F32),
                   jax.ShapeDtypeStruct((N_EXPERTS, LANES), F32)],
        scratch_shapes=[pltpu.VMEM((N_EXPERTS, LANES), F32)],
        compiler_params=_params("arbitrary"),
        name="moe_route",
    )(x, g, router_t)


def _moe_slots_kernel(base_ref, route_ref, slot_ref):
    r = route_ref[...]
    slot_ref[...] = jnp.zeros_like(slot_ref)
    for k in range(2):
        e, rank = r[2 * k:2 * k + 1, :], r[2 * k + 1:2 * k + 2, :]
        start = jnp.zeros_like(e)
        for j in range(N_EXPERTS):
            start = jnp.where(e == j, base_ref[j], start)
        slot_ref[k:k + 1, :] = start + rank


def _moe_slots(route, base, tm=2048):
    t = route.shape[1]
    return pl.pallas_call(
        _moe_slots_kernel,
        grid_spec=pltpu.PrefetchScalarGridSpec(
            num_scalar_prefetch=1,
            grid=(t // tm,),
            in_specs=[pl.BlockSpec((ROUTE_ROWS, tm), lambda i, *_: (0, i))],
            out_specs=pl.BlockSpec((ROUTE_ROWS, tm), lambda i, *_: (0, i))),
        out_shape=jax.ShapeDtypeStruct((ROUTE_ROWS, t), jnp.int32),
        compiler_params=_params("parallel"),
        name="moe_slots",
    )(base, route)


DMA_UNROLL = 8
COMBINE_ROWS = 128


def _moe_dispatch_kernel(tm, fs_ref, fe_ref, slot_ref, x_ref, xs_hbm, sem):
    i = pl.program_id(0)

    def row_copy(src_row, dst_row):
        return pltpu.make_async_copy(x_ref.at[pl.ds(src_row, 1)], xs_hbm.at[pl.ds(dst_row, 1)], sem)

    def issue(t, c):
        for k in range(2):
            row_copy(t, slot_ref[k, t]).start()
        return c

    lax.fori_loop(0, tm, issue, 0, unroll=DMA_UNROLL)
    for k in range(2):
        pltpu.make_async_copy(x_ref, xs_hbm.at[pl.ds(0, tm)], sem).wait()

    @pl.when(i == pl.num_programs(0) - 1)
    def _():
        for e in range(N_EXPERTS + 1):
            def fill(p, c):
                row_copy(0, p).start()
                return c

            def drain(p, c):
                row_copy(0, p).wait()
                return c

            lax.fori_loop(fs_ref[e], fe_ref[e], fill, 0)
            lax.fori_loop(fs_ref[e], fe_ref[e], drain, 0)


def _moe_dispatch(x, slots, fill_start, fill_end, n_slots, tm=1024):
    t = x.shape[0]
    return pl.pallas_call(
        functools.partial(_moe_dispatch_kernel, tm),
        grid_spec=pltpu.PrefetchScalarGridSpec(
            num_scalar_prefetch=2,
            grid=(t // tm,),
            in_specs=[pl.BlockSpec((ROUTE_ROWS, tm), lambda i, *_: (0, i), memory_space=pltpu.SMEM),
                      pl.BlockSpec((tm, D_MODEL), lambda i, *_: (i, 0))],
            out_specs=pl.BlockSpec(memory_space=pl.ANY),
            scratch_shapes=[pltpu.SemaphoreType.DMA(())]),
        out_shape=jax.ShapeDtypeStruct((n_slots, D_MODEL), F32),
        compiler_params=_params("arbitrary"),
        name="moe_dispatch",
    )(fill_start, fill_end, slots, x)


def _ffn_grouped_kernel(te_ref, nv_ref, x_ref, g_ref, wg_ref, wu_ref, wd_ref, y_ref):
    i = pl.program_id(0)

    @pl.when(i < nv_ref[0])
    def _():
        h = _rmsnorm(x_ref[...], g_ref[...]).astype(BF16)
        gate = jnp.dot(h, wg_ref[0], preferred_element_type=F32)
        up = jnp.dot(h, wu_ref[0], preferred_element_type=F32)
        y_ref[...] = jnp.dot((_silu(gate) * up).astype(BF16), wd_ref[0], preferred_element_type=F32)

    @pl.when(i >= nv_ref[0])
    def _():
        y_ref[...] = jnp.zeros_like(y_ref)


def _ffn_grouped(xs, g, wg, wu, wd, tile_expert, n_valid):
    n_slots = xs.shape[0]
    f = wg.shape[2]
    rows = pl.BlockSpec((MOE_TILE, D_MODEL), lambda i, te, nv: (i, 0))
    return pl.pallas_call(
        _ffn_grouped_kernel,
        grid_spec=pltpu.PrefetchScalarGridSpec(
            num_scalar_prefetch=2,
            grid=(n_slots // MOE_TILE,),
            in_specs=[rows,
                      pl.BlockSpec((1, D_MODEL), lambda i, te, nv: (0, 0)),
                      pl.BlockSpec((1, D_MODEL, f), lambda i, te, nv: (te[i], 0, 0)),
                      pl.BlockSpec((1, D_MODEL, f), lambda i, te, nv: (te[i], 0, 0)),
                      pl.BlockSpec((1, f, D_MODEL), lambda i, te, nv: (te[i], 0, 0))],
            out_specs=rows),
        out_shape=jax.ShapeDtypeStruct((n_slots, D_MODEL), F32),
        compiler_params=_params("arbitrary"),
        name="ffn_grouped",
    )(tile_expert, n_valid, xs, g, wg, wu, wd)


def _moe_combine_kernel(tm, final, slot_ref, x_ref, gate_ref, *rest):
    if final:
        gf_ref, rest = rest[0], rest[1:]
    ys_hbm, o_ref, y1_ref, y2_ref, sems = rest
    bufs = (y1_ref, y2_ref)

    def issue(t, c):
        for k in range(2):
            pltpu.make_async_copy(ys_hbm.at[pl.ds(slot_ref[k, t], 1)], bufs[k].at[pl.ds(t, 1)],
                                  sems.at[k]).start()
        return c

    lax.fori_loop(0, tm, issue, 0, unroll=DMA_UNROLL)
    for k in range(2):
        pltpu.make_async_copy(ys_hbm.at[pl.ds(0, tm)], bufs[k], sems.at[k]).wait()
    rc = COMBINE_ROWS

    def rows(ci, c):
        r = pl.ds(pl.multiple_of(ci * rc, rc), rc)
        g1 = jnp.broadcast_to(gate_ref[r, 0:1], (rc, LANES))
        g2 = jnp.broadcast_to(gate_ref[r, 1:2], (rc, LANES))
        ssq = jnp.zeros((rc, LANES), F32)
        for j in range(D_MODEL // LANES):
            cols = slice(j * LANES, (j + 1) * LANES)
            out = x_ref[r, cols] + g1 * y1_ref[r, cols] + g2 * y2_ref[r, cols]
            o_ref[r, cols] = out
            ssq = ssq + out * out
        if final:
            scale = lax.rsqrt(jnp.sum(ssq, axis=-1, keepdims=True) * (1.0 / D_MODEL) + EPS)
            for j in range(D_MODEL // LANES):
                cols = slice(j * LANES, (j + 1) * LANES)
                o_ref[r, cols] = o_ref[r, cols] * scale * gf_ref[:, cols]
        return c

    lax.fori_loop(0, tm // rc, rows, 0)


def _moe_combine(x, slots, gates, ys, final_g=None, tm=512):
    t = x.shape[0]
    final = final_g is not None
    in_specs = [pl.BlockSpec((ROUTE_ROWS, tm), lambda i: (0, i), memory_space=pltpu.SMEM),
                pl.BlockSpec((tm, D_MODEL), lambda i: (i, 0)),
                pl.BlockSpec((tm, LANES), lambda i: (i, 0))]
    args = [slots, x, gates]
    if final:
        in_specs.append(pl.BlockSpec((1, D_MODEL), lambda i: (0, 0)))
        args.append(final_g)
    in_specs.append(pl.BlockSpec(memory_space=pl.ANY))
    args.append(ys)
    return pl.pallas_call(
        functools.partial(_moe_combine_kernel, tm, final),
        grid=(t // tm,),
        in_specs=in_specs,
        out_specs=pl.BlockSpec((tm, D_MODEL), lambda i: (i, 0)),
        out_shape=jax.ShapeDtypeStruct((t, D_MODEL), F32),
        scratch_shapes=[pltpu.VMEM((tm, D_MODEL), F32), pltpu.VMEM((tm, D_MODEL), F32),
                        pltpu.SemaphoreType.DMA((2,))],
        compiler_params=_params("arbitrary"),
        name="moe_combine",
    )(*args)


def _moe_routed(x, g, router_t, wg, wu, wd, final_g=None):
    t = x.shape[0]
    n_tiles = 2 * t // MOE_TILE + N_EXPERTS
    route, gates, counts = _moe_route(x, g, router_t)
    cnt = counts[:, 0].astype(jnp.int32)
    caps = (cnt + MOE_TILE - 1) // MOE_TILE
    cum = jnp.cumsum(caps)
    base = (cum - caps) * MOE_TILE
    n_valid = cum[-1:]
    tile_expert = jnp.minimum(
        jnp.sum((cum[None, :] <= jnp.arange(n_tiles, dtype=jnp.int32)[:, None]).astype(jnp.int32), axis=1),
        N_EXPERTS - 1)
    fill_start = jnp.concatenate([base + cnt, n_valid * MOE_TILE])
    fill_end = jnp.concatenate([base + caps * MOE_TILE, jnp.full((1,), n_tiles * MOE_TILE, jnp.int32)])
    slots = _moe_slots(route, base)
    xs = _moe_dispatch(x, slots, fill_start, fill_end, n_tiles * MOE_TILE)
    ys = _ffn_grouped(xs, g, wg, wu, wd, tile_expert, n_valid)
    return _moe_combine(x, slots, gates, ys, final_g)


def _trunk(x, state_hgrn, state_pool, start_pos, w, seq, tm):
    batch = x.shape[0] // seq
    s_out, b_out = [], []
    for l in range(DEPTH):
        act = _mixer_in(x, w["norm_mix"][l], w["w_in"][l], w["lb_logits"], l, tm)
        if seq > 1:
            og, s_new = _hgrn_prompt(act, w["hg_norm"][l], batch, seq)
            yp, b_new = _pool_prompt(act, w["pool_w"][l], w["pool_scale"][l], batch, seq)
        else:
            og, s_new = _hgrn_step(act, w["hg_norm"][l], state_hgrn[l])
            yp, b_new = _pool_step(act, state_pool[l], w["pool_w"][l], w["pool_scale"][l], start_pos)
        s_out.append(s_new)
        b_out.append(b_new)
        routed = l % 2 == 1 and seq > 1
        x = _mixer_out(x, og, yp, w["w_o"][l], tm)
        j = l // 2
        final_g = w["norm_final"] if l == DEPTH - 1 else None
        if l % 2 == 0:
            x = _ffn(x, w["norm_ffn"][l], w["ffn_w_gate"][j:j + 1], w["ffn_w_up"][j:j + 1],
                     w["ffn_w_down"][j:j + 1], tm, 1408, final_g=final_g)
        elif routed:
            x = _moe_routed(x, w["norm_ffn"][l], w["router_t"][j], w["moe_w_gate"][j],
                            w["moe_w_up"][j], w["moe_w_down"][j], final_g=final_g)
        else:
            x = _ffn(x, w["norm_ffn"][l], w["moe_w_gate"][j], w["moe_w_up"][j], w["moe_w_down"][j],
                     tm, 1408, router=w["router"][j], final_g=final_g)
    return x, jnp.stack(s_out), jnp.stack(b_out)


def kernel(x_prompt, x_sample, state_hgrn, state_pool, lb_logits, norm_mix, w_in, w_o, hg_norm, pool_w,
           pool_scale, norm_ffn, ffn_w_gate, ffn_w_up, ffn_w_down, router, moe_w_gate, moe_w_up, moe_w_down,
           norm_final):
    batch, seq, _ = x_prompt.shape
    dec_batch, dec_seq, _ = x_sample.shape
    assert dec_seq == 1
    past_len = 16384
    w = dict(
        lb_logits=lb_logits,
        norm_mix=norm_mix.reshape(DEPTH, 1, D_MODEL),
        w_in=w_in.astype(BF16),
        w_o=w_o.astype(BF16),
        hg_norm=hg_norm.reshape(DEPTH, 1, HG_WIDTH),
        pool_w=pool_w.astype(BF16),
        pool_scale=pool_scale.reshape(DEPTH, 1, POOL_WIDTH),
        norm_ffn=norm_ffn.reshape(DEPTH, 1, D_MODEL),
        ffn_w_gate=ffn_w_gate.astype(BF16),
        ffn_w_up=ffn_w_up.astype(BF16),
        ffn_w_down=ffn_w_down.astype(BF16),
        router=jnp.pad(router, ((0, 0), (0, 0), (0, LANES - N_EXPERTS))),
        router_t=jnp.swapaxes(router, 1, 2),
        moe_w_gate=moe_w_gate.astype(BF16),
        moe_w_up=moe_w_up.astype(BF16),
        moe_w_down=moe_w_down.astype(BF16),
        norm_final=norm_final.reshape(1, D_MODEL),
    )
    yp, sp, bp = _trunk(x_prompt.reshape(batch * seq, D_MODEL), None, None, 0, w, seq, 512)
    ys, ss, bs = _trunk(x_sample.reshape(dec_batch, D_MODEL), state_hgrn, state_pool, past_len, w, 1, 128)
    return (yp.reshape(batch, seq, D_MODEL), ys.reshape(dec_batch, 1, D_MODEL), sp, ss, bp, bs)
```

```python
import functools

import jax
import jax.numpy as jnp
from jax import lax
from jax.experimental import pallas as pl
from jax.experimental.pallas import tpu as pltpu

F32 = jnp.float32
BF16 = jnp.bfloat16

D_MODEL = 1024
DEPTH = 4
HG_WIDTH = 512
HG_HEADS = 4
HG_D = 128
POOL_WIDTH = 512
POOL_WINDOWS = (2, 4, 8, 16)
POOL_GROUP_W = 128
POOL_BUF = 15
IN_WIDTH = 4 * HG_WIDTH + POOL_WIDTH
ACT_WIDTH = 4 * HG_WIDTH + POOL_WIDTH
ACT_POOL_PART = 4
N_EXPERTS = 8
EPS = 1e-6
LANES = 128
SUBLANES = 8
VMEM_LIMIT = 56 * 1024 * 1024
HGRN_CHUNK = 128


def _params(*sem):
    return pltpu.CompilerParams(dimension_semantics=sem, vmem_limit_bytes=VMEM_LIMIT)


def _rmsnorm(x, g):
    return x * lax.rsqrt(jnp.mean(x * x, axis=-1, keepdims=True) + EPS) * g


def _silu(x):
    return x * jax.nn.sigmoid(x)


def _mixer_in_kernel(layer, x_ref, g_ref, w_ref, lbl_ref, o_ref, wb_ref):
    @pl.when(pl.program_id(0) == 0)
    def _():
        wb_ref[...] = w_ref[0].astype(BF16)

    h = _rmsnorm(x_ref[...], g_ref[...])
    p = jnp.dot(h.astype(BF16), wb_ref[...], preferred_element_type=F32)
    lg = lbl_ref[...]
    e = jnp.exp(lg - jnp.max(lg, axis=0, keepdims=True))
    pr = e / jnp.sum(e, axis=0, keepdims=True)
    cum = pr[0:1]
    for j in range(1, layer + 1):
        cum = cum + pr[j:j + 1]
    lb = cum - pr[0:1]
    w = HG_WIDTH
    q, fx, ix, g, u = p[:, :w], p[:, w:2 * w], p[:, 2 * w:3 * w], p[:, 3 * w:4 * w], p[:, 4 * w:]
    f = lb + (1.0 - lb) * jax.nn.sigmoid(fx)
    o_ref[:, 0:w] = _silu(q)
    o_ref[:, w:2 * w] = jnp.log(f)
    o_ref[:, 2 * w:3 * w] = ix
    o_ref[:, 3 * w:4 * w] = _silu(g)
    o_ref[:, 4 * w:] = u


def _mixer_in(x, g, w_in, lb_logits, layer, tm):
    t = x.shape[0]
    return pl.pallas_call(
        functools.partial(_mixer_in_kernel, layer),
        grid=(t // tm,),
        in_specs=[
            pl.BlockSpec((tm, D_MODEL), lambda i: (i, 0)),
            pl.BlockSpec((1, D_MODEL), lambda i: (0, 0)),
            pl.BlockSpec((1, D_MODEL, IN_WIDTH), lambda i: (layer, 0, 0)),
            pl.BlockSpec((DEPTH, HG_WIDTH), lambda i: (0, 0)),
        ],
        out_specs=pl.BlockSpec((tm, ACT_WIDTH), lambda i: (i, 0)),
        out_shape=jax.ShapeDtypeStruct((t, ACT_WIDTH), F32),
        scratch_shapes=[pltpu.VMEM((D_MODEL, IN_WIDTH), BF16)],
        compiler_params=_params("arbitrary"),
        name="mixer_in",
    )(x, g, w_in, lb_logits)


SMALL_LEVELS = (2, 4, 8)


def _level_sum_matrix(n, ti, si):
    half = n // 2
    mid = (ti & (-n)) + (half - 1)
    lo = jnp.where((ti & half) != 0, mid, ti)
    hi = jnp.where((ti & half) != 0, ti, mid)
    return (si > lo) & (si <= hi)


def _level_factors(q, k, b, n, row, arg=None):
    c = b.shape[0]
    half = n // 2
    if arg is None:
        zero = jnp.zeros((half, HG_D), F32)
        qs, ks = [], []
        for r0 in range(0, c, n):
            mid = b[r0 + half - 1:r0 + half, :]
            ks += [k[r0:r0 + half] * jnp.exp(mid - b[r0:r0 + half]), zero]
            qs += [zero, q[r0 + half:r0 + n] * jnp.exp(b[r0 + half:r0 + n] - mid)]
        return jnp.concatenate(qs, axis=0), jnp.concatenate(ks, axis=0)
    right = (row & half) != 0
    qk = jnp.where(right, q, k) * jnp.exp(arg)
    return jnp.where(right, qk, 0.0), jnp.where(right, 0.0, qk)


def _split3(x):
    hi = x.astype(BF16)
    r1 = x - hi.astype(F32)
    mid = r1.astype(BF16)
    lo = (r1 - mid.astype(F32)).astype(BF16)
    return jnp.concatenate([hi, mid, lo], axis=0)


def _hgrn_prompt_kernel(tl, q_ref, lf_ref, v_ref, g_ref, nrm_ref, o_ref, s_ref, st_ref):
    c = HGRN_CHUNK
    li = pl.program_id(1)

    @pl.when(li == 0)
    def _():
        st_ref[...] = jnp.zeros_like(st_ref)

    row = lax.broadcasted_iota(jnp.int32, (c, HG_D), 0)
    ti = lax.broadcasted_iota(jnp.int32, (c, c), 0)
    si = lax.broadcasted_iota(jnp.int32, (c, c), 1)
    sums = jnp.concatenate(
        [jnp.where(m, 1.0, 0.0)
         for m in [si <= ti] + [_level_sum_matrix(n, ti, si) for n in SMALL_LEVELS]], axis=0).astype(BF16)
    sums = jnp.concatenate([sums] * 3, axis=1)
    apart = ti ^ si
    nt = (((1,), (1,)), ((), ()))
    tn = (((0,), (0,)), ((), ()))

    def chunk(ci, carry):
        rows = pl.ds(pl.multiple_of(ci * c, c), c)
        for hd in range(HG_HEADS):
            cols = slice(hd * HG_D, (hd + 1) * HG_D)
            q, lf, v = q_ref[rows, cols], lf_ref[rows, cols], v_ref[rows, cols]
            k = 1.0 - jnp.exp(lf)
            s1 = jnp.dot(sums, _split3(lf), preferred_element_type=F32)
            b = s1[:c]
            args = {n: s1[(i + 1) * c:(i + 2) * c] for i, n in enumerate(SMALL_LEVELS)}
            vb = v.astype(BF16)
            scores = None
            n = c
            while n >= 2:
                qt, kt = _level_factors(q, k, b, n, row, args.get(n))
                s_n = lax.dot_general(qt.astype(BF16), kt.astype(BF16), nt, preferred_element_type=F32)
                scores = s_n if scores is None else jnp.where(apart < n, s_n, scores)
                n //= 2
            st = st_ref[hd]
            qs = (q * jnp.exp(b)).astype(BF16)
            o = jnp.dot(scores.astype(BF16), vb, preferred_element_type=F32)
            o = o + lax.dot_general(qs, st.astype(BF16), nt, preferred_element_type=F32)
            o = o + jnp.sum(q * k, axis=-1, keepdims=True) * v
            bc = b[c - 1:c, :]
            kd = (k * jnp.exp(bc - b)).astype(BF16)
            st_ref[hd] = st * jnp.exp(bc) + lax.dot_general(vb, kd, tn, preferred_element_type=F32)
            o = o * lax.rsqrt(jnp.mean(o * o, axis=-1, keepdims=True) + EPS) * nrm_ref[:, cols]
            o_ref[rows, cols] = (o * g_ref[rows, cols]).astype(o_ref.dtype)
        return carry

    lax.fori_loop(0, tl // c, chunk, 0)

    @pl.when(li == pl.num_programs(1) - 1)
    def _():
        for hd in range(HG_HEADS):
            s_ref[0, hd] = st_ref[hd].T


def _hgrn_prompt(act, hg_norm, batch, seq, tl=1024):
    nl = seq // tl
    blk = lambda part: pl.BlockSpec((tl, HG_WIDTH), lambda b, i: (b * nl + i, part))
    return pl.pallas_call(
        functools.partial(_hgrn_prompt_kernel, tl),
        grid=(batch, nl),
        in_specs=[blk(0), blk(1), blk(2), blk(3),
                  pl.BlockSpec((1, HG_WIDTH), lambda b, i: (0, 0))],
        out_specs=[pl.BlockSpec((tl, HG_WIDTH), lambda b, i: (b * nl + i, 0)),
                   pl.BlockSpec((1, HG_HEADS, HG_D, HG_D), lambda b, i: (b, 0, 0, 0))],
        out_shape=[jax.ShapeDtypeStruct((batch * seq, HG_WIDTH), BF16),
                   jax.ShapeDtypeStruct((batch, HG_HEADS, HG_D, HG_D), F32)],
        scratch_shapes=[pltpu.VMEM((HG_HEADS, HG_D, HG_D), F32)],
        compiler_params=_params("parallel", "arbitrary"),
        name="hgrn_prompt",
    )(act, act, act, act, hg_norm)


HGRN_STEP_BT = 16


def _hgrn_step_kernel(q_ref, lf_ref, v_ref, g_ref, nrm_ref, s_ref, o_ref, so_ref):
    bt = HGRN_STEP_BT
    pad = jnp.zeros((HG_D - bt, HG_D), F32)
    col = lambda a: jnp.concatenate([a, pad], axis=0).T
    q = q_ref[...]
    v = v_ref[...]
    f = jnp.exp(lf_ref[...])
    ft, kt, qt = col(f), col(1.0 - f), col(q)
    rows = []
    for s in range(bt):
        sn = ft[:, s:s + 1] * s_ref[0, s, 0] + kt[:, s:s + 1] * v[s:s + 1, :]
        so_ref[s, 0] = sn
        rows.append(jnp.sum(sn * qt[:, s:s + 1], axis=0, keepdims=True))
    o = jnp.concatenate(rows, axis=0)
    o = o * lax.rsqrt(jnp.mean(o * o, axis=-1, keepdims=True) + EPS) * nrm_ref[...]
    o_ref[...] = (o * g_ref[...]).astype(o_ref.dtype)


def _hgrn_step(act, hg_norm, states, layer):
    batch = act.shape[0]
    bt = HGRN_STEP_BT
    blk = lambda part: pl.BlockSpec((bt, HG_D), lambda i, h: (i, part * HG_HEADS + h))
    return pl.pallas_call(
        _hgrn_step_kernel,
        grid=(batch // bt, HG_HEADS),
        in_specs=[blk(0), blk(1), blk(2), blk(3),
                  pl.BlockSpec((1, HG_D), lambda i, h: (0, h)),
                  pl.BlockSpec((1, bt, 1, HG_D, HG_D), lambda i, h: (layer, i, h, 0, 0))],
        out_specs=[pl.BlockSpec((bt, HG_D), lambda i, h: (i, h)),
                   pl.BlockSpec((bt, 1, HG_D, HG_D), lambda i, h: (i, h, 0, 0))],
        out_shape=[jax.ShapeDtypeStruct((batch, HG_WIDTH), BF16),
                   jax.ShapeDtypeStruct(states.shape[1:], F32)],
        compiler_params=_params("parallel", "parallel"),
        name="hgrn_step",
    )(act, act, act, act, hg_norm, states)


def _pool_map(d_groups, wp_ref, scale_ref, o_ref):
    for gi, d in enumerate(d_groups):
        sl = slice(gi * POOL_GROUP_W, (gi + 1) * POOL_GROUP_W)
        y = jnp.dot(d.astype(BF16), wp_ref[gi], preferred_element_type=F32)
        o_ref[:, sl] = (y * scale_ref[:, sl]).astype(o_ref.dtype)


POOL_HIST = 16


def _pool_prompt_kernel(tl, u_ref, wp_ref, scale_ref, o_ref, nb_ref, carry_ref):
    i = pl.program_id(1)

    @pl.when(i == 0)
    def _():
        carry_ref[...] = jnp.zeros_like(carry_ref)

    z = u_ref[...]
    ext = jnp.concatenate([carry_ref[...], z], axis=0)
    sums = {1: ext}
    w = 1
    while w < max(POOL_WINDOWS):
        sums[2 * w] = sums[w] + pltpu.roll(sums[w], w, 0)
        w *= 2
    pos = i * tl + lax.broadcasted_iota(jnp.int32, (tl, POOL_GROUP_W), 0)
    ds = []
    for gi, w in enumerate(POOL_WINDOWS):
        sl = slice(gi * POOL_GROUP_W, (gi + 1) * POOL_GROUP_W)
        cnt = jnp.minimum(pos + 1, w).astype(F32)
        ds.append(sums[w][POOL_HIST:, sl] / cnt - z[:, sl])
    _pool_map(ds, wp_ref, scale_ref, o_ref)
    carry_ref[...] = z[tl - POOL_HIST:, :]

    @pl.when(i == pl.num_programs(1) - 1)
    def _():
        nb_ref[0] = z[tl - POOL_BUF:, :]


def _pool_prompt(act, pool_w, pool_scale, batch, seq, tl=512):
    nl = seq // tl
    return pl.pallas_call(
        functools.partial(_pool_prompt_kernel, tl),
        grid=(batch, nl),
        in_specs=[pl.BlockSpec((tl, POOL_WIDTH), lambda b, i: (b * nl + i, ACT_POOL_PART)),
                  pl.BlockSpec((len(POOL_WINDOWS), POOL_GROUP_W, POOL_GROUP_W), lambda b, i: (0, 0, 0)),
                  pl.BlockSpec((1, POOL_WIDTH), lambda b, i: (0, 0))],
        out_specs=[pl.BlockSpec((tl, POOL_WIDTH), lambda b, i: (b * nl + i, 0)),
                   pl.BlockSpec((1, POOL_BUF, POOL_WIDTH), lambda b, i: (b, 0, 0))],
        out_shape=[jax.ShapeDtypeStruct((batch * seq, POOL_WIDTH), BF16),
                   jax.ShapeDtypeStruct((batch, POOL_BUF, POOL_WIDTH), F32)],
        scratch_shapes=[pltpu.VMEM((POOL_HIST, POOL_WIDTH), F32)],
        compiler_params=_params("parallel", "arbitrary"),
        name="pool_prompt",
    )(act, pool_w, pool_scale)


POOL_STEP_BT = 16


def _pool_step_kernel(start_pos, u_ref, buf_ref, wp_ref, scale_ref, o_ref, nb_ref):
    u = u_ref[...]
    buf = buf_ref[0]
    ds = []
    for gi, w in enumerate(POOL_WINDOWS):
        sl = slice(gi * POOL_GROUP_W, (gi + 1) * POOL_GROUP_W)
        tot = u[:, sl] + jnp.sum(buf[:, POOL_BUF - (w - 1):, sl], axis=1)
        ds.append(tot / float(min(start_pos + 1, w)) - u[:, sl])
    _pool_map(ds, wp_ref, scale_ref, o_ref)
    nb_ref[:, 0:POOL_BUF - 1, :] = buf[:, 1:, :]
    nb_ref[:, POOL_BUF - 1:, :] = u[:, None, :]


def _pool_step(act, bufs, layer, pool_w, pool_scale, start_pos):
    batch = act.shape[0]
    bt = POOL_STEP_BT
    return pl.pallas_call(
        functools.partial(_pool_step_kernel, start_pos),
        grid=(batch // bt,),
        in_specs=[pl.BlockSpec((bt, POOL_WIDTH), lambda i: (i, ACT_POOL_PART)),
                  pl.BlockSpec((1, bt, POOL_BUF, POOL_WIDTH), lambda i: (layer, i, 0, 0)),
                  pl.BlockSpec((len(POOL_WINDOWS), POOL_GROUP_W, POOL_GROUP_W), lambda i: (0, 0, 0)),
                  pl.BlockSpec((1, POOL_WIDTH), lambda i: (0, 0))],
        out_specs=[pl.BlockSpec((bt, POOL_WIDTH), lambda i: (i, 0)),
                   pl.BlockSpec((bt, POOL_BUF, POOL_WIDTH), lambda i: (i, 0, 0))],
        out_shape=[jax.ShapeDtypeStruct((batch, POOL_WIDTH), BF16),
                   jax.ShapeDtypeStruct(bufs.shape[1:], F32)],
        compiler_params=_params("parallel"),
        name="pool_step",
    )(act, bufs, pool_w, pool_scale)


def _mixer_out_kernel(x_ref, o_ref, p_ref, w_ref, y_ref, wb_ref):
    @pl.when(pl.program_id(0) == 0)
    def _():
        wb_ref[...] = w_ref[0].astype(BF16)

    y = jnp.dot(o_ref[...], wb_ref[0:HG_WIDTH, :], preferred_element_type=F32)
    y = y + jnp.dot(p_ref[...], wb_ref[HG_WIDTH:, :], preferred_element_type=F32)
    y_ref[...] = x_ref[...] + y


def _mixer_out(x, og, yp, w_o, layer, tm):
    t = x.shape[0]
    return pl.pallas_call(
        _mixer_out_kernel,
        grid=(t // tm,),
        in_specs=[pl.BlockSpec((tm, D_MODEL), lambda i: (i, 0)),
                  pl.BlockSpec((tm, HG_WIDTH), lambda i: (i, 0)),
                  pl.BlockSpec((tm, POOL_WIDTH), lambda i: (i, 0)),
                  pl.BlockSpec((1, D_MODEL, D_MODEL), lambda i: (layer, 0, 0))],
        out_specs=pl.BlockSpec((tm, D_MODEL), lambda i: (i, 0)),
        out_shape=jax.ShapeDtypeStruct((t, D_MODEL), F32),
        scratch_shapes=[pltpu.VMEM((D_MODEL, D_MODEL), BF16)],
        compiler_params=_params("arbitrary"),
        name="mixer_out",
    )(x, og, yp, w_o)


def _top2_combine(logits):
    lane = lax.broadcasted_iota(jnp.int32, logits.shape, 1).astype(F32)
    neg = jnp.float32(-jnp.inf)
    lg = jnp.where(lane < N_EXPERTS, logits, neg)
    m1 = jnp.max(lg, axis=-1, keepdims=True)
    i1 = jnp.min(jnp.where(lg == m1, lane, float(LANES)), axis=-1, keepdims=True)
    lg2 = jnp.where(lane == i1, neg, lg)
    m2 = jnp.max(lg2, axis=-1, keepdims=True)
    i2 = jnp.min(jnp.where(lg2 == m2, lane, float(LANES)), axis=-1, keepdims=True)
    e2 = jnp.exp(m2 - m1)
    g1 = 1.0 / (1.0 + e2)
    g2 = e2 / (1.0 + e2)
    return jnp.where(lane == i1, g1, 0.0) + jnp.where(lane == i2, g2, 0.0)


def _ffn_kernel(moe, final, x_ref, g_ref, *rest):
    if moe:
        r_ref, rest = rest[0], rest[1:]
    if final:
        gf_ref, rest = rest[0], rest[1:]
    wg_ref, wu_ref, wd_ref, y_ref, h_ref, acc_ref = rest[:6]
    e, j = pl.program_id(1), pl.program_id(2)
    first = (e == 0) & (j == 0)
    last = (e == pl.num_programs(1) - 1) & (j == pl.num_programs(2) - 1)

    @pl.when(first)
    def _():
        h = _rmsnorm(x_ref[...], g_ref[...])
        h_ref[...] = h.astype(BF16)
        acc_ref[...] = jnp.zeros_like(acc_ref)
        if moe:
            logits = jnp.dot(h, r_ref[...], precision=lax.Precision.HIGHEST, preferred_element_type=F32)
            rest[6][...] = _top2_combine(logits)

    h = h_ref[...]
    gate = jnp.dot(h, wg_ref[0], preferred_element_type=F32)
    up = jnp.dot(h, wu_ref[0], preferred_element_type=F32)
    y = jnp.dot((_silu(gate) * up).astype(BF16), wd_ref[0], preferred_element_type=F32)
    if moe:
        comb = rest[6][...]
        lane = lax.broadcasted_iota(jnp.int32, comb.shape, 1)
        y = jnp.sum(jnp.where(lane == e, comb, 0.0), axis=-1, keepdims=True) * y
    acc_ref[...] += y

    @pl.when(last)
    def _():
        out = x_ref[...] + acc_ref[...]
        if final:
            out = _rmsnorm(out, gf_ref[...])
        y_ref[...] = out


def _ffn(x, g, wg, wu, wd, tm, tf, router=None, final_g=None):
    t = x.shape[0]
    ne, _, f = wg.shape
    moe, final = router is not None, final_g is not None
    vec = pl.BlockSpec((1, D_MODEL), lambda i, e, j: (0, 0))
    in_specs = [pl.BlockSpec((tm, D_MODEL), lambda i, e, j: (i, 0)), vec]
    args = [x, g]
    scratch = [pltpu.VMEM((tm, D_MODEL), BF16), pltpu.VMEM((tm, D_MODEL), F32)]
    if moe:
        in_specs.append(pl.BlockSpec((D_MODEL, LANES), lambda i, e, j: (0, 0)))
        args.append(router)
        scratch.append(pltpu.VMEM((tm, LANES), F32))
    if final:
        in_specs.append(vec)
        args.append(final_g)
    in_specs += [pl.BlockSpec((1, D_MODEL, tf), lambda i, e, j: (e, 0, j)),
                 pl.BlockSpec((1, D_MODEL, tf), lambda i, e, j: (e, 0, j)),
                 pl.BlockSpec((1, tf, D_MODEL), lambda i, e, j: (e, j, 0))]
    args += [wg, wu, wd]
    return pl.pallas_call(
        functools.partial(_ffn_kernel, moe, final),
        grid=(t // tm, ne, f // tf),
        in_specs=in_specs,
        out_specs=pl.BlockSpec((tm, D_MODEL), lambda i, e, j: (i, 0)),
        out_shape=jax.ShapeDtypeStruct((t, D_MODEL), F32),
        scratch_shapes=scratch,
        compiler_params=_params("parallel", "arbitrary", "arbitrary"),
        name="ffn_moe" if moe else "ffn_dense",
    )(*args)


MOE_TILE = 512
ROUTE_ROWS = 8
NT_DIMS = (((1,), (1,)), ((), ()))


def _moe_route_kernel(x_ref, g_ref, rt_ref, route_ref, gate_ref, cnt_ref, carry_ref, earlier_ref):
    tm = x_ref.shape[0]

    @pl.when(pl.program_id(0) == 0)
    def _():
        carry_ref[...] = jnp.zeros_like(carry_ref)
        t0 = lax.broadcasted_iota(jnp.int32, (tm, tm), 0)
        t1 = lax.broadcasted_iota(jnp.int32, (tm, tm), 1)
        earlier_ref[...] = jnp.where(t0 < t1, 1.0, 0.0).astype(BF16)

    h = _rmsnorm(x_ref[...], g_ref[...])
    lt = lax.dot_general(rt_ref[...], h, NT_DIMS, precision=lax.Precision.HIGHEST,
                         preferred_element_type=F32)
    ex = lax.broadcasted_iota(jnp.int32, lt.shape, 0).astype(F32)
    neg = jnp.float32(-jnp.inf)
    m1 = jnp.max(lt, axis=0, keepdims=True)
    i1 = jnp.min(jnp.where(lt == m1, ex, float(N_EXPERTS)), axis=0, keepdims=True)
    l2 = jnp.where(ex == i1, neg, lt)
    m2 = jnp.max(l2, axis=0, keepdims=True)
    i2 = jnp.min(jnp.where(l2 == m2, ex, float(N_EXPERTS)), axis=0, keepdims=True)
    e2 = jnp.exp(m2 - m1)
    g1 = 1.0 / (1.0 + e2)
    g2 = e2 / (1.0 + e2)
    sel1, sel2 = ex == i1, ex == i2
    member = jnp.where(sel1 | sel2, 1.0, 0.0)
    rank = jnp.dot(member.astype(BF16), earlier_ref[...], preferred_element_type=F32) + carry_ref[:, 0:1]
    route_ref[...] = jnp.zeros_like(route_ref)
    route_ref[0:1, :] = i1.astype(jnp.int32)
    route_ref[1:2, :] = jnp.sum(jnp.where(sel1, rank, 0.0), axis=0, keepdims=True).astype(jnp.int32)
    route_ref[2:3, :] = i2.astype(jnp.int32)
    route_ref[3:4, :] = jnp.sum(jnp.where(sel2, rank, 0.0), axis=0, keepdims=True).astype(jnp.int32)
    carry_ref[...] += jnp.sum(member, axis=1, keepdims=True)
    cnt_ref[...] = carry_ref[...]
    row = lax.broadcasted_iota(jnp.int32, (LANES, tm), 0)
    gate_ref[...] = jnp.where(row == 0, g1, jnp.where(row == 1, g2, 0.0)).T


def _moe_route(x, g, router_t, tm=512):
    t = x.shape[0]
    return pl.pallas_call(
        _moe_route_kernel,
        grid=(t // tm,),
        in_specs=[pl.BlockSpec((tm, D_MODEL), lambda i: (i, 0)),
                  pl.BlockSpec((1, D_MODEL), lambda i: (0, 0)),
                  pl.BlockSpec((N_EXPERTS, D_MODEL), lambda i: (0, 0))],
        out_specs=[pl.BlockSpec((ROUTE_ROWS, tm), lambda i: (0, i)),
                   pl.BlockSpec((tm, LANES), lambda i: (i, 0)),
                   pl.BlockSpec((N_EXPERTS, LANES), lambda i: (0, 0))],
        out_shape=[jax.ShapeDtypeStruct((ROUTE_ROWS, t), jnp.int32),
                   jax.ShapeDtypeStruct((t, LANES), F32),
                   jax.ShapeDtypeStruct((N_EXPERTS, LANES), F32)],
        scratch_shapes=[pltpu.VMEM((N_EXPERTS, LANES), F32), pltpu.VMEM((tm, tm), BF16)],
        compiler_params=_params("arbitrary"),
        name="moe_route",
    )(x, g, router_t)


def _moe_slots_kernel(base_ref, route_ref, slot_ref):
    r = route_ref[...]
    slot_ref[...] = jnp.zeros_like(slot_ref)
    for k in range(2):
        e, rank = r[2 * k:2 * k + 1, :], r[2 * k + 1:2 * k + 2, :]
        start = jnp.zeros_like(e)
        for j in range(N_EXPERTS):
            start = jnp.where(e == j, base_ref[j], start)
        slot_ref[k:k + 1, :] = start + rank


def _moe_slots(route, base, tm=2048):
    t = route.shape[1]
    return pl.pallas_call(
        _moe_slots_kernel,
        grid_spec=pltpu.PrefetchScalarGridSpec(
            num_scalar_prefetch=1,
            grid=(t // tm,),
            in_specs=[pl.BlockSpec((ROUTE_ROWS, tm), lambda i, *_: (0, i))],
            out_specs=pl.BlockSpec((ROUTE_ROWS, tm), lambda i, *_: (0, i))),
        out_shape=jax.ShapeDtypeStruct((ROUTE_ROWS, t), jnp.int32),
        compiler_params=_params("parallel"),
        name="moe_slots",
    )(base, route)


DMA_UNROLL = 8
COMBINE_ROWS = 128


def _moe_dispatch_kernel(tm, fs_ref, fe_ref, slot_ref, x_ref, xs_hbm, sem):
    i = pl.program_id(0)

    def row_copy(src_row, dst_row):
        return pltpu.make_async_copy(x_ref.at[pl.ds(src_row, 1)], xs_hbm.at[pl.ds(dst_row, 1)], sem)

    def issue(t, c):
        for k in range(2):
            row_copy(t, slot_ref[k, t]).start()
        return c

    lax.fori_loop(0, tm, issue, 0, unroll=DMA_UNROLL)
    for k in range(2):
        pltpu.make_async_copy(x_ref, xs_hbm.at[pl.ds(0, tm)], sem).wait()

    @pl.when(i == pl.num_programs(0) - 1)
    def _():
        for e in range(N_EXPERTS + 1):
            def fill(p, c):
                row_copy(0, p).start()
                return c

            def drain(p, c):
                row_copy(0, p).wait()
                return c

            lax.fori_loop(fs_ref[e], fe_ref[e], fill, 0)
            lax.fori_loop(fs_ref[e], fe_ref[e], drain, 0)


def _moe_dispatch(x, slots, fill_start, fill_end, n_slots, tm=1024):
    t = x.shape[0]
    return pl.pallas_call(
        functools.partial(_moe_dispatch_kernel, tm),
        grid_spec=pltpu.PrefetchScalarGridSpec(
            num_scalar_prefetch=2,
            grid=(t // tm,),
            in_specs=[pl.BlockSpec((ROUTE_ROWS, tm), lambda i, *_: (0, i), memory_space=pltpu.SMEM),
                      pl.BlockSpec((tm, D_MODEL), lambda i, *_: (i, 0))],
            out_specs=pl.BlockSpec(memory_space=pl.ANY),
            scratch_shapes=[pltpu.SemaphoreType.DMA(())]),
        out_shape=jax.ShapeDtypeStruct((n_slots, D_MODEL), F32),
        compiler_params=_params("arbitrary"),
        name="moe_dispatch",
    )(fill_start, fill_end, slots, x)


def _ffn_grouped_kernel(te_ref, nv_ref, x_ref, g_ref, wg_ref, wu_ref, wd_ref, y_ref):
    i = pl.program_id(0)

    @pl.when(i < nv_ref[0])
    def _():
        h = _rmsnorm(x_ref[...], g_ref[...]).astype(BF16)
        gate = jnp.dot(h, wg_ref[0], preferred_element_type=F32)
        up = jnp.dot(h, wu_ref[0], preferred_element_type=F32)
        y_ref[...] = jnp.dot((_silu(gate) * up).astype(BF16), wd_ref[0], preferred_element_type=F32)

    @pl.when(i >= nv_ref[0])
    def _():
        y_ref[...] = jnp.zeros_like(y_ref)


def _ffn_grouped(xs, g, wg, wu, wd, tile_expert, n_valid):
    n_slots = xs.shape[0]
    f = wg.shape[2]
    rows = pl.BlockSpec((MOE_TILE, D_MODEL), lambda i, te, nv: (i, 0))
    return pl.pallas_call(
        _ffn_grouped_kernel,
        grid_spec=pltpu.PrefetchScalarGridSpec(
            num_scalar_prefetch=2,
            grid=(n_slots // MOE_TILE,),
            in_specs=[rows,
                      pl.BlockSpec((1, D_MODEL), lambda i, te, nv: (0, 0)),
                      pl.BlockSpec((1, D_MODEL, f), lambda i, te, nv: (te[i], 0, 0)),
                      pl.BlockSpec((1, D_MODEL, f), lambda i, te, nv: (te[i], 0, 0)),
                      pl.BlockSpec((1, f, D_MODEL), lambda i, te, nv: (te[i], 0, 0))],
            out_specs=rows),
        out_shape=jax.ShapeDtypeStruct((n_slots, D_MODEL), F32),
        compiler_params=_params("arbitrary"),
        name="ffn_grouped",
    )(tile_expert, n_valid, xs, g, wg, wu, wd)


def _moe_combine_kernel(tm, final, slot_ref, x_ref, gate_ref, *rest):
    if final:
        gf_ref, rest = rest[0], rest[1:]
    ys_hbm, o_ref, y1_ref, y2_ref, sems = rest
    bufs = (y1_ref, y2_ref)

    def issue(t, c):
        for k in range(2):
            pltpu.make_async_copy(ys_hbm.at[pl.ds(slot_ref[k, t], 1)], bufs[k].at[pl.ds(t, 1)],
                                  sems.at[k]).start()
        return c

    lax.fori_loop(0, tm, issue, 0, unroll=DMA_UNROLL)
    for k in range(2):
        pltpu.make_async_copy(ys_hbm.at[pl.ds(0, tm)], bufs[k], sems.at[k]).wait()
    rc = COMBINE_ROWS

    def rows(ci, c):
        r = pl.ds(pl.multiple_of(ci * rc, rc), rc)
        g1 = jnp.broadcast_to(gate_ref[r, 0:1], (rc, LANES))
        g2 = jnp.broadcast_to(gate_ref[r, 1:2], (rc, LANES))
        ssq = jnp.zeros((rc, LANES), F32)
        for j in range(D_MODEL // LANES):
            cols = slice(j * LANES, (j + 1) * LANES)
            out = x_ref[r, cols] + g1 * y1_ref[r, cols] + g2 * y2_ref[r, cols]
            o_ref[r, cols] = out
            ssq = ssq + out * out
        if final:
            scale = lax.rsqrt(jnp.sum(ssq, axis=-1, keepdims=True) * (1.0 / D_MODEL) + EPS)
            for j in range(D_MODEL // LANES):
                cols = slice(j * LANES, (j + 1) * LANES)
                o_ref[r, cols] = o_ref[r, cols] * scale * gf_ref[:, cols]
        return c

    lax.fori_loop(0, tm // rc, rows, 0)


def _moe_combine(x, slots, gates, ys, final_g=None, tm=512):
    t = x.shape[0]
    final = final_g is not None
    in_specs = [pl.BlockSpec((ROUTE_ROWS, tm), lambda i: (0, i), memory_space=pltpu.SMEM),
                pl.BlockSpec((tm, D_MODEL), lambda i: (i, 0)),
                pl.BlockSpec((tm, LANES), lambda i: (i, 0))]
    args = [slots, x, gates]
    if final:
        in_specs.append(pl.BlockSpec((1, D_MODEL), lambda i: (0, 0)))
        args.append(final_g)
    in_specs.append(pl.BlockSpec(memory_space=pl.ANY))
    args.append(ys)
    return pl.pallas_call(
        functools.partial(_moe_combine_kernel, tm, final),
        grid=(t // tm,),
        in_specs=in_specs,
        out_specs=pl.BlockSpec((tm, D_MODEL), lambda i: (i, 0)),
        out_shape=jax.ShapeDtypeStruct((t, D_MODEL), F32),
        scratch_shapes=[pltpu.VMEM((tm, D_MODEL), F32), pltpu.VMEM((tm, D_MODEL), F32),
                        pltpu.SemaphoreType.DMA((2,))],
        compiler_params=_params("arbitrary"),
        name="moe_combine",
    )(*args)


def _moe_routed(x, g, router_t, wg, wu, wd, final_g=None):
    t = x.shape[0]
    n_tiles = 2 * t // MOE_TILE + N_EXPERTS
    route, gates, counts = _moe_route(x, g, router_t)
    cnt = counts[:, 0].astype(jnp.int32)
    caps = (cnt + MOE_TILE - 1) // MOE_TILE
    cum = jnp.cumsum(caps)
    base = (cum - caps) * MOE_TILE
    n_valid = cum[-1:]
    tile_expert = jnp.minimum(
        jnp.sum((cum[None, :] <= jnp.arange(n_tiles, dtype=jnp.int32)[:, None]).astype(jnp.int32), axis=1),
        N_EXPERTS - 1)
    fill_start = jnp.concatenate([base + cnt, n_valid * MOE_TILE])
    fill_end = jnp.concatenate([base + caps * MOE_TILE, jnp.full((1,), n_tiles * MOE_TILE, jnp.int32)])
    slots = _moe_slots(route, base)
    xs = _moe_dispatch(x, slots, fill_start, fill_end, n_tiles * MOE_TILE)
    ys = _ffn_grouped(xs, g, wg, wu, wd, tile_expert, n_valid)
    return _moe_combine(x, slots, gates, ys, final_g)


def _trunk(x, state_hgrn, state_pool, start_pos, w, seq, tm):
    batch = x.shape[0] // seq
    s_out, b_out = [], []
    for l in range(DEPTH):
        act = _mixer_in(x, w["norm_mix"][l], w["w_in"], w["lb_logits"], l, tm)
        if seq > 1:
            og, s_new = _hgrn_prompt(act, w["hg_norm"][l], batch, seq)
            yp, b_new = _pool_prompt(act, w["pool_w"][l], w["pool_scale"][l], batch, seq)
        else:
            og, s_new = _hgrn_step(act, w["hg_norm"][l], state_hgrn, l)
            yp, b_new = _pool_step(act, state_pool, l, w["pool_w"][l], w["pool_scale"][l], start_pos)
        s_out.append(s_new)
        b_out.append(b_new)
        routed = l % 2 == 1 and seq > 1
        x = _mixer_out(x, og, yp, w["w_o"], l, tm)
        j = l // 2
        final_g = w["norm_final"] if l == DEPTH - 1 else None
        if l % 2 == 0:
            x = _ffn(x, w["norm_ffn"][l], w["ffn_w_gate"][j:j + 1], w["ffn_w_up"][j:j + 1],
                     w["ffn_w_down"][j:j + 1], tm, 1408, final_g=final_g)
        elif routed:
            x = _moe_routed(x, w["norm_ffn"][l], w["router_t"][j], w["moe_w_gate"][j],
                            w["moe_w_up"][j], w["moe_w_down"][j], final_g=final_g)
        else:
            x = _ffn(x, w["norm_ffn"][l], w["moe_w_gate"][j], w["moe_w_up"][j], w["moe_w_down"][j],
                     tm, 1408, router=w["router"][j], final_g=final_g)
    return x, jnp.stack(s_out), jnp.stack(b_out)


def kernel(x_prompt, x_sample, state_hgrn, state_pool, lb_logits, norm_mix, w_in, w_o, hg_norm, pool_w,
           pool_scale, norm_ffn, ffn_w_gate, ffn_w_up, ffn_w_down, router, moe_w_gate, moe_w_up, moe_w_down,
           norm_final):
    batch, seq, _ = x_prompt.shape
    dec_batch, dec_seq, _ = x_sample.shape
    assert dec_seq == 1
    past_len = 16384
    w = dict(
        lb_logits=lb_logits,
        norm_mix=norm_mix.reshape(DEPTH, 1, D_MODEL),
        w_in=w_in,
        w_o=w_o,
        hg_norm=hg_norm.reshape(DEPTH, 1, HG_WIDTH),
        pool_w=pool_w.astype(BF16),
        pool_scale=pool_scale.reshape(DEPTH, 1, POOL_WIDTH),
        norm_ffn=norm_ffn.reshape(DEPTH, 1, D_MODEL),
        ffn_w_gate=ffn_w_gate.astype(BF16),
        ffn_w_up=ffn_w_up.astype(BF16),
        ffn_w_down=ffn_w_down.astype(BF16),
        router=jnp.pad(router, ((0, 0), (0, 0), (0, LANES - N_EXPERTS))),
        router_t=jnp.swapaxes(router, 1, 2),
        moe_w_gate=moe_w_gate.astype(BF16),
        moe_w_up=moe_w_up.astype(BF16),
        moe_w_down=moe_w_down.astype(BF16),
        norm_final=norm_final.reshape(1, D_MODEL),
    )
    yp, sp, bp = _trunk(x_prompt.reshape(batch * seq, D_MODEL), None, None, 0, w, seq, 512)
    ys, ss, bs = _trunk(x_sample.reshape(dec_batch, D_MODEL), state_hgrn, state_pool, past_len, w, 1, 128)
    return (yp.reshape(batch, seq, D_MODEL), ys.reshape(dec_batch, 1, D_MODEL), sp, ss, bp, bs)
```

```python
import functools

import jax
import jax.numpy as jnp
from jax import lax
from jax.experimental import pallas as pl
from jax.experimental.pallas import tpu as pltpu

F32 = jnp.float32
BF16 = jnp.bfloat16

D_MODEL = 1024
DEPTH = 4
HG_WIDTH = 512
HG_HEADS = 4
HG_D = 128
POOL_WIDTH = 512
POOL_WINDOWS = (2, 4, 8, 16)
POOL_GROUP_W = 128
POOL_BUF = 15
IN_WIDTH = 4 * HG_WIDTH + POOL_WIDTH
ACT_WIDTH = 4 * HG_WIDTH + POOL_WIDTH
ACT_POOL_PART = 4
N_EXPERTS = 8
EPS = 1e-6
LANES = 128
SUBLANES = 8
VMEM_LIMIT = 56 * 1024 * 1024
HGRN_CHUNK = 128


def _params(*sem):
    return pltpu.CompilerParams(dimension_semantics=sem, vmem_limit_bytes=VMEM_LIMIT)


def _rmsnorm(x, g):
    return x * lax.rsqrt(jnp.mean(x * x, axis=-1, keepdims=True) + EPS) * g


def _silu(x):
    return x * jax.nn.sigmoid(x)


def _mixer_in_kernel(layer, x_ref, g_ref, w_ref, lbl_ref, o_ref, wb_ref):
    @pl.when(pl.program_id(0) == 0)
    def _():
        wb_ref[...] = w_ref[0].astype(BF16)

    h = _rmsnorm(x_ref[...], g_ref[...])
    p = jnp.dot(h.astype(BF16), wb_ref[...], preferred_element_type=F32)
    lg = lbl_ref[...]
    e = jnp.exp(lg - jnp.max(lg, axis=0, keepdims=True))
    pr = e / jnp.sum(e, axis=0, keepdims=True)
    cum = pr[0:1]
    for j in range(1, layer + 1):
        cum = cum + pr[j:j + 1]
    lb = cum - pr[0:1]
    w = HG_WIDTH
    q, fx, ix, g, u = p[:, :w], p[:, w:2 * w], p[:, 2 * w:3 * w], p[:, 3 * w:4 * w], p[:, 4 * w:]
    f = lb + (1.0 - lb) * jax.nn.sigmoid(fx)
    o_ref[:, 0:w] = _silu(q)
    o_ref[:, w:2 * w] = jnp.log(f)
    o_ref[:, 2 * w:3 * w] = ix
    o_ref[:, 3 * w:4 * w] = _silu(g)
    o_ref[:, 4 * w:] = u


def _mixer_in(x, g, w_in, lb_logits, layer, tm):
    t = x.shape[0]
    return pl.pallas_call(
        functools.partial(_mixer_in_kernel, layer),
        grid=(t // tm,),
        in_specs=[
            pl.BlockSpec((tm, D_MODEL), lambda i: (i, 0)),
            pl.BlockSpec((1, D_MODEL), lambda i: (0, 0)),
            pl.BlockSpec((1, D_MODEL, IN_WIDTH), lambda i: (layer, 0, 0)),
            pl.BlockSpec((DEPTH, HG_WIDTH), lambda i: (0, 0)),
        ],
        out_specs=pl.BlockSpec((tm, ACT_WIDTH), lambda i: (i, 0)),
        out_shape=jax.ShapeDtypeStruct((t, ACT_WIDTH), F32),
        scratch_shapes=[pltpu.VMEM((D_MODEL, IN_WIDTH), BF16)],
        compiler_params=_params("arbitrary"),
        name="mixer_in",
    )(x, g, w_in, lb_logits)


SMALL_LEVELS = (2, 4, 8)


def _level_sum_matrix(n, ti, si):
    half = n // 2
    mid = (ti & (-n)) + (half - 1)
    lo = jnp.where((ti & half) != 0, mid, ti)
    hi = jnp.where((ti & half) != 0, ti, mid)
    return (si > lo) & (si <= hi)


def _level_factors(q, k, b, n, row, arg=None):
    c = b.shape[0]
    half = n // 2
    if arg is None:
        zero = jnp.zeros((half, b.shape[1]), F32)
        qs, ks = [], []
        for r0 in range(0, c, n):
            mid = b[r0 + half - 1:r0 + half, :]
            ks += [k[r0:r0 + half] * jnp.exp(mid - b[r0:r0 + half]), zero]
            qs += [zero, q[r0 + half:r0 + n] * jnp.exp(b[r0 + half:r0 + n] - mid)]
        return jnp.concatenate(qs, axis=0), jnp.concatenate(ks, axis=0)
    right = (row & half) != 0
    qk = jnp.where(right, q, k) * jnp.exp(arg)
    return jnp.where(right, qk, 0.0), jnp.where(right, 0.0, qk)


def _split3(x):
    hi = x.astype(BF16)
    r1 = x - hi.astype(F32)
    mid = r1.astype(BF16)
    lo = (r1 - mid.astype(F32)).astype(BF16)
    return jnp.concatenate([hi, mid, lo], axis=0)


HEAD_PAIRS = HG_HEADS // 2
PAIR_W = 2 * HG_D


def _pair_blocks(a):
    zero = jnp.zeros((a.shape[0], HG_D), a.dtype)
    return jnp.concatenate([jnp.concatenate([a[:, :HG_D], zero], axis=1),
                            jnp.concatenate([zero, a[:, HG_D:]], axis=1)], axis=0)


def _per_head(fn, a):
    return jnp.concatenate([fn(a[:, :HG_D]), fn(a[:, HG_D:])], axis=1)


def _hgrn_prompt_kernel(tl, q_ref, lf_ref, v_ref, g_ref, nrm_ref, o_ref, s_ref, st_ref):
    c = HGRN_CHUNK
    li = pl.program_id(1)

    @pl.when(li == 0)
    def _():
        st_ref[...] = jnp.zeros_like(st_ref)

    row = lax.broadcasted_iota(jnp.int32, (c, PAIR_W), 0)
    ti = lax.broadcasted_iota(jnp.int32, (c, c), 0)
    si = lax.broadcasted_iota(jnp.int32, (c, c), 1)
    sums = jnp.concatenate(
        [jnp.where(m, 1.0, 0.0)
         for m in [si <= ti] + [_level_sum_matrix(n, ti, si) for n in SMALL_LEVELS]], axis=0).astype(BF16)
    sums = jnp.concatenate([sums] * 3, axis=1)
    apart = jnp.concatenate([ti ^ si] * 2, axis=1)
    pi = lax.broadcasted_iota(jnp.int32, (PAIR_W, PAIR_W), 0)
    pj = lax.broadcasted_iota(jnp.int32, (PAIR_W, PAIR_W), 1)
    own = (pi < HG_D) == (pj < HG_D)
    nt = (((1,), (1,)), ((), ()))
    tn = (((0,), (0,)), ((), ()))

    def chunk(ci, carry):
        rows = pl.ds(pl.multiple_of(ci * c, c), c)
        for p in range(HEAD_PAIRS):
            cols = slice(p * PAIR_W, (p + 1) * PAIR_W)
            q, lf, v = q_ref[rows, cols], lf_ref[rows, cols], v_ref[rows, cols]
            k = 1.0 - jnp.exp(lf)
            s1 = jnp.dot(sums, _split3(lf), preferred_element_type=F32)
            b = s1[:c]
            args = {n: s1[(i + 1) * c:(i + 2) * c] for i, n in enumerate(SMALL_LEVELS)}
            vbd = _pair_blocks(v.astype(BF16))
            scores = None
            n = c
            while n >= 2:
                qt, kt = _level_factors(q, k, b, n, row, args.get(n))
                s_n = lax.dot_general(qt.astype(BF16), _pair_blocks(kt.astype(BF16)), nt,
                                      preferred_element_type=F32)
                scores = s_n if scores is None else jnp.where(apart < n, s_n, scores)
                n //= 2
            st = st_ref[p]
            qs = (q * jnp.exp(b)).astype(BF16)
            o = jnp.dot(scores.astype(BF16), vbd, preferred_element_type=F32)
            o = o + lax.dot_general(qs, st.astype(BF16), nt, preferred_element_type=F32)
            o = o + _per_head(lambda a: jnp.broadcast_to(jnp.sum(a, axis=-1, keepdims=True), a.shape), q * k) * v
            bc = b[c - 1:c, :]
            kd = (k * jnp.exp(bc - b)).astype(BF16)
            upd = lax.dot_general(v.astype(BF16), kd, tn, preferred_element_type=F32)
            st_ref[p] = st * jnp.exp(bc) + jnp.where(own, upd, 0.0)
            ms = _per_head(lambda a: jnp.broadcast_to(jnp.mean(a, axis=-1, keepdims=True), a.shape), o * o)
            o = o * lax.rsqrt(ms + EPS) * nrm_ref[:, cols]
            o_ref[rows, cols] = (o * g_ref[rows, cols]).astype(o_ref.dtype)
        return carry

    lax.fori_loop(0, tl // c, chunk, 0)

    @pl.when(li == pl.num_programs(1) - 1)
    def _():
        for p in range(HEAD_PAIRS):
            st = st_ref[p]
            s_ref[0, 2 * p] = st[:HG_D, :HG_D].T
            s_ref[0, 2 * p + 1] = st[HG_D:, HG_D:].T


def _hgrn_prompt(act, hg_norm, batch, seq, tl=1024):
    nl = seq // tl
    blk = lambda part: pl.BlockSpec((tl, HG_WIDTH), lambda b, i: (b * nl + i, part))
    return pl.pallas_call(
        functools.partial(_hgrn_prompt_kernel, tl),
        grid=(batch, nl),
        in_specs=[blk(0), blk(1), blk(2), blk(3),
                  pl.BlockSpec((1, HG_WIDTH), lambda b, i: (0, 0))],
        out_specs=[pl.BlockSpec((tl, HG_WIDTH), lambda b, i: (b * nl + i, 0)),
                   pl.BlockSpec((1, HG_HEADS, HG_D, HG_D), lambda b, i: (b, 0, 0, 0))],
        out_shape=[jax.ShapeDtypeStruct((batch * seq, HG_WIDTH), BF16),
                   jax.ShapeDtypeStruct((batch, HG_HEADS, HG_D, HG_D), F32)],
        scratch_shapes=[pltpu.VMEM((HEAD_PAIRS, PAIR_W, PAIR_W), F32)],
        compiler_params=_params("parallel", "arbitrary"),
        name="hgrn_prompt",
    )(act, act, act, act, hg_norm)


HGRN_STEP_BT = 16


def _hgrn_step_kernel(q_ref, lf_ref, v_ref, g_ref, nrm_ref, s_ref, o_ref, so_ref):
    bt = HGRN_STEP_BT
    q = q_ref[...]
    v = v_ref[...]
    f = jnp.exp(lf_ref[...])
    ri = lax.broadcasted_iota(jnp.int32, (3 * bt, PAIR_W), 0) % bt
    half = lax.broadcasted_iota(jnp.int32, (3 * bt, PAIR_W), 1) // HG_D
    tn = (((0,), (0,)), ((), ()))
    pieces = [_split3(a) for a in (f, 1.0 - f, q)]
    rows = []
    for s in range(0, bt, 2):
        pick = jnp.where(ri == s + half, 1.0, 0.0).astype(BF16)
        fc, kc, qc = [lax.dot_general(p, pick, tn, preferred_element_type=F32) for p in pieces]
        for d in range(2):
            cols = slice(d * HG_D, (d + 1) * HG_D)
            sn = fc[:, cols] * s_ref[0, s + d, 0] + kc[:, cols] * v[s + d:s + d + 1, :]
            so_ref[s + d, 0] = sn
            rows.append(jnp.sum(sn * qc[:, cols], axis=0, keepdims=True))
    o = jnp.concatenate(rows, axis=0)
    o = o * lax.rsqrt(jnp.mean(o * o, axis=-1, keepdims=True) + EPS) * nrm_ref[...]
    o_ref[...] = (o * g_ref[...]).astype(o_ref.dtype)


def _hgrn_step(act, hg_norm, states, layer):
    batch = act.shape[0]
    bt = HGRN_STEP_BT
    blk = lambda part: pl.BlockSpec((bt, HG_D), lambda i, h: (i, part * HG_HEADS + h))
    return pl.pallas_call(
        _hgrn_step_kernel,
        grid=(batch // bt, HG_HEADS),
        in_specs=[blk(0), blk(1), blk(2), blk(3),
                  pl.BlockSpec((1, HG_D), lambda i, h: (0, h)),
                  pl.BlockSpec((1, bt, 1, HG_D, HG_D), lambda i, h: (layer, i, h, 0, 0))],
        out_specs=[pl.BlockSpec((bt, HG_D), lambda i, h: (i, h)),
                   pl.BlockSpec((bt, 1, HG_D, HG_D), lambda i, h: (i, h, 0, 0))],
        out_shape=[jax.ShapeDtypeStruct((batch, HG_WIDTH), BF16),
                   jax.ShapeDtypeStruct(states.shape[1:], F32)],
        compiler_params=_params("parallel", "parallel"),
        name="hgrn_step",
    )(act, act, act, act, hg_norm, states)


def _pool_map(d_groups, wp_ref, scale_ref, o_ref):
    for gi, d in enumerate(d_groups):
        sl = slice(gi * POOL_GROUP_W, (gi + 1) * POOL_GROUP_W)
        y = jnp.dot(d.astype(BF16), wp_ref[gi], preferred_element_type=F32)
        o_ref[:, sl] = (y * scale_ref[:, sl]).astype(o_ref.dtype)


POOL_HIST = 16


def _pool_tile(i, tl, z, carry_ref, wp_ref, scale_ref, o_ref, nb_ref):
    @pl.when(i == 0)
    def _():
        carry_ref[...] = jnp.zeros_like(carry_ref)

    ext = jnp.concatenate([carry_ref[...], z], axis=0)
    sums = {1: ext}
    w = 1
    while w < max(POOL_WINDOWS):
        sums[2 * w] = sums[w] + pltpu.roll(sums[w], w, 0)
        w *= 2
    pos = i * tl + lax.broadcasted_iota(jnp.int32, (tl, POOL_GROUP_W), 0)
    ds = []
    for gi, w in enumerate(POOL_WINDOWS):
        sl = slice(gi * POOL_GROUP_W, (gi + 1) * POOL_GROUP_W)
        cnt = jnp.minimum(pos + 1, w).astype(F32)
        ds.append(sums[w][POOL_HIST:, sl] / cnt - z[:, sl])
    _pool_map(ds, wp_ref, scale_ref, o_ref)
    carry_ref[...] = z[tl - POOL_HIST:, :]

    @pl.when(i == pl.num_programs(1) - 1)
    def _():
        nb_ref[0] = z[tl - POOL_BUF:, :]


def _mixer_out_pool_kernel(tl, x_ref, o_ref, u_ref, w_ref, wp_ref, scale_ref, y_ref, nb_ref,
                           wb_ref, carry_ref, yp_ref):
    b, i = pl.program_id(0), pl.program_id(1)

    @pl.when((b == 0) & (i == 0))
    def _():
        wb_ref[...] = w_ref[0].astype(BF16)

    _pool_tile(i, tl, u_ref[...], carry_ref, wp_ref, scale_ref, yp_ref, nb_ref)
    y = jnp.dot(o_ref[...], wb_ref[0:HG_WIDTH, :], preferred_element_type=F32)
    y = y + jnp.dot(yp_ref[...], wb_ref[HG_WIDTH:, :], preferred_element_type=F32)
    y_ref[...] = x_ref[...] + y


def _mixer_out_pool(x, og, act, w_o, layer, pool_w, pool_scale, batch, seq, tl=512):
    nl = seq // tl
    tok = lambda cols: (lambda b, i: (b * nl + i, cols))
    return pl.pallas_call(
        functools.partial(_mixer_out_pool_kernel, tl),
        grid=(batch, nl),
        in_specs=[pl.BlockSpec((tl, D_MODEL), tok(0)),
                  pl.BlockSpec((tl, HG_WIDTH), tok(0)),
                  pl.BlockSpec((tl, POOL_WIDTH), tok(ACT_POOL_PART)),
                  pl.BlockSpec((1, D_MODEL, D_MODEL), lambda b, i: (layer, 0, 0)),
                  pl.BlockSpec((len(POOL_WINDOWS), POOL_GROUP_W, POOL_GROUP_W), lambda b, i: (0, 0, 0)),
                  pl.BlockSpec((1, POOL_WIDTH), lambda b, i: (0, 0))],
        out_specs=[pl.BlockSpec((tl, D_MODEL), tok(0)),
                   pl.BlockSpec((1, POOL_BUF, POOL_WIDTH), lambda b, i: (b, 0, 0))],
        out_shape=[jax.ShapeDtypeStruct((batch * seq, D_MODEL), F32),
                   jax.ShapeDtypeStruct((batch, POOL_BUF, POOL_WIDTH), F32)],
        scratch_shapes=[pltpu.VMEM((D_MODEL, D_MODEL), BF16), pltpu.VMEM((POOL_HIST, POOL_WIDTH), F32),
                        pltpu.VMEM((tl, POOL_WIDTH), BF16)],
        compiler_params=_params("arbitrary", "arbitrary"),
        name="mixer_out_pool",
    )(x, og, act, w_o, pool_w, pool_scale)


POOL_STEP_BT = 16


def _pool_step_kernel(start_pos, u_ref, buf_ref, wp_ref, scale_ref, o_ref, nb_ref):
    u = u_ref[...]
    buf = buf_ref[0]
    ds = []
    for gi, w in enumerate(POOL_WINDOWS):
        sl = slice(gi * POOL_GROUP_W, (gi + 1) * POOL_GROUP_W)
        tot = u[:, sl] + jnp.sum(buf[:, POOL_BUF - (w - 1):, sl], axis=1)
        ds.append(tot / float(min(start_pos + 1, w)) - u[:, sl])
    _pool_map(ds, wp_ref, scale_ref, o_ref)
    nb_ref[:, 0:POOL_BUF - 1, :] = buf[:, 1:, :]
    nb_ref[:, POOL_BUF - 1:, :] = u[:, None, :]


def _pool_step(act, bufs, layer, pool_w, pool_scale, start_pos):
    batch = act.shape[0]
    bt = POOL_STEP_BT
    return pl.pallas_call(
        functools.partial(_pool_step_kernel, start_pos),
        grid=(batch // bt,),
        in_specs=[pl.BlockSpec((bt, POOL_WIDTH), lambda i: (i, ACT_POOL_PART)),
                  pl.BlockSpec((1, bt, POOL_BUF, POOL_WIDTH), lambda i: (layer, i, 0, 0)),
                  pl.BlockSpec((len(POOL_WINDOWS), POOL_GROUP_W, POOL_GROUP_W), lambda i: (0, 0, 0)),
                  pl.BlockSpec((1, POOL_WIDTH), lambda i: (0, 0))],
        out_specs=[pl.BlockSpec((bt, POOL_WIDTH), lambda i: (i, 0)),
                   pl.BlockSpec((bt, POOL_BUF, POOL_WIDTH), lambda i: (i, 0, 0))],
        out_shape=[jax.ShapeDtypeStruct((batch, POOL_WIDTH), BF16),
                   jax.ShapeDtypeStruct(bufs.shape[1:], F32)],
        compiler_params=_params("parallel"),
        name="pool_step",
    )(act, bufs, pool_w, pool_scale)


def _mixer_out_kernel(x_ref, o_ref, p_ref, w_ref, y_ref, wb_ref):
    @pl.when(pl.program_id(0) == 0)
    def _():
        wb_ref[...] = w_ref[0].astype(BF16)

    y = jnp.dot(o_ref[...], wb_ref[0:HG_WIDTH, :], preferred_element_type=F32)
    y = y + jnp.dot(p_ref[...], wb_ref[HG_WIDTH:, :], preferred_element_type=F32)
    y_ref[...] = x_ref[...] + y


def _mixer_out(x, og, yp, w_o, layer, tm):
    t = x.shape[0]
    return pl.pallas_call(
        _mixer_out_kernel,
        grid=(t // tm,),
        in_specs=[pl.BlockSpec((tm, D_MODEL), lambda i: (i, 0)),
                  pl.BlockSpec((tm, HG_WIDTH), lambda i: (i, 0)),
                  pl.BlockSpec((tm, POOL_WIDTH), lambda i: (i, 0)),
                  pl.BlockSpec((1, D_MODEL, D_MODEL), lambda i: (layer, 0, 0))],
        out_specs=pl.BlockSpec((tm, D_MODEL), lambda i: (i, 0)),
        out_shape=jax.ShapeDtypeStruct((t, D_MODEL), F32),
        scratch_shapes=[pltpu.VMEM((D_MODEL, D_MODEL), BF16)],
        compiler_params=_params("arbitrary"),
        name="mixer_out",
    )(x, og, yp, w_o)


def _top2_combine(logits):
    lane = lax.broadcasted_iota(jnp.int32, logits.shape, 1).astype(F32)
    neg = jnp.float32(-jnp.inf)
    lg = jnp.where(lane < N_EXPERTS, logits, neg)
    m1 = jnp.max(lg, axis=-1, keepdims=True)
    i1 = jnp.min(jnp.where(lg == m1, lane, float(LANES)), axis=-1, keepdims=True)
    lg2 = jnp.where(lane == i1, neg, lg)
    m2 = jnp.max(lg2, axis=-1, keepdims=True)
    i2 = jnp.min(jnp.where(lg2 == m2, lane, float(LANES)), axis=-1, keepdims=True)
    e2 = jnp.exp(m2 - m1)
    g1 = 1.0 / (1.0 + e2)
    g2 = e2 / (1.0 + e2)
    return jnp.where(lane == i1, g1, 0.0) + jnp.where(lane == i2, g2, 0.0)


def _ffn_kernel(moe, final, x_ref, g_ref, *rest):
    if moe:
        r_ref, rest = rest[0], rest[1:]
    if final:
        gf_ref, rest = rest[0], rest[1:]
    wg_ref, wu_ref, wd_ref, y_ref, h_ref, acc_ref = rest[:6]
    e, j = pl.program_id(1), pl.program_id(2)
    first = (e == 0) & (j == 0)
    last = (e == pl.num_programs(1) - 1) & (j == pl.num_programs(2) - 1)

    @pl.when(first)
    def _():
        h = _rmsnorm(x_ref[...], g_ref[...])
        h_ref[...] = h.astype(BF16)
        acc_ref[...] = jnp.zeros_like(acc_ref)
        if moe:
            logits = jnp.dot(h, r_ref[...], precision=lax.Precision.HIGHEST, preferred_element_type=F32)
            rest[6][...] = _top2_combine(logits)

    h = h_ref[...]
    gate = jnp.dot(h, wg_ref[0], preferred_element_type=F32)
    up = jnp.dot(h, wu_ref[0], preferred_element_type=F32)
    y = jnp.dot((_silu(gate) * up).astype(BF16), wd_ref[0], preferred_element_type=F32)
    if moe:
        comb = rest[6][...]
        lane = lax.broadcasted_iota(jnp.int32, comb.shape, 1)
        y = jnp.sum(jnp.where(lane == e, comb, 0.0), axis=-1, keepdims=True) * y
    acc_ref[...] += y

    @pl.when(last)
    def _():
        out = x_ref[...] + acc_ref[...]
        if final:
            out = _rmsnorm(out, gf_ref[...])
        y_ref[...] = out


def _ffn(x, g, wg, wu, wd, tm, tf, router=None, final_g=None):
    t = x.shape[0]
    ne, _, f = wg.shape
    moe, final = router is not None, final_g is not None
    vec = pl.BlockSpec((1, D_MODEL), lambda i, e, j: (0, 0))
    in_specs = [pl.BlockSpec((tm, D_MODEL), lambda i, e, j: (i, 0)), vec]
    args = [x, g]
    scratch = [pltpu.VMEM((tm, D_MODEL), BF16), pltpu.VMEM((tm, D_MODEL), F32)]
    if moe:
        in_specs.append(pl.BlockSpec((D_MODEL, LANES), lambda i, e, j: (0, 0)))
        args.append(router)
        scratch.append(pltpu.VMEM((tm, LANES), F32))
    if final:
        in_specs.append(vec)
        args.append(final_g)
    in_specs += [pl.BlockSpec((1, D_MODEL, tf), lambda i, e, j: (e, 0, j)),
                 pl.BlockSpec((1, D_MODEL, tf), lambda i, e, j: (e, 0, j)),
                 pl.BlockSpec((1, tf, D_MODEL), lambda i, e, j: (e, j, 0))]
    args += [wg, wu, wd]
    return pl.pallas_call(
        functools.partial(_ffn_kernel, moe, final),
        grid=(t // tm, ne, f // tf),
        in_specs=in_specs,
        out_specs=pl.BlockSpec((tm, D_MODEL), lambda i, e, j: (i, 0)),
        out_shape=jax.ShapeDtypeStruct((t, D_MODEL), F32),
        scratch_shapes=scratch,
        compiler_params=_params("parallel", "arbitrary", "arbitrary"),
        name="ffn_moe" if moe else "ffn_dense",
    )(*args)


MOE_TILE = 512
ROUTE_ROWS = 8
NT_DIMS = (((1,), (1,)), ((), ()))


def _moe_route_kernel(x_ref, g_ref, rt_ref, route_ref, gate_ref, cnt_ref, carry_ref, earlier_ref):
    tm = x_ref.shape[0]

    @pl.when(pl.program_id(0) == 0)
    def _():
        carry_ref[...] = jnp.zeros_like(carry_ref)
        t0 = lax.broadcasted_iota(jnp.int32, (tm, tm), 0)
        t1 = lax.broadcasted_iota(jnp.int32, (tm, tm), 1)
        earlier_ref[...] = jnp.where(t0 < t1, 1.0, 0.0).astype(BF16)

    h = _rmsnorm(x_ref[...], g_ref[...])
    lt = lax.dot_general(rt_ref[...], h, NT_DIMS, precision=lax.Precision.HIGHEST,
                         preferred_element_type=F32)
    ex = lax.broadcasted_iota(jnp.int32, lt.shape, 0).astype(F32)
    neg = jnp.float32(-jnp.inf)
    m1 = jnp.max(lt, axis=0, keepdims=True)
    i1 = jnp.min(jnp.where(lt == m1, ex, float(N_EXPERTS)), axis=0, keepdims=True)
    l2 = jnp.where(ex == i1, neg, lt)
    m2 = jnp.max(l2, axis=0, keepdims=True)
    i2 = jnp.min(jnp.where(l2 == m2, ex, float(N_EXPERTS)), axis=0, keepdims=True)
    e2 = jnp.exp(m2 - m1)
    g1 = 1.0 / (1.0 + e2)
    g2 = e2 / (1.0 + e2)
    sel1, sel2 = ex == i1, ex == i2
    member = jnp.where(sel1 | sel2, 1.0, 0.0)
    rank = jnp.dot(member.astype(BF16), earlier_ref[...], preferred_element_type=F32) + carry_ref[:, 0:1]
    route_ref[...] = jnp.zeros_like(route_ref)
    route_ref[0:1, :] = i1.astype(jnp.int32)
    route_ref[1:2, :] = jnp.sum(jnp.where(sel1, rank, 0.0), axis=0, keepdims=True).astype(jnp.int32)
    route_ref[2:3, :] = i2.astype(jnp.int32)
    route_ref[3:4, :] = jnp.sum(jnp.where(sel2, rank, 0.0), axis=0, keepdims=True).astype(jnp.int32)
    carry_ref[...] += jnp.sum(member, axis=1, keepdims=True)
    cnt_ref[...] = carry_ref[...]
    row = lax.broadcasted_iota(jnp.int32, (LANES, tm), 0)
    gate_ref[...] = jnp.where(row == 0, g1, jnp.where(row == 1, g2, 0.0)).T


def _moe_route(x, g, router_t, tm=512):
    t = x.shape[0]
    return pl.pallas_call(
        _moe_route_kernel,
        grid=(t // tm,),
        in_specs=[pl.BlockSpec((tm, D_MODEL), lambda i: (i, 0)),
                  pl.BlockSpec((1, D_MODEL), lambda i: (0, 0)),
                  pl.BlockSpec((N_EXPERTS, D_MODEL), lambda i: (0, 0))],
        out_specs=[pl.BlockSpec((ROUTE_ROWS, tm), lambda i: (0, i)),
                   pl.BlockSpec((tm, LANES), lambda i: (i, 0)),
                   pl.BlockSpec((N_EXPERTS, LANES), lambda i: (0, 0))],
        out_shape=[jax.ShapeDtypeStruct((ROUTE_ROWS, t), jnp.int32),
                   jax.ShapeDtypeStruct((t, LANES), F32),
                   jax.ShapeDtypeStruct((N_EXPERTS, LANES), F32)],
        scratch_shapes=[pltpu.VMEM((N_EXPERTS, LANES), F32), pltpu.VMEM((tm, tm), BF16)],
        compiler_params=_params("arbitrary"),
        name="moe_route",
    )(x, g, router_t)


def _moe_slots_kernel(base_ref, route_ref, slot_ref):
    r = route_ref[...]
    slot_ref[...] = jnp.zeros_like(slot_ref)
    for k in range(2):
        e, rank = r[2 * k:2 * k + 1, :], r[2 * k + 1:2 * k + 2, :]
        start = jnp.zeros_like(e)
        for j in range(N_EXPERTS):
            start = jnp.where(e == j, base_ref[j], start)
        slot_ref[k:k + 1, :] = start + rank


def _moe_slots(route, base, tm=2048):
    t = route.shape[1]
    return pl.pallas_call(
        _moe_slots_kernel,
        grid_spec=pltpu.PrefetchScalarGridSpec(
            num_scalar_prefetch=1,
            grid=(t // tm,),
            in_specs=[pl.BlockSpec((ROUTE_ROWS, tm), lambda i, *_: (0, i))],
            out_specs=pl.BlockSpec((ROUTE_ROWS, tm), lambda i, *_: (0, i))),
        out_shape=jax.ShapeDtypeStruct((ROUTE_ROWS, t), jnp.int32),
        compiler_params=_params("parallel"),
        name="moe_slots",
    )(base, route)


DMA_UNROLL = 8
COMBINE_ROWS = 128


def _moe_dispatch_kernel(tm, fs_ref, fe_ref, slot_ref, x_ref, xs_hbm, sem):
    i = pl.program_id(0)

    def row_copy(src_row, dst_row):
        return pltpu.make_async_copy(x_ref.at[pl.ds(src_row, 1)], xs_hbm.at[pl.ds(dst_row, 1)], sem)

    def issue(t, c):
        for k in range(2):
            row_copy(t, slot_ref[k, t]).start()
        return c

    lax.fori_loop(0, tm, issue, 0, unroll=DMA_UNROLL)
    for k in range(2):
        pltpu.make_async_copy(x_ref, xs_hbm.at[pl.ds(0, tm)], sem).wait()

    @pl.when(i == pl.num_programs(0) - 1)
    def _():
        for e in range(N_EXPERTS + 1):
            def fill(p, c):
                row_copy(0, p).start()
                return c

            def drain(p, c):
                row_copy(0, p).wait()
                return c

            lax.fori_loop(fs_ref[e], fe_ref[e], fill, 0)
            lax.fori_loop(fs_ref[e], fe_ref[e], drain, 0)


def _moe_dispatch(x, slots, fill_start, fill_end, n_slots, tm=1024):
    t = x.shape[0]
    return pl.pallas_call(
        functools.partial(_moe_dispatch_kernel, tm),
        grid_spec=pltpu.PrefetchScalarGridSpec(
            num_scalar_prefetch=2,
            grid=(t // tm,),
            in_specs=[pl.BlockSpec((ROUTE_ROWS, tm), lambda i, *_: (0, i), memory_space=pltpu.SMEM),
                      pl.BlockSpec((tm, D_MODEL), lambda i, *_: (i, 0))],
            out_specs=pl.BlockSpec(memory_space=pl.ANY),
            scratch_shapes=[pltpu.SemaphoreType.DMA(())]),
        out_shape=jax.ShapeDtypeStruct((n_slots, D_MODEL), F32),
        compiler_params=_params("arbitrary"),
        name="moe_dispatch",
    )(fill_start, fill_end, slots, x)


def _ffn_grouped_kernel(te_ref, nv_ref, x_ref, g_ref, wg_ref, wu_ref, wd_ref, y_ref):
    i = pl.program_id(0)

    @pl.when(i < nv_ref[0])
    def _():
        h = _rmsnorm(x_ref[...], g_ref[...]).astype(BF16)
        gate = jnp.dot(h, wg_ref[0], preferred_element_type=F32)
        up = jnp.dot(h, wu_ref[0], preferred_element_type=F32)
        y_ref[...] = jnp.dot((_silu(gate) * up).astype(BF16), wd_ref[0], preferred_element_type=F32)

    @pl.when(i >= nv_ref[0])
    def _():
        y_ref[...] = jnp.zeros_like(y_ref)


def _ffn_grouped(xs, g, wg, wu, wd, tile_expert, n_valid):
    n_slots = xs.shape[0]
    f = wg.shape[2]
    rows = pl.BlockSpec((MOE_TILE, D_MODEL), lambda i, te, nv: (i, 0))
    return pl.pallas_call(
        _ffn_grouped_kernel,
        grid_spec=pltpu.PrefetchScalarGridSpec(
            num_scalar_prefetch=2,
            grid=(n_slots // MOE_TILE,),
            in_specs=[rows,
                      pl.BlockSpec((1, D_MODEL), lambda i, te, nv: (0, 0)),
                      pl.BlockSpec((1, D_MODEL, f), lambda i, te, nv: (te[i], 0, 0)),
                      pl.BlockSpec((1, D_MODEL, f), lambda i, te, nv: (te[i], 0, 0)),
                      pl.BlockSpec((1, f, D_MODEL), lambda i, te, nv: (te[i], 0, 0))],
            out_specs=rows),
        out_shape=jax.ShapeDtypeStruct((n_slots, D_MODEL), F32),
        compiler_params=_params("arbitrary"),
        name="ffn_grouped",
    )(tile_expert, n_valid, xs, g, wg, wu, wd)


def _moe_combine_kernel(tm, final, slot_ref, x_ref, gate_ref, *rest):
    if final:
        gf_ref, rest = rest[0], rest[1:]
    ys_hbm, o_ref, y1_ref, y2_ref, sems = rest
    bufs = (y1_ref, y2_ref)

    def issue(t, c):
        for k in range(2):
            pltpu.make_async_copy(ys_hbm.at[pl.ds(slot_ref[k, t], 1)], bufs[k].at[pl.ds(t, 1)],
                                  sems.at[k]).start()
        return c

    lax.fori_loop(0, tm, issue, 0, unroll=DMA_UNROLL)
    for k in range(2):
        pltpu.make_async_copy(ys_hbm.at[pl.ds(0, tm)], bufs[k], sems.at[k]).wait()
    rc = COMBINE_ROWS

    def rows(ci, c):
        r = pl.ds(pl.multiple_of(ci * rc, rc), rc)
        g1 = jnp.broadcast_to(gate_ref[r, 0:1], (rc, LANES))
        g2 = jnp.broadcast_to(gate_ref[r, 1:2], (rc, LANES))
        ssq = jnp.zeros((rc, LANES), F32)
        for j in range(D_MODEL // LANES):
            cols = slice(j * LANES, (j + 1) * LANES)
            out = x_ref[r, cols] + g1 * y1_ref[r, cols] + g2 * y2_ref[r, cols]
            o_ref[r, cols] = out
            ssq = ssq + out * out
        if final:
            scale = lax.rsqrt(jnp.sum(ssq, axis=-1, keepdims=True) * (1.0 / D_MODEL) + EPS)
            for j in range(D_MODEL // LANES):
                cols = slice(j * LANES, (j + 1) * LANES)
                o_ref[r, cols] = o_ref[r, cols] * scale * gf_ref[:, cols]
        return c

    lax.fori_loop(0, tm // rc, rows, 0)


def _moe_combine(x, slots, gates, ys, final_g=None, tm=512):
    t = x.shape[0]
    final = final_g is not None
    in_specs = [pl.BlockSpec((ROUTE_ROWS, tm), lambda i: (0, i), memory_space=pltpu.SMEM),
                pl.BlockSpec((tm, D_MODEL), lambda i: (i, 0)),
                pl.BlockSpec((tm, LANES), lambda i: (i, 0))]
    args = [slots, x, gates]
    if final:
        in_specs.append(pl.BlockSpec((1, D_MODEL), lambda i: (0, 0)))
        args.append(final_g)
    in_specs.append(pl.BlockSpec(memory_space=pl.ANY))
    args.append(ys)
    return pl.pallas_call(
        functools.partial(_moe_combine_kernel, tm, final),
        grid=(t // tm,),
        in_specs=in_specs,
        out_specs=pl.BlockSpec((tm, D_MODEL), lambda i: (i, 0)),
        out_shape=jax.ShapeDtypeStruct((t, D_MODEL), F32),
        scratch_shapes=[pltpu.VMEM((tm, D_MODEL), F32), pltpu.VMEM((tm, D_MODEL), F32),
                        pltpu.SemaphoreType.DMA((2,))],
        compiler_params=_params("arbitrary"),
        name="moe_combine",
    )(*args)


def _moe_routed(x, g, router_t, wg, wu, wd, final_g=None):
    t = x.shape[0]
    n_tiles = 2 * t // MOE_TILE + N_EXPERTS
    route, gates, counts = _moe_route(x, g, router_t)
    cnt = counts[:, 0].astype(jnp.int32)
    caps = (cnt + MOE_TILE - 1) // MOE_TILE
    cum = jnp.cumsum(caps)
    base = (cum - caps) * MOE_TILE
    n_valid = cum[-1:]
    tile_expert = jnp.minimum(
        jnp.sum((cum[None, :] <= jnp.arange(n_tiles, dtype=jnp.int32)[:, None]).astype(jnp.int32), axis=1),
        N_EXPERTS - 1)
    fill_start = jnp.concatenate([base + cnt, n_valid * MOE_TILE])
    fill_end = jnp.concatenate([base + caps * MOE_TILE, jnp.full((1,), n_tiles * MOE_TILE, jnp.int32)])
    slots = _moe_slots(route, base)
    xs = _moe_dispatch(x, slots, fill_start, fill_end, n_tiles * MOE_TILE)
    ys = _ffn_grouped(xs, g, wg, wu, wd, tile_expert, n_valid)
    return _moe_combine(x, slots, gates, ys, final_g)


def _trunk(x, state_hgrn, state_pool, start_pos, w, seq, tm):
    batch = x.shape[0] // seq
    s_out, b_out = [], []
    for l in range(DEPTH):
        act = _mixer_in(x, w["norm_mix"][l], w["w_in"], w["lb_logits"], l, tm)
        if seq > 1:
            og, s_new = _hgrn_prompt(act, w["hg_norm"][l], batch, seq)
            x, b_new = _mixer_out_pool(x, og, act, w["w_o"], l, w["pool_w"][l], w["pool_scale"][l], batch, seq)
        else:
            og, s_new = _hgrn_step(act, w["hg_norm"][l], state_hgrn, l)
            yp, b_new = _pool_step(act, state_pool, l, w["pool_w"][l], w["pool_scale"][l], start_pos)
            x = _mixer_out(x, og, yp, w["w_o"], l, tm)
        s_out.append(s_new)
        b_out.append(b_new)
        routed = l % 2 == 1 and seq > 1
        j = l // 2
        final_g = w["norm_final"] if l == DEPTH - 1 else None
        if l % 2 == 0:
            x = _ffn(x, w["norm_ffn"][l], w["ffn_w_gate"][j:j + 1], w["ffn_w_up"][j:j + 1],
                     w["ffn_w_down"][j:j + 1], tm, 1408, final_g=final_g)
        elif routed:
            x = _moe_routed(x, w["norm_ffn"][l], w["router_t"][j], w["moe_w_gate"][j],
                            w["moe_w_up"][j], w["moe_w_down"][j], final_g=final_g)
        else:
            x = _ffn(x, w["norm_ffn"][l], w["moe_w_gate"][j], w["moe_w_up"][j], w["moe_w_down"][j],
                     tm, 1408, router=w["router"][j], final_g=final_g)
    return x, jnp.stack(s_out), jnp.stack(b_out)


def kernel(x_prompt, x_sample, state_hgrn, state_pool, lb_logits, norm_mix, w_in, w_o, hg_norm, pool_w,
           pool_scale, norm_ffn, ffn_w_gate, ffn_w_up, ffn_w_down, router, moe_w_gate, moe_w_up, moe_w_down,
           norm_final):
    batch, seq, _ = x_prompt.shape
    dec_batch, dec_seq, _ = x_sample.shape
    assert dec_seq == 1
    past_len = 16384
    w = dict(
        lb_logits=lb_logits,
        norm_mix=norm_mix.reshape(DEPTH, 1, D_MODEL),
        w_in=w_in,
        w_o=w_o,
        hg_norm=hg_norm.reshape(DEPTH, 1, HG_WIDTH),
        pool_w=pool_w.astype(BF16),
        pool_scale=pool_scale.reshape(DEPTH, 1, POOL_WIDTH),
        norm_ffn=norm_ffn.reshape(DEPTH, 1, D_MODEL),
        ffn_w_gate=ffn_w_gate.astype(BF16),
        ffn_w_up=ffn_w_up.astype(BF16),
        ffn_w_down=ffn_w_down.astype(BF16),
        router=jnp.pad(router, ((0, 0), (0, 0), (0, LANES - N_EXPERTS))),
        router_t=jnp.swapaxes(router, 1, 2),
        moe_w_gate=moe_w_gate.astype(BF16),
        moe_w_up=moe_w_up.astype(BF16),
        moe_w_down=moe_w_down.astype(BF16),
        norm_final=norm_final.reshape(1, D_MODEL),
    )
    yp, sp, bp = _trunk(x_prompt.reshape(batch * seq, D_MODEL), None, None, 0, w, seq, 512)
    ys, ss, bs = _trunk(x_sample.reshape(dec_batch, D_MODEL), state_hgrn, state_pool, past_len, w, 1, 128)
    return (yp.reshape(batch, seq, D_MODEL), ys.reshape(dec_batch, 1, D_MODEL), sp, ss, bp, bs)
```

```python
import functools

import jax
import jax.numpy as jnp
from jax import lax
from jax.experimental import pallas as pl
from jax.experimental.pallas import tpu as pltpu

F32 = jnp.float32
BF16 = jnp.bfloat16

D_MODEL = 1024
DEPTH = 4
HG_WIDTH = 512
HG_HEADS = 4
HG_D = 128
POOL_WIDTH = 512
POOL_WINDOWS = (2, 4, 8, 16)
POOL_GROUP_W = 128
POOL_BUF = 15
IN_WIDTH = 4 * HG_WIDTH + POOL_WIDTH
ACT_WIDTH = 4 * HG_WIDTH + POOL_WIDTH
ACT_POOL_PART = 4
N_EXPERTS = 8
EPS = 1e-6
LANES = 128
SUBLANES = 8
VMEM_LIMIT = 56 * 1024 * 1024
HGRN_CHUNK = 128


def _params(*sem):
    return pltpu.CompilerParams(dimension_semantics=sem, vmem_limit_bytes=VMEM_LIMIT)


def _rmsnorm(x, g):
    return x * lax.rsqrt(jnp.mean(x * x, axis=-1, keepdims=True) + EPS) * g


def _silu(x):
    return x * jax.nn.sigmoid(x)


def _mixer_in_kernel(layer, x_ref, g_ref, w_ref, lbl_ref, o_ref, wb_ref):
    @pl.when(pl.program_id(0) == 0)
    def _():
        wb_ref[...] = w_ref[0].astype(BF16)

    h = _rmsnorm(x_ref[...], g_ref[...])
    p = jnp.dot(h.astype(BF16), wb_ref[...], preferred_element_type=F32)
    lg = lbl_ref[...]
    e = jnp.exp(lg - jnp.max(lg, axis=0, keepdims=True))
    pr = e / jnp.sum(e, axis=0, keepdims=True)
    cum = pr[0:1]
    for j in range(1, layer + 1):
        cum = cum + pr[j:j + 1]
    lb = cum - pr[0:1]
    w = HG_WIDTH
    q, fx, ix, g, u = p[:, :w], p[:, w:2 * w], p[:, 2 * w:3 * w], p[:, 3 * w:4 * w], p[:, 4 * w:]
    f = lb + (1.0 - lb) * jax.nn.sigmoid(fx)
    o_ref[:, 0:w] = _silu(q)
    o_ref[:, w:2 * w] = jnp.log(f)
    o_ref[:, 2 * w:3 * w] = ix
    o_ref[:, 3 * w:4 * w] = _silu(g)
    o_ref[:, 4 * w:] = u


def _mixer_in(x, g, w_in, lb_logits, layer, tm):
    t = x.shape[0]
    return pl.pallas_call(
        functools.partial(_mixer_in_kernel, layer),
        grid=(t // tm,),
        in_specs=[
            pl.BlockSpec((tm, D_MODEL), lambda i: (i, 0)),
            pl.BlockSpec((1, D_MODEL), lambda i: (0, 0)),
            pl.BlockSpec((1, D_MODEL, IN_WIDTH), lambda i: (layer, 0, 0)),
            pl.BlockSpec((DEPTH, HG_WIDTH), lambda i: (0, 0)),
        ],
        out_specs=pl.BlockSpec((tm, ACT_WIDTH), lambda i: (i, 0)),
        out_shape=jax.ShapeDtypeStruct((t, ACT_WIDTH), F32),
        scratch_shapes=[pltpu.VMEM((D_MODEL, IN_WIDTH), BF16)],
        compiler_params=_params("arbitrary"),
        name="mixer_in",
    )(x, g, w_in, lb_logits)


SMALL_LEVELS = (2, 4, 8)


def _level_sum_matrix(n, ti, si):
    half = n // 2
    mid = (ti & (-n)) + (half - 1)
    lo = jnp.where((ti & half) != 0, mid, ti)
    hi = jnp.where((ti & half) != 0, ti, mid)
    return (si > lo) & (si <= hi)


def _level_factors(q, k, b, n, row, arg=None):
    c = b.shape[0]
    half = n // 2
    if arg is None:
        zero = jnp.zeros((half, b.shape[1]), F32)
        qs, ks = [], []
        for r0 in range(0, c, n):
            mid = b[r0 + half - 1:r0 + half, :]
            ks += [k[r0:r0 + half] * jnp.exp(mid - b[r0:r0 + half]), zero]
            qs += [zero, q[r0 + half:r0 + n] * jnp.exp(b[r0 + half:r0 + n] - mid)]
        return jnp.concatenate(qs, axis=0), jnp.concatenate(ks, axis=0)
    right = (row & half) != 0
    qk = jnp.where(right, q, k) * jnp.exp(arg)
    return jnp.where(right, qk, 0.0), jnp.where(right, 0.0, qk)


def _split3(x):
    hi = x.astype(BF16)
    r1 = x - hi.astype(F32)
    mid = r1.astype(BF16)
    lo = (r1 - mid.astype(F32)).astype(BF16)
    return jnp.concatenate([hi, mid, lo], axis=0)


HEAD_PAIRS = HG_HEADS // 2
PAIR_W = 2 * HG_D


def _pair_blocks(a):
    zero = jnp.zeros((a.shape[0], HG_D), a.dtype)
    return jnp.concatenate([jnp.concatenate([a[:, :HG_D], zero], axis=1),
                            jnp.concatenate([zero, a[:, HG_D:]], axis=1)], axis=0)


def _per_head(fn, a):
    return jnp.concatenate([fn(a[:, :HG_D]), fn(a[:, HG_D:])], axis=1)


def _hgrn_prompt_kernel(tl, q_ref, lf_ref, v_ref, g_ref, nrm_ref, o_ref, s_ref, st_ref):
    c = HGRN_CHUNK
    li = pl.program_id(1)

    @pl.when(li == 0)
    def _():
        st_ref[...] = jnp.zeros_like(st_ref)

    row = lax.broadcasted_iota(jnp.int32, (c, PAIR_W), 0)
    ti = lax.broadcasted_iota(jnp.int32, (c, c), 0)
    si = lax.broadcasted_iota(jnp.int32, (c, c), 1)
    sums = jnp.concatenate(
        [jnp.where(m, 1.0, 0.0)
         for m in [si <= ti] + [_level_sum_matrix(n, ti, si) for n in SMALL_LEVELS]], axis=0).astype(BF16)
    sums = jnp.concatenate([sums] * 3, axis=1)
    apart = jnp.concatenate([ti ^ si] * 2, axis=1)
    pi = lax.broadcasted_iota(jnp.int32, (PAIR_W, PAIR_W), 0)
    pj = lax.broadcasted_iota(jnp.int32, (PAIR_W, PAIR_W), 1)
    own = (pi < HG_D) == (pj < HG_D)
    nt = (((1,), (1,)), ((), ()))
    tn = (((0,), (0,)), ((), ()))

    def chunk(ci, carry):
        rows = pl.ds(pl.multiple_of(ci * c, c), c)
        for p in range(HEAD_PAIRS):
            cols = slice(p * PAIR_W, (p + 1) * PAIR_W)
            q, lf, v = q_ref[rows, cols], lf_ref[rows, cols], v_ref[rows, cols]
            k = 1.0 - jnp.exp(lf)
            s1 = jnp.dot(sums, _split3(lf), preferred_element_type=F32)
            b = s1[:c]
            args = {n: s1[(i + 1) * c:(i + 2) * c] for i, n in enumerate(SMALL_LEVELS)}
            vbd = _pair_blocks(v.astype(BF16))
            scores = None
            n = c
            while n >= 2:
                qt, kt = _level_factors(q, k, b, n, row, args.get(n))
                s_n = lax.dot_general(qt.astype(BF16), _pair_blocks(kt.astype(BF16)), nt,
                                      preferred_element_type=F32)
                scores = s_n if scores is None else jnp.where(apart < n, s_n, scores)
                n //= 2
            st = st_ref[p]
            qs = (q * jnp.exp(b)).astype(BF16)
            o = jnp.dot(scores.astype(BF16), vbd, preferred_element_type=F32)
            o = o + lax.dot_general(qs, st.astype(BF16), nt, preferred_element_type=F32)
            o = o + _per_head(lambda a: jnp.broadcast_to(jnp.sum(a, axis=-1, keepdims=True), a.shape), q * k) * v
            bc = b[c - 1:c, :]
            kd = (k * jnp.exp(bc - b)).astype(BF16)
            upd = lax.dot_general(v.astype(BF16), kd, tn, preferred_element_type=F32)
            st_ref[p] = st * jnp.exp(bc) + jnp.where(own, upd, 0.0)
            ms = _per_head(lambda a: jnp.broadcast_to(jnp.mean(a, axis=-1, keepdims=True), a.shape), o * o)
            o = o * lax.rsqrt(ms + EPS) * nrm_ref[:, cols]
            o_ref[rows, cols] = (o * g_ref[rows, cols]).astype(o_ref.dtype)
        return carry

    lax.fori_loop(0, tl // c, chunk, 0, unroll=2)

    @pl.when(li == pl.num_programs(1) - 1)
    def _():
        for p in range(HEAD_PAIRS):
            st = st_ref[p]
            s_ref[0, 2 * p] = st[:HG_D, :HG_D].T
            s_ref[0, 2 * p + 1] = st[HG_D:, HG_D:].T


def _hgrn_prompt(act, hg_norm, batch, seq, tl=1024):
    nl = seq // tl
    blk = lambda part: pl.BlockSpec((tl, HG_WIDTH), lambda b, i: (b * nl + i, part))
    return pl.pallas_call(
        functools.partial(_hgrn_prompt_kernel, tl),
        grid=(batch, nl),
        in_specs=[blk(0), blk(1), blk(2), blk(3),
                  pl.BlockSpec((1, HG_WIDTH), lambda b, i: (0, 0))],
        out_specs=[pl.BlockSpec((tl, HG_WIDTH), lambda b, i: (b * nl + i, 0)),
                   pl.BlockSpec((1, HG_HEADS, HG_D, HG_D), lambda b, i: (b, 0, 0, 0))],
        out_shape=[jax.ShapeDtypeStruct((batch * seq, HG_WIDTH), BF16),
                   jax.ShapeDtypeStruct((batch, HG_HEADS, HG_D, HG_D), F32)],
        scratch_shapes=[pltpu.VMEM((HEAD_PAIRS, PAIR_W, PAIR_W), F32)],
        compiler_params=_params("parallel", "arbitrary"),
        name="hgrn_prompt",
    )(act, act, act, act, hg_norm)


HGRN_STEP_BT = 16


def _hgrn_step_kernel(q_ref, lf_ref, v_ref, g_ref, nrm_ref, s_ref, o_ref, so_ref):
    bt = HGRN_STEP_BT
    q = q_ref[...]
    v = v_ref[...]
    f = jnp.exp(lf_ref[...])
    ri = lax.broadcasted_iota(jnp.int32, (3 * bt, PAIR_W), 0) % bt
    half = lax.broadcasted_iota(jnp.int32, (3 * bt, PAIR_W), 1) // HG_D
    tn = (((0,), (0,)), ((), ()))
    pieces = [_split3(a) for a in (f, 1.0 - f, q)]
    rows = []
    for s in range(0, bt, 2):
        pick = jnp.where(ri == s + half, 1.0, 0.0).astype(BF16)
        fc, kc, qc = [lax.dot_general(p, pick, tn, preferred_element_type=F32) for p in pieces]
        for d in range(2):
            cols = slice(d * HG_D, (d + 1) * HG_D)
            sn = fc[:, cols] * s_ref[0, s + d, 0] + kc[:, cols] * v[s + d:s + d + 1, :]
            so_ref[s + d, 0] = sn
            rows.append(jnp.sum(sn * qc[:, cols], axis=0, keepdims=True))
    o = jnp.concatenate(rows, axis=0)
    o = o * lax.rsqrt(jnp.mean(o * o, axis=-1, keepdims=True) + EPS) * nrm_ref[...]
    o_ref[...] = (o * g_ref[...]).astype(o_ref.dtype)


def _hgrn_step(act, hg_norm, states, layer):
    batch = act.shape[0]
    bt = HGRN_STEP_BT
    blk = lambda part: pl.BlockSpec((bt, HG_D), lambda i, h: (i, part * HG_HEADS + h))
    return pl.pallas_call(
        _hgrn_step_kernel,
        grid=(batch // bt, HG_HEADS),
        in_specs=[blk(0), blk(1), blk(2), blk(3),
                  pl.BlockSpec((1, HG_D), lambda i, h: (0, h)),
                  pl.BlockSpec((1, bt, 1, HG_D, HG_D), lambda i, h: (layer, i, h, 0, 0))],
        out_specs=[pl.BlockSpec((bt, HG_D), lambda i, h: (i, h)),
                   pl.BlockSpec((bt, 1, HG_D, HG_D), lambda i, h: (i, h, 0, 0))],
        out_shape=[jax.ShapeDtypeStruct((batch, HG_WIDTH), BF16),
                   jax.ShapeDtypeStruct(states.shape[1:], F32)],
        compiler_params=_params("parallel", "parallel"),
        name="hgrn_step",
    )(act, act, act, act, hg_norm, states)


def _pool_map(d_groups, wp_ref, scale_ref, o_ref):
    for gi, d in enumerate(d_groups):
        sl = slice(gi * POOL_GROUP_W, (gi + 1) * POOL_GROUP_W)
        y = jnp.dot(d.astype(BF16), wp_ref[gi], preferred_element_type=F32)
        o_ref[:, sl] = (y * scale_ref[:, sl]).astype(o_ref.dtype)


POOL_HIST = 16


def _pool_tile(i, tl, z, carry_ref, wp_ref, scale_ref, o_ref, nb_ref):
    @pl.when(i == 0)
    def _():
        carry_ref[...] = jnp.zeros_like(carry_ref)

    ext = jnp.concatenate([carry_ref[...], z], axis=0)
    sums = {1: ext}
    w = 1
    while w < max(POOL_WINDOWS):
        sums[2 * w] = sums[w] + pltpu.roll(sums[w], w, 0)
        w *= 2
    pos = i * tl + lax.broadcasted_iota(jnp.int32, (tl, POOL_GROUP_W), 0)
    ds = []
    for gi, w in enumerate(POOL_WINDOWS):
        sl = slice(gi * POOL_GROUP_W, (gi + 1) * POOL_GROUP_W)
        cnt = jnp.minimum(pos + 1, w).astype(F32)
        ds.append(sums[w][POOL_HIST:, sl] / cnt - z[:, sl])
    _pool_map(ds, wp_ref, scale_ref, o_ref)
    carry_ref[...] = z[tl - POOL_HIST:, :]

    @pl.when(i == pl.num_programs(1) - 1)
    def _():
        nb_ref[0] = z[tl - POOL_BUF:, :]


def _mixer_out_pool_kernel(tl, x_ref, o_ref, u_ref, w_ref, wp_ref, scale_ref, y_ref, nb_ref,
                           wb_ref, carry_ref, yp_ref):
    b, i = pl.program_id(0), pl.program_id(1)

    @pl.when((b == 0) & (i == 0))
    def _():
        wb_ref[...] = w_ref[0].astype(BF16)

    _pool_tile(i, tl, u_ref[...], carry_ref, wp_ref, scale_ref, yp_ref, nb_ref)
    y = jnp.dot(o_ref[...], wb_ref[0:HG_WIDTH, :], preferred_element_type=F32)
    y = y + jnp.dot(yp_ref[...], wb_ref[HG_WIDTH:, :], preferred_element_type=F32)
    y_ref[...] = x_ref[...] + y


def _mixer_out_pool(x, og, act, w_o, layer, pool_w, pool_scale, batch, seq, tl=512):
    nl = seq // tl
    tok = lambda cols: (lambda b, i: (b * nl + i, cols))
    return pl.pallas_call(
        functools.partial(_mixer_out_pool_kernel, tl),
        grid=(batch, nl),
        in_specs=[pl.BlockSpec((tl, D_MODEL), tok(0)),
                  pl.BlockSpec((tl, HG_WIDTH), tok(0)),
                  pl.BlockSpec((tl, POOL_WIDTH), tok(ACT_POOL_PART)),
                  pl.BlockSpec((1, D_MODEL, D_MODEL), lambda b, i: (layer, 0, 0)),
                  pl.BlockSpec((len(POOL_WINDOWS), POOL_GROUP_W, POOL_GROUP_W), lambda b, i: (0, 0, 0)),
                  pl.BlockSpec((1, POOL_WIDTH), lambda b, i: (0, 0))],
        out_specs=[pl.BlockSpec((tl, D_MODEL), tok(0)),
                   pl.BlockSpec((1, POOL_BUF, POOL_WIDTH), lambda b, i: (b, 0, 0))],
        out_shape=[jax.ShapeDtypeStruct((batch * seq, D_MODEL), F32),
                   jax.ShapeDtypeStruct((batch, POOL_BUF, POOL_WIDTH), F32)],
        scratch_shapes=[pltpu.VMEM((D_MODEL, D_MODEL), BF16), pltpu.VMEM((POOL_HIST, POOL_WIDTH), F32),
                        pltpu.VMEM((tl, POOL_WIDTH), BF16)],
        compiler_params=_params("arbitrary", "arbitrary"),
        name="mixer_out_pool",
    )(x, og, act, w_o, pool_w, pool_scale)


POOL_STEP_BT = 16


def _pool_step_kernel(start_pos, u_ref, buf_ref, wp_ref, scale_ref, o_ref, nb_ref):
    u = u_ref[...]
    buf = buf_ref[0]
    ds = []
    for gi, w in enumerate(POOL_WINDOWS):
        sl = slice(gi * POOL_GROUP_W, (gi + 1) * POOL_GROUP_W)
        tot = u[:, sl] + jnp.sum(buf[:, POOL_BUF - (w - 1):, sl], axis=1)
        ds.append(tot / float(min(start_pos + 1, w)) - u[:, sl])
    _pool_map(ds, wp_ref, scale_ref, o_ref)
    nb_ref[:, 0:POOL_BUF - 1, :] = buf[:, 1:, :]
    nb_ref[:, POOL_BUF - 1:, :] = u[:, None, :]


def _pool_step(act, bufs, layer, pool_w, pool_scale, start_pos):
    batch = act.shape[0]
    bt = POOL_STEP_BT
    return pl.pallas_call(
        functools.partial(_pool_step_kernel, start_pos),
        grid=(batch // bt,),
        in_specs=[pl.BlockSpec((bt, POOL_WIDTH), lambda i: (i, ACT_POOL_PART)),
                  pl.BlockSpec((1, bt, POOL_BUF, POOL_WIDTH), lambda i: (layer, i, 0, 0)),
                  pl.BlockSpec((len(POOL_WINDOWS), POOL_GROUP_W, POOL_GROUP_W), lambda i: (0, 0, 0)),
                  pl.BlockSpec((1, POOL_WIDTH), lambda i: (0, 0))],
        out_specs=[pl.BlockSpec((bt, POOL_WIDTH), lambda i: (i, 0)),
                   pl.BlockSpec((bt, POOL_BUF, POOL_WIDTH), lambda i: (i, 0, 0))],
        out_shape=[jax.ShapeDtypeStruct((batch, POOL_WIDTH), BF16),
                   jax.ShapeDtypeStruct(bufs.shape[1:], F32)],
        compiler_params=_params("parallel"),
        name="pool_step",
    )(act, bufs, pool_w, pool_scale)


def _mixer_out_kernel(x_ref, o_ref, p_ref, w_ref, y_ref, wb_ref):
    @pl.when(pl.program_id(0) == 0)
    def _():
        wb_ref[...] = w_ref[0].astype(BF16)

    y = jnp.dot(o_ref[...], wb_ref[0:HG_WIDTH, :], preferred_element_type=F32)
    y = y + jnp.dot(p_ref[...], wb_ref[HG_WIDTH:, :], preferred_element_type=F32)
    y_ref[...] = x_ref[...] + y


def _mixer_out(x, og, yp, w_o, layer, tm):
    t = x.shape[0]
    return pl.pallas_call(
        _mixer_out_kernel,
        grid=(t // tm,),
        in_specs=[pl.BlockSpec((tm, D_MODEL), lambda i: (i, 0)),
                  pl.BlockSpec((tm, HG_WIDTH), lambda i: (i, 0)),
                  pl.BlockSpec((tm, POOL_WIDTH), lambda i: (i, 0)),
                  pl.BlockSpec((1, D_MODEL, D_MODEL), lambda i: (layer, 0, 0))],
        out_specs=pl.BlockSpec((tm, D_MODEL), lambda i: (i, 0)),
        out_shape=jax.ShapeDtypeStruct((t, D_MODEL), F32),
        scratch_shapes=[pltpu.VMEM((D_MODEL, D_MODEL), BF16)],
        compiler_params=_params("arbitrary"),
        name="mixer_out",
    )(x, og, yp, w_o)


def _top2_combine(logits):
    lane = lax.broadcasted_iota(jnp.int32, logits.shape, 1).astype(F32)
    neg = jnp.float32(-jnp.inf)
    lg = jnp.where(lane < N_EXPERTS, logits, neg)
    m1 = jnp.max(lg, axis=-1, keepdims=True)
    i1 = jnp.min(jnp.where(lg == m1, lane, float(LANES)), axis=-1, keepdims=True)
    lg2 = jnp.where(lane == i1, neg, lg)
    m2 = jnp.max(lg2, axis=-1, keepdims=True)
    i2 = jnp.min(jnp.where(lg2 == m2, lane, float(LANES)), axis=-1, keepdims=True)
    e2 = jnp.exp(m2 - m1)
    g1 = 1.0 / (1.0 + e2)
    g2 = e2 / (1.0 + e2)
    return jnp.where(lane == i1, g1, 0.0) + jnp.where(lane == i2, g2, 0.0)


def _ffn_kernel(moe, final, x_ref, g_ref, *rest):
    if moe:
        r_ref, rest = rest[0], rest[1:]
    if final:
        gf_ref, rest = rest[0], rest[1:]
    wg_ref, wu_ref, wd_ref, y_ref, h_ref, acc_ref = rest[:6]
    e, j = pl.program_id(1), pl.program_id(2)
    first = (e == 0) & (j == 0)
    last = (e == pl.num_programs(1) - 1) & (j == pl.num_programs(2) - 1)

    @pl.when(first)
    def _():
        h = _rmsnorm(x_ref[...], g_ref[...])
        h_ref[...] = h.astype(BF16)
        acc_ref[...] = jnp.zeros_like(acc_ref)
        if moe:
            logits = jnp.dot(h, r_ref[...], precision=lax.Precision.HIGHEST, preferred_element_type=F32)
            rest[6][...] = _top2_combine(logits)

    h = h_ref[...]
    gate = jnp.dot(h, wg_ref[0, 0], preferred_element_type=F32)
    up = jnp.dot(h, wu_ref[0, 0], preferred_element_type=F32)
    y = jnp.dot((_silu(gate) * up).astype(BF16), wd_ref[0, 0], preferred_element_type=F32)
    if moe:
        comb = rest[6][...]
        lane = lax.broadcasted_iota(jnp.int32, comb.shape, 1)
        y = jnp.sum(jnp.where(lane == e, comb, 0.0), axis=-1, keepdims=True) * y
    acc_ref[...] += y

    @pl.when(last)
    def _():
        out = x_ref[...] + acc_ref[...]
        if final:
            out = _rmsnorm(out, gf_ref[...])
        y_ref[...] = out


def _ffn(x, g, wg, wu, wd, layer, tm, tf, router=None, final_g=None):
    t = x.shape[0]
    _, ne, _, f = wg.shape
    moe, final = router is not None, final_g is not None
    vec = pl.BlockSpec((1, D_MODEL), lambda i, e, j: (0, 0))
    in_specs = [pl.BlockSpec((tm, D_MODEL), lambda i, e, j: (i, 0)), vec]
    args = [x, g]
    scratch = [pltpu.VMEM((tm, D_MODEL), BF16), pltpu.VMEM((tm, D_MODEL), F32)]
    if moe:
        in_specs.append(pl.BlockSpec((D_MODEL, LANES), lambda i, e, j: (0, 0)))
        args.append(router)
        scratch.append(pltpu.VMEM((tm, LANES), F32))
    if final:
        in_specs.append(vec)
        args.append(final_g)
    in_specs += [pl.BlockSpec((1, 1, D_MODEL, tf), lambda i, e, j: (layer, e, 0, j)),
                 pl.BlockSpec((1, 1, D_MODEL, tf), lambda i, e, j: (layer, e, 0, j)),
                 pl.BlockSpec((1, 1, tf, D_MODEL), lambda i, e, j: (layer, e, j, 0))]
    args += [wg, wu, wd]
    return pl.pallas_call(
        functools.partial(_ffn_kernel, moe, final),
        grid=(t // tm, ne, f // tf),
        in_specs=in_specs,
        out_specs=pl.BlockSpec((tm, D_MODEL), lambda i, e, j: (i, 0)),
        out_shape=jax.ShapeDtypeStruct((t, D_MODEL), F32),
        scratch_shapes=scratch,
        compiler_params=_params("parallel", "arbitrary", "arbitrary"),
        name="ffn_moe" if moe else "ffn_dense",
    )(*args)


MOE_TILE = 512
ROUTE_ROWS = 8
NT_DIMS = (((1,), (1,)), ((), ()))


def _moe_route_kernel(x_ref, g_ref, rt_ref, route_ref, gate_ref, cnt_ref, carry_ref, earlier_ref):
    tm = x_ref.shape[0]

    @pl.when(pl.program_id(0) == 0)
    def _():
        carry_ref[...] = jnp.zeros_like(carry_ref)
        t0 = lax.broadcasted_iota(jnp.int32, (tm, tm), 0)
        t1 = lax.broadcasted_iota(jnp.int32, (tm, tm), 1)
        earlier_ref[...] = jnp.where(t0 < t1, 1.0, 0.0).astype(BF16)

    h = _rmsnorm(x_ref[...], g_ref[...])
    lt = lax.dot_general(rt_ref[...], h, NT_DIMS, precision=lax.Precision.HIGHEST,
                         preferred_element_type=F32)
    ex = lax.broadcasted_iota(jnp.int32, lt.shape, 0).astype(F32)
    neg = jnp.float32(-jnp.inf)
    m1 = jnp.max(lt, axis=0, keepdims=True)
    i1 = jnp.min(jnp.where(lt == m1, ex, float(N_EXPERTS)), axis=0, keepdims=True)
    l2 = jnp.where(ex == i1, neg, lt)
    m2 = jnp.max(l2, axis=0, keepdims=True)
    i2 = jnp.min(jnp.where(l2 == m2, ex, float(N_EXPERTS)), axis=0, keepdims=True)
    e2 = jnp.exp(m2 - m1)
    g1 = 1.0 / (1.0 + e2)
    g2 = e2 / (1.0 + e2)
    sel1, sel2 = ex == i1, ex == i2
    member = jnp.where(sel1 | sel2, 1.0, 0.0)
    rank = jnp.dot(member.astype(BF16), earlier_ref[...], preferred_element_type=F32) + carry_ref[:, 0:1]
    route_ref[...] = jnp.zeros_like(route_ref)
    route_ref[0:1, :] = i1.astype(jnp.int32)
    route_ref[1:2, :] = jnp.sum(jnp.where(sel1, rank, 0.0), axis=0, keepdims=True).astype(jnp.int32)
    route_ref[2:3, :] = i2.astype(jnp.int32)
    route_ref[3:4, :] = jnp.sum(jnp.where(sel2, rank, 0.0), axis=0, keepdims=True).astype(jnp.int32)
    carry_ref[...] += jnp.sum(member, axis=1, keepdims=True)
    cnt_ref[...] = carry_ref[...]
    row = lax.broadcasted_iota(jnp.int32, (LANES, tm), 0)
    gate_ref[...] = jnp.where(row == 0, g1, jnp.where(row == 1, g2, 0.0)).T


def _moe_route(x, g, router_t, tm=512):
    t = x.shape[0]
    return pl.pallas_call(
        _moe_route_kernel,
        grid=(t // tm,),
        in_specs=[pl.BlockSpec((tm, D_MODEL), lambda i: (i, 0)),
                  pl.BlockSpec((1, D_MODEL), lambda i: (0, 0)),
                  pl.BlockSpec((N_EXPERTS, D_MODEL), lambda i: (0, 0))],
        out_specs=[pl.BlockSpec((ROUTE_ROWS, tm), lambda i: (0, i)),
                   pl.BlockSpec((tm, LANES), lambda i: (i, 0)),
                   pl.BlockSpec((N_EXPERTS, LANES), lambda i: (0, 0))],
        out_shape=[jax.ShapeDtypeStruct((ROUTE_ROWS, t), jnp.int32),
                   jax.ShapeDtypeStruct((t, LANES), F32),
                   jax.ShapeDtypeStruct((N_EXPERTS, LANES), F32)],
        scratch_shapes=[pltpu.VMEM((N_EXPERTS, LANES), F32), pltpu.VMEM((tm, tm), BF16)],
        compiler_params=_params("arbitrary"),
        name="moe_route",
    )(x, g, router_t)


def _moe_slots_kernel(base_ref, route_ref, slot_ref):
    r = route_ref[...]
    slot_ref[...] = jnp.zeros_like(slot_ref)
    for k in range(2):
        e, rank = r[2 * k:2 * k + 1, :], r[2 * k + 1:2 * k + 2, :]
        start = jnp.zeros_like(e)
        for j in range(N_EXPERTS):
            start = jnp.where(e == j, base_ref[j], start)
        slot_ref[k:k + 1, :] = start + rank


def _moe_slots(route, base, tm=2048):
    t = route.shape[1]
    return pl.pallas_call(
        _moe_slots_kernel,
        grid_spec=pltpu.PrefetchScalarGridSpec(
            num_scalar_prefetch=1,
            grid=(t // tm,),
            in_specs=[pl.BlockSpec((ROUTE_ROWS, tm), lambda i, *_: (0, i))],
            out_specs=pl.BlockSpec((ROUTE_ROWS, tm), lambda i, *_: (0, i))),
        out_shape=jax.ShapeDtypeStruct((ROUTE_ROWS, t), jnp.int32),
        compiler_params=_params("parallel"),
        name="moe_slots",
    )(base, route)


DMA_UNROLL = 8
COMBINE_ROWS = 128


def _moe_dispatch_kernel(tm, fs_ref, fe_ref, slot_ref, x_ref, xs_hbm, sem):
    i = pl.program_id(0)

    def row_copy(src_row, dst_row):
        return pltpu.make_async_copy(x_ref.at[pl.ds(src_row, 1)], xs_hbm.at[pl.ds(dst_row, 1)], sem)

    def issue(t, c):
        for k in range(2):
            row_copy(t, slot_ref[k, t]).start()
        return c

    lax.fori_loop(0, tm, issue, 0, unroll=DMA_UNROLL)
    for k in range(2):
        pltpu.make_async_copy(x_ref, xs_hbm.at[pl.ds(0, tm)], sem).wait()

    @pl.when(i == pl.num_programs(0) - 1)
    def _():
        for e in range(N_EXPERTS + 1):
            def fill(p, c):
                row_copy(0, p).start()
                return c

            def drain(p, c):
                row_copy(0, p).wait()
                return c

            lax.fori_loop(fs_ref[e], fe_ref[e], fill, 0)
            lax.fori_loop(fs_ref[e], fe_ref[e], drain, 0)


def _moe_dispatch(x, slots, fill_start, fill_end, n_slots, tm=1024):
    t = x.shape[0]
    return pl.pallas_call(
        functools.partial(_moe_dispatch_kernel, tm),
        grid_spec=pltpu.PrefetchScalarGridSpec(
            num_scalar_prefetch=2,
            grid=(t // tm,),
            in_specs=[pl.BlockSpec((ROUTE_ROWS, tm), lambda i, *_: (0, i), memory_space=pltpu.SMEM),
                      pl.BlockSpec((tm, D_MODEL), lambda i, *_: (i, 0))],
            out_specs=pl.BlockSpec(memory_space=pl.ANY),
            scratch_shapes=[pltpu.SemaphoreType.DMA(())]),
        out_shape=jax.ShapeDtypeStruct((n_slots, D_MODEL), F32),
        compiler_params=_params("arbitrary"),
        name="moe_dispatch",
    )(fill_start, fill_end, slots, x)


def _ffn_grouped_kernel(te_ref, nv_ref, x_ref, g_ref, wg_ref, wu_ref, wd_ref, y_ref):
    i = pl.program_id(0)

    @pl.when(i < nv_ref[0])
    def _():
        h = _rmsnorm(x_ref[...], g_ref[...]).astype(BF16)
        gate = jnp.dot(h, wg_ref[0, 0], preferred_element_type=F32)
        up = jnp.dot(h, wu_ref[0, 0], preferred_element_type=F32)
        y_ref[...] = jnp.dot((_silu(gate) * up).astype(BF16), wd_ref[0, 0], preferred_element_type=F32)

    @pl.when(i >= nv_ref[0])
    def _():
        y_ref[...] = jnp.zeros_like(y_ref)


def _ffn_grouped(xs, g, wg, wu, wd, layer, tile_expert, n_valid):
    n_slots = xs.shape[0]
    f = wg.shape[3]
    rows = pl.BlockSpec((MOE_TILE, D_MODEL), lambda i, te, nv: (i, 0))
    return pl.pallas_call(
        _ffn_grouped_kernel,
        grid_spec=pltpu.PrefetchScalarGridSpec(
            num_scalar_prefetch=2,
            grid=(n_slots // MOE_TILE,),
            in_specs=[rows,
                      pl.BlockSpec((1, D_MODEL), lambda i, te, nv: (0, 0)),
                      pl.BlockSpec((1, 1, D_MODEL, f), lambda i, te, nv: (layer, te[i], 0, 0)),
                      pl.BlockSpec((1, 1, D_MODEL, f), lambda i, te, nv: (layer, te[i], 0, 0)),
                      pl.BlockSpec((1, 1, f, D_MODEL), lambda i, te, nv: (layer, te[i], 0, 0))],
            out_specs=rows),
        out_shape=jax.ShapeDtypeStruct((n_slots, D_MODEL), F32),
        compiler_params=_params("arbitrary"),
        name="ffn_grouped",
    )(tile_expert, n_valid, xs, g, wg, wu, wd)


def _moe_combine_kernel(tm, final, slot_ref, x_ref, gate_ref, *rest):
    if final:
        gf_ref, rest = rest[0], rest[1:]
    ys_hbm, o_ref, y1_ref, y2_ref, sems = rest
    bufs = (y1_ref, y2_ref)

    def issue(t, c):
        for k in range(2):
            pltpu.make_async_copy(ys_hbm.at[pl.ds(slot_ref[k, t], 1)], bufs[k].at[pl.ds(t, 1)],
                                  sems.at[k]).start()
        return c

    lax.fori_loop(0, tm, issue, 0, unroll=DMA_UNROLL)
    for k in range(2):
        pltpu.make_async_copy(ys_hbm.at[pl.ds(0, tm)], bufs[k], sems.at[k]).wait()
    rc = COMBINE_ROWS

    def rows(ci, c):
        r = pl.ds(pl.multiple_of(ci * rc, rc), rc)
        g1 = jnp.broadcast_to(gate_ref[r, 0:1], (rc, LANES))
        g2 = jnp.broadcast_to(gate_ref[r, 1:2], (rc, LANES))
        ssq = jnp.zeros((rc, LANES), F32)
        for j in range(D_MODEL // LANES):
            cols = slice(j * LANES, (j + 1) * LANES)
            out = x_ref[r, cols] + g1 * y1_ref[r, cols] + g2 * y2_ref[r, cols]
            o_ref[r, cols] = out
            ssq = ssq + out * out
        if final:
            scale = lax.rsqrt(jnp.sum(ssq, axis=-1, keepdims=True) * (1.0 / D_MODEL) + EPS)
            for j in range(D_MODEL // LANES):
                cols = slice(j * LANES, (j + 1) * LANES)
                o_ref[r, cols] = o_ref[r, cols] * scale * gf_ref[:, cols]
        return c

    lax.fori_loop(0, tm // rc, rows, 0)


def _moe_combine(x, slots, gates, ys, final_g=None, tm=512):
    t = x.shape[0]
    final = final_g is not None
    in_specs = [pl.BlockSpec((ROUTE_ROWS, tm), lambda i: (0, i), memory_space=pltpu.SMEM),
                pl.BlockSpec((tm, D_MODEL), lambda i: (i, 0)),
                pl.BlockSpec((tm, LANES), lambda i: (i, 0))]
    args = [slots, x, gates]
    if final:
        in_specs.append(pl.BlockSpec((1, D_MODEL), lambda i: (0, 0)))
        args.append(final_g)
    in_specs.append(pl.BlockSpec(memory_space=pl.ANY))
    args.append(ys)
    return pl.pallas_call(
        functools.partial(_moe_combine_kernel, tm, final),
        grid=(t // tm,),
        in_specs=in_specs,
        out_specs=pl.BlockSpec((tm, D_MODEL), lambda i: (i, 0)),
        out_shape=jax.ShapeDtypeStruct((t, D_MODEL), F32),
        scratch_shapes=[pltpu.VMEM((tm, D_MODEL), F32), pltpu.VMEM((tm, D_MODEL), F32),
                        pltpu.SemaphoreType.DMA((2,))],
        compiler_params=_params("arbitrary"),
        name="moe_combine",
    )(*args)


def _moe_routed(x, g, router_t, wg, wu, wd, layer, final_g=None):
    t = x.shape[0]
    n_tiles = 2 * t // MOE_TILE + N_EXPERTS
    route, gates, counts = _moe_route(x, g, router_t)
    cnt = counts[:, 0].astype(jnp.int32)
    caps = (cnt + MOE_TILE - 1) // MOE_TILE
    cum = jnp.cumsum(caps)
    base = (cum - caps) * MOE_TILE
    n_valid = cum[-1:]
    tile_expert = jnp.minimum(
        jnp.sum((cum[None, :] <= jnp.arange(n_tiles, dtype=jnp.int32)[:, None]).astype(jnp.int32), axis=1),
        N_EXPERTS - 1)
    fill_start = jnp.concatenate([base + cnt, n_valid * MOE_TILE])
    fill_end = jnp.concatenate([base + caps * MOE_TILE, jnp.full((1,), n_tiles * MOE_TILE, jnp.int32)])
    slots = _moe_slots(route, base)
    xs = _moe_dispatch(x, slots, fill_start, fill_end, n_tiles * MOE_TILE)
    ys = _ffn_grouped(xs, g, wg, wu, wd, layer, tile_expert, n_valid)
    return _moe_combine(x, slots, gates, ys, final_g)


def _trunk(x, state_hgrn, state_pool, start_pos, w, seq, tm):
    batch = x.shape[0] // seq
    s_out, b_out = [], []
    for l in range(DEPTH):
        act = _mixer_in(x, w["norm_mix"][l], w["w_in"], w["lb_logits"], l, tm)
        if seq > 1:
            og, s_new = _hgrn_prompt(act, w["hg_norm"][l], batch, seq)
            x, b_new = _mixer_out_pool(x, og, act, w["w_o"], l, w["pool_w"][l], w["pool_scale"][l], batch, seq)
        else:
            og, s_new = _hgrn_step(act, w["hg_norm"][l], state_hgrn, l)
            yp, b_new = _pool_step(act, state_pool, l, w["pool_w"][l], w["pool_scale"][l], start_pos)
            x = _mixer_out(x, og, yp, w["w_o"], l, tm)
        s_out.append(s_new)
        b_out.append(b_new)
        routed = l % 2 == 1 and seq > 1
        j = l // 2
        final_g = w["norm_final"] if l == DEPTH - 1 else None
        if l % 2 == 0:
            x = _ffn(x, w["norm_ffn"][l], w["ffn_w_gate"], w["ffn_w_up"], w["ffn_w_down"], j, tm, 1408,
                     final_g=final_g)
        elif routed:
            x = _moe_routed(x, w["norm_ffn"][l], w["router_t"][j], w["moe_w_gate"], w["moe_w_up"],
                            w["moe_w_down"], j, final_g=final_g)
        else:
            x = _ffn(x, w["norm_ffn"][l], w["moe_w_gate"], w["moe_w_up"], w["moe_w_down"], j, tm, 1408,
                     router=w["router"][j], final_g=final_g)
    return x, jnp.stack(s_out), jnp.stack(b_out)


def kernel(x_prompt, x_sample, state_hgrn, state_pool, lb_logits, norm_mix, w_in, w_o, hg_norm, pool_w,
           pool_scale, norm_ffn, ffn_w_gate, ffn_w_up, ffn_w_down, router, moe_w_gate, moe_w_up, moe_w_down,
           norm_final):
    batch, seq, _ = x_prompt.shape
    dec_batch, dec_seq, _ = x_sample.shape
    assert dec_seq == 1
    past_len = 16384
    w = dict(
        lb_logits=lb_logits,
        norm_mix=norm_mix.reshape(DEPTH, 1, D_MODEL),
        w_in=w_in,
        w_o=w_o,
        hg_norm=hg_norm.reshape(DEPTH, 1, HG_WIDTH),
        pool_w=pool_w.astype(BF16),
        pool_scale=pool_scale.reshape(DEPTH, 1, POOL_WIDTH),
        norm_ffn=norm_ffn.reshape(DEPTH, 1, D_MODEL),
        ffn_w_gate=ffn_w_gate.astype(BF16)[:, None],
        ffn_w_up=ffn_w_up.astype(BF16)[:, None],
        ffn_w_down=ffn_w_down.astype(BF16)[:, None],
        router=jnp.pad(router, ((0, 0), (0, 0), (0, LANES - N_EXPERTS))),
        router_t=jnp.swapaxes(router, 1, 2),
        moe_w_gate=moe_w_gate.astype(BF16),
        moe_w_up=moe_w_up.astype(BF16),
        moe_w_down=moe_w_down.astype(BF16),
        norm_final=norm_final.reshape(1, D_MODEL),
    )
    yp, sp, bp = _trunk(x_prompt.reshape(batch * seq, D_MODEL), None, None, 0, w, seq, 512)
    ys, ss, bs = _trunk(x_sample.reshape(dec_batch, D_MODEL), state_hgrn, state_pool, past_len, w, 1, 128)
    return (yp.reshape(batch, seq, D_MODEL), ys.reshape(dec_batch, 1, D_MODEL), sp, ss, bp, bs)
```

```python
import functools

import jax
import jax.numpy as jnp
from jax import lax
from jax.experimental import pallas as pl
from jax.experimental.pallas import tpu as pltpu

F32 = jnp.float32
BF16 = jnp.bfloat16

D_MODEL = 1024
DEPTH = 4
HG_WIDTH = 512
HG_HEADS = 4
HG_D = 128
POOL_WIDTH = 512
POOL_WINDOWS = (2, 4, 8, 16)
POOL_GROUP_W = 128
POOL_BUF = 15
IN_WIDTH = 4 * HG_WIDTH + POOL_WIDTH
ACT_WIDTH = 4 * HG_WIDTH + POOL_WIDTH
ACT_POOL_PART = 4
N_EXPERTS = 8
EPS = 1e-6
LANES = 128
SUBLANES = 8
VMEM_LIMIT = 56 * 1024 * 1024
HGRN_CHUNK = 128


def _params(*sem):
    return pltpu.CompilerParams(dimension_semantics=sem, vmem_limit_bytes=VMEM_LIMIT)


def _rmsnorm(x, g):
    return x * lax.rsqrt(jnp.mean(x * x, axis=-1, keepdims=True) + EPS) * g


def _silu(x):
    return x * jax.nn.sigmoid(x)


def _mixer_in_kernel(layer, x_ref, g_ref, w_ref, lbl_ref, o_ref, wb_ref):
    @pl.when(pl.program_id(0) == 0)
    def _():
        wb_ref[...] = w_ref[0].astype(BF16)

    h = _rmsnorm(x_ref[...], g_ref[...])
    p = jnp.dot(h.astype(BF16), wb_ref[...], preferred_element_type=F32)
    lg = lbl_ref[...]
    e = jnp.exp(lg - jnp.max(lg, axis=0, keepdims=True))
    pr = e / jnp.sum(e, axis=0, keepdims=True)
    cum = pr[0:1]
    for j in range(1, layer + 1):
        cum = cum + pr[j:j + 1]
    lb = cum - pr[0:1]
    w = HG_WIDTH
    q, fx, ix, g, u = p[:, :w], p[:, w:2 * w], p[:, 2 * w:3 * w], p[:, 3 * w:4 * w], p[:, 4 * w:]
    f = lb + (1.0 - lb) * jax.nn.sigmoid(fx)
    o_ref[:, 0:w] = _silu(q)
    o_ref[:, w:2 * w] = jnp.log(f)
    o_ref[:, 2 * w:3 * w] = ix
    o_ref[:, 3 * w:4 * w] = _silu(g)
    o_ref[:, 4 * w:] = u


def _mixer_in(x, g, w_in, lb_logits, layer, tm):
    t = x.shape[0]
    return pl.pallas_call(
        functools.partial(_mixer_in_kernel, layer),
        grid=(t // tm,),
        in_specs=[
            pl.BlockSpec((tm, D_MODEL), lambda i: (i, 0)),
            pl.BlockSpec((1, D_MODEL), lambda i: (0, 0)),
            pl.BlockSpec((1, D_MODEL, IN_WIDTH), lambda i: (layer, 0, 0)),
            pl.BlockSpec((DEPTH, HG_WIDTH), lambda i: (0, 0)),
        ],
        out_specs=pl.BlockSpec((tm, ACT_WIDTH), lambda i: (i, 0)),
        out_shape=jax.ShapeDtypeStruct((t, ACT_WIDTH), F32),
        scratch_shapes=[pltpu.VMEM((D_MODEL, IN_WIDTH), BF16)],
        compiler_params=_params("arbitrary"),
        name="mixer_in",
    )(x, g, w_in, lb_logits)


SMALL_LEVELS = (2, 4, 8)


def _level_sum_matrix(n, ti, si):
    half = n // 2
    mid = (ti & (-n)) + (half - 1)
    lo = jnp.where((ti & half) != 0, mid, ti)
    hi = jnp.where((ti & half) != 0, ti, mid)
    return (si > lo) & (si <= hi)


def _level_factors(q, k, b, n, row, arg=None):
    c = b.shape[0]
    half = n // 2
    if arg is None:
        zero = jnp.zeros((half, b.shape[1]), F32)
        qs, ks = [], []
        for r0 in range(0, c, n):
            mid = b[r0 + half - 1:r0 + half, :]
            ks += [k[r0:r0 + half] * jnp.exp(mid - b[r0:r0 + half]), zero]
            qs += [zero, q[r0 + half:r0 + n] * jnp.exp(b[r0 + half:r0 + n] - mid)]
        return jnp.concatenate(qs, axis=0), jnp.concatenate(ks, axis=0)
    right = (row & half) != 0
    qk = jnp.where(right, q, k) * jnp.exp(arg)
    return jnp.where(right, qk, 0.0), jnp.where(right, 0.0, qk)


def _split3(x):
    hi = x.astype(BF16)
    r1 = x - hi.astype(F32)
    mid = r1.astype(BF16)
    lo = (r1 - mid.astype(F32)).astype(BF16)
    return jnp.concatenate([hi, mid, lo], axis=0)


HEAD_PAIRS = HG_HEADS // 2
PAIR_W = 2 * HG_D


def _pair_blocks(a):
    zero = jnp.zeros((a.shape[0], HG_D), a.dtype)
    return jnp.concatenate([jnp.concatenate([a[:, :HG_D], zero], axis=1),
                            jnp.concatenate([zero, a[:, HG_D:]], axis=1)], axis=0)


def _per_head(fn, a):
    return jnp.concatenate([fn(a[:, :HG_D]), fn(a[:, HG_D:])], axis=1)


def _hgrn_prompt_kernel(tl, q_ref, lf_ref, v_ref, g_ref, nrm_ref, o_ref, s_ref, st_ref):
    c = HGRN_CHUNK
    li = pl.program_id(1)

    @pl.when(li == 0)
    def _():
        st_ref[...] = jnp.zeros_like(st_ref)

    row = lax.broadcasted_iota(jnp.int32, (c, PAIR_W), 0)
    ti = lax.broadcasted_iota(jnp.int32, (c, c), 0)
    si = lax.broadcasted_iota(jnp.int32, (c, c), 1)
    sums = jnp.concatenate(
        [jnp.where(m, 1.0, 0.0)
         for m in [si <= ti] + [_level_sum_matrix(n, ti, si) for n in SMALL_LEVELS]], axis=0).astype(BF16)
    sums = jnp.concatenate([sums] * 3, axis=1)
    apart = jnp.concatenate([ti ^ si] * 2, axis=1)
    pi = lax.broadcasted_iota(jnp.int32, (PAIR_W, PAIR_W), 0)
    pj = lax.broadcasted_iota(jnp.int32, (PAIR_W, PAIR_W), 1)
    own = (pi < HG_D) == (pj < HG_D)
    nt = (((1,), (1,)), ((), ()))
    tn = (((0,), (0,)), ((), ()))

    def chunk(ci, carry):
        rows = pl.ds(pl.multiple_of(ci * c, c), c)
        for p in range(HEAD_PAIRS):
            cols = slice(p * PAIR_W, (p + 1) * PAIR_W)
            q, lf, v = q_ref[rows, cols], lf_ref[rows, cols], v_ref[rows, cols]
            k = 1.0 - jnp.exp(lf)
            s1 = jnp.dot(sums, _split3(lf), preferred_element_type=F32)
            b = s1[:c]
            args = {n: s1[(i + 1) * c:(i + 2) * c] for i, n in enumerate(SMALL_LEVELS)}
            vbd = _pair_blocks(v.astype(BF16))
            scores = None
            n = c
            while n >= 2:
                qt, kt = _level_factors(q, k, b, n, row, args.get(n))
                s_n = lax.dot_general(qt.astype(BF16), _pair_blocks(kt.astype(BF16)), nt,
                                      preferred_element_type=F32)
                scores = s_n if scores is None else jnp.where(apart < n, s_n, scores)
                n //= 2
            st = st_ref[p]
            qs = (q * jnp.exp(b)).astype(BF16)
            o = jnp.dot(scores.astype(BF16), vbd, preferred_element_type=F32)
            o = o + lax.dot_general(qs, st.astype(BF16), nt, preferred_element_type=F32)
            o = o + _per_head(lambda a: jnp.broadcast_to(jnp.sum(a, axis=-1, keepdims=True), a.shape), q * k) * v
            bc = b[c - 1:c, :]
            kd = (k * jnp.exp(bc - b)).astype(BF16)
            upd = lax.dot_general(v.astype(BF16), kd, tn, preferred_element_type=F32)
            st_ref[p] = st * jnp.exp(bc) + jnp.where(own, upd, 0.0)
            ms = _per_head(lambda a: jnp.broadcast_to(jnp.mean(a, axis=-1, keepdims=True), a.shape), o * o)
            o = o * lax.rsqrt(ms + EPS) * nrm_ref[:, cols]
            o_ref[rows, cols] = (o * g_ref[rows, cols]).astype(o_ref.dtype)
        return carry

    lax.fori_loop(0, tl // c, chunk, 0, unroll=2)

    @pl.when(li == pl.num_programs(1) - 1)
    def _():
        for p in range(HEAD_PAIRS):
            st = st_ref[p]
            s_ref[0, 2 * p] = st[:HG_D, :HG_D].T
            s_ref[0, 2 * p + 1] = st[HG_D:, HG_D:].T


def _hgrn_prompt(act, hg_norm, batch, seq, tl=1024):
    nl = seq // tl
    blk = lambda part: pl.BlockSpec((tl, HG_WIDTH), lambda b, i: (b * nl + i, part))
    return pl.pallas_call(
        functools.partial(_hgrn_prompt_kernel, tl),
        grid=(batch, nl),
        in_specs=[blk(0), blk(1), blk(2), blk(3),
                  pl.BlockSpec((1, HG_WIDTH), lambda b, i: (0, 0))],
        out_specs=[pl.BlockSpec((tl, HG_WIDTH), lambda b, i: (b * nl + i, 0)),
                   pl.BlockSpec((1, HG_HEADS, HG_D, HG_D), lambda b, i: (b, 0, 0, 0))],
        out_shape=[jax.ShapeDtypeStruct((batch * seq, HG_WIDTH), BF16),
                   jax.ShapeDtypeStruct((batch, HG_HEADS, HG_D, HG_D), F32)],
        scratch_shapes=[pltpu.VMEM((HEAD_PAIRS, PAIR_W, PAIR_W), F32)],
        compiler_params=_params("parallel", "arbitrary"),
        name="hgrn_prompt",
    )(act, act, act, act, hg_norm)


HGRN_STEP_BT = 16


def _hgrn_step_kernel(q_ref, lf_ref, v_ref, g_ref, nrm_ref, s_ref, o_ref, so_ref):
    bt = HGRN_STEP_BT
    q = q_ref[...]
    v = v_ref[...]
    f = jnp.exp(lf_ref[...])
    ri = lax.broadcasted_iota(jnp.int32, (3 * bt, PAIR_W), 0) % bt
    half = lax.broadcasted_iota(jnp.int32, (3 * bt, PAIR_W), 1) // HG_D
    tn = (((0,), (0,)), ((), ()))
    pieces = [_split3(a) for a in (f, 1.0 - f, q)]
    rows = []
    for s in range(0, bt, 2):
        pick = jnp.where(ri == s + half, 1.0, 0.0).astype(BF16)
        fc, kc, qc = [lax.dot_general(p, pick, tn, preferred_element_type=F32) for p in pieces]
        for d in range(2):
            cols = slice(d * HG_D, (d + 1) * HG_D)
            sn = fc[:, cols] * s_ref[0, s + d, 0] + kc[:, cols] * v[s + d:s + d + 1, :]
            so_ref[s + d, 0] = sn
            rows.append(jnp.sum(sn * qc[:, cols], axis=0, keepdims=True))
    o = jnp.concatenate(rows, axis=0)
    o = o * lax.rsqrt(jnp.mean(o * o, axis=-1, keepdims=True) + EPS) * nrm_ref[...]
    o_ref[...] = (o * g_ref[...]).astype(o_ref.dtype)


def _hgrn_step(act, hg_norm, states, layer):
    batch = act.shape[0]
    bt = HGRN_STEP_BT
    blk = lambda part: pl.BlockSpec((bt, HG_D), lambda i, h: (i, part * HG_HEADS + h))
    return pl.pallas_call(
        _hgrn_step_kernel,
        grid=(batch // bt, HG_HEADS),
        in_specs=[blk(0), blk(1), blk(2), blk(3),
                  pl.BlockSpec((1, HG_D), lambda i, h: (0, h)),
                  pl.BlockSpec((1, bt, 1, HG_D, HG_D), lambda i, h: (layer, i, h, 0, 0))],
        out_specs=[pl.BlockSpec((bt, HG_D), lambda i, h: (i, h)),
                   pl.BlockSpec((bt, 1, HG_D, HG_D), lambda i, h: (i, h, 0, 0))],
        out_shape=[jax.ShapeDtypeStruct((batch, HG_WIDTH), BF16),
                   jax.ShapeDtypeStruct(states.shape[1:], F32)],
        compiler_params=_params("parallel", "parallel"),
        name="hgrn_step",
    )(act, act, act, act, hg_norm, states)


def _pool_map(d_groups, wp_ref, scale_ref, o_ref):
    for gi, d in enumerate(d_groups):
        sl = slice(gi * POOL_GROUP_W, (gi + 1) * POOL_GROUP_W)
        y = jnp.dot(d.astype(BF16), wp_ref[gi], preferred_element_type=F32)
        o_ref[:, sl] = (y * scale_ref[:, sl]).astype(o_ref.dtype)


POOL_HIST = 16


def _pool_tile(i, tl, z, carry_ref, wp_ref, scale_ref, o_ref, nb_ref):
    @pl.when(i == 0)
    def _():
        carry_ref[...] = jnp.zeros_like(carry_ref)

    ext = jnp.concatenate([carry_ref[...], z], axis=0)
    sums = {1: ext}
    w = 1
    while w < max(POOL_WINDOWS):
        sums[2 * w] = sums[w] + pltpu.roll(sums[w], w, 0)
        w *= 2
    pos = i * tl + lax.broadcasted_iota(jnp.int32, (tl, POOL_GROUP_W), 0)
    ds = []
    for gi, w in enumerate(POOL_WINDOWS):
        sl = slice(gi * POOL_GROUP_W, (gi + 1) * POOL_GROUP_W)
        cnt = jnp.minimum(pos + 1, w).astype(F32)
        ds.append(sums[w][POOL_HIST:, sl] / cnt - z[:, sl])
    _pool_map(ds, wp_ref, scale_ref, o_ref)
    carry_ref[...] = z[tl - POOL_HIST:, :]

    @pl.when(i == pl.num_programs(1) - 1)
    def _():
        nb_ref[0] = z[tl - POOL_BUF:, :]


def _mixer_out_pool_kernel(tl, x_ref, o_ref, u_ref, w_ref, wp_ref, scale_ref, y_ref, nb_ref,
                           wb_ref, carry_ref, yp_ref):
    b, i = pl.program_id(0), pl.program_id(1)

    @pl.when((b == 0) & (i == 0))
    def _():
        wb_ref[...] = w_ref[0].astype(BF16)

    _pool_tile(i, tl, u_ref[...], carry_ref, wp_ref, scale_ref, yp_ref, nb_ref)
    y = jnp.dot(o_ref[...], wb_ref[0:HG_WIDTH, :], preferred_element_type=F32)
    y = y + jnp.dot(yp_ref[...], wb_ref[HG_WIDTH:, :], preferred_element_type=F32)
    y_ref[...] = x_ref[...] + y


def _mixer_out_pool(x, og, act, w_o, layer, pool_w, pool_scale, batch, seq, tl=512):
    nl = seq // tl
    tok = lambda cols: (lambda b, i: (b * nl + i, cols))
    return pl.pallas_call(
        functools.partial(_mixer_out_pool_kernel, tl),
        grid=(batch, nl),
        in_specs=[pl.BlockSpec((tl, D_MODEL), tok(0)),
                  pl.BlockSpec((tl, HG_WIDTH), tok(0)),
                  pl.BlockSpec((tl, POOL_WIDTH), tok(ACT_POOL_PART)),
                  pl.BlockSpec((1, D_MODEL, D_MODEL), lambda b, i: (layer, 0, 0)),
                  pl.BlockSpec((len(POOL_WINDOWS), POOL_GROUP_W, POOL_GROUP_W), lambda b, i: (0, 0, 0)),
                  pl.BlockSpec((1, POOL_WIDTH), lambda b, i: (0, 0))],
        out_specs=[pl.BlockSpec((tl, D_MODEL), tok(0)),
                   pl.BlockSpec((1, POOL_BUF, POOL_WIDTH), lambda b, i: (b, 0, 0))],
        out_shape=[jax.ShapeDtypeStruct((batch * seq, D_MODEL), F32),
                   jax.ShapeDtypeStruct((batch, POOL_BUF, POOL_WIDTH), F32)],
        scratch_shapes=[pltpu.VMEM((D_MODEL, D_MODEL), BF16), pltpu.VMEM((POOL_HIST, POOL_WIDTH), F32),
                        pltpu.VMEM((tl, POOL_WIDTH), BF16)],
        compiler_params=_params("arbitrary", "arbitrary"),
        name="mixer_out_pool",
    )(x, og, act, w_o, pool_w, pool_scale)


POOL_STEP_BT = 16


def _pool_step_kernel(start_pos, u_ref, buf_ref, wp_ref, scale_ref, o_ref, nb_ref):
    u = u_ref[...]
    buf = buf_ref[0]
    ds = []
    for gi, w in enumerate(POOL_WINDOWS):
        sl = slice(gi * POOL_GROUP_W, (gi + 1) * POOL_GROUP_W)
        tot = u[:, sl] + jnp.sum(buf[:, POOL_BUF - (w - 1):, sl], axis=1)
        ds.append(tot / float(min(start_pos + 1, w)) - u[:, sl])
    _pool_map(ds, wp_ref, scale_ref, o_ref)
    nb_ref[:, 0:POOL_BUF - 1, :] = buf[:, 1:, :]
    nb_ref[:, POOL_BUF - 1:, :] = u[:, None, :]


def _pool_step(act, bufs, layer, pool_w, pool_scale, start_pos):
    batch = act.shape[0]
    bt = POOL_STEP_BT
    return pl.pallas_call(
        functools.partial(_pool_step_kernel, start_pos),
        grid=(batch // bt,),
        in_specs=[pl.BlockSpec((bt, POOL_WIDTH), lambda i: (i, ACT_POOL_PART)),
                  pl.BlockSpec((1, bt, POOL_BUF, POOL_WIDTH), lambda i: (layer, i, 0, 0)),
                  pl.BlockSpec((len(POOL_WINDOWS), POOL_GROUP_W, POOL_GROUP_W), lambda i: (0, 0, 0)),
                  pl.BlockSpec((1, POOL_WIDTH), lambda i: (0, 0))],
        out_specs=[pl.BlockSpec((bt, POOL_WIDTH), lambda i: (i, 0)),
                   pl.BlockSpec((bt, POOL_BUF, POOL_WIDTH), lambda i: (i, 0, 0))],
        out_shape=[jax.ShapeDtypeStruct((batch, POOL_WIDTH), BF16),
                   jax.ShapeDtypeStruct(bufs.shape[1:], F32)],
        compiler_params=_params("parallel"),
        name="pool_step",
    )(act, bufs, pool_w, pool_scale)


def _mixer_out_kernel(x_ref, o_ref, p_ref, w_ref, y_ref, wb_ref):
    @pl.when(pl.program_id(0) == 0)
    def _():
        wb_ref[...] = w_ref[0].astype(BF16)

    y = jnp.dot(o_ref[...], wb_ref[0:HG_WIDTH, :], preferred_element_type=F32)
    y = y + jnp.dot(p_ref[...], wb_ref[HG_WIDTH:, :], preferred_element_type=F32)
    y_ref[...] = x_ref[...] + y


def _mixer_out(x, og, yp, w_o, layer, tm):
    t = x.shape[0]
    return pl.pallas_call(
        _mixer_out_kernel,
        grid=(t // tm,),
        in_specs=[pl.BlockSpec((tm, D_MODEL), lambda i: (i, 0)),
                  pl.BlockSpec((tm, HG_WIDTH), lambda i: (i, 0)),
                  pl.BlockSpec((tm, POOL_WIDTH), lambda i: (i, 0)),
                  pl.BlockSpec((1, D_MODEL, D_MODEL), lambda i: (layer, 0, 0))],
        out_specs=pl.BlockSpec((tm, D_MODEL), lambda i: (i, 0)),
        out_shape=jax.ShapeDtypeStruct((t, D_MODEL), F32),
        scratch_shapes=[pltpu.VMEM((D_MODEL, D_MODEL), BF16)],
        compiler_params=_params("arbitrary"),
        name="mixer_out",
    )(x, og, yp, w_o)


def _top2_combine(logits):
    lane = lax.broadcasted_iota(jnp.int32, logits.shape, 1).astype(F32)
    neg = jnp.float32(-jnp.inf)
    lg = jnp.where(lane < N_EXPERTS, logits, neg)
    m1 = jnp.max(lg, axis=-1, keepdims=True)
    i1 = jnp.min(jnp.where(lg == m1, lane, float(LANES)), axis=-1, keepdims=True)
    lg2 = jnp.where(lane == i1, neg, lg)
    m2 = jnp.max(lg2, axis=-1, keepdims=True)
    i2 = jnp.min(jnp.where(lg2 == m2, lane, float(LANES)), axis=-1, keepdims=True)
    e2 = jnp.exp(m2 - m1)
    g1 = 1.0 / (1.0 + e2)
    g2 = e2 / (1.0 + e2)
    return jnp.where(lane == i1, g1, 0.0) + jnp.where(lane == i2, g2, 0.0)


def _ffn_kernel(moe, final, x_ref, g_ref, *rest):
    if moe:
        r_ref, rest = rest[0], rest[1:]
    if final:
        gf_ref, rest = rest[0], rest[1:]
    wg_ref, wu_ref, wd_ref, y_ref, h_ref, acc_ref = rest[:6]
    e, j = pl.program_id(1), pl.program_id(2)
    first = (e == 0) & (j == 0)
    last = (e == pl.num_programs(1) - 1) & (j == pl.num_programs(2) - 1)

    @pl.when(first)
    def _():
        h = _rmsnorm(x_ref[...], g_ref[...])
        h_ref[...] = h.astype(BF16)
        acc_ref[...] = jnp.zeros_like(acc_ref)
        if moe:
            logits = jnp.dot(h, r_ref[...], precision=lax.Precision.HIGHEST, preferred_element_type=F32)
            rest[6][...] = _top2_combine(logits)

    h = h_ref[...]
    gate = jnp.dot(h, wg_ref[0, 0], preferred_element_type=F32)
    up = jnp.dot(h, wu_ref[0, 0], preferred_element_type=F32)
    y = jnp.dot((_silu(gate) * up).astype(BF16), wd_ref[0, 0], preferred_element_type=F32)
    if moe:
        comb = rest[6][...]
        lane = lax.broadcasted_iota(jnp.int32, comb.shape, 1)
        y = jnp.sum(jnp.where(lane == e, comb, 0.0), axis=-1, keepdims=True) * y
    acc_ref[...] += y

    @pl.when(last)
    def _():
        out = x_ref[...] + acc_ref[...]
        if final:
            out = _rmsnorm(out, gf_ref[...])
        y_ref[...] = out


def _ffn(x, g, wg, wu, wd, layer, tm, tf, router=None, final_g=None):
    t = x.shape[0]
    _, ne, _, f = wg.shape
    moe, final = router is not None, final_g is not None
    vec = pl.BlockSpec((1, D_MODEL), lambda i, e, j: (0, 0))
    in_specs = [pl.BlockSpec((tm, D_MODEL), lambda i, e, j: (i, 0)), vec]
    args = [x, g]
    scratch = [pltpu.VMEM((tm, D_MODEL), BF16), pltpu.VMEM((tm, D_MODEL), F32)]
    if moe:
        in_specs.append(pl.BlockSpec((D_MODEL, LANES), lambda i, e, j: (0, 0)))
        args.append(router)
        scratch.append(pltpu.VMEM((tm, LANES), F32))
    if final:
        in_specs.append(vec)
        args.append(final_g)
    in_specs += [pl.BlockSpec((1, 1, D_MODEL, tf), lambda i, e, j: (layer, e, 0, j)),
                 pl.BlockSpec((1, 1, D_MODEL, tf), lambda i, e, j: (layer, e, 0, j)),
                 pl.BlockSpec((1, 1, tf, D_MODEL), lambda i, e, j: (layer, e, j, 0))]
    args += [wg, wu, wd]
    return pl.pallas_call(
        functools.partial(_ffn_kernel, moe, final),
        grid=(t // tm, ne, f // tf),
        in_specs=in_specs,
        out_specs=pl.BlockSpec((tm, D_MODEL), lambda i, e, j: (i, 0)),
        out_shape=jax.ShapeDtypeStruct((t, D_MODEL), F32),
        scratch_shapes=scratch,
        compiler_params=_params("parallel", "arbitrary", "arbitrary"),
        name="ffn_moe" if moe else "ffn_dense",
    )(*args)


MOE_TILE = 512
ROUTE_ROWS = 8
NT_DIMS = (((1,), (1,)), ((), ()))


def _moe_route_kernel(x_ref, g_ref, rt_ref, route_ref, gate_ref, cnt_ref, carry_ref, earlier_ref):
    tm = x_ref.shape[0]

    @pl.when(pl.program_id(0) == 0)
    def _():
        carry_ref[...] = jnp.zeros_like(carry_ref)
        t0 = lax.broadcasted_iota(jnp.int32, (tm, tm), 0)
        t1 = lax.broadcasted_iota(jnp.int32, (tm, tm), 1)
        earlier_ref[...] = jnp.where(t0 < t1, 1.0, 0.0).astype(BF16)

    h = _rmsnorm(x_ref[...], g_ref[...])
    lt = lax.dot_general(rt_ref[...], h, NT_DIMS, precision=lax.Precision.HIGHEST,
                         preferred_element_type=F32)
    ex = lax.broadcasted_iota(jnp.int32, lt.shape, 0).astype(F32)
    neg = jnp.float32(-jnp.inf)
    m1 = jnp.max(lt, axis=0, keepdims=True)
    i1 = jnp.min(jnp.where(lt == m1, ex, float(N_EXPERTS)), axis=0, keepdims=True)
    l2 = jnp.where(ex == i1, neg, lt)
    m2 = jnp.max(l2, axis=0, keepdims=True)
    i2 = jnp.min(jnp.where(l2 == m2, ex, float(N_EXPERTS)), axis=0, keepdims=True)
    e2 = jnp.exp(m2 - m1)
    g1 = 1.0 / (1.0 + e2)
    g2 = e2 / (1.0 + e2)
    sel1, sel2 = ex == i1, ex == i2
    member = jnp.where(sel1 | sel2, 1.0, 0.0)
    rank = jnp.dot(member.astype(BF16), earlier_ref[...], preferred_element_type=F32) + carry_ref[:, 0:1]
    route_ref[...] = jnp.zeros_like(route_ref)
    route_ref[0:1, :] = i1.astype(jnp.int32)
    route_ref[1:2, :] = jnp.sum(jnp.where(sel1, rank, 0.0), axis=0, keepdims=True).astype(jnp.int32)
    route_ref[2:3, :] = i2.astype(jnp.int32)
    route_ref[3:4, :] = jnp.sum(jnp.where(sel2, rank, 0.0), axis=0, keepdims=True).astype(jnp.int32)
    carry_ref[...] += jnp.sum(member, axis=1, keepdims=True)
    cnt_ref[...] = carry_ref[...]
    row = lax.broadcasted_iota(jnp.int32, (LANES, tm), 0)
    gate_ref[...] = jnp.where(row == 0, g1, jnp.where(row == 1, g2, 0.0)).T


def _moe_route(x, g, router_t, tm=512):
    t = x.shape[0]
    return pl.pallas_call(
        _moe_route_kernel,
        grid=(t // tm,),
        in_specs=[pl.BlockSpec((tm, D_MODEL), lambda i: (i, 0)),
                  pl.BlockSpec((1, D_MODEL), lambda i: (0, 0)),
                  pl.BlockSpec((N_EXPERTS, D_MODEL), lambda i: (0, 0))],
        out_specs=[pl.BlockSpec((ROUTE_ROWS, tm), lambda i: (0, i)),
                   pl.BlockSpec((tm, LANES), lambda i: (i, 0)),
                   pl.BlockSpec((N_EXPERTS, LANES), lambda i: (0, 0))],
        out_shape=[jax.ShapeDtypeStruct((ROUTE_ROWS, t), jnp.int32),
                   jax.ShapeDtypeStruct((t, LANES), F32),
                   jax.ShapeDtypeStruct((N_EXPERTS, LANES), F32)],
        scratch_shapes=[pltpu.VMEM((N_EXPERTS, LANES), F32), pltpu.VMEM((tm, tm), BF16)],
        compiler_params=_params("arbitrary"),
        name="moe_route",
    )(x, g, router_t)


def _moe_slots_kernel(base_ref, route_ref, slot_ref):
    r = route_ref[...]
    slot_ref[...] = jnp.zeros_like(slot_ref)
    for k in range(2):
        e, rank = r[2 * k:2 * k + 1, :], r[2 * k + 1:2 * k + 2, :]
        start = jnp.zeros_like(e)
        for j in range(N_EXPERTS):
            start = jnp.where(e == j, base_ref[j], start)
        slot_ref[k:k + 1, :] = start + rank


def _moe_slots(route, base, tm=2048):
    t = route.shape[1]
    return pl.pallas_call(
        _moe_slots_kernel,
        grid_spec=pltpu.PrefetchScalarGridSpec(
            num_scalar_prefetch=1,
            grid=(t // tm,),
            in_specs=[pl.BlockSpec((ROUTE_ROWS, tm), lambda i, *_: (0, i))],
            out_specs=pl.BlockSpec((ROUTE_ROWS, tm), lambda i, *_: (0, i))),
        out_shape=jax.ShapeDtypeStruct((ROUTE_ROWS, t), jnp.int32),
        compiler_params=_params("parallel"),
        name="moe_slots",
    )(base, route)


COMBINE_ROWS = 128


def _rows_done(hbm, rows, sem):
    pltpu.make_async_copy(hbm.at[pl.ds(0, rows)], hbm.at[pl.ds(0, rows)], sem).wait()


def _moe_dispatch_kernel(tm, fs_ref, fe_ref, s0_ref, s1_ref, x_ref, xs_hbm, sem):
    i = pl.program_id(0)

    def row_copy(group, sub, dst_row):
        return pltpu.make_async_copy(x_ref.at[group, pl.ds(sub, 1), :], xs_hbm.at[pl.ds(dst_row, 1)], sem)

    def issue(c, carry):
        for u in range(SUBLANES):
            for k, s_ref in enumerate((s0_ref, s1_ref)):
                row_copy(c, u, s_ref[c * SUBLANES + u]).start(priority=k)
        return carry

    lax.fori_loop(0, tm // SUBLANES, issue, 0)
    _rows_done(xs_hbm, 2 * tm, sem)

    @pl.when(i == pl.num_programs(0) - 1)
    def _():
        for e in range(N_EXPERTS + 1):
            def fill(p, c):
                row_copy(0, 0, p).start()
                return c

            def drain(p, c):
                row_copy(0, 0, p).wait()
                return c

            lax.fori_loop(fs_ref[e], fe_ref[e], fill, 0)
            lax.fori_loop(fs_ref[e], fe_ref[e], drain, 0)


def _moe_dispatch(x, slots, fill_start, fill_end, n_slots, tm=1024):
    t = x.shape[0]
    slot_spec = pl.BlockSpec((tm,), lambda i, *_: (i,), memory_space=pltpu.SMEM)
    return pl.pallas_call(
        functools.partial(_moe_dispatch_kernel, tm),
        grid_spec=pltpu.PrefetchScalarGridSpec(
            num_scalar_prefetch=2,
            grid=(t // tm,),
            in_specs=[slot_spec, slot_spec,
                      pl.BlockSpec((tm // SUBLANES, SUBLANES, D_MODEL), lambda i, *_: (i, 0, 0))],
            out_specs=pl.BlockSpec(memory_space=pl.ANY),
            scratch_shapes=[pltpu.SemaphoreType.DMA(())]),
        out_shape=jax.ShapeDtypeStruct((n_slots, D_MODEL), F32),
        compiler_params=_params("arbitrary"),
        name="moe_dispatch",
    )(fill_start, fill_end, slots[0], slots[1], x.reshape(t // SUBLANES, SUBLANES, D_MODEL))


def _ffn_grouped_kernel(te_ref, nv_ref, x_ref, g_ref, wg_ref, wu_ref, wd_ref, y_ref):
    i = pl.program_id(0)

    @pl.when(i < nv_ref[0])
    def _():
        h = _rmsnorm(x_ref[...], g_ref[...]).astype(BF16)
        gate = jnp.dot(h, wg_ref[0, 0], preferred_element_type=F32)
        up = jnp.dot(h, wu_ref[0, 0], preferred_element_type=F32)
        y_ref[...] = jnp.dot((_silu(gate) * up).astype(BF16), wd_ref[0, 0], preferred_element_type=F32)

    @pl.when(i >= nv_ref[0])
    def _():
        y_ref[...] = jnp.zeros_like(y_ref)


def _ffn_grouped(xs, g, wg, wu, wd, layer, tile_expert, n_valid):
    n_slots = xs.shape[0]
    f = wg.shape[3]
    rows = pl.BlockSpec((MOE_TILE, D_MODEL), lambda i, te, nv: (i, 0))
    return pl.pallas_call(
        _ffn_grouped_kernel,
        grid_spec=pltpu.PrefetchScalarGridSpec(
            num_scalar_prefetch=2,
            grid=(n_slots // MOE_TILE,),
            in_specs=[rows,
                      pl.BlockSpec((1, D_MODEL), lambda i, te, nv: (0, 0)),
                      pl.BlockSpec((1, 1, D_MODEL, f), lambda i, te, nv: (layer, te[i], 0, 0)),
                      pl.BlockSpec((1, 1, D_MODEL, f), lambda i, te, nv: (layer, te[i], 0, 0)),
                      pl.BlockSpec((1, 1, f, D_MODEL), lambda i, te, nv: (layer, te[i], 0, 0))],
            out_specs=rows),
        out_shape=jax.ShapeDtypeStruct((n_slots, D_MODEL), F32),
        compiler_params=_params("arbitrary"),
        name="ffn_grouped",
    )(tile_expert, n_valid, xs, g, wg, wu, wd)


def _moe_combine_kernel(tm, final, s0_ref, s1_ref, x_ref, gate_ref, *rest):
    if final:
        gf_ref, rest = rest[0], rest[1:]
    ys_hbm, o_ref, y1_ref, y2_ref, sems = rest

    def issue(c, carry):
        for u in range(SUBLANES):
            for k, (s_ref, buf) in enumerate(((s0_ref, y1_ref), (s1_ref, y2_ref))):
                pltpu.make_async_copy(ys_hbm.at[pl.ds(s_ref[c * SUBLANES + u], 1)],
                                      buf.at[c, pl.ds(u, 1), :], sems.at[k]).start(priority=k)
        return carry

    lax.fori_loop(0, tm // SUBLANES, issue, 0)
    for k in range(2):
        _rows_done(ys_hbm, tm, sems.at[k])
    rc = COMBINE_ROWS

    def rows(ci, c):
        r = pl.ds(pl.multiple_of(ci * rc, rc), rc)
        rg = pl.ds(pl.multiple_of(ci * (rc // SUBLANES), rc // SUBLANES), rc // SUBLANES)
        g1 = jnp.broadcast_to(gate_ref[r, 0:1], (rc, LANES))
        g2 = jnp.broadcast_to(gate_ref[r, 1:2], (rc, LANES))
        ssq = jnp.zeros((rc, LANES), F32)
        for j in range(D_MODEL // LANES):
            cols = slice(j * LANES, (j + 1) * LANES)
            y1 = y1_ref[rg, :, cols].reshape(rc, LANES)
            y2 = y2_ref[rg, :, cols].reshape(rc, LANES)
            out = x_ref[r, cols] + g1 * y1 + g2 * y2
            o_ref[r, cols] = out
            ssq = ssq + out * out
        if final:
            scale = lax.rsqrt(jnp.sum(ssq, axis=-1, keepdims=True) * (1.0 / D_MODEL) + EPS)
            for j in range(D_MODEL // LANES):
                cols = slice(j * LANES, (j + 1) * LANES)
                o_ref[r, cols] = o_ref[r, cols] * scale * gf_ref[:, cols]
        return c

    lax.fori_loop(0, tm // rc, rows, 0)


def _moe_combine(x, slots, gates, ys, final_g=None, tm=512):
    t = x.shape[0]
    final = final_g is not None
    slot_spec = pl.BlockSpec((tm,), lambda i: (i,), memory_space=pltpu.SMEM)
    in_specs = [slot_spec, slot_spec,
                pl.BlockSpec((tm, D_MODEL), lambda i: (i, 0)),
                pl.BlockSpec((tm, LANES), lambda i: (i, 0))]
    args = [slots[0], slots[1], x, gates]
    if final:
        in_specs.append(pl.BlockSpec((1, D_MODEL), lambda i: (0, 0)))
        args.append(final_g)
    in_specs.append(pl.BlockSpec(memory_space=pl.ANY))
    args.append(ys)
    return pl.pallas_call(
        functools.partial(_moe_combine_kernel, tm, final),
        grid=(t // tm,),
        in_specs=in_specs,
        out_specs=pl.BlockSpec((tm, D_MODEL), lambda i: (i, 0)),
        out_shape=jax.ShapeDtypeStruct((t, D_MODEL), F32),
        scratch_shapes=[pltpu.VMEM((tm // SUBLANES, SUBLANES, D_MODEL), F32),
                        pltpu.VMEM((tm // SUBLANES, SUBLANES, D_MODEL), F32),
                        pltpu.SemaphoreType.DMA((2,))],
        compiler_params=_params("arbitrary"),
        name="moe_combine",
    )(*args)


def _moe_routed(x, g, router_t, wg, wu, wd, layer, final_g=None):
    t = x.shape[0]
    n_tiles = 2 * t // MOE_TILE + N_EXPERTS
    route, gates, counts = _moe_route(x, g, router_t)
    cnt = counts[:, 0].astype(jnp.int32)
    caps = (cnt + MOE_TILE - 1) // MOE_TILE
    cum = jnp.cumsum(caps)
    base = (cum - caps) * MOE_TILE
    n_valid = cum[-1:]
    tile_expert = jnp.minimum(
        jnp.sum((cum[None, :] <= jnp.arange(n_tiles, dtype=jnp.int32)[:, None]).astype(jnp.int32), axis=1),
        N_EXPERTS - 1)
    fill_start = jnp.concatenate([base + cnt, n_valid * MOE_TILE])
    fill_end = jnp.concatenate([base + caps * MOE_TILE, jnp.full((1,), n_tiles * MOE_TILE, jnp.int32)])
    slots = _moe_slots(route, base)
    xs = _moe_dispatch(x, slots, fill_start, fill_end, n_tiles * MOE_TILE)
    ys = _ffn_grouped(xs, g, wg, wu, wd, layer, tile_expert, n_valid)
    return _moe_combine(x, slots, gates, ys, final_g)


def _trunk(x, state_hgrn, state_pool, start_pos, w, seq, tm):
    batch = x.shape[0] // seq
    s_out, b_out = [], []
    for l in range(DEPTH):
        act = _mixer_in(x, w["norm_mix"][l], w["w_in"], w["lb_logits"], l, tm)
        if seq > 1:
            og, s_new = _hgrn_prompt(act, w["hg_norm"][l], batch, seq)
            x, b_new = _mixer_out_pool(x, og, act, w["w_o"], l, w["pool_w"][l], w["pool_scale"][l], batch, seq)
        else:
            og, s_new = _hgrn_step(act, w["hg_norm"][l], state_hgrn, l)
            yp, b_new = _pool_step(act, state_pool, l, w["pool_w"][l], w["pool_scale"][l], start_pos)
            x = _mixer_out(x, og, yp, w["w_o"], l, tm)
        s_out.append(s_new)
        b_out.append(b_new)
        routed = l % 2 == 1 and seq > 1
        j = l // 2
        final_g = w["norm_final"] if l == DEPTH - 1 else None
        if l % 2 == 0:
            x = _ffn(x, w["norm_ffn"][l], w["ffn_w_gate"], w["ffn_w_up"], w["ffn_w_down"], j, tm, 1408,
                     final_g=final_g)
        elif routed:
            x = _moe_routed(x, w["norm_ffn"][l], w["router_t"][j], w["moe_w_gate"], w["moe_w_up"],
                            w["moe_w_down"], j, final_g=final_g)
        else:
            x = _ffn(x, w["norm_ffn"][l], w["moe_w_gate"], w["moe_w_up"], w["moe_w_down"], j, tm, 1408,
                     router=w["router"][j], final_g=final_g)
    return x, jnp.stack(s_out), jnp.stack(b_out)


def kernel(x_prompt, x_sample, state_hgrn, state_pool, lb_logits, norm_mix, w_in, w_o, hg_norm, pool_w,
           pool_scale, norm_ffn, ffn_w_gate, ffn_w_up, ffn_w_down, router, moe_w_gate, moe_w_up, moe_w_down,
           norm_final):
    batch, seq, _ = x_prompt.shape
    dec_batch, dec_seq, _ = x_sample.shape
    assert dec_seq == 1
    past_len = 16384
    w = dict(
        lb_logits=lb_logits,
        norm_mix=norm_mix.reshape(DEPTH, 1, D_MODEL),
        w_in=w_in,
        w_o=w_o,
        hg_norm=hg_norm.reshape(DEPTH, 1, HG_WIDTH),
        pool_w=pool_w.astype(BF16),
        pool_scale=pool_scale.reshape(DEPTH, 1, POOL_WIDTH),
        norm_ffn=norm_ffn.reshape(DEPTH, 1, D_MODEL),
        ffn_w_gate=ffn_w_gate.astype(BF16)[:, None],
        ffn_w_up=ffn_w_up.astype(BF16)[:, None],
        ffn_w_down=ffn_w_down.astype(BF16)[:, None],
        router=jnp.pad(router, ((0, 0), (0, 0), (0, LANES - N_EXPERTS))),
        router_t=jnp.swapaxes(router, 1, 2),
        moe_w_gate=moe_w_gate.astype(BF16),
        moe_w_up=moe_w_up.astype(BF16),
        moe_w_down=moe_w_down.astype(BF16),
        norm_final=norm_final.reshape(1, D_MODEL),
    )
    yp, sp, bp = _trunk(x_prompt.reshape(batch * seq, D_MODEL), None, None, 0, w, seq, 512)
    ys, ss, bs = _trunk(x_sample.reshape(dec_batch, D_MODEL), state_hgrn, state_pool, past_len, w, 1, 128)
    return (yp.reshape(batch, seq, D_MODEL), ys.reshape(dec_batch, 1, D_MODEL), sp, ss, bp, bs)
```

```python
import functools

import jax
import jax.numpy as jnp
from jax import lax
from jax.experimental import pallas as pl
from jax.experimental.pallas import tpu as pltpu

F32 = jnp.float32
BF16 = jnp.bfloat16

D_MODEL = 1024
DEPTH = 4
HG_WIDTH = 512
HG_HEADS = 4
HG_D = 128
POOL_WIDTH = 512
POOL_WINDOWS = (2, 4, 8, 16)
POOL_GROUP_W = 128
POOL_BUF = 15
IN_WIDTH = 4 * HG_WIDTH + POOL_WIDTH
ACT_WIDTH = 4 * HG_WIDTH + POOL_WIDTH
ACT_POOL_PART = 4
N_EXPERTS = 8
EPS = 1e-6
LANES = 128
SUBLANES = 8
VMEM_LIMIT = 56 * 1024 * 1024
HGRN_CHUNK = 128


def _params(*sem):
    return pltpu.CompilerParams(dimension_semantics=sem, vmem_limit_bytes=VMEM_LIMIT)


def _rmsnorm(x, g):
    return x * lax.rsqrt(jnp.mean(x * x, axis=-1, keepdims=True) + EPS) * g


def _silu(x):
    return x * jax.nn.sigmoid(x)


def _mixer_in_kernel(layer, x_ref, g_ref, w_ref, lbl_ref, o_ref, wb_ref):
    @pl.when(pl.program_id(0) == 0)
    def _():
        wb_ref[...] = w_ref[0].astype(BF16)

    h = _rmsnorm(x_ref[...], g_ref[...])
    p = jnp.dot(h.astype(BF16), wb_ref[...], preferred_element_type=F32)
    lg = lbl_ref[...]
    e = jnp.exp(lg - jnp.max(lg, axis=0, keepdims=True))
    pr = e / jnp.sum(e, axis=0, keepdims=True)
    cum = pr[0:1]
    for j in range(1, layer + 1):
        cum = cum + pr[j:j + 1]
    lb = cum - pr[0:1]
    w = HG_WIDTH
    q, fx, ix, g, u = p[:, :w], p[:, w:2 * w], p[:, 2 * w:3 * w], p[:, 3 * w:4 * w], p[:, 4 * w:]
    f = lb + (1.0 - lb) * jax.nn.sigmoid(fx)
    o_ref[:, 0:w] = _silu(q)
    o_ref[:, w:2 * w] = jnp.log(f)
    o_ref[:, 2 * w:3 * w] = ix
    o_ref[:, 3 * w:4 * w] = _silu(g)
    o_ref[:, 4 * w:] = u


def _mixer_in(x, g, w_in, lb_logits, layer, tm):
    t = x.shape[0]
    return pl.pallas_call(
        functools.partial(_mixer_in_kernel, layer),
        grid=(t // tm,),
        in_specs=[
            pl.BlockSpec((tm, D_MODEL), lambda i: (i, 0)),
            pl.BlockSpec((1, D_MODEL), lambda i: (0, 0)),
            pl.BlockSpec((1, D_MODEL, IN_WIDTH), lambda i: (layer, 0, 0)),
            pl.BlockSpec((DEPTH, HG_WIDTH), lambda i: (0, 0)),
        ],
        out_specs=pl.BlockSpec((tm, ACT_WIDTH), lambda i: (i, 0)),
        out_shape=jax.ShapeDtypeStruct((t, ACT_WIDTH), F32),
        scratch_shapes=[pltpu.VMEM((D_MODEL, IN_WIDTH), BF16)],
        compiler_params=_params("arbitrary"),
        name="mixer_in",
    )(x, g, w_in, lb_logits)


SMALL_LEVELS = (2, 4, 8)


def _level_sum_matrix(n, ti, si):
    half = n // 2
    mid = (ti & (-n)) + (half - 1)
    lo = jnp.where((ti & half) != 0, mid, ti)
    hi = jnp.where((ti & half) != 0, ti, mid)
    return (si > lo) & (si <= hi)


def _level_factors(q, k, b, n, row, arg=None):
    c = b.shape[0]
    half = n // 2
    if arg is None:
        zero = jnp.zeros((half, b.shape[1]), F32)
        qs, ks = [], []
        for r0 in range(0, c, n):
            mid = b[r0 + half - 1:r0 + half, :]
            ks += [k[r0:r0 + half] * jnp.exp(mid - b[r0:r0 + half]), zero]
            qs += [zero, q[r0 + half:r0 + n] * jnp.exp(b[r0 + half:r0 + n] - mid)]
        return jnp.concatenate(qs, axis=0), jnp.concatenate(ks, axis=0)
    right = (row & half) != 0
    qk = jnp.where(right, q, k) * jnp.exp(arg)
    return jnp.where(right, qk, 0.0), jnp.where(right, 0.0, qk)


def _split3(x):
    hi = x.astype(BF16)
    r1 = x - hi.astype(F32)
    mid = r1.astype(BF16)
    lo = (r1 - mid.astype(F32)).astype(BF16)
    return jnp.concatenate([hi, mid, lo], axis=0)


HEAD_PAIRS = HG_HEADS // 2
PAIR_W = 2 * HG_D


def _pair_blocks(a):
    zero = jnp.zeros((a.shape[0], HG_D), a.dtype)
    return jnp.concatenate([jnp.concatenate([a[:, :HG_D], zero], axis=1),
                            jnp.concatenate([zero, a[:, HG_D:]], axis=1)], axis=0)


def _per_head(fn, a):
    return jnp.concatenate([fn(a[:, :HG_D]), fn(a[:, HG_D:])], axis=1)


def _hgrn_prompt_kernel(tl, q_ref, lf_ref, v_ref, g_ref, nrm_ref, o_ref, s_ref, st_ref):
    c = HGRN_CHUNK
    li = pl.program_id(1)

    @pl.when(li == 0)
    def _():
        st_ref[...] = jnp.zeros_like(st_ref)

    row = lax.broadcasted_iota(jnp.int32, (c, PAIR_W), 0)
    ti = lax.broadcasted_iota(jnp.int32, (c, c), 0)
    si = lax.broadcasted_iota(jnp.int32, (c, c), 1)
    sums = jnp.concatenate(
        [jnp.where(m, 1.0, 0.0)
         for m in [si <= ti] + [_level_sum_matrix(n, ti, si) for n in SMALL_LEVELS]], axis=0).astype(BF16)
    sums = jnp.concatenate([sums] * 3, axis=1)
    apart = jnp.concatenate([ti ^ si] * 2, axis=1)
    pi = lax.broadcasted_iota(jnp.int32, (PAIR_W, PAIR_W), 0)
    pj = lax.broadcasted_iota(jnp.int32, (PAIR_W, PAIR_W), 1)
    own = (pi < HG_D) == (pj < HG_D)
    nt = (((1,), (1,)), ((), ()))
    tn = (((0,), (0,)), ((), ()))

    def chunk(ci, carry):
        rows = pl.ds(pl.multiple_of(ci * c, c), c)
        for p in range(HEAD_PAIRS):
            cols = slice(p * PAIR_W, (p + 1) * PAIR_W)
            q, lf, v = q_ref[rows, cols], lf_ref[rows, cols], v_ref[rows, cols]
            k = 1.0 - jnp.exp(lf)
            s1 = jnp.dot(sums, _split3(lf), preferred_element_type=F32)
            b = s1[:c]
            args = {n: s1[(i + 1) * c:(i + 2) * c] for i, n in enumerate(SMALL_LEVELS)}
            vbd = _pair_blocks(v.astype(BF16))
            scores = None
            n = c
            while n >= 2:
                qt, kt = _level_factors(q, k, b, n, row, args.get(n))
                s_n = lax.dot_general(qt.astype(BF16), _pair_blocks(kt.astype(BF16)), nt,
                                      preferred_element_type=F32)
                scores = s_n if scores is None else jnp.where(apart < n, s_n, scores)
                n //= 2
            st = st_ref[p]
            qs = (q * jnp.exp(b)).astype(BF16)
            o = jnp.dot(scores.astype(BF16), vbd, preferred_element_type=F32)
            o = o + lax.dot_general(qs, st.astype(BF16), nt, preferred_element_type=F32)
            o = o + _per_head(lambda a: jnp.broadcast_to(jnp.sum(a, axis=-1, keepdims=True), a.shape), q * k) * v
            bc = b[c - 1:c, :]
            kd = (k * jnp.exp(bc - b)).astype(BF16)
            upd = lax.dot_general(v.astype(BF16), kd, tn, preferred_element_type=F32)
            st_ref[p] = st * jnp.exp(bc) + jnp.where(own, upd, 0.0)
            ms = _per_head(lambda a: jnp.broadcast_to(jnp.mean(a, axis=-1, keepdims=True), a.shape), o * o)
            o = o * lax.rsqrt(ms + EPS) * nrm_ref[:, cols]
            o_ref[rows, cols] = (o * g_ref[rows, cols]).astype(o_ref.dtype)
        return carry

    lax.fori_loop(0, tl // c, chunk, 0, unroll=2)

    @pl.when(li == pl.num_programs(1) - 1)
    def _():
        for p in range(HEAD_PAIRS):
            st = st_ref[p]
            s_ref[0, 2 * p] = st[:HG_D, :HG_D].T
            s_ref[0, 2 * p + 1] = st[HG_D:, HG_D:].T


def _hgrn_prompt(act, hg_norm, batch, seq, tl=1024):
    nl = seq // tl
    blk = lambda part: pl.BlockSpec((tl, HG_WIDTH), lambda b, i: (b * nl + i, part))
    return pl.pallas_call(
        functools.partial(_hgrn_prompt_kernel, tl),
        grid=(batch, nl),
        in_specs=[blk(0), blk(1), blk(2), blk(3),
                  pl.BlockSpec((1, HG_WIDTH), lambda b, i: (0, 0))],
        out_specs=[pl.BlockSpec((tl, HG_WIDTH), lambda b, i: (b * nl + i, 0)),
                   pl.BlockSpec((1, HG_HEADS, HG_D, HG_D), lambda b, i: (b, 0, 0, 0))],
        out_shape=[jax.ShapeDtypeStruct((batch * seq, HG_WIDTH), BF16),
                   jax.ShapeDtypeStruct((batch, HG_HEADS, HG_D, HG_D), F32)],
        scratch_shapes=[pltpu.VMEM((HEAD_PAIRS, PAIR_W, PAIR_W), F32)],
        compiler_params=_params("parallel", "arbitrary"),
        name="hgrn_prompt",
    )(act, act, act, act, hg_norm)


HGRN_STEP_BT = 16


def _hgrn_step_kernel(q_ref, lf_ref, v_ref, g_ref, nrm_ref, s_ref, o_ref, so_ref):
    bt = HGRN_STEP_BT
    q = q_ref[...]
    v = v_ref[...]
    f = jnp.exp(lf_ref[...])
    ri = lax.broadcasted_iota(jnp.int32, (3 * bt, PAIR_W), 0) % bt
    half = lax.broadcasted_iota(jnp.int32, (3 * bt, PAIR_W), 1) // HG_D
    tn = (((0,), (0,)), ((), ()))
    pieces = [_split3(a) for a in (f, 1.0 - f, q)]
    rows = []
    for s in range(0, bt, 2):
        pick = jnp.where(ri == s + half, 1.0, 0.0).astype(BF16)
        fc, kc, qc = [lax.dot_general(p, pick, tn, preferred_element_type=F32) for p in pieces]
        for d in range(2):
            cols = slice(d * HG_D, (d + 1) * HG_D)
            sn = fc[:, cols] * s_ref[0, s + d, 0] + kc[:, cols] * v[s + d:s + d + 1, :]
            so_ref[s + d, 0] = sn
            rows.append(jnp.sum(sn * qc[:, cols], axis=0, keepdims=True))
    o = jnp.concatenate(rows, axis=0)
    o = o * lax.rsqrt(jnp.mean(o * o, axis=-1, keepdims=True) + EPS) * nrm_ref[...]
    o_ref[...] = (o * g_ref[...]).astype(o_ref.dtype)


def _hgrn_step(act, hg_norm, states, layer):
    batch = act.shape[0]
    bt = HGRN_STEP_BT
    blk = lambda part: pl.BlockSpec((bt, HG_D), lambda i, h: (i, part * HG_HEADS + h))
    return pl.pallas_call(
        _hgrn_step_kernel,
        grid=(batch // bt, HG_HEADS),
        in_specs=[blk(0), blk(1), blk(2), blk(3),
                  pl.BlockSpec((1, HG_D), lambda i, h: (0, h)),
                  pl.BlockSpec((1, bt, 1, HG_D, HG_D), lambda i, h: (layer, i, h, 0, 0))],
        out_specs=[pl.BlockSpec((bt, HG_D), lambda i, h: (i, h)),
                   pl.BlockSpec((bt, 1, HG_D, HG_D), lambda i, h: (i, h, 0, 0))],
        out_shape=[jax.ShapeDtypeStruct((batch, HG_WIDTH), BF16),
                   jax.ShapeDtypeStruct(states.shape[1:], F32)],
        compiler_params=_params("parallel", "parallel"),
        name="hgrn_step",
    )(act, act, act, act, hg_norm, states)


def _pool_map(d_groups, wp_ref, scale_ref, o_ref):
    for gi, d in enumerate(d_groups):
        sl = slice(gi * POOL_GROUP_W, (gi + 1) * POOL_GROUP_W)
        y = jnp.dot(d.astype(BF16), wp_ref[gi], preferred_element_type=F32)
        o_ref[:, sl] = (y * scale_ref[:, sl]).astype(o_ref.dtype)


POOL_HIST = 16


def _pool_tile(i, tl, z, carry_ref, wp_ref, scale_ref, o_ref, nb_ref):
    @pl.when(i == 0)
    def _():
        carry_ref[...] = jnp.zeros_like(carry_ref)

    ext = jnp.concatenate([carry_ref[...], z], axis=0)
    sums = {1: ext}
    w = 1
    while w < max(POOL_WINDOWS):
        sums[2 * w] = sums[w] + pltpu.roll(sums[w], w, 0)
        w *= 2
    pos = i * tl + lax.broadcasted_iota(jnp.int32, (tl, POOL_GROUP_W), 0)
    ds = []
    for gi, w in enumerate(POOL_WINDOWS):
        sl = slice(gi * POOL_GROUP_W, (gi + 1) * POOL_GROUP_W)
        cnt = jnp.minimum(pos + 1, w).astype(F32)
        ds.append(sums[w][POOL_HIST:, sl] / cnt - z[:, sl])
    _pool_map(ds, wp_ref, scale_ref, o_ref)
    carry_ref[...] = z[tl - POOL_HIST:, :]

    @pl.when(i == pl.num_programs(1) - 1)
    def _():
        nb_ref[0] = z[tl - POOL_BUF:, :]


POOL_STEP_BT = 16


def _pool_step_kernel(start_pos, u_ref, buf_ref, wp_ref, scale_ref, o_ref, nb_ref):
    u = u_ref[...]
    buf = buf_ref[0]
    ds = []
    for gi, w in enumerate(POOL_WINDOWS):
        sl = slice(gi * POOL_GROUP_W, (gi + 1) * POOL_GROUP_W)
        tot = u[:, sl] + jnp.sum(buf[:, POOL_BUF - (w - 1):, sl], axis=1)
        ds.append(tot / float(min(start_pos + 1, w)) - u[:, sl])
    _pool_map(ds, wp_ref, scale_ref, o_ref)
    nb_ref[:, 0:POOL_BUF - 1, :] = buf[:, 1:, :]
    nb_ref[:, POOL_BUF - 1:, :] = u[:, None, :]


def _pool_step(act, bufs, layer, pool_w, pool_scale, start_pos):
    batch = act.shape[0]
    bt = POOL_STEP_BT
    return pl.pallas_call(
        functools.partial(_pool_step_kernel, start_pos),
        grid=(batch // bt,),
        in_specs=[pl.BlockSpec((bt, POOL_WIDTH), lambda i: (i, ACT_POOL_PART)),
                  pl.BlockSpec((1, bt, POOL_BUF, POOL_WIDTH), lambda i: (layer, i, 0, 0)),
                  pl.BlockSpec((len(POOL_WINDOWS), POOL_GROUP_W, POOL_GROUP_W), lambda i: (0, 0, 0)),
                  pl.BlockSpec((1, POOL_WIDTH), lambda i: (0, 0))],
        out_specs=[pl.BlockSpec((bt, POOL_WIDTH), lambda i: (i, 0)),
                   pl.BlockSpec((bt, POOL_BUF, POOL_WIDTH), lambda i: (i, 0, 0))],
        out_shape=[jax.ShapeDtypeStruct((batch, POOL_WIDTH), BF16),
                   jax.ShapeDtypeStruct(bufs.shape[1:], F32)],
        compiler_params=_params("parallel"),
        name="pool_step",
    )(act, bufs, pool_w, pool_scale)


def _mixer_out_kernel(x_ref, o_ref, p_ref, w_ref, y_ref, wb_ref):
    @pl.when(pl.program_id(0) == 0)
    def _():
        wb_ref[...] = w_ref[0].astype(BF16)

    y = jnp.dot(o_ref[...], wb_ref[0:HG_WIDTH, :], preferred_element_type=F32)
    y = y + jnp.dot(p_ref[...], wb_ref[HG_WIDTH:, :], preferred_element_type=F32)
    y_ref[...] = x_ref[...] + y


def _mixer_out(x, og, yp, w_o, layer, tm):
    t = x.shape[0]
    return pl.pallas_call(
        _mixer_out_kernel,
        grid=(t // tm,),
        in_specs=[pl.BlockSpec((tm, D_MODEL), lambda i: (i, 0)),
                  pl.BlockSpec((tm, HG_WIDTH), lambda i: (i, 0)),
                  pl.BlockSpec((tm, POOL_WIDTH), lambda i: (i, 0)),
                  pl.BlockSpec((1, D_MODEL, D_MODEL), lambda i: (layer, 0, 0))],
        out_specs=pl.BlockSpec((tm, D_MODEL), lambda i: (i, 0)),
        out_shape=jax.ShapeDtypeStruct((t, D_MODEL), F32),
        scratch_shapes=[pltpu.VMEM((D_MODEL, D_MODEL), BF16)],
        compiler_params=_params("arbitrary"),
        name="mixer_out",
    )(x, og, yp, w_o)


def _top2_combine(logits):
    lane = lax.broadcasted_iota(jnp.int32, logits.shape, 1).astype(F32)
    neg = jnp.float32(-jnp.inf)
    lg = jnp.where(lane < N_EXPERTS, logits, neg)
    m1 = jnp.max(lg, axis=-1, keepdims=True)
    i1 = jnp.min(jnp.where(lg == m1, lane, float(LANES)), axis=-1, keepdims=True)
    lg2 = jnp.where(lane == i1, neg, lg)
    m2 = jnp.max(lg2, axis=-1, keepdims=True)
    i2 = jnp.min(jnp.where(lg2 == m2, lane, float(LANES)), axis=-1, keepdims=True)
    e2 = jnp.exp(m2 - m1)
    g1 = 1.0 / (1.0 + e2)
    g2 = e2 / (1.0 + e2)
    return jnp.where(lane == i1, g1, 0.0) + jnp.where(lane == i2, g2, 0.0)


def _ffn_kernel(moe, final, x_ref, g_ref, *rest):
    if moe:
        r_ref, rest = rest[0], rest[1:]
    if final:
        gf_ref, rest = rest[0], rest[1:]
    wg_ref, wu_ref, wd_ref, y_ref, h_ref, acc_ref = rest[:6]
    e, j = pl.program_id(1), pl.program_id(2)
    first = (e == 0) & (j == 0)
    last = (e == pl.num_programs(1) - 1) & (j == pl.num_programs(2) - 1)

    @pl.when(first)
    def _():
        h = _rmsnorm(x_ref[...], g_ref[...])
        h_ref[...] = h.astype(BF16)
        acc_ref[...] = jnp.zeros_like(acc_ref)
        if moe:
            logits = jnp.dot(h, r_ref[...], precision=lax.Precision.HIGHEST, preferred_element_type=F32)
            rest[6][...] = _top2_combine(logits)

    h = h_ref[...]
    gate = jnp.dot(h, wg_ref[0, 0], preferred_element_type=F32)
    up = jnp.dot(h, wu_ref[0, 0], preferred_element_type=F32)
    y = jnp.dot((_silu(gate) * up).astype(BF16), wd_ref[0, 0], preferred_element_type=F32)
    if moe:
        comb = rest[6][...]
        lane = lax.broadcasted_iota(jnp.int32, comb.shape, 1)
        y = jnp.sum(jnp.where(lane == e, comb, 0.0), axis=-1, keepdims=True) * y
    acc_ref[...] += y

    @pl.when(last)
    def _():
        out = x_ref[...] + acc_ref[...]
        if final:
            out = _rmsnorm(out, gf_ref[...])
        y_ref[...] = out


def _ffn(x, g, wg, wu, wd, layer, tm, tf, router=None, final_g=None):
    t = x.shape[0]
    _, ne, _, f = wg.shape
    moe, final = router is not None, final_g is not None
    vec = pl.BlockSpec((1, D_MODEL), lambda i, e, j: (0, 0))
    in_specs = [pl.BlockSpec((tm, D_MODEL), lambda i, e, j: (i, 0)), vec]
    args = [x, g]
    scratch = [pltpu.VMEM((tm, D_MODEL), BF16), pltpu.VMEM((tm, D_MODEL), F32)]
    if moe:
        in_specs.append(pl.BlockSpec((D_MODEL, LANES), lambda i, e, j: (0, 0)))
        args.append(router)
        scratch.append(pltpu.VMEM((tm, LANES), F32))
    if final:
        in_specs.append(vec)
        args.append(final_g)
    in_specs += [pl.BlockSpec((1, 1, D_MODEL, tf), lambda i, e, j: (layer, e, 0, j)),
                 pl.BlockSpec((1, 1, D_MODEL, tf), lambda i, e, j: (layer, e, 0, j)),
                 pl.BlockSpec((1, 1, tf, D_MODEL), lambda i, e, j: (layer, e, j, 0))]
    args += [wg, wu, wd]
    return pl.pallas_call(
        functools.partial(_ffn_kernel, moe, final),
        grid=(t // tm, ne, f // tf),
        in_specs=in_specs,
        out_specs=pl.BlockSpec((tm, D_MODEL), lambda i, e, j: (i, 0)),
        out_shape=jax.ShapeDtypeStruct((t, D_MODEL), F32),
        scratch_shapes=scratch,
        compiler_params=_params("parallel", "arbitrary", "arbitrary"),
        name="ffn_moe" if moe else "ffn_dense",
    )(*args)


def _mixer_tile(b, i, tl, x_ref, o_ref, u_ref, w_ref, wp_ref, scale_ref, nb_ref, wb_ref, carry_ref, yp_ref):
    @pl.when((b == 0) & (i == 0))
    def _():
        wb_ref[...] = w_ref[0].astype(BF16)

    _pool_tile(i, tl, u_ref[...], carry_ref, wp_ref, scale_ref, yp_ref, nb_ref)
    y = jnp.dot(o_ref[...], wb_ref[0:HG_WIDTH, :], preferred_element_type=F32)
    y = y + jnp.dot(yp_ref[...], wb_ref[HG_WIDTH:, :], preferred_element_type=F32)
    return x_ref[...] + y


def _mixer_ffn_kernel(tl, x_ref, o_ref, u_ref, w_ref, wp_ref, scale_ref, g_ref, wg_ref, wu_ref, wd_ref,
                      y_ref, nb_ref, wb_ref, carry_ref, yp_ref, x1_ref, h_ref, acc_ref):
    b, i, j = pl.program_id(0), pl.program_id(1), pl.program_id(2)

    @pl.when(j == 0)
    def _():
        x1 = _mixer_tile(b, i, tl, x_ref, o_ref, u_ref, w_ref, wp_ref, scale_ref, nb_ref, wb_ref, carry_ref, yp_ref)
        x1_ref[...] = x1
        h_ref[...] = _rmsnorm(x1, g_ref[...]).astype(BF16)
        acc_ref[...] = jnp.zeros_like(acc_ref)

    h = h_ref[...]
    gate = jnp.dot(h, wg_ref[0, 0], preferred_element_type=F32)
    up = jnp.dot(h, wu_ref[0, 0], preferred_element_type=F32)
    acc_ref[...] += jnp.dot((_silu(gate) * up).astype(BF16), wd_ref[0, 0], preferred_element_type=F32)

    @pl.when(j == pl.num_programs(2) - 1)
    def _():
        y_ref[...] = x1_ref[...] + acc_ref[...]


def _mixer_specs(nl, tl, layer, tok, fixed):
    ins = [pl.BlockSpec((tl, D_MODEL), tok(0)),
           pl.BlockSpec((tl, HG_WIDTH), tok(0)),
           pl.BlockSpec((tl, POOL_WIDTH), tok(ACT_POOL_PART)),
           pl.BlockSpec((1, D_MODEL, D_MODEL), fixed(layer, 0, 0)),
           pl.BlockSpec((len(POOL_WINDOWS), POOL_GROUP_W, POOL_GROUP_W), fixed(0, 0, 0)),
           pl.BlockSpec((1, POOL_WIDTH), fixed(0, 0))]
    scratch = [pltpu.VMEM((D_MODEL, D_MODEL), BF16), pltpu.VMEM((POOL_HIST, POOL_WIDTH), F32),
               pltpu.VMEM((tl, POOL_WIDTH), BF16)]
    return ins, scratch


def _mixer_ffn(x, og, act, w_o, layer, pool_w, pool_scale, g, wg, wu, wd, wset, batch, seq, tl=512, tf=1408):
    nl = seq // tl
    f = wg.shape[3]
    tok = lambda cols: (lambda b, i, j: (b * nl + i, cols))
    fixed = lambda *idx: (lambda b, i, j: idx)
    ins, scratch = _mixer_specs(nl, tl, layer, tok, fixed)
    ins += [pl.BlockSpec((1, D_MODEL), fixed(0, 0)),
            pl.BlockSpec((1, 1, D_MODEL, tf), lambda b, i, j: (wset, 0, 0, j)),
            pl.BlockSpec((1, 1, D_MODEL, tf), lambda b, i, j: (wset, 0, 0, j)),
            pl.BlockSpec((1, 1, tf, D_MODEL), lambda b, i, j: (wset, 0, j, 0))]
    scratch += [pltpu.VMEM((tl, D_MODEL), F32), pltpu.VMEM((tl, D_MODEL), BF16), pltpu.VMEM((tl, D_MODEL), F32)]
    return pl.pallas_call(
        functools.partial(_mixer_ffn_kernel, tl),
        grid=(batch, nl, f // tf),
        in_specs=ins,
        out_specs=[pl.BlockSpec((tl, D_MODEL), tok(0)),
                   pl.BlockSpec((1, POOL_BUF, POOL_WIDTH), lambda b, i, j: (b, 0, 0))],
        out_shape=[jax.ShapeDtypeStruct((batch * seq, D_MODEL), F32),
                   jax.ShapeDtypeStruct((batch, POOL_BUF, POOL_WIDTH), F32)],
        scratch_shapes=scratch,
        compiler_params=_params("arbitrary", "arbitrary", "arbitrary"),
        name="mixer_ffn",
    )(x, og, act, w_o, pool_w, pool_scale, g, wg, wu, wd)


MOE_TILE = 512
ROUTE_ROWS = 8
NT_DIMS = (((1,), (1,)), ((), ()))


def _mixer_route_kernel(tm, x_ref, o_ref, u_ref, w_ref, wp_ref, scale_ref, g_ref, rt_ref,
                        x1_ref, nb_ref, route_ref, gate_ref, cnt_ref,
                        wb_ref, pool_carry_ref, yp_ref, carry_ref, earlier_ref):
    b, i = pl.program_id(0), pl.program_id(1)

    @pl.when((b == 0) & (i == 0))
    def _():
        carry_ref[...] = jnp.zeros_like(carry_ref)
        t0 = lax.broadcasted_iota(jnp.int32, (tm, tm), 0)
        t1 = lax.broadcasted_iota(jnp.int32, (tm, tm), 1)
        earlier_ref[...] = jnp.where(t0 < t1, 1.0, 0.0).astype(BF16)

    x1 = _mixer_tile(b, i, tm, x_ref, o_ref, u_ref, w_ref, wp_ref, scale_ref, nb_ref, wb_ref, pool_carry_ref, yp_ref)
    x1_ref[...] = x1
    h = _rmsnorm(x1, g_ref[...])
    lt = lax.dot_general(rt_ref[...], h, NT_DIMS, precision=lax.Precision.HIGHEST,
                         preferred_element_type=F32)
    ex = lax.broadcasted_iota(jnp.int32, lt.shape, 0).astype(F32)
    neg = jnp.float32(-jnp.inf)
    m1 = jnp.max(lt, axis=0, keepdims=True)
    i1 = jnp.min(jnp.where(lt == m1, ex, float(N_EXPERTS)), axis=0, keepdims=True)
    l2 = jnp.where(ex == i1, neg, lt)
    m2 = jnp.max(l2, axis=0, keepdims=True)
    i2 = jnp.min(jnp.where(l2 == m2, ex, float(N_EXPERTS)), axis=0, keepdims=True)
    e2 = jnp.exp(m2 - m1)
    g1 = 1.0 / (1.0 + e2)
    g2 = e2 / (1.0 + e2)
    sel1, sel2 = ex == i1, ex == i2
    member = jnp.where(sel1 | sel2, 1.0, 0.0)
    rank = jnp.dot(member.astype(BF16), earlier_ref[...], preferred_element_type=F32) + carry_ref[:, 0:1]
    route_ref[...] = jnp.zeros_like(route_ref)
    route_ref[0:1, :] = i1.astype(jnp.int32)
    route_ref[1:2, :] = jnp.sum(jnp.where(sel1, rank, 0.0), axis=0, keepdims=True).astype(jnp.int32)
    route_ref[2:3, :] = i2.astype(jnp.int32)
    route_ref[3:4, :] = jnp.sum(jnp.where(sel2, rank, 0.0), axis=0, keepdims=True).astype(jnp.int32)
    carry_ref[...] += jnp.sum(member, axis=1, keepdims=True)
    cnt_ref[...] = carry_ref[...]
    row = lax.broadcasted_iota(jnp.int32, (LANES, tm), 0)
    gate_ref[...] = jnp.where(row == 0, g1, jnp.where(row == 1, g2, 0.0)).T


def _mixer_route(x, og, act, w_o, layer, pool_w, pool_scale, g, router_t, batch, seq, tm=512):
    nl = seq // tm
    t = batch * seq
    tok = lambda cols: (lambda b, i: (b * nl + i, cols))
    fixed = lambda *idx: (lambda b, i: idx)
    ins, scratch = _mixer_specs(nl, tm, layer, tok, fixed)
    ins += [pl.BlockSpec((1, D_MODEL), fixed(0, 0)),
            pl.BlockSpec((N_EXPERTS, D_MODEL), fixed(0, 0))]
    scratch += [pltpu.VMEM((N_EXPERTS, LANES), F32), pltpu.VMEM((tm, tm), BF16)]
    return pl.pallas_call(
        functools.partial(_mixer_route_kernel, tm),
        grid=(batch, nl),
        in_specs=ins,
        out_specs=[pl.BlockSpec((tm, D_MODEL), tok(0)),
                   pl.BlockSpec((1, POOL_BUF, POOL_WIDTH), lambda b, i: (b, 0, 0)),
                   pl.BlockSpec((ROUTE_ROWS, tm), lambda b, i: (0, b * nl + i)),
                   pl.BlockSpec((tm, LANES), tok(0)),
                   pl.BlockSpec((N_EXPERTS, LANES), fixed(0, 0))],
        out_shape=[jax.ShapeDtypeStruct((t, D_MODEL), F32),
                   jax.ShapeDtypeStruct((batch, POOL_BUF, POOL_WIDTH), F32),
                   jax.ShapeDtypeStruct((ROUTE_ROWS, t), jnp.int32),
                   jax.ShapeDtypeStruct((t, LANES), F32),
                   jax.ShapeDtypeStruct((N_EXPERTS, LANES), F32)],
        scratch_shapes=scratch,
        compiler_params=_params("arbitrary", "arbitrary"),
        name="mixer_route",
    )(x, og, act, w_o, pool_w, pool_scale, g, router_t)


def _moe_slots_kernel(base_ref, route_ref, slot_ref):
    r = route_ref[...]
    slot_ref[...] = jnp.zeros_like(slot_ref)
    for k in range(2):
        e, rank = r[2 * k:2 * k + 1, :], r[2 * k + 1:2 * k + 2, :]
        start = jnp.zeros_like(e)
        for j in range(N_EXPERTS):
            start = jnp.where(e == j, base_ref[j], start)
        slot_ref[k:k + 1, :] = start + rank


def _moe_slots(route, base, tm=2048):
    t = route.shape[1]
    return pl.pallas_call(
        _moe_slots_kernel,
        grid_spec=pltpu.PrefetchScalarGridSpec(
            num_scalar_prefetch=1,
            grid=(t // tm,),
            in_specs=[pl.BlockSpec((ROUTE_ROWS, tm), lambda i, *_: (0, i))],
            out_specs=pl.BlockSpec((ROUTE_ROWS, tm), lambda i, *_: (0, i))),
        out_shape=jax.ShapeDtypeStruct((ROUTE_ROWS, t), jnp.int32),
        compiler_params=_params("parallel"),
        name="moe_slots",
    )(base, route)


COMBINE_ROWS = 128


def _rows_done(hbm, rows, sem):
    pltpu.make_async_copy(hbm.at[pl.ds(0, rows)], hbm.at[pl.ds(0, rows)], sem).wait()


def _moe_dispatch_kernel(tm, fs_ref, fe_ref, s0_ref, s1_ref, x_ref, xs_hbm, sem):
    i = pl.program_id(0)

    def row_copy(group, sub, dst_row):
        return pltpu.make_async_copy(x_ref.at[group, pl.ds(sub, 1), :], xs_hbm.at[pl.ds(dst_row, 1)], sem)

    def issue(c, carry):
        for u in range(SUBLANES):
            for k, s_ref in enumerate((s0_ref, s1_ref)):
                row_copy(c, u, s_ref[c * SUBLANES + u]).start(priority=k)
        return carry

    lax.fori_loop(0, tm // SUBLANES, issue, 0)
    _rows_done(xs_hbm, 2 * tm, sem)

    @pl.when(i == pl.num_programs(0) - 1)
    def _():
        for e in range(N_EXPERTS + 1):
            def fill(p, c):
                row_copy(0, 0, p).start()
                return c

            def drain(p, c):
                row_copy(0, 0, p).wait()
                return c

            lax.fori_loop(fs_ref[e], fe_ref[e], fill, 0)
            lax.fori_loop(fs_ref[e], fe_ref[e], drain, 0)


def _moe_dispatch(x, slots, fill_start, fill_end, n_slots, tm=1024):
    t = x.shape[0]
    slot_spec = pl.BlockSpec((tm,), lambda i, *_: (i,), memory_space=pltpu.SMEM)
    return pl.pallas_call(
        functools.partial(_moe_dispatch_kernel, tm),
        grid_spec=pltpu.PrefetchScalarGridSpec(
            num_scalar_prefetch=2,
            grid=(t // tm,),
            in_specs=[slot_spec, slot_spec,
                      pl.BlockSpec((tm // SUBLANES, SUBLANES, D_MODEL), lambda i, *_: (i, 0, 0))],
            out_specs=pl.BlockSpec(memory_space=pl.ANY),
            scratch_shapes=[pltpu.SemaphoreType.DMA(())]),
        out_shape=jax.ShapeDtypeStruct((n_slots, D_MODEL), F32),
        compiler_params=_params("arbitrary"),
        name="moe_dispatch",
    )(fill_start, fill_end, slots[0], slots[1], x.reshape(t // SUBLANES, SUBLANES, D_MODEL))


def _ffn_grouped_kernel(te_ref, nv_ref, x_ref, g_ref, wg_ref, wu_ref, wd_ref, y_ref):
    i = pl.program_id(0)

    @pl.when(i < nv_ref[0])
    def _():
        h = _rmsnorm(x_ref[...], g_ref[...]).astype(BF16)
        gate = jnp.dot(h, wg_ref[0, 0], preferred_element_type=F32)
        up = jnp.dot(h, wu_ref[0, 0], preferred_element_type=F32)
        y_ref[...] = jnp.dot((_silu(gate) * up).astype(BF16), wd_ref[0, 0], preferred_element_type=F32)

    @pl.when(i >= nv_ref[0])
    def _():
        y_ref[...] = jnp.zeros_like(y_ref)


def _ffn_grouped(xs, g, wg, wu, wd, layer, tile_expert, n_valid):
    n_slots = xs.shape[0]
    f = wg.shape[3]
    rows = pl.BlockSpec((MOE_TILE, D_MODEL), lambda i, te, nv: (i, 0))
    return pl.pallas_call(
        _ffn_grouped_kernel,
        grid_spec=pltpu.PrefetchScalarGridSpec(
            num_scalar_prefetch=2,
            grid=(n_slots // MOE_TILE,),
            in_specs=[rows,
                      pl.BlockSpec((1, D_MODEL), lambda i, te, nv: (0, 0)),
                      pl.BlockSpec((1, 1, D_MODEL, f), lambda i, te, nv: (layer, te[i], 0, 0)),
                      pl.BlockSpec((1, 1, D_MODEL, f), lambda i, te, nv: (layer, te[i], 0, 0)),
                      pl.BlockSpec((1, 1, f, D_MODEL), lambda i, te, nv: (layer, te[i], 0, 0))],
            out_specs=rows),
        out_shape=jax.ShapeDtypeStruct((n_slots, D_MODEL), F32),
        compiler_params=_params("arbitrary"),
        name="ffn_grouped",
    )(tile_expert, n_valid, xs, g, wg, wu, wd)


def _moe_combine_kernel(tm, final, s0_ref, s1_ref, x_ref, gate_ref, *rest):
    if final:
        gf_ref, rest = rest[0], rest[1:]
    ys_hbm, o_ref, y1_ref, y2_ref, sems = rest

    def issue(c, carry):
        for u in range(SUBLANES):
            for k, (s_ref, buf) in enumerate(((s0_ref, y1_ref), (s1_ref, y2_ref))):
                pltpu.make_async_copy(ys_hbm.at[pl.ds(s_ref[c * SUBLANES + u], 1)],
                                      buf.at[c, pl.ds(u, 1), :], sems.at[k]).start(priority=k)
        return carry

    lax.fori_loop(0, tm // SUBLANES, issue, 0)
    for k in range(2):
        _rows_done(ys_hbm, tm, sems.at[k])
    rc = COMBINE_ROWS

    def rows(ci, c):
        r = pl.ds(pl.multiple_of(ci * rc, rc), rc)
        rg = pl.ds(pl.multiple_of(ci * (rc // SUBLANES), rc // SUBLANES), rc // SUBLANES)
        g1 = jnp.broadcast_to(gate_ref[r, 0:1], (rc, LANES))
        g2 = jnp.broadcast_to(gate_ref[r, 1:2], (rc, LANES))
        ssq = jnp.zeros((rc, LANES), F32)
        for j in range(D_MODEL // LANES):
            cols = slice(j * LANES, (j + 1) * LANES)
            y1 = y1_ref[rg, :, cols].reshape(rc, LANES)
            y2 = y2_ref[rg, :, cols].reshape(rc, LANES)
            out = x_ref[r, cols] + g1 * y1 + g2 * y2
            o_ref[r, cols] = out
            ssq = ssq + out * out
        if final:
            scale = lax.rsqrt(jnp.sum(ssq, axis=-1, keepdims=True) * (1.0 / D_MODEL) + EPS)
            for j in range(D_MODEL // LANES):
                cols = slice(j * LANES, (j + 1) * LANES)
                o_ref[r, cols] = o_ref[r, cols] * scale * gf_ref[:, cols]
        return c

    lax.fori_loop(0, tm // rc, rows, 0)


def _moe_combine(x, slots, gates, ys, final_g=None, tm=512):
    t = x.shape[0]
    final = final_g is not None
    slot_spec = pl.BlockSpec((tm,), lambda i: (i,), memory_space=pltpu.SMEM)
    in_specs = [slot_spec, slot_spec,
                pl.BlockSpec((tm, D_MODEL), lambda i: (i, 0)),
                pl.BlockSpec((tm, LANES), lambda i: (i, 0))]
    args = [slots[0], slots[1], x, gates]
    if final:
        in_specs.append(pl.BlockSpec((1, D_MODEL), lambda i: (0, 0)))
        args.append(final_g)
    in_specs.append(pl.BlockSpec(memory_space=pl.ANY))
    args.append(ys)
    return pl.pallas_call(
        functools.partial(_moe_combine_kernel, tm, final),
        grid=(t // tm,),
        in_specs=in_specs,
        out_specs=pl.BlockSpec((tm, D_MODEL), lambda i: (i, 0)),
        out_shape=jax.ShapeDtypeStruct((t, D_MODEL), F32),
        scratch_shapes=[pltpu.VMEM((tm // SUBLANES, SUBLANES, D_MODEL), F32),
                        pltpu.VMEM((tm // SUBLANES, SUBLANES, D_MODEL), F32),
                        pltpu.SemaphoreType.DMA((2,))],
        compiler_params=_params("arbitrary"),
        name="moe_combine",
    )(*args)


def _moe_routed(x, route, gates, counts, g, wg, wu, wd, layer, final_g=None):
    t = x.shape[0]
    n_tiles = 2 * t // MOE_TILE + N_EXPERTS
    cnt = counts[:, 0].astype(jnp.int32)
    caps = (cnt + MOE_TILE - 1) // MOE_TILE
    cum = jnp.cumsum(caps)
    base = (cum - caps) * MOE_TILE
    n_valid = cum[-1:]
    tile_expert = jnp.minimum(
        jnp.sum((cum[None, :] <= jnp.arange(n_tiles, dtype=jnp.int32)[:, None]).astype(jnp.int32), axis=1),
        N_EXPERTS - 1)
    fill_start = jnp.concatenate([base + cnt, n_valid * MOE_TILE])
    fill_end = jnp.concatenate([base + caps * MOE_TILE, jnp.full((1,), n_tiles * MOE_TILE, jnp.int32)])
    slots = _moe_slots(route, base)
    xs = _moe_dispatch(x, slots, fill_start, fill_end, n_tiles * MOE_TILE)
    ys = _ffn_grouped(xs, g, wg, wu, wd, layer, tile_expert, n_valid)
    return _moe_combine(x, slots, gates, ys, final_g)


def _trunk(x, state_hgrn, state_pool, start_pos, w, seq, tm):
    batch = x.shape[0] // seq
    s_out, b_out = [], []
    for l in range(DEPTH):
        act = _mixer_in(x, w["norm_mix"][l], w["w_in"], w["lb_logits"], l, tm)
        j = l // 2
        dense = l % 2 == 0
        final_g = w["norm_final"] if l == DEPTH - 1 else None
        g_ffn = w["norm_ffn"][l]
        if seq > 1:
            og, s_new = _hgrn_prompt(act, w["hg_norm"][l], batch, seq)
            mixer = (x, og, act, w["w_o"], l, w["pool_w"][l], w["pool_scale"][l])
            if dense:
                assert final_g is None
                x, b_new = _mixer_ffn(*mixer, g_ffn, w["ffn_w_gate"], w["ffn_w_up"], w["ffn_w_down"], j,
                                      batch, seq)
            else:
                x, b_new, route, gates, counts = _mixer_route(*mixer, g_ffn, w["router_t"][j], batch, seq)
                x = _moe_routed(x, route, gates, counts, g_ffn, w["moe_w_gate"], w["moe_w_up"],
                                w["moe_w_down"], j, final_g=final_g)
        else:
            og, s_new = _hgrn_step(act, w["hg_norm"][l], state_hgrn, l)
            yp, b_new = _pool_step(act, state_pool, l, w["pool_w"][l], w["pool_scale"][l], start_pos)
            x = _mixer_out(x, og, yp, w["w_o"], l, tm)
            if dense:
                x = _ffn(x, g_ffn, w["ffn_w_gate"], w["ffn_w_up"], w["ffn_w_down"], j, tm, 1408, final_g=final_g)
            else:
                x = _ffn(x, g_ffn, w["moe_w_gate"], w["moe_w_up"], w["moe_w_down"], j, tm, 1408,
                         router=w["router"][j], final_g=final_g)
        s_out.append(s_new)
        b_out.append(b_new)
    return x, jnp.stack(s_out), jnp.stack(b_out)


def kernel(x_prompt, x_sample, state_hgrn, state_pool, lb_logits, norm_mix, w_in, w_o, hg_norm, pool_w,
           pool_scale, norm_ffn, ffn_w_gate, ffn_w_up, ffn_w_down, router, moe_w_gate, moe_w_up, moe_w_down,
           norm_final):
    batch, seq, _ = x_prompt.shape
    dec_batch, dec_seq, _ = x_sample.shape
    assert dec_seq == 1
    past_len = 16384
    w = dict(
        lb_logits=lb_logits,
        norm_mix=norm_mix.reshape(DEPTH, 1, D_MODEL),
        w_in=w_in,
        w_o=w_o,
        hg_norm=hg_norm.reshape(DEPTH, 1, HG_WIDTH),
        pool_w=pool_w.astype(BF16),
        pool_scale=pool_scale.reshape(DEPTH, 1, POOL_WIDTH),
        norm_ffn=norm_ffn.reshape(DEPTH, 1, D_MODEL),
        ffn_w_gate=ffn_w_gate.astype(BF16)[:, None],
        ffn_w_up=ffn_w_up.astype(BF16)[:, None],
        ffn_w_down=ffn_w_down.astype(BF16)[:, None],
        router=jnp.pad(router, ((0, 0), (0, 0), (0, LANES - N_EXPERTS))),
        router_t=jnp.swapaxes(router, 1, 2),
        moe_w_gate=moe_w_gate.astype(BF16),
        moe_w_up=moe_w_up.astype(BF16),
        moe_w_down=moe_w_down.astype(BF16),
        norm_final=norm_final.reshape(1, D_MODEL),
    )
    yp, sp, bp = _trunk(x_prompt.reshape(batch * seq, D_MODEL), None, None, 0, w, seq, 512)
    ys, ss, bs = _trunk(x_sample.reshape(dec_batch, D_MODEL), state_hgrn, state_pool, past_len, w, 1, 128)
    return (yp.reshape(batch, seq, D_MODEL), ys.reshape(dec_batch, 1, D_MODEL), sp, ss, bp, bs)
```

```python
import functools

import jax
import jax.numpy as jnp
from jax import lax
from jax.experimental import pallas as pl
from jax.experimental.pallas import tpu as pltpu

F32 = jnp.float32
BF16 = jnp.bfloat16

D_MODEL = 1024
DEPTH = 4
HG_WIDTH = 512
HG_HEADS = 4
HG_D = 128
POOL_WIDTH = 512
POOL_WINDOWS = (2, 4, 8, 16)
POOL_GROUP_W = 128
POOL_BUF = 15
IN_WIDTH = 4 * HG_WIDTH + POOL_WIDTH
ACT_WIDTH = 4 * HG_WIDTH + POOL_WIDTH
ACT_POOL_PART = 4
N_EXPERTS = 8
EPS = 1e-6
LANES = 128
SUBLANES = 8
VMEM_LIMIT = 56 * 1024 * 1024
HGRN_CHUNK = 128


def _params(*sem):
    return pltpu.CompilerParams(dimension_semantics=sem, vmem_limit_bytes=VMEM_LIMIT)


def _rmsnorm(x, g):
    return x * lax.rsqrt(jnp.mean(x * x, axis=-1, keepdims=True) + EPS) * g


def _silu(x):
    return x * jax.nn.sigmoid(x)


def _mixer_in_kernel(layer, x_ref, g_ref, w_ref, lbl_ref, o_ref, wb_ref):
    @pl.when(pl.program_id(0) == 0)
    def _():
        wb_ref[...] = w_ref[0].astype(BF16)

    h = _rmsnorm(x_ref[...], g_ref[...])
    p = jnp.dot(h.astype(BF16), wb_ref[...], preferred_element_type=F32)
    lg = lbl_ref[...]
    e = jnp.exp(lg - jnp.max(lg, axis=0, keepdims=True))
    pr = e / jnp.sum(e, axis=0, keepdims=True)
    cum = pr[0:1]
    for j in range(1, layer + 1):
        cum = cum + pr[j:j + 1]
    lb = cum - pr[0:1]
    w = HG_WIDTH
    q, fx, ix, g, u = p[:, :w], p[:, w:2 * w], p[:, 2 * w:3 * w], p[:, 3 * w:4 * w], p[:, 4 * w:]
    f = lb + (1.0 - lb) * jax.nn.sigmoid(fx)
    o_ref[:, 0:w] = _silu(q)
    o_ref[:, w:2 * w] = jnp.log(f)
    o_ref[:, 2 * w:3 * w] = ix
    o_ref[:, 3 * w:4 * w] = _silu(g)
    o_ref[:, 4 * w:] = u


def _mixer_in(x, g, w_in, lb_logits, layer, tm):
    t = x.shape[0]
    return pl.pallas_call(
        functools.partial(_mixer_in_kernel, layer),
        grid=(t // tm,),
        in_specs=[
            pl.BlockSpec((tm, D_MODEL), lambda i: (i, 0)),
            pl.BlockSpec((1, D_MODEL), lambda i: (0, 0)),
            pl.BlockSpec((1, D_MODEL, IN_WIDTH), lambda i: (layer, 0, 0)),
            pl.BlockSpec((DEPTH, HG_WIDTH), lambda i: (0, 0)),
        ],
        out_specs=pl.BlockSpec((tm, ACT_WIDTH), lambda i: (i, 0)),
        out_shape=jax.ShapeDtypeStruct((t, ACT_WIDTH), F32),
        scratch_shapes=[pltpu.VMEM((D_MODEL, IN_WIDTH), BF16)],
        compiler_params=_params("arbitrary"),
        name="mixer_in",
    )(x, g, w_in, lb_logits)


SMALL_LEVELS = (2, 4, 8)


def _level_sum_matrix(n, ti, si):
    half = n // 2
    mid = (ti & (-n)) + (half - 1)
    lo = jnp.where((ti & half) != 0, mid, ti)
    hi = jnp.where((ti & half) != 0, ti, mid)
    return (si > lo) & (si <= hi)


def _level_factors(q, k, b, n, row, arg=None):
    c = b.shape[0]
    half = n // 2
    if arg is None:
        zero = jnp.zeros((half, b.shape[1]), F32)
        qs, ks = [], []
        for r0 in range(0, c, n):
            mid = b[r0 + half - 1:r0 + half, :]
            ks += [k[r0:r0 + half] * jnp.exp(mid - b[r0:r0 + half]), zero]
            qs += [zero, q[r0 + half:r0 + n] * jnp.exp(b[r0 + half:r0 + n] - mid)]
        return jnp.concatenate(qs, axis=0), jnp.concatenate(ks, axis=0)
    right = (row & half) != 0
    qk = jnp.where(right, q, k) * jnp.exp(arg)
    return jnp.where(right, qk, 0.0), jnp.where(right, 0.0, qk)


def _split3(x):
    hi = x.astype(BF16)
    r1 = x - hi.astype(F32)
    mid = r1.astype(BF16)
    lo = (r1 - mid.astype(F32)).astype(BF16)
    return jnp.concatenate([hi, mid, lo], axis=0)


HEAD_PAIRS = HG_HEADS // 2
PAIR_W = 2 * HG_D


def _pair_blocks(a):
    zero = jnp.zeros((a.shape[0], HG_D), a.dtype)
    return jnp.concatenate([jnp.concatenate([a[:, :HG_D], zero], axis=1),
                            jnp.concatenate([zero, a[:, HG_D:]], axis=1)], axis=0)


def _per_head(fn, a):
    return jnp.concatenate([fn(a[:, :HG_D]), fn(a[:, HG_D:])], axis=1)


def _hgrn_prompt_kernel(tl, q_ref, lf_ref, v_ref, g_ref, nrm_ref, o_ref, s_ref, st_ref):
    c = HGRN_CHUNK
    li = pl.program_id(1)

    @pl.when(li == 0)
    def _():
        st_ref[...] = jnp.zeros_like(st_ref)

    row = lax.broadcasted_iota(jnp.int32, (c, PAIR_W), 0)
    ti = lax.broadcasted_iota(jnp.int32, (c, c), 0)
    si = lax.broadcasted_iota(jnp.int32, (c, c), 1)
    sums = jnp.concatenate(
        [jnp.where(m, 1.0, 0.0)
         for m in [si <= ti] + [_level_sum_matrix(n, ti, si) for n in SMALL_LEVELS]], axis=0).astype(BF16)
    sums = jnp.concatenate([sums] * 3, axis=1)
    apart = jnp.concatenate([ti ^ si] * 2, axis=1)
    pi = lax.broadcasted_iota(jnp.int32, (PAIR_W, PAIR_W), 0)
    pj = lax.broadcasted_iota(jnp.int32, (PAIR_W, PAIR_W), 1)
    own = (pi < HG_D) == (pj < HG_D)
    nt = (((1,), (1,)), ((), ()))
    tn = (((0,), (0,)), ((), ()))

    def chunk(ci, carry):
        rows = pl.ds(pl.multiple_of(ci * c, c), c)
        for p in range(HEAD_PAIRS):
            cols = slice(p * PAIR_W, (p + 1) * PAIR_W)
            q, lf, v = q_ref[rows, cols], lf_ref[rows, cols], v_ref[rows, cols]
            k = 1.0 - jnp.exp(lf)
            s1 = jnp.dot(sums, _split3(lf), preferred_element_type=F32)
            b = s1[:c]
            args = {n: s1[(i + 1) * c:(i + 2) * c] for i, n in enumerate(SMALL_LEVELS)}
            vbd = _pair_blocks(v.astype(BF16))
            scores = None
            n = c
            while n >= 2:
                qt, kt = _level_factors(q, k, b, n, row, args.get(n))
                s_n = lax.dot_general(qt.astype(BF16), _pair_blocks(kt.astype(BF16)), nt,
                                      preferred_element_type=F32)
                scores = s_n if scores is None else jnp.where(apart < n, s_n, scores)
                n //= 2
            st = st_ref[p]
            qs = (q * jnp.exp(b)).astype(BF16)
            o = jnp.dot(scores.astype(BF16), vbd, preferred_element_type=F32)
            o = o + lax.dot_general(qs, st.astype(BF16), nt, preferred_element_type=F32)
            o = o + _per_head(lambda a: jnp.broadcast_to(jnp.sum(a, axis=-1, keepdims=True), a.shape), q * k) * v
            bc = b[c - 1:c, :]
            kd = (k * jnp.exp(bc - b)).astype(BF16)
            upd = lax.dot_general(v.astype(BF16), kd, tn, preferred_element_type=F32)
            st_ref[p] = st * jnp.exp(bc) + jnp.where(own, upd, 0.0)
            ms = _per_head(lambda a: jnp.broadcast_to(jnp.mean(a, axis=-1, keepdims=True), a.shape), o * o)
            o = o * lax.rsqrt(ms + EPS) * nrm_ref[:, cols]
            o_ref[rows, cols] = (o * g_ref[rows, cols]).astype(o_ref.dtype)
        return carry

    lax.fori_loop(0, tl // c, chunk, 0, unroll=2)

    @pl.when(li == pl.num_programs(1) - 1)
    def _():
        for p in range(HEAD_PAIRS):
            st = st_ref[p]
            s_ref[0, 2 * p] = st[:HG_D, :HG_D].T
            s_ref[0, 2 * p + 1] = st[HG_D:, HG_D:].T


def _hgrn_prompt(act, hg_norm, batch, seq, tl=1024):
    nl = seq // tl
    blk = lambda part: pl.BlockSpec((tl, HG_WIDTH), lambda b, i: (b * nl + i, part))
    return pl.pallas_call(
        functools.partial(_hgrn_prompt_kernel, tl),
        grid=(batch, nl),
        in_specs=[blk(0), blk(1), blk(2), blk(3),
                  pl.BlockSpec((1, HG_WIDTH), lambda b, i: (0, 0))],
        out_specs=[pl.BlockSpec((tl, HG_WIDTH), lambda b, i: (b * nl + i, 0)),
                   pl.BlockSpec((1, HG_HEADS, HG_D, HG_D), lambda b, i: (b, 0, 0, 0))],
        out_shape=[jax.ShapeDtypeStruct((batch * seq, HG_WIDTH), BF16),
                   jax.ShapeDtypeStruct((batch, HG_HEADS, HG_D, HG_D), F32)],
        scratch_shapes=[pltpu.VMEM((HEAD_PAIRS, PAIR_W, PAIR_W), F32)],
        compiler_params=_params("parallel", "arbitrary"),
        name="hgrn_prompt",
    )(act, act, act, act, hg_norm)


HGRN_STEP_BT = 32


def _hgrn_step_kernel(q_ref, lf_ref, v_ref, g_ref, nrm_ref, s_ref, o_ref, so_ref):
    bt = HGRN_STEP_BT
    q = q_ref[...]
    v = v_ref[...]
    f = jnp.exp(lf_ref[...])
    ri = lax.broadcasted_iota(jnp.int32, (3 * bt, PAIR_W), 0) % bt
    half = lax.broadcasted_iota(jnp.int32, (3 * bt, PAIR_W), 1) // HG_D
    tn = (((0,), (0,)), ((), ()))
    pieces = [_split3(a) for a in (f, 1.0 - f, q)]
    rows = []
    for s in range(0, bt, 2):
        pick = jnp.where(ri == s + half, 1.0, 0.0).astype(BF16)
        fc, kc, qc = [lax.dot_general(p, pick, tn, preferred_element_type=F32) for p in pieces]
        for d in range(2):
            cols = slice(d * HG_D, (d + 1) * HG_D)
            sn = fc[:, cols] * s_ref[0, s + d, 0] + kc[:, cols] * v[s + d:s + d + 1, :]
            so_ref[s + d, 0] = sn
            rows.append(jnp.sum(sn * qc[:, cols], axis=0, keepdims=True))
    o = jnp.concatenate(rows, axis=0)
    o = o * lax.rsqrt(jnp.mean(o * o, axis=-1, keepdims=True) + EPS) * nrm_ref[...]
    o_ref[...] = (o * g_ref[...]).astype(o_ref.dtype)


def _hgrn_step(act, hg_norm, states, layer):
    batch = act.shape[0]
    bt = HGRN_STEP_BT
    blk = lambda part: pl.BlockSpec((bt, HG_D), lambda i, h: (i, part * HG_HEADS + h))
    return pl.pallas_call(
        _hgrn_step_kernel,
        grid=(batch // bt, HG_HEADS),
        in_specs=[blk(0), blk(1), blk(2), blk(3),
                  pl.BlockSpec((1, HG_D), lambda i, h: (0, h)),
                  pl.BlockSpec((1, bt, 1, HG_D, HG_D), lambda i, h: (layer, i, h, 0, 0))],
        out_specs=[pl.BlockSpec((bt, HG_D), lambda i, h: (i, h)),
                   pl.BlockSpec((bt, 1, HG_D, HG_D), lambda i, h: (i, h, 0, 0))],
        out_shape=[jax.ShapeDtypeStruct((batch, HG_WIDTH), BF16),
                   jax.ShapeDtypeStruct(states.shape[1:], F32)],
        compiler_params=_params("parallel", "parallel"),
        name="hgrn_step",
    )(act, act, act, act, hg_norm, states)


def _pool_map(d_groups, wp_ref, scale_ref, o_ref):
    for gi, d in enumerate(d_groups):
        sl = slice(gi * POOL_GROUP_W, (gi + 1) * POOL_GROUP_W)
        y = jnp.dot(d.astype(BF16), wp_ref[gi], preferred_element_type=F32)
        o_ref[:, sl] = (y * scale_ref[:, sl]).astype(o_ref.dtype)


POOL_HIST = 16


def _pool_tile(i, tl, z, carry_ref, wp_ref, scale_ref, o_ref, nb_ref):
    @pl.when(i == 0)
    def _():
        carry_ref[...] = jnp.zeros_like(carry_ref)

    ext = jnp.concatenate([carry_ref[...], z], axis=0)
    sums = {1: ext}
    w = 1
    while w < max(POOL_WINDOWS):
        sums[2 * w] = sums[w] + pltpu.roll(sums[w], w, 0)
        w *= 2
    pos = i * tl + lax.broadcasted_iota(jnp.int32, (tl, POOL_GROUP_W), 0)
    ds = []
    for gi, w in enumerate(POOL_WINDOWS):
        sl = slice(gi * POOL_GROUP_W, (gi + 1) * POOL_GROUP_W)
        cnt = jnp.minimum(pos + 1, w).astype(F32)
        ds.append(sums[w][POOL_HIST:, sl] / cnt - z[:, sl])
    _pool_map(ds, wp_ref, scale_ref, o_ref)
    carry_ref[...] = z[tl - POOL_HIST:, :]

    @pl.when(i == pl.num_programs(1) - 1)
    def _():
        nb_ref[0] = z[tl - POOL_BUF:, :]


POOL_STEP_BT = 16


def _pool_step_kernel(start_pos, u_ref, buf_ref, wp_ref, scale_ref, o_ref, nb_ref):
    u = u_ref[...]
    buf = buf_ref[0]
    ds = []
    for gi, w in enumerate(POOL_WINDOWS):
        sl = slice(gi * POOL_GROUP_W, (gi + 1) * POOL_GROUP_W)
        tot = u[:, sl] + jnp.sum(buf[:, POOL_BUF - (w - 1):, sl], axis=1)
        ds.append(tot / float(min(start_pos + 1, w)) - u[:, sl])
    _pool_map(ds, wp_ref, scale_ref, o_ref)
    nb_ref[:, 0:POOL_BUF - 1, :] = buf[:, 1:, :]
    nb_ref[:, POOL_BUF - 1:, :] = u[:, None, :]


def _pool_step(act, bufs, layer, pool_w, pool_scale, start_pos):
    batch = act.shape[0]
    bt = POOL_STEP_BT
    return pl.pallas_call(
        functools.partial(_pool_step_kernel, start_pos),
        grid=(batch // bt,),
        in_specs=[pl.BlockSpec((bt, POOL_WIDTH), lambda i: (i, ACT_POOL_PART)),
                  pl.BlockSpec((1, bt, POOL_BUF, POOL_WIDTH), lambda i: (layer, i, 0, 0)),
                  pl.BlockSpec((len(POOL_WINDOWS), POOL_GROUP_W, POOL_GROUP_W), lambda i: (0, 0, 0)),
                  pl.BlockSpec((1, POOL_WIDTH), lambda i: (0, 0))],
        out_specs=[pl.BlockSpec((bt, POOL_WIDTH), lambda i: (i, 0)),
                   pl.BlockSpec((bt, POOL_BUF, POOL_WIDTH), lambda i: (i, 0, 0))],
        out_shape=[jax.ShapeDtypeStruct((batch, POOL_WIDTH), BF16),
                   jax.ShapeDtypeStruct(bufs.shape[1:], F32)],
        compiler_params=_params("parallel"),
        name="pool_step",
    )(act, bufs, pool_w, pool_scale)


def _mixer_out_kernel(x_ref, o_ref, p_ref, w_ref, y_ref, wb_ref):
    @pl.when(pl.program_id(0) == 0)
    def _():
        wb_ref[...] = w_ref[0].astype(BF16)

    y = jnp.dot(o_ref[...], wb_ref[0:HG_WIDTH, :], preferred_element_type=F32)
    y = y + jnp.dot(p_ref[...], wb_ref[HG_WIDTH:, :], preferred_element_type=F32)
    y_ref[...] = x_ref[...] + y


def _mixer_out(x, og, yp, w_o, layer, tm):
    t = x.shape[0]
    return pl.pallas_call(
        _mixer_out_kernel,
        grid=(t // tm,),
        in_specs=[pl.BlockSpec((tm, D_MODEL), lambda i: (i, 0)),
                  pl.BlockSpec((tm, HG_WIDTH), lambda i: (i, 0)),
                  pl.BlockSpec((tm, POOL_WIDTH), lambda i: (i, 0)),
                  pl.BlockSpec((1, D_MODEL, D_MODEL), lambda i: (layer, 0, 0))],
        out_specs=pl.BlockSpec((tm, D_MODEL), lambda i: (i, 0)),
        out_shape=jax.ShapeDtypeStruct((t, D_MODEL), F32),
        scratch_shapes=[pltpu.VMEM((D_MODEL, D_MODEL), BF16)],
        compiler_params=_params("arbitrary"),
        name="mixer_out",
    )(x, og, yp, w_o)


def _top2_combine(logits):
    lane = lax.broadcasted_iota(jnp.int32, logits.shape, 1).astype(F32)
    neg = jnp.float32(-jnp.inf)
    lg = jnp.where(lane < N_EXPERTS, logits, neg)
    m1 = jnp.max(lg, axis=-1, keepdims=True)
    i1 = jnp.min(jnp.where(lg == m1, lane, float(LANES)), axis=-1, keepdims=True)
    lg2 = jnp.where(lane == i1, neg, lg)
    m2 = jnp.max(lg2, axis=-1, keepdims=True)
    i2 = jnp.min(jnp.where(lg2 == m2, lane, float(LANES)), axis=-1, keepdims=True)
    e2 = jnp.exp(m2 - m1)
    g1 = 1.0 / (1.0 + e2)
    g2 = e2 / (1.0 + e2)
    return jnp.where(lane == i1, g1, 0.0) + jnp.where(lane == i2, g2, 0.0)


def _ffn_kernel(moe, final, x_ref, g_ref, *rest):
    if moe:
        r_ref, rest = rest[0], rest[1:]
    if final:
        gf_ref, rest = rest[0], rest[1:]
    wg_ref, wu_ref, wd_ref, y_ref, h_ref, acc_ref = rest[:6]
    e, j = pl.program_id(1), pl.program_id(2)
    first = (e == 0) & (j == 0)
    last = (e == pl.num_programs(1) - 1) & (j == pl.num_programs(2) - 1)

    @pl.when(first)
    def _():
        h = _rmsnorm(x_ref[...], g_ref[...])
        h_ref[...] = h.astype(BF16)
        acc_ref[...] = jnp.zeros_like(acc_ref)
        if moe:
            logits = jnp.dot(h, r_ref[...], precision=lax.Precision.HIGHEST, preferred_element_type=F32)
            rest[6][...] = _top2_combine(logits)

    h = h_ref[...]
    gate = jnp.dot(h, wg_ref[0, 0], preferred_element_type=F32)
    up = jnp.dot(h, wu_ref[0, 0], preferred_element_type=F32)
    y = jnp.dot((_silu(gate) * up).astype(BF16), wd_ref[0, 0], preferred_element_type=F32)
    if moe:
        comb = rest[6][...]
        lane = lax.broadcasted_iota(jnp.int32, comb.shape, 1)
        y = jnp.sum(jnp.where(lane == e, comb, 0.0), axis=-1, keepdims=True) * y
    acc_ref[...] += y

    @pl.when(last)
    def _():
        out = x_ref[...] + acc_ref[...]
        if final:
            out = _rmsnorm(out, gf_ref[...])
        y_ref[...] = out


def _ffn(x, g, wg, wu, wd, layer, tm, tf, router=None, final_g=None):
    t = x.shape[0]
    _, ne, _, f = wg.shape
    moe, final = router is not None, final_g is not None
    vec = pl.BlockSpec((1, D_MODEL), lambda i, e, j: (0, 0))
    in_specs = [pl.BlockSpec((tm, D_MODEL), lambda i, e, j: (i, 0)), vec]
    args = [x, g]
    scratch = [pltpu.VMEM((tm, D_MODEL), BF16), pltpu.VMEM((tm, D_MODEL), F32)]
    if moe:
        in_specs.append(pl.BlockSpec((D_MODEL, LANES), lambda i, e, j: (0, 0)))
        args.append(router)
        scratch.append(pltpu.VMEM((tm, LANES), F32))
    if final:
        in_specs.append(vec)
        args.append(final_g)
    in_specs += [pl.BlockSpec((1, 1, D_MODEL, tf), lambda i, e, j: (layer, e, 0, j)),
                 pl.BlockSpec((1, 1, D_MODEL, tf), lambda i, e, j: (layer, e, 0, j)),
                 pl.BlockSpec((1, 1, tf, D_MODEL), lambda i, e, j: (layer, e, j, 0))]
    args += [wg, wu, wd]
    return pl.pallas_call(
        functools.partial(_ffn_kernel, moe, final),
        grid=(t // tm, ne, f // tf),
        in_specs=in_specs,
        out_specs=pl.BlockSpec((tm, D_MODEL), lambda i, e, j: (i, 0)),
        out_shape=jax.ShapeDtypeStruct((t, D_MODEL), F32),
        scratch_shapes=scratch,
        compiler_params=_params("parallel", "arbitrary", "arbitrary"),
        name="ffn_moe" if moe else "ffn_dense",
    )(*args)


def _mixer_tile(b, i, tl, x_ref, o_ref, u_ref, w_ref, wp_ref, scale_ref, nb_ref, wb_ref, carry_ref, yp_ref):
    @pl.when((b == 0) & (i == 0))
    def _():
        wb_ref[...] = w_ref[0].astype(BF16)

    _pool_tile(i, tl, u_ref[...], carry_ref, wp_ref, scale_ref, yp_ref, nb_ref)
    y = jnp.dot(o_ref[...], wb_ref[0:HG_WIDTH, :], preferred_element_type=F32)
    y = y + jnp.dot(yp_ref[...], wb_ref[HG_WIDTH:, :], preferred_element_type=F32)
    return x_ref[...] + y


def _mixer_ffn_kernel(tl, x_ref, o_ref, u_ref, w_ref, wp_ref, scale_ref, g_ref, wg_ref, wu_ref, wd_ref,
                      y_ref, nb_ref, wb_ref, carry_ref, yp_ref, x1_ref, h_ref, acc_ref):
    b, i, j = pl.program_id(0), pl.program_id(1), pl.program_id(2)

    @pl.when(j == 0)
    def _():
        x1 = _mixer_tile(b, i, tl, x_ref, o_ref, u_ref, w_ref, wp_ref, scale_ref, nb_ref, wb_ref, carry_ref, yp_ref)
        x1_ref[...] = x1
        h_ref[...] = _rmsnorm(x1, g_ref[...]).astype(BF16)
        acc_ref[...] = jnp.zeros_like(acc_ref)

    h = h_ref[...]
    gate = jnp.dot(h, wg_ref[0, 0], preferred_element_type=F32)
    up = jnp.dot(h, wu_ref[0, 0], preferred_element_type=F32)
    acc_ref[...] += jnp.dot((_silu(gate) * up).astype(BF16), wd_ref[0, 0], preferred_element_type=F32)

    @pl.when(j == pl.num_programs(2) - 1)
    def _():
        y_ref[...] = x1_ref[...] + acc_ref[...]


def _mixer_specs(nl, tl, layer, tok, fixed):
    ins = [pl.BlockSpec((tl, D_MODEL), tok(0)),
           pl.BlockSpec((tl, HG_WIDTH), tok(0)),
           pl.BlockSpec((tl, POOL_WIDTH), tok(ACT_POOL_PART)),
           pl.BlockSpec((1, D_MODEL, D_MODEL), fixed(layer, 0, 0)),
           pl.BlockSpec((len(POOL_WINDOWS), POOL_GROUP_W, POOL_GROUP_W), fixed(0, 0, 0)),
           pl.BlockSpec((1, POOL_WIDTH), fixed(0, 0))]
    scratch = [pltpu.VMEM((D_MODEL, D_MODEL), BF16), pltpu.VMEM((POOL_HIST, POOL_WIDTH), F32),
               pltpu.VMEM((tl, POOL_WIDTH), BF16)]
    return ins, scratch


def _mixer_ffn(x, og, act, w_o, layer, pool_w, pool_scale, g, wg, wu, wd, wset, batch, seq, tl=512, tf=1408):
    nl = seq // tl
    f = wg.shape[3]
    tok = lambda cols: (lambda b, i, j: (b * nl + i, cols))
    fixed = lambda *idx: (lambda b, i, j: idx)
    ins, scratch = _mixer_specs(nl, tl, layer, tok, fixed)
    ins += [pl.BlockSpec((1, D_MODEL), fixed(0, 0)),
            pl.BlockSpec((1, 1, D_MODEL, tf), lambda b, i, j: (wset, 0, 0, j)),
            pl.BlockSpec((1, 1, D_MODEL, tf), lambda b, i, j: (wset, 0, 0, j)),
            pl.BlockSpec((1, 1, tf, D_MODEL), lambda b, i, j: (wset, 0, j, 0))]
    scratch += [pltpu.VMEM((tl, D_MODEL), F32), pltpu.VMEM((tl, D_MODEL), BF16), pltpu.VMEM((tl, D_MODEL), F32)]
    return pl.pallas_call(
        functools.partial(_mixer_ffn_kernel, tl),
        grid=(batch, nl, f // tf),
        in_specs=ins,
        out_specs=[pl.BlockSpec((tl, D_MODEL), tok(0)),
                   pl.BlockSpec((1, POOL_BUF, POOL_WIDTH), lambda b, i, j: (b, 0, 0))],
        out_shape=[jax.ShapeDtypeStruct((batch * seq, D_MODEL), F32),
                   jax.ShapeDtypeStruct((batch, POOL_BUF, POOL_WIDTH), F32)],
        scratch_shapes=scratch,
        compiler_params=_params("arbitrary", "arbitrary", "arbitrary"),
        name="mixer_ffn",
    )(x, og, act, w_o, pool_w, pool_scale, g, wg, wu, wd)


PACKED_W = D_MODEL // 2


def _pack_bf16_rows(h):
    bits = lax.bitcast_convert_type(h.astype(BF16).astype(F32), jnp.uint32)
    return (bits[:, :PACKED_W] >> 16) | (bits[:, PACKED_W:] & jnp.uint32(0xFFFF0000))


def _unpack_bf16_rows(p):
    lo = lax.bitcast_convert_type(p << 16, F32)
    hi = lax.bitcast_convert_type(p & jnp.uint32(0xFFFF0000), F32)
    return jnp.concatenate([lo, hi], axis=1).astype(BF16)


MOE_TILE = 512
ROUTE_ROWS = 8
NT_DIMS = (((1,), (1,)), ((), ()))


def _mixer_route_kernel(tm, x_ref, o_ref, u_ref, w_ref, wp_ref, scale_ref, g_ref, rt_ref,
                        x1_ref, nb_ref, hp_ref, route_ref, gate_ref, cnt_ref,
                        wb_ref, pool_carry_ref, yp_ref, carry_ref, earlier_ref):
    b, i = pl.program_id(0), pl.program_id(1)

    @pl.when((b == 0) & (i == 0))
    def _():
        carry_ref[...] = jnp.zeros_like(carry_ref)
        t0 = lax.broadcasted_iota(jnp.int32, (tm, tm), 0)
        t1 = lax.broadcasted_iota(jnp.int32, (tm, tm), 1)
        earlier_ref[...] = jnp.where(t0 < t1, 1.0, 0.0).astype(BF16)

    x1 = _mixer_tile(b, i, tm, x_ref, o_ref, u_ref, w_ref, wp_ref, scale_ref, nb_ref, wb_ref, pool_carry_ref, yp_ref)
    x1_ref[...] = x1
    h = _rmsnorm(x1, g_ref[...])
    hp_ref[...] = _pack_bf16_rows(h)
    lt = lax.dot_general(rt_ref[...], h, NT_DIMS, precision=lax.Precision.HIGHEST,
                         preferred_element_type=F32)
    ex = lax.broadcasted_iota(jnp.int32, lt.shape, 0).astype(F32)
    neg = jnp.float32(-jnp.inf)
    m1 = jnp.max(lt, axis=0, keepdims=True)
    i1 = jnp.min(jnp.where(lt == m1, ex, float(N_EXPERTS)), axis=0, keepdims=True)
    l2 = jnp.where(ex == i1, neg, lt)
    m2 = jnp.max(l2, axis=0, keepdims=True)
    i2 = jnp.min(jnp.where(l2 == m2, ex, float(N_EXPERTS)), axis=0, keepdims=True)
    e2 = jnp.exp(m2 - m1)
    g1 = 1.0 / (1.0 + e2)
    g2 = e2 / (1.0 + e2)
    sel1, sel2 = ex == i1, ex == i2
    member = jnp.where(sel1 | sel2, 1.0, 0.0)
    rank = jnp.dot(member.astype(BF16), earlier_ref[...], preferred_element_type=F32) + carry_ref[:, 0:1]
    route_ref[...] = jnp.zeros_like(route_ref)
    route_ref[0:1, :] = i1.astype(jnp.int32)
    route_ref[1:2, :] = jnp.sum(jnp.where(sel1, rank, 0.0), axis=0, keepdims=True).astype(jnp.int32)
    route_ref[2:3, :] = i2.astype(jnp.int32)
    route_ref[3:4, :] = jnp.sum(jnp.where(sel2, rank, 0.0), axis=0, keepdims=True).astype(jnp.int32)
    carry_ref[...] += jnp.sum(member, axis=1, keepdims=True)
    cnt_ref[...] = carry_ref[...]
    row = lax.broadcasted_iota(jnp.int32, (LANES, tm), 0)
    gate_ref[...] = jnp.where(row == 0, g1, jnp.where(row == 1, g2, 0.0)).T


def _mixer_route(x, og, act, w_o, layer, pool_w, pool_scale, g, router_t, batch, seq, tm=512):
    nl = seq // tm
    t = batch * seq
    tok = lambda cols: (lambda b, i: (b * nl + i, cols))
    fixed = lambda *idx: (lambda b, i: idx)
    ins, scratch = _mixer_specs(nl, tm, layer, tok, fixed)
    ins += [pl.BlockSpec((1, D_MODEL), fixed(0, 0)),
            pl.BlockSpec((N_EXPERTS, D_MODEL), fixed(0, 0))]
    scratch += [pltpu.VMEM((N_EXPERTS, LANES), F32), pltpu.VMEM((tm, tm), BF16)]
    return pl.pallas_call(
        functools.partial(_mixer_route_kernel, tm),
        grid=(batch, nl),
        in_specs=ins,
        out_specs=[pl.BlockSpec((tm, D_MODEL), tok(0)),
                   pl.BlockSpec((1, POOL_BUF, POOL_WIDTH), lambda b, i: (b, 0, 0)),
                   pl.BlockSpec((tm, PACKED_W), tok(0)),
                   pl.BlockSpec((ROUTE_ROWS, tm), lambda b, i: (0, b * nl + i)),
                   pl.BlockSpec((tm, LANES), tok(0)),
                   pl.BlockSpec((N_EXPERTS, LANES), fixed(0, 0))],
        out_shape=[jax.ShapeDtypeStruct((t, D_MODEL), F32),
                   jax.ShapeDtypeStruct((batch, POOL_BUF, POOL_WIDTH), F32),
                   jax.ShapeDtypeStruct((t, PACKED_W), jnp.uint32),
                   jax.ShapeDtypeStruct((ROUTE_ROWS, t), jnp.int32),
                   jax.ShapeDtypeStruct((t, LANES), F32),
                   jax.ShapeDtypeStruct((N_EXPERTS, LANES), F32)],
        scratch_shapes=scratch,
        compiler_params=_params("arbitrary", "arbitrary"),
        name="mixer_route",
    )(x, og, act, w_o, pool_w, pool_scale, g, router_t)


def _moe_slots_kernel(base_ref, route_ref, slot_ref):
    r = route_ref[...]
    slot_ref[...] = jnp.zeros_like(slot_ref)
    for k in range(2):
        e, rank = r[2 * k:2 * k + 1, :], r[2 * k + 1:2 * k + 2, :]
        start = jnp.zeros_like(e)
        for j in range(N_EXPERTS):
            start = jnp.where(e == j, base_ref[j], start)
        slot_ref[k:k + 1, :] = start + rank


def _moe_slots(route, base, tm=2048):
    t = route.shape[1]
    return pl.pallas_call(
        _moe_slots_kernel,
        grid_spec=pltpu.PrefetchScalarGridSpec(
            num_scalar_prefetch=1,
            grid=(t // tm,),
            in_specs=[pl.BlockSpec((ROUTE_ROWS, tm), lambda i, *_: (0, i))],
            out_specs=pl.BlockSpec((ROUTE_ROWS, tm), lambda i, *_: (0, i))),
        out_shape=jax.ShapeDtypeStruct((ROUTE_ROWS, t), jnp.int32),
        compiler_params=_params("parallel"),
        name="moe_slots",
    )(base, route)


COMBINE_ROWS = 128


def _rows_done(hbm, rows, sem):
    pltpu.make_async_copy(hbm.at[pl.ds(0, rows)], hbm.at[pl.ds(0, rows)], sem).wait()


def _moe_dispatch_kernel(tm, fs_ref, fe_ref, s0_ref, s1_ref, x_ref, xs_hbm, sem):
    i = pl.program_id(0)

    def row_copy(group, sub, dst_row):
        return pltpu.make_async_copy(x_ref.at[group, pl.ds(sub, 1), :], xs_hbm.at[pl.ds(dst_row, 1)], sem)

    def issue(c, carry):
        for u in range(SUBLANES):
            for k, s_ref in enumerate((s0_ref, s1_ref)):
                row_copy(c, u, s_ref[c * SUBLANES + u]).start(priority=k)
        return carry

    lax.fori_loop(0, tm // SUBLANES, issue, 0)
    _rows_done(xs_hbm, 2 * tm, sem)

    @pl.when(i == pl.num_programs(0) - 1)
    def _():
        for e in range(N_EXPERTS + 1):
            def fill(p, c):
                row_copy(0, 0, p).start()
                return c

            def drain(p, c):
                row_copy(0, 0, p).wait()
                return c

            lax.fori_loop(fs_ref[e], fe_ref[e], fill, 0)
            lax.fori_loop(fs_ref[e], fe_ref[e], drain, 0)


def _moe_dispatch(rows, slots, fill_start, fill_end, n_slots, tm=1024):
    t, width = rows.shape
    slot_spec = pl.BlockSpec((tm,), lambda i, *_: (i,), memory_space=pltpu.SMEM)
    return pl.pallas_call(
        functools.partial(_moe_dispatch_kernel, tm),
        grid_spec=pltpu.PrefetchScalarGridSpec(
            num_scalar_prefetch=2,
            grid=(t // tm,),
            in_specs=[slot_spec, slot_spec,
                      pl.BlockSpec((tm // SUBLANES, SUBLANES, width), lambda i, *_: (i, 0, 0))],
            out_specs=pl.BlockSpec(memory_space=pl.ANY),
            scratch_shapes=[pltpu.SemaphoreType.DMA(())]),
        out_shape=jax.ShapeDtypeStruct((n_slots, width), rows.dtype),
        compiler_params=_params("arbitrary"),
        name="moe_dispatch",
    )(fill_start, fill_end, slots[0], slots[1], rows.reshape(t // SUBLANES, SUBLANES, width))


def _ffn_grouped_kernel(te_ref, nv_ref, x_ref, wg_ref, wu_ref, wd_ref, y_ref):
    i = pl.program_id(0)

    @pl.when(i < nv_ref[0])
    def _():
        h = _unpack_bf16_rows(x_ref[...])
        gate = jnp.dot(h, wg_ref[0, 0], preferred_element_type=F32)
        up = jnp.dot(h, wu_ref[0, 0], preferred_element_type=F32)
        y_ref[...] = jnp.dot((_silu(gate) * up).astype(BF16), wd_ref[0, 0], preferred_element_type=F32)

    @pl.when(i >= nv_ref[0])
    def _():
        y_ref[...] = jnp.zeros_like(y_ref)


def _ffn_grouped(xs, wg, wu, wd, layer, tile_expert, n_valid):
    n_slots = xs.shape[0]
    f = wg.shape[3]
    rows = pl.BlockSpec((MOE_TILE, D_MODEL), lambda i, te, nv: (i, 0))
    return pl.pallas_call(
        _ffn_grouped_kernel,
        grid_spec=pltpu.PrefetchScalarGridSpec(
            num_scalar_prefetch=2,
            grid=(n_slots // MOE_TILE,),
            in_specs=[pl.BlockSpec((MOE_TILE, PACKED_W), lambda i, te, nv: (i, 0)),
                      pl.BlockSpec((1, 1, D_MODEL, f), lambda i, te, nv: (layer, te[i], 0, 0)),
                      pl.BlockSpec((1, 1, D_MODEL, f), lambda i, te, nv: (layer, te[i], 0, 0)),
                      pl.BlockSpec((1, 1, f, D_MODEL), lambda i, te, nv: (layer, te[i], 0, 0))],
            out_specs=rows),
        out_shape=jax.ShapeDtypeStruct((n_slots, D_MODEL), F32),
        compiler_params=_params("arbitrary"),
        name="ffn_grouped",
    )(tile_expert, n_valid, xs, wg, wu, wd)


def _moe_combine_kernel(tm, final, s0_ref, s1_ref, x_ref, gate_ref, *rest):
    if final:
        gf_ref, rest = rest[0], rest[1:]
    ys_hbm, o_ref, y1_ref, y2_ref, sems = rest

    def issue(c, carry):
        for u in range(SUBLANES):
            for k, (s_ref, buf) in enumerate(((s0_ref, y1_ref), (s1_ref, y2_ref))):
                pltpu.make_async_copy(ys_hbm.at[pl.ds(s_ref[c * SUBLANES + u], 1)],
                                      buf.at[c, pl.ds(u, 1), :], sems.at[k]).start(priority=k)
        return carry

    lax.fori_loop(0, tm // SUBLANES, issue, 0)
    for k in range(2):
        _rows_done(ys_hbm, tm, sems.at[k])
    rc = COMBINE_ROWS

    def rows(ci, c):
        r = pl.ds(pl.multiple_of(ci * rc, rc), rc)
        rg = pl.ds(pl.multiple_of(ci * (rc // SUBLANES), rc // SUBLANES), rc // SUBLANES)
        g1 = jnp.broadcast_to(gate_ref[r, 0:1], (rc, LANES))
        g2 = jnp.broadcast_to(gate_ref[r, 1:2], (rc, LANES))
        ssq = jnp.zeros((rc, LANES), F32)
        for j in range(D_MODEL // LANES):
            cols = slice(j * LANES, (j + 1) * LANES)
            y1 = y1_ref[rg, :, cols].reshape(rc, LANES)
            y2 = y2_ref[rg, :, cols].reshape(rc, LANES)
            out = x_ref[r, cols] + g1 * y1 + g2 * y2
            o_ref[r, cols] = out
            ssq = ssq + out * out
        if final:
            scale = lax.rsqrt(jnp.sum(ssq, axis=-1, keepdims=True) * (1.0 / D_MODEL) + EPS)
            for j in range(D_MODEL // LANES):
                cols = slice(j * LANES, (j + 1) * LANES)
                o_ref[r, cols] = o_ref[r, cols] * scale * gf_ref[:, cols]
        return c

    lax.fori_loop(0, tm // rc, rows, 0)


def _moe_combine(x, slots, gates, ys, final_g=None, tm=512):
    t = x.shape[0]
    final = final_g is not None
    slot_spec = pl.BlockSpec((tm,), lambda i: (i,), memory_space=pltpu.SMEM)
    in_specs = [slot_spec, slot_spec,
                pl.BlockSpec((tm, D_MODEL), lambda i: (i, 0)),
                pl.BlockSpec((tm, LANES), lambda i: (i, 0))]
    args = [slots[0], slots[1], x, gates]
    if final:
        in_specs.append(pl.BlockSpec((1, D_MODEL), lambda i: (0, 0)))
        args.append(final_g)
    in_specs.append(pl.BlockSpec(memory_space=pl.ANY))
    args.append(ys)
    return pl.pallas_call(
        functools.partial(_moe_combine_kernel, tm, final),
        grid=(t // tm,),
        in_specs=in_specs,
        out_specs=pl.BlockSpec((tm, D_MODEL), lambda i: (i, 0)),
        out_shape=jax.ShapeDtypeStruct((t, D_MODEL), F32),
        scratch_shapes=[pltpu.VMEM((tm // SUBLANES, SUBLANES, D_MODEL), F32),
                        pltpu.VMEM((tm // SUBLANES, SUBLANES, D_MODEL), F32),
                        pltpu.SemaphoreType.DMA((2,))],
        compiler_params=_params("arbitrary"),
        name="moe_combine",
    )(*args)


def _moe_routed(x, hp, route, gates, counts, wg, wu, wd, layer, final_g=None):
    t = x.shape[0]
    n_tiles = 2 * t // MOE_TILE + N_EXPERTS
    cnt = counts[:, 0].astype(jnp.int32)
    caps = (cnt + MOE_TILE - 1) // MOE_TILE
    cum = jnp.cumsum(caps)
    base = (cum - caps) * MOE_TILE
    n_valid = cum[-1:]
    tile_expert = jnp.minimum(
        jnp.sum((cum[None, :] <= jnp.arange(n_tiles, dtype=jnp.int32)[:, None]).astype(jnp.int32), axis=1),
        N_EXPERTS - 1)
    fill_start = jnp.concatenate([base + cnt, n_valid * MOE_TILE])
    fill_end = jnp.concatenate([base + caps * MOE_TILE, jnp.full((1,), n_tiles * MOE_TILE, jnp.int32)])
    slots = _moe_slots(route, base)
    xs = _moe_dispatch(hp, slots, fill_start, fill_end, n_tiles * MOE_TILE)
    ys = _ffn_grouped(xs, wg, wu, wd, layer, tile_expert, n_valid)
    return _moe_combine(x, slots, gates, ys, final_g)


def _trunk(x, state_hgrn, state_pool, start_pos, w, seq, tm):
    batch = x.shape[0] // seq
    s_out, b_out = [], []
    for l in range(DEPTH):
        act = _mixer_in(x, w["norm_mix"][l], w["w_in"], w["lb_logits"], l, tm)
        j = l // 2
        dense = l % 2 == 0
        final_g = w["norm_final"] if l == DEPTH - 1 else None
        g_ffn = w["norm_ffn"][l]
        if seq > 1:
            og, s_new = _hgrn_prompt(act, w["hg_norm"][l], batch, seq)
            mixer = (x, og, act, w["w_o"], l, w["pool_w"][l], w["pool_scale"][l])
            if dense:
                assert final_g is None
                x, b_new = _mixer_ffn(*mixer, g_ffn, w["ffn_w_gate"], w["ffn_w_up"], w["ffn_w_down"], j,
                                      batch, seq)
            else:
                x, b_new, hp, route, gates, counts = _mixer_route(*mixer, g_ffn, w["router_t"][j], batch, seq)
                x = _moe_routed(x, hp, route, gates, counts, w["moe_w_gate"], w["moe_w_up"],
                                w["moe_w_down"], j, final_g=final_g)
        else:
            og, s_new = _hgrn_step(act, w["hg_norm"][l], state_hgrn, l)
            yp, b_new = _pool_step(act, state_pool, l, w["pool_w"][l], w["pool_scale"][l], start_pos)
            x = _mixer_out(x, og, yp, w["w_o"], l, tm)
            if dense:
                x = _ffn(x, g_ffn, w["ffn_w_gate"], w["ffn_w_up"], w["ffn_w_down"], j, tm, 1408, final_g=final_g)
            else:
                x = _ffn(x, g_ffn, w["moe_w_gate"], w["moe_w_up"], w["moe_w_down"], j, tm, 1408,
                         router=w["router"][j], final_g=final_g)
        s_out.append(s_new)
        b_out.append(b_new)
    return x, jnp.stack(s_out), jnp.stack(b_out)


def kernel(x_prompt, x_sample, state_hgrn, state_pool, lb_logits, norm_mix, w_in, w_o, hg_norm, pool_w,
           pool_scale, norm_ffn, ffn_w_gate, ffn_w_up, ffn_w_down, router, moe_w_gate, moe_w_up, moe_w_down,
           norm_final):
    batch, seq, _ = x_prompt.shape
    dec_batch, dec_seq, _ = x_sample.shape
    assert dec_seq == 1
    past_len = 16384
    w = dict(
        lb_logits=lb_logits,
        norm_mix=norm_mix.reshape(DEPTH, 1, D_MODEL),
        w_in=w_in,
        w_o=w_o,
        hg_norm=hg_norm.reshape(DEPTH, 1, HG_WIDTH),
        pool_w=pool_w.astype(BF16),
        pool_scale=pool_scale.reshape(DEPTH, 1, POOL_WIDTH),
        norm_ffn=norm_ffn.reshape(DEPTH, 1, D_MODEL),
        ffn_w_gate=ffn_w_gate.astype(BF16)[:, None],
        ffn_w_up=ffn_w_up.astype(BF16)[:, None],
        ffn_w_down=ffn_w_down.astype(BF16)[:, None],
        router=jnp.pad(router, ((0, 0), (0, 0), (0, LANES - N_EXPERTS))),
        router_t=jnp.swapaxes(router, 1, 2),
        moe_w_gate=moe_w_gate.astype(BF16),
        moe_w_up=moe_w_up.astype(BF16),
        moe_w_down=moe_w_down.astype(BF16),
        norm_final=norm_final.reshape(1, D_MODEL),
    )
    yp, sp, bp = _trunk(x_prompt.reshape(batch * seq, D_MODEL), None, None, 0, w, seq, 512)
    ys, ss, bs = _trunk(x_sample.reshape(dec_batch, D_MODEL), state_hgrn, state_pool, past_len, w, 1, 128)
    return (yp.reshape(batch, seq, D_MODEL), ys.reshape(dec_batch, 1, D_MODEL), sp, ss, bp, bs)
```

```python
import functools

import jax
import jax.numpy as jnp
from jax import lax
from jax.experimental import pallas as pl
from jax.experimental.pallas import tpu as pltpu

F32 = jnp.float32
BF16 = jnp.bfloat16

D_MODEL = 1024
DEPTH = 4
HG_WIDTH = 512
HG_HEADS = 4
HG_D = 128
POOL_WIDTH = 512
POOL_WINDOWS = (2, 4, 8, 16)
POOL_GROUP_W = 128
POOL_BUF = 15
IN_WIDTH = 4 * HG_WIDTH + POOL_WIDTH
ACT_WIDTH = 4 * HG_WIDTH + POOL_WIDTH
ACT_POOL_PART = 4
N_EXPERTS = 8
EPS = 1e-6
LANES = 128
SUBLANES = 8
VMEM_LIMIT = 56 * 1024 * 1024
HGRN_CHUNK = 128


def _params(*sem):
    return pltpu.CompilerParams(dimension_semantics=sem, vmem_limit_bytes=VMEM_LIMIT)


def _rmsnorm(x, g):
    return x * lax.rsqrt(jnp.mean(x * x, axis=-1, keepdims=True) + EPS) * g


def _silu(x):
    return x * jax.nn.sigmoid(x)


def _mixer_in_kernel(layer, x_ref, g_ref, w_ref, lbl_ref, o_ref, wb_ref):
    @pl.when(pl.program_id(0) == 0)
    def _():
        wb_ref[...] = w_ref[0].astype(BF16)

    h = _rmsnorm(x_ref[...], g_ref[...])
    p = jnp.dot(h.astype(BF16), wb_ref[...], preferred_element_type=F32)
    lg = lbl_ref[...]
    e = jnp.exp(lg - jnp.max(lg, axis=0, keepdims=True))
    pr = e / jnp.sum(e, axis=0, keepdims=True)
    cum = pr[0:1]
    for j in range(1, layer + 1):
        cum = cum + pr[j:j + 1]
    lb = cum - pr[0:1]
    w = HG_WIDTH
    q, fx, ix, g, u = p[:, :w], p[:, w:2 * w], p[:, 2 * w:3 * w], p[:, 3 * w:4 * w], p[:, 4 * w:]
    f = lb + (1.0 - lb) * jax.nn.sigmoid(fx)
    o_ref[:, 0:w] = _silu(q)
    o_ref[:, w:2 * w] = jnp.log(f)
    o_ref[:, 2 * w:3 * w] = ix
    o_ref[:, 3 * w:4 * w] = _silu(g)
    o_ref[:, 4 * w:] = u


def _mixer_in(x, g, w_in, lb_logits, layer, tm):
    t = x.shape[0]
    return pl.pallas_call(
        functools.partial(_mixer_in_kernel, layer),
        grid=(t // tm,),
        in_specs=[
            pl.BlockSpec((tm, D_MODEL), lambda i: (i, 0)),
            pl.BlockSpec((1, D_MODEL), lambda i: (0, 0)),
            pl.BlockSpec((1, D_MODEL, IN_WIDTH), lambda i: (layer, 0, 0)),
            pl.BlockSpec((DEPTH, HG_WIDTH), lambda i: (0, 0)),
        ],
        out_specs=pl.BlockSpec((tm, ACT_WIDTH), lambda i: (i, 0)),
        out_shape=jax.ShapeDtypeStruct((t, ACT_WIDTH), F32),
        scratch_shapes=[pltpu.VMEM((D_MODEL, IN_WIDTH), BF16)],
        compiler_params=_params("arbitrary"),
        name="mixer_in",
    )(x, g, w_in, lb_logits)


SMALL_LEVELS = (2, 4, 8)


def _level_sum_matrix(n, ti, si):
    half = n // 2
    mid = (ti & (-n)) + (half - 1)
    lo = jnp.where((ti & half) != 0, mid, ti)
    hi = jnp.where((ti & half) != 0, ti, mid)
    return (si > lo) & (si <= hi)


def _level_factors(q, k, b, n, row, arg=None):
    c = b.shape[0]
    half = n // 2
    if arg is None:
        zero = jnp.zeros((half, b.shape[1]), F32)
        qs, ks = [], []
        for r0 in range(0, c, n):
            mid = b[r0 + half - 1:r0 + half, :]
            ks += [k[r0:r0 + half] * jnp.exp(mid - b[r0:r0 + half]), zero]
            qs += [zero, q[r0 + half:r0 + n] * jnp.exp(b[r0 + half:r0 + n] - mid)]
        return jnp.concatenate(qs, axis=0), jnp.concatenate(ks, axis=0)
    right = (row & half) != 0
    qk = jnp.where(right, q, k) * jnp.exp(arg)
    return jnp.where(right, qk, 0.0), jnp.where(right, 0.0, qk)


def _split3(x):
    hi = x.astype(BF16)
    r1 = x - hi.astype(F32)
    mid = r1.astype(BF16)
    lo = (r1 - mid.astype(F32)).astype(BF16)
    return jnp.concatenate([hi, mid, lo], axis=0)


HEAD_PAIRS = HG_HEADS // 2
PAIR_W = 2 * HG_D


def _pair_blocks(a):
    zero = jnp.zeros((a.shape[0], HG_D), a.dtype)
    return jnp.concatenate([jnp.concatenate([a[:, :HG_D], zero], axis=1),
                            jnp.concatenate([zero, a[:, HG_D:]], axis=1)], axis=0)


def _per_head(fn, a):
    return jnp.concatenate([fn(a[:, :HG_D]), fn(a[:, HG_D:])], axis=1)


def _hgrn_prompt_kernel(tl, q_ref, lf_ref, v_ref, g_ref, nrm_ref, o_ref, s_ref, st_ref):
    c = HGRN_CHUNK
    li = pl.program_id(1)

    @pl.when(li == 0)
    def _():
        st_ref[...] = jnp.zeros_like(st_ref)

    row = lax.broadcasted_iota(jnp.int32, (c, PAIR_W), 0)
    ti = lax.broadcasted_iota(jnp.int32, (c, c), 0)
    si = lax.broadcasted_iota(jnp.int32, (c, c), 1)
    sums = jnp.concatenate(
        [jnp.where(m, 1.0, 0.0)
         for m in [si <= ti] + [_level_sum_matrix(n, ti, si) for n in SMALL_LEVELS]], axis=0).astype(BF16)
    sums = jnp.concatenate([sums] * 3, axis=1)
    apart = jnp.concatenate([ti ^ si] * 2, axis=1)
    pi = lax.broadcasted_iota(jnp.int32, (PAIR_W, PAIR_W), 0)
    pj = lax.broadcasted_iota(jnp.int32, (PAIR_W, PAIR_W), 1)
    own = (pi < HG_D) == (pj < HG_D)
    nt = (((1,), (1,)), ((), ()))
    tn = (((0,), (0,)), ((), ()))

    def chunk(ci, carry):
        rows = pl.ds(pl.multiple_of(ci * c, c), c)
        for p in range(HEAD_PAIRS):
            cols = slice(p * PAIR_W, (p + 1) * PAIR_W)
            q, lf, v = q_ref[rows, cols], lf_ref[rows, cols], v_ref[rows, cols]
            k = 1.0 - jnp.exp(lf)
            s1 = jnp.dot(sums, _split3(lf), preferred_element_type=F32)
            b = s1[:c]
            args = {n: s1[(i + 1) * c:(i + 2) * c] for i, n in enumerate(SMALL_LEVELS)}
            vbd = _pair_blocks(v.astype(BF16))
            scores = None
            n = c
            while n >= 2:
                qt, kt = _level_factors(q, k, b, n, row, args.get(n))
                s_n = lax.dot_general(qt.astype(BF16), _pair_blocks(kt.astype(BF16)), nt,
                                      preferred_element_type=F32)
                scores = s_n if scores is None else jnp.where(apart < n, s_n, scores)
                n //= 2
            st = st_ref[p]
            qs = (q * jnp.exp(b)).astype(BF16)
            o = jnp.dot(scores.astype(BF16), vbd, preferred_element_type=F32)
            o = o + lax.dot_general(qs, st.astype(BF16), nt, preferred_element_type=F32)
            o = o + _per_head(lambda a: jnp.broadcast_to(jnp.sum(a, axis=-1, keepdims=True), a.shape), q * k) * v
            bc = b[c - 1:c, :]
            kd = (k * jnp.exp(bc - b)).astype(BF16)
            upd = lax.dot_general(v.astype(BF16), kd, tn, preferred_element_type=F32)
            st_ref[p] = st * jnp.exp(bc) + jnp.where(own, upd, 0.0)
            ms = _per_head(lambda a: jnp.broadcast_to(jnp.mean(a, axis=-1, keepdims=True), a.shape), o * o)
            o = o * lax.rsqrt(ms + EPS) * nrm_ref[:, cols]
            o_ref[rows, cols] = (o * g_ref[rows, cols]).astype(o_ref.dtype)
        return carry

    lax.fori_loop(0, tl // c, chunk, 0, unroll=2)

    @pl.when(li == pl.num_programs(1) - 1)
    def _():
        for p in range(HEAD_PAIRS):
            st = st_ref[p]
            s_ref[0, 2 * p] = st[:HG_D, :HG_D].T
            s_ref[0, 2 * p + 1] = st[HG_D:, HG_D:].T


def _hgrn_prompt(act, hg_norm, batch, seq, tl=2048):
    nl = seq // tl
    blk = lambda part: pl.BlockSpec((tl, HG_WIDTH), lambda b, i: (b * nl + i, part))
    return pl.pallas_call(
        functools.partial(_hgrn_prompt_kernel, tl),
        grid=(batch, nl),
        in_specs=[blk(0), blk(1), blk(2), blk(3),
                  pl.BlockSpec((1, HG_WIDTH), lambda b, i: (0, 0))],
        out_specs=[pl.BlockSpec((tl, HG_WIDTH), lambda b, i: (b * nl + i, 0)),
                   pl.BlockSpec((1, HG_HEADS, HG_D, HG_D), lambda b, i: (b, 0, 0, 0))],
        out_shape=[jax.ShapeDtypeStruct((batch * seq, HG_WIDTH), BF16),
                   jax.ShapeDtypeStruct((batch, HG_HEADS, HG_D, HG_D), F32)],
        scratch_shapes=[pltpu.VMEM((HEAD_PAIRS, PAIR_W, PAIR_W), F32)],
        compiler_params=_params("parallel", "arbitrary"),
        name="hgrn_prompt",
    )(act, act, act, act, hg_norm)


HGRN_STEP_BT = 32


def _hgrn_step_kernel(q_ref, lf_ref, v_ref, g_ref, nrm_ref, s_ref, o_ref, so_ref):
    bt = HGRN_STEP_BT
    q = q_ref[...]
    v = v_ref[...]
    f = jnp.exp(lf_ref[...])
    ri = lax.broadcasted_iota(jnp.int32, (3 * bt, PAIR_W), 0) % bt
    half = lax.broadcasted_iota(jnp.int32, (3 * bt, PAIR_W), 1) // HG_D
    tn = (((0,), (0,)), ((), ()))
    pieces = [_split3(a) for a in (f, 1.0 - f, q)]
    rows = []
    for s in range(0, bt, 2):
        pick = jnp.where(ri == s + half, 1.0, 0.0).astype(BF16)
        fc, kc, qc = [lax.dot_general(p, pick, tn, preferred_element_type=F32) for p in pieces]
        for d in range(2):
            cols = slice(d * HG_D, (d + 1) * HG_D)
            sn = fc[:, cols] * s_ref[0, s + d, 0] + kc[:, cols] * v[s + d:s + d + 1, :]
            so_ref[s + d, 0] = sn
            rows.append(jnp.sum(sn * qc[:, cols], axis=0, keepdims=True))
    o = jnp.concatenate(rows, axis=0)
    o = o * lax.rsqrt(jnp.mean(o * o, axis=-1, keepdims=True) + EPS) * nrm_ref[...]
    o_ref[...] = (o * g_ref[...]).astype(o_ref.dtype)


def _hgrn_step(act, hg_norm, states, layer):
    batch = act.shape[0]
    bt = HGRN_STEP_BT
    blk = lambda part: pl.BlockSpec((bt, HG_D), lambda i, h: (i, part * HG_HEADS + h))
    return pl.pallas_call(
        _hgrn_step_kernel,
        grid=(batch // bt, HG_HEADS),
        in_specs=[blk(0), blk(1), blk(2), blk(3),
                  pl.BlockSpec((1, HG_D), lambda i, h: (0, h)),
                  pl.BlockSpec((1, bt, 1, HG_D, HG_D), lambda i, h: (layer, i, h, 0, 0))],
        out_specs=[pl.BlockSpec((bt, HG_D), lambda i, h: (i, h)),
                   pl.BlockSpec((bt, 1, HG_D, HG_D), lambda i, h: (i, h, 0, 0))],
        out_shape=[jax.ShapeDtypeStruct((batch, HG_WIDTH), BF16),
                   jax.ShapeDtypeStruct(states.shape[1:], F32)],
        compiler_params=_params("parallel", "parallel"),
        name="hgrn_step",
    )(act, act, act, act, hg_norm, states)


def _pool_map(d_groups, wp_ref, scale_ref, o_ref):
    for gi, d in enumerate(d_groups):
        sl = slice(gi * POOL_GROUP_W, (gi + 1) * POOL_GROUP_W)
        y = jnp.dot(d.astype(BF16), wp_ref[gi], preferred_element_type=F32)
        o_ref[:, sl] = (y * scale_ref[:, sl]).astype(o_ref.dtype)


POOL_HIST = 16


def _pool_tile(i, tl, z, carry_ref, wp_ref, scale_ref, o_ref, nb_ref):
    @pl.when(i == 0)
    def _():
        carry_ref[...] = jnp.zeros_like(carry_ref)

    ext = jnp.concatenate([carry_ref[...], z], axis=0)
    sums = {1: ext}
    w = 1
    while w < max(POOL_WINDOWS):
        sums[2 * w] = sums[w] + pltpu.roll(sums[w], w, 0)
        w *= 2
    pos = i * tl + lax.broadcasted_iota(jnp.int32, (tl, POOL_GROUP_W), 0)
    ds = []
    for gi, w in enumerate(POOL_WINDOWS):
        sl = slice(gi * POOL_GROUP_W, (gi + 1) * POOL_GROUP_W)
        cnt = jnp.minimum(pos + 1, w).astype(F32)
        ds.append(sums[w][POOL_HIST:, sl] / cnt - z[:, sl])
    _pool_map(ds, wp_ref, scale_ref, o_ref)
    carry_ref[...] = z[tl - POOL_HIST:, :]

    @pl.when(i == pl.num_programs(1) - 1)
    def _():
        nb_ref[0] = z[tl - POOL_BUF:, :]


POOL_STEP_BT = 16


def _pool_step_kernel(start_pos, u_ref, buf_ref, wp_ref, scale_ref, o_ref, nb_ref):
    u = u_ref[...]
    buf = buf_ref[0]
    ds = []
    for gi, w in enumerate(POOL_WINDOWS):
        sl = slice(gi * POOL_GROUP_W, (gi + 1) * POOL_GROUP_W)
        tot = u[:, sl] + jnp.sum(buf[:, POOL_BUF - (w - 1):, sl], axis=1)
        ds.append(tot / float(min(start_pos + 1, w)) - u[:, sl])
    _pool_map(ds, wp_ref, scale_ref, o_ref)
    nb_ref[:, 0:POOL_BUF - 1, :] = buf[:, 1:, :]
    nb_ref[:, POOL_BUF - 1:, :] = u[:, None, :]


def _pool_step(act, bufs, layer, pool_w, pool_scale, start_pos):
    batch = act.shape[0]
    bt = POOL_STEP_BT
    return pl.pallas_call(
        functools.partial(_pool_step_kernel, start_pos),
        grid=(batch // bt,),
        in_specs=[pl.BlockSpec((bt, POOL_WIDTH), lambda i: (i, ACT_POOL_PART)),
                  pl.BlockSpec((1, bt, POOL_BUF, POOL_WIDTH), lambda i: (layer, i, 0, 0)),
                  pl.BlockSpec((len(POOL_WINDOWS), POOL_GROUP_W, POOL_GROUP_W), lambda i: (0, 0, 0)),
                  pl.BlockSpec((1, POOL_WIDTH), lambda i: (0, 0))],
        out_specs=[pl.BlockSpec((bt, POOL_WIDTH), lambda i: (i, 0)),
                   pl.BlockSpec((bt, POOL_BUF, POOL_WIDTH), lambda i: (i, 0, 0))],
        out_shape=[jax.ShapeDtypeStruct((batch, POOL_WIDTH), BF16),
                   jax.ShapeDtypeStruct(bufs.shape[1:], F32)],
        compiler_params=_params("parallel"),
        name="pool_step",
    )(act, bufs, pool_w, pool_scale)


def _mixer_out_kernel(x_ref, o_ref, p_ref, w_ref, y_ref, wb_ref):
    @pl.when(pl.program_id(0) == 0)
    def _():
        wb_ref[...] = w_ref[0].astype(BF16)

    y = jnp.dot(o_ref[...], wb_ref[0:HG_WIDTH, :], preferred_element_type=F32)
    y = y + jnp.dot(p_ref[...], wb_ref[HG_WIDTH:, :], preferred_element_type=F32)
    y_ref[...] = x_ref[...] + y


def _mixer_out(x, og, yp, w_o, layer, tm):
    t = x.shape[0]
    return pl.pallas_call(
        _mixer_out_kernel,
        grid=(t // tm,),
        in_specs=[pl.BlockSpec((tm, D_MODEL), lambda i: (i, 0)),
                  pl.BlockSpec((tm, HG_WIDTH), lambda i: (i, 0)),
                  pl.BlockSpec((tm, POOL_WIDTH), lambda i: (i, 0)),
                  pl.BlockSpec((1, D_MODEL, D_MODEL), lambda i: (layer, 0, 0))],
        out_specs=pl.BlockSpec((tm, D_MODEL), lambda i: (i, 0)),
        out_shape=jax.ShapeDtypeStruct((t, D_MODEL), F32),
        scratch_shapes=[pltpu.VMEM((D_MODEL, D_MODEL), BF16)],
        compiler_params=_params("arbitrary"),
        name="mixer_out",
    )(x, og, yp, w_o)


def _top2_combine(logits):
    lane = lax.broadcasted_iota(jnp.int32, logits.shape, 1).astype(F32)
    neg = jnp.float32(-jnp.inf)
    lg = jnp.where(lane < N_EXPERTS, logits, neg)
    m1 = jnp.max(lg, axis=-1, keepdims=True)
    i1 = jnp.min(jnp.where(lg == m1, lane, float(LANES)), axis=-1, keepdims=True)
    lg2 = jnp.where(lane == i1, neg, lg)
    m2 = jnp.max(lg2, axis=-1, keepdims=True)
    i2 = jnp.min(jnp.where(lg2 == m2, lane, float(LANES)), axis=-1, keepdims=True)
    e2 = jnp.exp(m2 - m1)
    g1 = 1.0 / (1.0 + e2)
    g2 = e2 / (1.0 + e2)
    return jnp.where(lane == i1, g1, 0.0) + jnp.where(lane == i2, g2, 0.0)


def _ffn_kernel(moe, final, x_ref, g_ref, *rest):
    if moe:
        r_ref, rest = rest[0], rest[1:]
    if final:
        gf_ref, rest = rest[0], rest[1:]
    wg_ref, wu_ref, wd_ref, y_ref, h_ref, acc_ref = rest[:6]
    e, j = pl.program_id(1), pl.program_id(2)
    first = (e == 0) & (j == 0)
    last = (e == pl.num_programs(1) - 1) & (j == pl.num_programs(2) - 1)

    @pl.when(first)
    def _():
        h = _rmsnorm(x_ref[...], g_ref[...])
        h_ref[...] = h.astype(BF16)
        acc_ref[...] = jnp.zeros_like(acc_ref)
        if moe:
            logits = jnp.dot(h, r_ref[...], precision=lax.Precision.HIGHEST, preferred_element_type=F32)
            rest[6][...] = _top2_combine(logits)

    h = h_ref[...]
    gate = jnp.dot(h, wg_ref[0, 0], preferred_element_type=F32)
    up = jnp.dot(h, wu_ref[0, 0], preferred_element_type=F32)
    y = jnp.dot((_silu(gate) * up).astype(BF16), wd_ref[0, 0], preferred_element_type=F32)
    if moe:
        comb = rest[6][...]
        lane = lax.broadcasted_iota(jnp.int32, comb.shape, 1)
        y = jnp.sum(jnp.where(lane == e, comb, 0.0), axis=-1, keepdims=True) * y
    acc_ref[...] += y

    @pl.when(last)
    def _():
        out = x_ref[...] + acc_ref[...]
        if final:
            out = _rmsnorm(out, gf_ref[...])
        y_ref[...] = out


def _ffn(x, g, wg, wu, wd, layer, tm, tf, router=None, final_g=None):
    t = x.shape[0]
    _, ne, _, f = wg.shape
    moe, final = router is not None, final_g is not None
    vec = pl.BlockSpec((1, D_MODEL), lambda i, e, j: (0, 0))
    in_specs = [pl.BlockSpec((tm, D_MODEL), lambda i, e, j: (i, 0)), vec]
    args = [x, g]
    scratch = [pltpu.VMEM((tm, D_MODEL), BF16), pltpu.VMEM((tm, D_MODEL), F32)]
    if moe:
        in_specs.append(pl.BlockSpec((D_MODEL, LANES), lambda i, e, j: (0, 0)))
        args.append(router)
        scratch.append(pltpu.VMEM((tm, LANES), F32))
    if final:
        in_specs.append(vec)
        args.append(final_g)
    in_specs += [pl.BlockSpec((1, 1, D_MODEL, tf), lambda i, e, j: (layer, e, 0, j)),
                 pl.BlockSpec((1, 1, D_MODEL, tf), lambda i, e, j: (layer, e, 0, j)),
                 pl.BlockSpec((1, 1, tf, D_MODEL), lambda i, e, j: (layer, e, j, 0))]
    args += [wg, wu, wd]
    return pl.pallas_call(
        functools.partial(_ffn_kernel, moe, final),
        grid=(t // tm, ne, f // tf),
        in_specs=in_specs,
        out_specs=pl.BlockSpec((tm, D_MODEL), lambda i, e, j: (i, 0)),
        out_shape=jax.ShapeDtypeStruct((t, D_MODEL), F32),
        scratch_shapes=scratch,
        compiler_params=_params("parallel", "arbitrary", "arbitrary"),
        name="ffn_moe" if moe else "ffn_dense",
    )(*args)


def _mixer_tile(b, i, tl, x_ref, o_ref, u_ref, w_ref, wp_ref, scale_ref, nb_ref, wb_ref, carry_ref, yp_ref):
    @pl.when((b == 0) & (i == 0))
    def _():
        wb_ref[...] = w_ref[0].astype(BF16)

    _pool_tile(i, tl, u_ref[...], carry_ref, wp_ref, scale_ref, yp_ref, nb_ref)
    y = jnp.dot(o_ref[...], wb_ref[0:HG_WIDTH, :], preferred_element_type=F32)
    y = y + jnp.dot(yp_ref[...], wb_ref[HG_WIDTH:, :], preferred_element_type=F32)
    return x_ref[...] + y


def _mixer_ffn_kernel(tl, x_ref, o_ref, u_ref, w_ref, wp_ref, scale_ref, g_ref, wg_ref, wu_ref, wd_ref,
                      y_ref, nb_ref, wb_ref, carry_ref, yp_ref, x1_ref, h_ref, acc_ref):
    b, i, j = pl.program_id(0), pl.program_id(1), pl.program_id(2)

    @pl.when(j == 0)
    def _():
        x1 = _mixer_tile(b, i, tl, x_ref, o_ref, u_ref, w_ref, wp_ref, scale_ref, nb_ref, wb_ref, carry_ref, yp_ref)
        x1_ref[...] = x1
        h_ref[...] = _rmsnorm(x1, g_ref[...]).astype(BF16)
        acc_ref[...] = jnp.zeros_like(acc_ref)

    h = h_ref[...]
    gate = jnp.dot(h, wg_ref[0, 0], preferred_element_type=F32)
    up = jnp.dot(h, wu_ref[0, 0], preferred_element_type=F32)
    acc_ref[...] += jnp.dot((_silu(gate) * up).astype(BF16), wd_ref[0, 0], preferred_element_type=F32)

    @pl.when(j == pl.num_programs(2) - 1)
    def _():
        y_ref[...] = x1_ref[...] + acc_ref[...]


def _mixer_specs(nl, tl, layer, tok, fixed):
    ins = [pl.BlockSpec((tl, D_MODEL), tok(0)),
           pl.BlockSpec((tl, HG_WIDTH), tok(0)),
           pl.BlockSpec((tl, POOL_WIDTH), tok(ACT_POOL_PART)),
           pl.BlockSpec((1, D_MODEL, D_MODEL), fixed(layer, 0, 0)),
           pl.BlockSpec((len(POOL_WINDOWS), POOL_GROUP_W, POOL_GROUP_W), fixed(0, 0, 0)),
           pl.BlockSpec((1, POOL_WIDTH), fixed(0, 0))]
    scratch = [pltpu.VMEM((D_MODEL, D_MODEL), BF16), pltpu.VMEM((POOL_HIST, POOL_WIDTH), F32),
               pltpu.VMEM((tl, POOL_WIDTH), BF16)]
    return ins, scratch


def _mixer_ffn(x, og, act, w_o, layer, pool_w, pool_scale, g, wg, wu, wd, wset, batch, seq, tl=512, tf=1408):
    nl = seq // tl
    f = wg.shape[3]
    tok = lambda cols: (lambda b, i, j: (b * nl + i, cols))
    fixed = lambda *idx: (lambda b, i, j: idx)
    ins, scratch = _mixer_specs(nl, tl, layer, tok, fixed)
    ins += [pl.BlockSpec((1, D_MODEL), fixed(0, 0)),
            pl.BlockSpec((1, 1, D_MODEL, tf), lambda b, i, j: (wset, 0, 0, j)),
            pl.BlockSpec((1, 1, D_MODEL, tf), lambda b, i, j: (wset, 0, 0, j)),
            pl.BlockSpec((1, 1, tf, D_MODEL), lambda b, i, j: (wset, 0, j, 0))]
    scratch += [pltpu.VMEM((tl, D_MODEL), F32), pltpu.VMEM((tl, D_MODEL), BF16), pltpu.VMEM((tl, D_MODEL), F32)]
    return pl.pallas_call(
        functools.partial(_mixer_ffn_kernel, tl),
        grid=(batch, nl, f // tf),
        in_specs=ins,
        out_specs=[pl.BlockSpec((tl, D_MODEL), tok(0)),
                   pl.BlockSpec((1, POOL_BUF, POOL_WIDTH), lambda b, i, j: (b, 0, 0))],
        out_shape=[jax.ShapeDtypeStruct((batch * seq, D_MODEL), F32),
                   jax.ShapeDtypeStruct((batch, POOL_BUF, POOL_WIDTH), F32)],
        scratch_shapes=scratch,
        compiler_params=_params("arbitrary", "arbitrary", "arbitrary"),
        name="mixer_ffn",
    )(x, og, act, w_o, pool_w, pool_scale, g, wg, wu, wd)


MOE_TILE = 512
ROUTE_ROWS = 8
NT_DIMS = (((1,), (1,)), ((), ()))


def _mixer_route_kernel(tm, x_ref, o_ref, u_ref, w_ref, wp_ref, scale_ref, g_ref, rt_ref,
                        x1_ref, nb_ref, route_ref, gate_ref, cnt_ref,
                        wb_ref, pool_carry_ref, yp_ref, carry_ref, earlier_ref):
    b, i = pl.program_id(0), pl.program_id(1)

    @pl.when((b == 0) & (i == 0))
    def _():
        carry_ref[...] = jnp.zeros_like(carry_ref)
        t0 = lax.broadcasted_iota(jnp.int32, (tm, tm), 0)
        t1 = lax.broadcasted_iota(jnp.int32, (tm, tm), 1)
        earlier_ref[...] = jnp.where(t0 < t1, 1.0, 0.0).astype(BF16)

    x1 = _mixer_tile(b, i, tm, x_ref, o_ref, u_ref, w_ref, wp_ref, scale_ref, nb_ref, wb_ref, pool_carry_ref, yp_ref)
    x1_ref[...] = x1
    h = _rmsnorm(x1, g_ref[...])
    lt = lax.dot_general(rt_ref[...], h, NT_DIMS, precision=lax.Precision.HIGHEST,
                         preferred_element_type=F32)
    ex = lax.broadcasted_iota(jnp.int32, lt.shape, 0).astype(F32)
    neg = jnp.float32(-jnp.inf)
    m1 = jnp.max(lt, axis=0, keepdims=True)
    i1 = jnp.min(jnp.where(lt == m1, ex, float(N_EXPERTS)), axis=0, keepdims=True)
    l2 = jnp.where(ex == i1, neg, lt)
    m2 = jnp.max(l2, axis=0, keepdims=True)
    i2 = jnp.min(jnp.where(l2 == m2, ex, float(N_EXPERTS)), axis=0, keepdims=True)
    e2 = jnp.exp(m2 - m1)
    g1 = 1.0 / (1.0 + e2)
    g2 = e2 / (1.0 + e2)
    sel1, sel2 = ex == i1, ex == i2
    member = jnp.where(sel1 | sel2, 1.0, 0.0)
    rank = jnp.dot(member.astype(BF16), earlier_ref[...], preferred_element_type=F32) + carry_ref[:, 0:1]
    route_ref[...] = jnp.zeros_like(route_ref)
    route_ref[0:1, :] = i1.astype(jnp.int32)
    route_ref[1:2, :] = jnp.sum(jnp.where(sel1, rank, 0.0), axis=0, keepdims=True).astype(jnp.int32)
    route_ref[2:3, :] = i2.astype(jnp.int32)
    route_ref[3:4, :] = jnp.sum(jnp.where(sel2, rank, 0.0), axis=0, keepdims=True).astype(jnp.int32)
    carry_ref[...] += jnp.sum(member, axis=1, keepdims=True)
    cnt_ref[...] = carry_ref[...]
    row = lax.broadcasted_iota(jnp.int32, (LANES, tm), 0)
    gate_ref[...] = jnp.where(row == 0, g1, jnp.where(row == 1, g2, 0.0)).T


def _mixer_route(x, og, act, w_o, layer, pool_w, pool_scale, g, router_t, batch, seq, tm=512):
    nl = seq // tm
    t = batch * seq
    tok = lambda cols: (lambda b, i: (b * nl + i, cols))
    fixed = lambda *idx: (lambda b, i: idx)
    ins, scratch = _mixer_specs(nl, tm, layer, tok, fixed)
    ins += [pl.BlockSpec((1, D_MODEL), fixed(0, 0)),
            pl.BlockSpec((N_EXPERTS, D_MODEL), fixed(0, 0))]
    scratch += [pltpu.VMEM((N_EXPERTS, LANES), F32), pltpu.VMEM((tm, tm), BF16)]
    return pl.pallas_call(
        functools.partial(_mixer_route_kernel, tm),
        grid=(batch, nl),
        in_specs=ins,
        out_specs=[pl.BlockSpec((tm, D_MODEL), tok(0)),
                   pl.BlockSpec((1, POOL_BUF, POOL_WIDTH), lambda b, i: (b, 0, 0)),
                   pl.BlockSpec((ROUTE_ROWS, tm), lambda b, i: (0, b * nl + i)),
                   pl.BlockSpec((tm, LANES), tok(0)),
                   pl.BlockSpec((N_EXPERTS, LANES), fixed(0, 0))],
        out_shape=[jax.ShapeDtypeStruct((t, D_MODEL), F32),
                   jax.ShapeDtypeStruct((batch, POOL_BUF, POOL_WIDTH), F32),
                   jax.ShapeDtypeStruct((ROUTE_ROWS, t), jnp.int32),
                   jax.ShapeDtypeStruct((t, LANES), F32),
                   jax.ShapeDtypeStruct((N_EXPERTS, LANES), F32)],
        scratch_shapes=scratch,
        compiler_params=_params("arbitrary", "arbitrary"),
        name="mixer_route",
    )(x, og, act, w_o, pool_w, pool_scale, g, router_t)


def _moe_slots_kernel(base_ref, route_ref, slot_ref):
    r = route_ref[...]
    slot_ref[...] = jnp.zeros_like(slot_ref)
    for k in range(2):
        e, rank = r[2 * k:2 * k + 1, :], r[2 * k + 1:2 * k + 2, :]
        start = jnp.zeros_like(e)
        for j in range(N_EXPERTS):
            start = jnp.where(e == j, base_ref[j], start)
        slot_ref[k:k + 1, :] = start + rank


def _moe_slots(route, base, tm=2048):
    t = route.shape[1]
    return pl.pallas_call(
        _moe_slots_kernel,
        grid_spec=pltpu.PrefetchScalarGridSpec(
            num_scalar_prefetch=1,
            grid=(t // tm,),
            in_specs=[pl.BlockSpec((ROUTE_ROWS, tm), lambda i, *_: (0, i))],
            out_specs=pl.BlockSpec((ROUTE_ROWS, tm), lambda i, *_: (0, i))),
        out_shape=jax.ShapeDtypeStruct((ROUTE_ROWS, t), jnp.int32),
        compiler_params=_params("parallel"),
        name="moe_slots",
    )(base, route)


COMBINE_ROWS = 128


def _rows_done(hbm, rows, sem):
    pltpu.make_async_copy(hbm.at[pl.ds(0, rows)], hbm.at[pl.ds(0, rows)], sem).wait()


def _moe_dispatch_kernel(tm, fs_ref, fe_ref, s0_ref, s1_ref, x_ref, xs_hbm, sem):
    i = pl.program_id(0)

    def row_copy(group, sub, dst_row):
        return pltpu.make_async_copy(x_ref.at[group, pl.ds(sub, 1), :], xs_hbm.at[pl.ds(dst_row, 1)], sem)

    def issue(c, carry):
        for u in range(SUBLANES):
            for k, s_ref in enumerate((s0_ref, s1_ref)):
                row_copy(c, u, s_ref[c * SUBLANES + u]).start(priority=k)
        return carry

    lax.fori_loop(0, tm // SUBLANES, issue, 0)
    _rows_done(xs_hbm, 2 * tm, sem)

    @pl.when(i == pl.num_programs(0) - 1)
    def _():
        for e in range(N_EXPERTS + 1):
            def fill(p, c):
                row_copy(0, 0, p).start()
                return c

            def drain(p, c):
                row_copy(0, 0, p).wait()
                return c

            lax.fori_loop(fs_ref[e], fe_ref[e], fill, 0)
            lax.fori_loop(fs_ref[e], fe_ref[e], drain, 0)


def _moe_dispatch(rows, slots, fill_start, fill_end, n_slots, tm=1024):
    t, width = rows.shape
    slot_spec = pl.BlockSpec((tm,), lambda i, *_: (i,), memory_space=pltpu.SMEM)
    return pl.pallas_call(
        functools.partial(_moe_dispatch_kernel, tm),
        grid_spec=pltpu.PrefetchScalarGridSpec(
            num_scalar_prefetch=2,
            grid=(t // tm,),
            in_specs=[slot_spec, slot_spec,
                      pl.BlockSpec((tm // SUBLANES, SUBLANES, width), lambda i, *_: (i, 0, 0))],
            out_specs=pl.BlockSpec(memory_space=pl.ANY),
            scratch_shapes=[pltpu.SemaphoreType.DMA(())]),
        out_shape=jax.ShapeDtypeStruct((n_slots, width), rows.dtype),
        compiler_params=_params("arbitrary"),
        name="moe_dispatch",
    )(fill_start, fill_end, slots[0], slots[1], rows.reshape(t // SUBLANES, SUBLANES, width))


def _ffn_grouped_kernel(te_ref, nv_ref, x_ref, g_ref, wg_ref, wu_ref, wd_ref, y_ref):
    i = pl.program_id(0)

    @pl.when(i < nv_ref[0])
    def _():
        h = _rmsnorm(x_ref[...], g_ref[...]).astype(BF16)
        gate = jnp.dot(h, wg_ref[0, 0], preferred_element_type=F32)
        up = jnp.dot(h, wu_ref[0, 0], preferred_element_type=F32)
        y_ref[...] = jnp.dot((_silu(gate) * up).astype(BF16), wd_ref[0, 0], preferred_element_type=F32)

    @pl.when(i >= nv_ref[0])
    def _():
        y_ref[...] = jnp.zeros_like(y_ref)


def _ffn_grouped(xs, g, wg, wu, wd, layer, tile_expert, n_valid):
    n_slots = xs.shape[0]
    f = wg.shape[3]
    rows = pl.BlockSpec((MOE_TILE, D_MODEL), lambda i, te, nv: (i, 0))
    return pl.pallas_call(
        _ffn_grouped_kernel,
        grid_spec=pltpu.PrefetchScalarGridSpec(
            num_scalar_prefetch=2,
            grid=(n_slots // MOE_TILE,),
            in_specs=[rows,
                      pl.BlockSpec((1, D_MODEL), lambda i, te, nv: (0, 0)),
                      pl.BlockSpec((1, 1, D_MODEL, f), lambda i, te, nv: (layer, te[i], 0, 0)),
                      pl.BlockSpec((1, 1, D_MODEL, f), lambda i, te, nv: (layer, te[i], 0, 0)),
                      pl.BlockSpec((1, 1, f, D_MODEL), lambda i, te, nv: (layer, te[i], 0, 0))],
            out_specs=rows),
        out_shape=jax.ShapeDtypeStruct((n_slots, D_MODEL), F32),
        compiler_params=_params("arbitrary"),
        name="ffn_grouped",
    )(tile_expert, n_valid, xs, g, wg, wu, wd)


def _moe_combine_kernel(tm, final, s0_ref, s1_ref, x_ref, gate_ref, *rest):
    if final:
        gf_ref, rest = rest[0], rest[1:]
    ys_hbm, o_ref, y1_ref, y2_ref, sems = rest

    def issue(c, carry):
        for u in range(SUBLANES):
            for k, (s_ref, buf) in enumerate(((s0_ref, y1_ref), (s1_ref, y2_ref))):
                pltpu.make_async_copy(ys_hbm.at[pl.ds(s_ref[c * SUBLANES + u], 1)],
                                      buf.at[c, pl.ds(u, 1), :], sems.at[k]).start(priority=k)
        return carry

    lax.fori_loop(0, tm // SUBLANES, issue, 0)
    for k in range(2):
        _rows_done(ys_hbm, tm, sems.at[k])
    rc = COMBINE_ROWS

    def rows(ci, c):
        r = pl.ds(pl.multiple_of(ci * rc, rc), rc)
        rg = pl.ds(pl.multiple_of(ci * (rc // SUBLANES), rc // SUBLANES), rc // SUBLANES)
        g1 = jnp.broadcast_to(gate_ref[r, 0:1], (rc, LANES))
        g2 = jnp.broadcast_to(gate_ref[r, 1:2], (rc, LANES))
        ssq = jnp.zeros((rc, LANES), F32)
        for j in range(D_MODEL // LANES):
            cols = slice(j * LANES, (j + 1) * LANES)
            y1 = y1_ref[rg, :, cols].reshape(rc, LANES)
            y2 = y2_ref[rg, :, cols].reshape(rc, LANES)
            out = x_ref[r, cols] + g1 * y1 + g2 * y2
            o_ref[r, cols] = out
            ssq = ssq + out * out
        if final:
            scale = lax.rsqrt(jnp.sum(ssq, axis=-1, keepdims=True) * (1.0 / D_MODEL) + EPS)
            for j in range(D_MODEL // LANES):
                cols = slice(j * LANES, (j + 1) * LANES)
                o_ref[r, cols] = o_ref[r, cols] * scale * gf_ref[:, cols]
        return c

    lax.fori_loop(0, tm // rc, rows, 0)


def _moe_combine(x, slots, gates, ys, final_g=None, tm=512):
    t = x.shape[0]
    final = final_g is not None
    slot_spec = pl.BlockSpec((tm,), lambda i: (i,), memory_space=pltpu.SMEM)
    in_specs = [slot_spec, slot_spec,
                pl.BlockSpec((tm, D_MODEL), lambda i: (i, 0)),
                pl.BlockSpec((tm, LANES), lambda i: (i, 0))]
    args = [slots[0], slots[1], x, gates]
    if final:
        in_specs.append(pl.BlockSpec((1, D_MODEL), lambda i: (0, 0)))
        args.append(final_g)
    in_specs.append(pl.BlockSpec(memory_space=pl.ANY))
    args.append(ys)
    return pl.pallas_call(
        functools.partial(_moe_combine_kernel, tm, final),
        grid=(t // tm,),
        in_specs=in_specs,
        out_specs=pl.BlockSpec((tm, D_MODEL), lambda i: (i, 0)),
        out_shape=jax.ShapeDtypeStruct((t, D_MODEL), F32),
        scratch_shapes=[pltpu.VMEM((tm // SUBLANES, SUBLANES, D_MODEL), F32),
                        pltpu.VMEM((tm // SUBLANES, SUBLANES, D_MODEL), F32),
                        pltpu.SemaphoreType.DMA((2,))],
        compiler_params=_params("arbitrary"),
        name="moe_combine",
    )(*args)


def _moe_routed(x, route, gates, counts, g, wg, wu, wd, layer, final_g=None):
    t = x.shape[0]
    n_tiles = 2 * t // MOE_TILE + N_EXPERTS
    cnt = counts[:, 0].astype(jnp.int32)
    caps = (cnt + MOE_TILE - 1) // MOE_TILE
    cum = jnp.cumsum(caps)
    base = (cum - caps) * MOE_TILE
    n_valid = cum[-1:]
    tile_expert = jnp.minimum(
        jnp.sum((cum[None, :] <= jnp.arange(n_tiles, dtype=jnp.int32)[:, None]).astype(jnp.int32), axis=1),
        N_EXPERTS - 1)
    fill_start = jnp.concatenate([base + cnt, n_valid * MOE_TILE])
    fill_end = jnp.concatenate([base + caps * MOE_TILE, jnp.full((1,), n_tiles * MOE_TILE, jnp.int32)])
    slots = _moe_slots(route, base)
    xs = _moe_dispatch(x, slots, fill_start, fill_end, n_tiles * MOE_TILE)
    ys = _ffn_grouped(xs, g, wg, wu, wd, layer, tile_expert, n_valid)
    return _moe_combine(x, slots, gates, ys, final_g)


def _trunk(x, state_hgrn, state_pool, start_pos, w, seq, tm):
    batch = x.shape[0] // seq
    s_out, b_out = [], []
    for l in range(DEPTH):
        act = _mixer_in(x, w["norm_mix"][l], w["w_in"], w["lb_logits"], l, tm)
        j = l // 2
        dense = l % 2 == 0
        final_g = w["norm_final"] if l == DEPTH - 1 else None
        g_ffn = w["norm_ffn"][l]
        if seq > 1:
            og, s_new = _hgrn_prompt(act, w["hg_norm"][l], batch, seq)
            mixer = (x, og, act, w["w_o"], l, w["pool_w"][l], w["pool_scale"][l])
            if dense:
                assert final_g is None
                x, b_new = _mixer_ffn(*mixer, g_ffn, w["ffn_w_gate"], w["ffn_w_up"], w["ffn_w_down"], j,
                                      batch, seq)
            else:
                x, b_new, route, gates, counts = _mixer_route(*mixer, g_ffn, w["router_t"][j], batch, seq)
                x = _moe_routed(x, route, gates, counts, g_ffn, w["moe_w_gate"], w["moe_w_up"],
                                w["moe_w_down"], j, final_g=final_g)
        else:
            og, s_new = _hgrn_step(act, w["hg_norm"][l], state_hgrn, l)
            yp, b_new = _pool_step(act, state_pool, l, w["pool_w"][l], w["pool_scale"][l], start_pos)
            x = _mixer_out(x, og, yp, w["w_o"], l, tm)
            if dense:
                x = _ffn(x, g_ffn, w["ffn_w_gate"], w["ffn_w_up"], w["ffn_w_down"], j, tm, 1408, final_g=final_g)
            else:
                x = _ffn(x, g_ffn, w["moe_w_gate"], w["moe_w_up"], w["moe_w_down"], j, tm, 1408,
                         router=w["router"][j], final_g=final_g)
        s_out.append(s_new)
        b_out.append(b_new)
    return x, jnp.stack(s_out), jnp.stack(b_out)


def kernel(x_prompt, x_sample, state_hgrn, state_pool, lb_logits, norm_mix, w_in, w_o, hg_norm, pool_w,
           pool_scale, norm_ffn, ffn_w_gate, ffn_w_up, ffn_w_down, router, moe_w_gate, moe_w_up, moe_w_down,
           norm_final):
    batch, seq, _ = x_prompt.shape
    dec_batch, dec_seq, _ = x_sample.shape
    assert dec_seq == 1
    past_len = 16384
    w = dict(
        lb_logits=lb_logits,
        norm_mix=norm_mix.reshape(DEPTH, 1, D_MODEL),
        w_in=w_in,
        w_o=w_o,
        hg_norm=hg_norm.reshape(DEPTH, 1, HG_WIDTH),
        pool_w=pool_w.astype(BF16),
        pool_scale=pool_scale.reshape(DEPTH, 1, POOL_WIDTH),
        norm_ffn=norm_ffn.reshape(DEPTH, 1, D_MODEL),
        ffn_w_gate=ffn_w_gate.astype(BF16)[:, None],
        ffn_w_up=ffn_w_up.astype(BF16)[:, None],
        ffn_w_down=ffn_w_down.astype(BF16)[:, None],
        router=jnp.pad(router, ((0, 0), (0, 0), (0, LANES - N_EXPERTS))),
        router_t=jnp.swapaxes(router, 1, 2),
        moe_w_gate=moe_w_gate.astype(BF16),
        moe_w_up=moe_w_up.astype(BF16),
        moe_w_down=moe_w_down.astype(BF16),
        norm_final=norm_final.reshape(1, D_MODEL),
    )
    yp, sp, bp = _trunk(x_prompt.reshape(batch * seq, D_MODEL), None, None, 0, w, seq, 512)
    ys, ss, bs = _trunk(x_sample.reshape(dec_batch, D_MODEL), state_hgrn, state_pool, past_len, w, 1, 128)
    return (yp.reshape(batch, seq, D_MODEL), ys.reshape(dec_batch, 1, D_MODEL), sp, ss, bp, bs)
```

```python
import functools

import jax
import jax.numpy as jnp
from jax import lax
from jax.experimental import pallas as pl
from jax.experimental.pallas import tpu as pltpu

F32 = jnp.float32
BF16 = jnp.bfloat16

D_MODEL = 1024
DEPTH = 4
HG_WIDTH = 512
HG_HEADS = 4
HG_D = 128
POOL_WIDTH = 512
POOL_WINDOWS = (2, 4, 8, 16)
POOL_GROUP_W = 128
POOL_BUF = 15
IN_WIDTH = 4 * HG_WIDTH + POOL_WIDTH
ACT_WIDTH = 4 * HG_WIDTH + POOL_WIDTH
ACT_POOL_PART = 4
N_EXPERTS = 8
EPS = 1e-6
LANES = 128
SUBLANES = 8
VMEM_LIMIT = 56 * 1024 * 1024
HGRN_CHUNK = 128


def _params(*sem):
    return pltpu.CompilerParams(dimension_semantics=sem, vmem_limit_bytes=VMEM_LIMIT)


def _rmsnorm(x, g):
    return x * lax.rsqrt(jnp.mean(x * x, axis=-1, keepdims=True) + EPS) * g


def _silu(x):
    return x * jax.nn.sigmoid(x)


def _mixer_in_kernel(layer, x_ref, g_ref, w_ref, lbl_ref, o_ref, wb_ref):
    @pl.when(pl.program_id(0) == 0)
    def _():
        wb_ref[...] = w_ref[0].astype(BF16)

    h = _rmsnorm(x_ref[...], g_ref[...])
    p = jnp.dot(h.astype(BF16), wb_ref[...], preferred_element_type=F32)
    lg = lbl_ref[...]
    e = jnp.exp(lg - jnp.max(lg, axis=0, keepdims=True))
    pr = e / jnp.sum(e, axis=0, keepdims=True)
    cum = pr[0:1]
    for j in range(1, layer + 1):
        cum = cum + pr[j:j + 1]
    lb = cum - pr[0:1]
    w = HG_WIDTH
    q, fx, ix, g, u = p[:, :w], p[:, w:2 * w], p[:, 2 * w:3 * w], p[:, 3 * w:4 * w], p[:, 4 * w:]
    f = lb + (1.0 - lb) * jax.nn.sigmoid(fx)
    o_ref[:, 0:w] = _silu(q)
    o_ref[:, w:2 * w] = jnp.log(f)
    o_ref[:, 2 * w:3 * w] = ix
    o_ref[:, 3 * w:4 * w] = _silu(g)
    o_ref[:, 4 * w:] = u


def _mixer_in(x, g, w_in, lb_logits, layer, tm):
    t = x.shape[0]
    return pl.pallas_call(
        functools.partial(_mixer_in_kernel, layer),
        grid=(t // tm,),
        in_specs=[
            pl.BlockSpec((tm, D_MODEL), lambda i: (i, 0)),
            pl.BlockSpec((1, D_MODEL), lambda i: (0, 0)),
            pl.BlockSpec((1, D_MODEL, IN_WIDTH), lambda i: (layer, 0, 0)),
            pl.BlockSpec((DEPTH, HG_WIDTH), lambda i: (0, 0)),
        ],
        out_specs=pl.BlockSpec((tm, ACT_WIDTH), lambda i: (i, 0)),
        out_shape=jax.ShapeDtypeStruct((t, ACT_WIDTH), F32),
        scratch_shapes=[pltpu.VMEM((D_MODEL, IN_WIDTH), BF16)],
        compiler_params=_params("arbitrary"),
        name="mixer_in",
    )(x, g, w_in, lb_logits)


SMALL_LEVELS = (2, 4, 8)


def _level_sum_matrix(n, ti, si):
    half = n // 2
    mid = (ti & (-n)) + (half - 1)
    lo = jnp.where((ti & half) != 0, mid, ti)
    hi = jnp.where((ti & half) != 0, ti, mid)
    return (si > lo) & (si <= hi)


def _level_factors(q, k, b, n, row, arg=None):
    c = b.shape[0]
    half = n // 2
    if arg is None:
        zero = jnp.zeros((half, b.shape[1]), F32)
        qs, ks = [], []
        for r0 in range(0, c, n):
            mid = b[r0 + half - 1:r0 + half, :]
            ks += [k[r0:r0 + half] * jnp.exp(mid - b[r0:r0 + half]), zero]
            qs += [q[r0 + half:r0 + n] * jnp.exp(b[r0 + half:r0 + n] - mid)]
        return jnp.concatenate(qs, axis=0), jnp.concatenate(ks, axis=0)
    right = (row & half) != 0
    qk = jnp.where(right, q, k) * jnp.exp(arg)
    return jnp.where(right, qk, 0.0), jnp.where(right, 0.0, qk)


def _split3(x):
    hi = x.astype(BF16)
    r1 = x - hi.astype(F32)
    mid = r1.astype(BF16)
    lo = (r1 - mid.astype(F32)).astype(BF16)
    return jnp.concatenate([hi, mid, lo], axis=0)


HEAD_PAIRS = HG_HEADS // 2
PAIR_W = 2 * HG_D


def _pair_blocks(a):
    zero = jnp.zeros((a.shape[0], HG_D), a.dtype)
    return jnp.concatenate([jnp.concatenate([a[:, :HG_D], zero], axis=1),
                            jnp.concatenate([zero, a[:, HG_D:]], axis=1)], axis=0)


def _per_head(fn, a):
    return jnp.concatenate([fn(a[:, :HG_D]), fn(a[:, HG_D:])], axis=1)


def _hgrn_prompt_kernel(tl, q_ref, lf_ref, v_ref, g_ref, nrm_ref, o_ref, s_ref, st_ref):
    c = HGRN_CHUNK
    li = pl.program_id(1)

    @pl.when(li == 0)
    def _():
        st_ref[...] = jnp.zeros_like(st_ref)

    row = lax.broadcasted_iota(jnp.int32, (c, PAIR_W), 0)
    ti = lax.broadcasted_iota(jnp.int32, (c, c), 0)
    si = lax.broadcasted_iota(jnp.int32, (c, c), 1)
    ones = lambda m: jnp.where(m, 1.0, 0.0).astype(BF16)
    cum3 = jnp.concatenate([ones(si <= ti)] * 3, axis=1)
    lvl2 = jnp.concatenate([jnp.concatenate([jnp.where(_level_sum_matrix(n, ti, si), 1.0, 0.0)
                                             for n in SMALL_LEVELS], axis=0).astype(BF16)] * 2, axis=1)
    apart = jnp.concatenate([ti ^ si] * 2, axis=1)
    pi = lax.broadcasted_iota(jnp.int32, (PAIR_W, PAIR_W), 0)
    pj = lax.broadcasted_iota(jnp.int32, (PAIR_W, PAIR_W), 1)
    own = (pi < HG_D) == (pj < HG_D)
    nt = (((1,), (1,)), ((), ()))
    tn = (((0,), (0,)), ((), ()))

    def chunk(pairs, ci, carry):
        rows = pl.ds(pl.multiple_of(ci * c, c), c)
        for p in pairs:
            cols = slice(p * PAIR_W, (p + 1) * PAIR_W)
            q, lf, v = q_ref[rows, cols], lf_ref[rows, cols], v_ref[rows, cols]
            k = 1.0 - jnp.exp(lf)
            lf3 = _split3(lf)
            b = jnp.dot(cum3, lf3, preferred_element_type=F32)
            s1 = jnp.dot(lvl2, lf3[:2 * c], preferred_element_type=F32)
            args = {n: s1[i * c:(i + 1) * c] for i, n in enumerate(SMALL_LEVELS)}
            vbd = _pair_blocks(v.astype(BF16))
            scores = None
            n = c
            while n >= 2:
                qt, kt = _level_factors(q, k, b, n, row, args.get(n))
                s_n = lax.dot_general(qt.astype(BF16), _pair_blocks(kt.astype(BF16)), nt,
                                      preferred_element_type=F32)
                if n in args:
                    scores = jnp.where(apart < n, s_n, scores)
                else:
                    half = n // 2
                    parts = []
                    for j, r0 in enumerate(range(0, c, n)):
                        new = s_n[j * half:(j + 1) * half]
                        if scores is None:
                            parts += [jnp.zeros_like(new), new]
                        else:
                            keep = scores[r0 + half:r0 + n]
                            parts += [scores[r0:r0 + half], jnp.where(apart[r0 + half:r0 + n] < n, new, keep)]
                    scores = jnp.concatenate(parts, axis=0)
                n //= 2
            st = st_ref[p]
            qs = (q * jnp.exp(b)).astype(BF16)
            o = jnp.dot(scores.astype(BF16), vbd, preferred_element_type=F32)
            o = o + lax.dot_general(qs, st.astype(BF16), nt, preferred_element_type=F32)
            o = o + _per_head(lambda a: jnp.broadcast_to(jnp.sum(a, axis=-1, keepdims=True), a.shape), q * k) * v
            bc = b[c - 1:c, :]
            kd = (k * jnp.exp(bc - b)).astype(BF16)
            upd = lax.dot_general(v.astype(BF16), kd, tn, preferred_element_type=F32)
            st_ref[p] = st * jnp.exp(bc) + jnp.where(own, upd, 0.0)
            ms = _per_head(lambda a: jnp.broadcast_to(jnp.mean(a, axis=-1, keepdims=True), a.shape), o * o)
            o = o * lax.rsqrt(ms + EPS) * nrm_ref[:, cols]
            o_ref[rows, cols] = (o * g_ref[rows, cols]).astype(o_ref.dtype)
        return carry

    lax.fori_loop(0, tl // c, functools.partial(chunk, range(HEAD_PAIRS)), 0, unroll=2)

    @pl.when(li == pl.num_programs(1) - 1)
    def _():
        for p in range(HEAD_PAIRS):
            st = st_ref[p]
            s_ref[0, 2 * p] = st[:HG_D, :HG_D].T
            s_ref[0, 2 * p + 1] = st[HG_D:, HG_D:].T


def _hgrn_prompt(act, hg_norm, batch, seq, tl=2048):
    nl = seq // tl
    blk = lambda part: pl.BlockSpec((tl, HG_WIDTH), lambda b, i: (b * nl + i, part))
    return pl.pallas_call(
        functools.partial(_hgrn_prompt_kernel, tl),
        grid=(batch, nl),
        in_specs=[blk(0), blk(1), blk(2), blk(3),
                  pl.BlockSpec((1, HG_WIDTH), lambda b, i: (0, 0))],
        out_specs=[pl.BlockSpec((tl, HG_WIDTH), lambda b, i: (b * nl + i, 0)),
                   pl.BlockSpec((1, HG_HEADS, HG_D, HG_D), lambda b, i: (b, 0, 0, 0))],
        out_shape=[jax.ShapeDtypeStruct((batch * seq, HG_WIDTH), BF16),
                   jax.ShapeDtypeStruct((batch, HG_HEADS, HG_D, HG_D), F32)],
        scratch_shapes=[pltpu.VMEM((HEAD_PAIRS, PAIR_W, PAIR_W), F32)],
        compiler_params=_params("parallel", "arbitrary"),
        name="hgrn_prompt",
    )(act, act, act, act, hg_norm)


HGRN_STEP_BT = 32


def _hgrn_step_kernel(q_ref, lf_ref, v_ref, g_ref, nrm_ref, s_ref, o_ref, so_ref):
    bt = HGRN_STEP_BT
    q = q_ref[...]
    v = v_ref[...]
    f = jnp.exp(lf_ref[...])
    ri = lax.broadcasted_iota(jnp.int32, (3 * bt, PAIR_W), 0) % bt
    half = lax.broadcasted_iota(jnp.int32, (3 * bt, PAIR_W), 1) // HG_D
    tn = (((0,), (0,)), ((), ()))
    pieces = [_split3(a) for a in (f, 1.0 - f, q)]
    rows = []
    for s in range(0, bt, 2):
        pick = jnp.where(ri == s + half, 1.0, 0.0).astype(BF16)
        fc, kc, qc = [lax.dot_general(p, pick, tn, preferred_element_type=F32) for p in pieces]
        for d in range(2):
            cols = slice(d * HG_D, (d + 1) * HG_D)
            sn = fc[:, cols] * s_ref[0, s + d, 0] + kc[:, cols] * v[s + d:s + d + 1, :]
            so_ref[s + d, 0] = sn
            rows.append(jnp.sum(sn * qc[:, cols], axis=0, keepdims=True))
    o = jnp.concatenate(rows, axis=0)
    o = o * lax.rsqrt(jnp.mean(o * o, axis=-1, keepdims=True) + EPS) * nrm_ref[...]
    o_ref[...] = (o * g_ref[...]).astype(o_ref.dtype)


def _hgrn_step(act, hg_norm, states, layer):
    batch = act.shape[0]
    bt = HGRN_STEP_BT
    blk = lambda part: pl.BlockSpec((bt, HG_D), lambda i, h: (i, part * HG_HEADS + h))
    return pl.pallas_call(
        _hgrn_step_kernel,
        grid=(batch // bt, HG_HEADS),
        in_specs=[blk(0), blk(1), blk(2), blk(3),
                  pl.BlockSpec((1, HG_D), lambda i, h: (0, h)),
                  pl.BlockSpec((1, bt, 1, HG_D, HG_D), lambda i, h: (layer, i, h, 0, 0))],
        out_specs=[pl.BlockSpec((bt, HG_D), lambda i, h: (i, h)),
                   pl.BlockSpec((bt, 1, HG_D, HG_D), lambda i, h: (i, h, 0, 0))],
        out_shape=[jax.ShapeDtypeStruct((batch, HG_WIDTH), BF16),
                   jax.ShapeDtypeStruct(states.shape[1:], F32)],
        compiler_params=_params("parallel", "parallel"),
        name="hgrn_step",
    )(act, act, act, act, hg_norm, states)


def _pool_map(d_groups, wp_ref, scale_ref, o_ref):
    for gi, d in enumerate(d_groups):
        sl = slice(gi * POOL_GROUP_W, (gi + 1) * POOL_GROUP_W)
        y = jnp.dot(d.astype(BF16), wp_ref[gi], preferred_element_type=F32)
        o_ref[:, sl] = (y * scale_ref[:, sl]).astype(o_ref.dtype)


POOL_HIST = 16


def _pool_tile(i, tl, z, carry_ref, wp_ref, scale_ref, o_ref, nb_ref):
    @pl.when(i == 0)
    def _():
        carry_ref[...] = jnp.zeros_like(carry_ref)

    ext = jnp.concatenate([carry_ref[...], z], axis=0)
    sums = {1: ext}
    w = 1
    while w < max(POOL_WINDOWS):
        sums[2 * w] = sums[w] + pltpu.roll(sums[w], w, 0)
        w *= 2
    pos = i * tl + lax.broadcasted_iota(jnp.int32, (tl, POOL_GROUP_W), 0)
    ds = []
    for gi, w in enumerate(POOL_WINDOWS):
        sl = slice(gi * POOL_GROUP_W, (gi + 1) * POOL_GROUP_W)
        cnt = jnp.minimum(pos + 1, w).astype(F32)
        ds.append(sums[w][POOL_HIST:, sl] / cnt - z[:, sl])
    _pool_map(ds, wp_ref, scale_ref, o_ref)
    carry_ref[...] = z[tl - POOL_HIST:, :]

    @pl.when(i == pl.num_programs(1) - 1)
    def _():
        nb_ref[0] = z[tl - POOL_BUF:, :]


POOL_STEP_BT = 16


def _pool_step_kernel(start_pos, u_ref, buf_ref, wp_ref, scale_ref, o_ref, nb_ref):
    u = u_ref[...]
    buf = buf_ref[0]
    ds = []
    for gi, w in enumerate(POOL_WINDOWS):
        sl = slice(gi * POOL_GROUP_W, (gi + 1) * POOL_GROUP_W)
        tot = u[:, sl] + jnp.sum(buf[:, POOL_BUF - (w - 1):, sl], axis=1)
        ds.append(tot / float(min(start_pos + 1, w)) - u[:, sl])
    _pool_map(ds, wp_ref, scale_ref, o_ref)
    nb_ref[:, 0:POOL_BUF - 1, :] = buf[:, 1:, :]
    nb_ref[:, POOL_BUF - 1:, :] = u[:, None, :]


def _pool_step(act, bufs, layer, pool_w, pool_scale, start_pos):
    batch = act.shape[0]
    bt = POOL_STEP_BT
    return pl.pallas_call(
        functools.partial(_pool_step_kernel, start_pos),
        grid=(batch // bt,),
        in_specs=[pl.BlockSpec((bt, POOL_WIDTH), lambda i: (i, ACT_POOL_PART)),
                  pl.BlockSpec((1, bt, POOL_BUF, POOL_WIDTH), lambda i: (layer, i, 0, 0)),
                  pl.BlockSpec((len(POOL_WINDOWS), POOL_GROUP_W, POOL_GROUP_W), lambda i: (0, 0, 0)),
                  pl.BlockSpec((1, POOL_WIDTH), lambda i: (0, 0))],
        out_specs=[pl.BlockSpec((bt, POOL_WIDTH), lambda i: (i, 0)),
                   pl.BlockSpec((bt, POOL_BUF, POOL_WIDTH), lambda i: (i, 0, 0))],
        out_shape=[jax.ShapeDtypeStruct((batch, POOL_WIDTH), BF16),
                   jax.ShapeDtypeStruct(bufs.shape[1:], F32)],
        compiler_params=_params("parallel"),
        name="pool_step",
    )(act, bufs, pool_w, pool_scale)


def _mixer_out_kernel(x_ref, o_ref, p_ref, w_ref, y_ref, wb_ref):
    @pl.when(pl.program_id(0) == 0)
    def _():
        wb_ref[...] = w_ref[0].astype(BF16)

    y = jnp.dot(o_ref[...], wb_ref[0:HG_WIDTH, :], preferred_element_type=F32)
    y = y + jnp.dot(p_ref[...], wb_ref[HG_WIDTH:, :], preferred_element_type=F32)
    y_ref[...] = x_ref[...] + y


def _mixer_out(x, og, yp, w_o, layer, tm):
    t = x.shape[0]
    return pl.pallas_call(
        _mixer_out_kernel,
        grid=(t // tm,),
        in_specs=[pl.BlockSpec((tm, D_MODEL), lambda i: (i, 0)),
                  pl.BlockSpec((tm, HG_WIDTH), lambda i: (i, 0)),
                  pl.BlockSpec((tm, POOL_WIDTH), lambda i: (i, 0)),
                  pl.BlockSpec((1, D_MODEL, D_MODEL), lambda i: (layer, 0, 0))],
        out_specs=pl.BlockSpec((tm, D_MODEL), lambda i: (i, 0)),
        out_shape=jax.ShapeDtypeStruct((t, D_MODEL), F32),
        scratch_shapes=[pltpu.VMEM((D_MODEL, D_MODEL), BF16)],
        compiler_params=_params("arbitrary"),
        name="mixer_out",
    )(x, og, yp, w_o)


def _top2_combine(logits):
    lane = lax.broadcasted_iota(jnp.int32, logits.shape, 1).astype(F32)
    neg = jnp.float32(-jnp.inf)
    lg = jnp.where(lane < N_EXPERTS, logits, neg)
    m1 = jnp.max(lg, axis=-1, keepdims=True)
    i1 = jnp.min(jnp.where(lg == m1, lane, float(LANES)), axis=-1, keepdims=True)
    lg2 = jnp.where(lane == i1, neg, lg)
    m2 = jnp.max(lg2, axis=-1, keepdims=True)
    i2 = jnp.min(jnp.where(lg2 == m2, lane, float(LANES)), axis=-1, keepdims=True)
    e2 = jnp.exp(m2 - m1)
    g1 = 1.0 / (1.0 + e2)
    g2 = e2 / (1.0 + e2)
    return jnp.where(lane == i1, g1, 0.0) + jnp.where(lane == i2, g2, 0.0)


def _ffn_kernel(moe, final, x_ref, g_ref, *rest):
    if moe:
        r_ref, rest = rest[0], rest[1:]
    if final:
        gf_ref, rest = rest[0], rest[1:]
    wg_ref, wu_ref, wd_ref, y_ref, h_ref, acc_ref = rest[:6]
    e, j = pl.program_id(1), pl.program_id(2)
    first = (e == 0) & (j == 0)
    last = (e == pl.num_programs(1) - 1) & (j == pl.num_programs(2) - 1)

    @pl.when(first)
    def _():
        h = _rmsnorm(x_ref[...], g_ref[...])
        h_ref[...] = h.astype(BF16)
        acc_ref[...] = jnp.zeros_like(acc_ref)
        if moe:
            logits = jnp.dot(h, r_ref[...], precision=lax.Precision.HIGHEST, preferred_element_type=F32)
            rest[6][...] = _top2_combine(logits)

    h = h_ref[...]
    gate = jnp.dot(h, wg_ref[0, 0], preferred_element_type=F32)
    up = jnp.dot(h, wu_ref[0, 0], preferred_element_type=F32)
    y = jnp.dot((_silu(gate) * up).astype(BF16), wd_ref[0, 0], preferred_element_type=F32)
    if moe:
        comb = rest[6][...]
        lane = lax.broadcasted_iota(jnp.int32, comb.shape, 1)
        y = jnp.sum(jnp.where(lane == e, comb, 0.0), axis=-1, keepdims=True) * y
    acc_ref[...] += y

    @pl.when(last)
    def _():
        out = x_ref[...] + acc_ref[...]
        if final:
            out = _rmsnorm(out, gf_ref[...])
        y_ref[...] = out


def _ffn(x, g, wg, wu, wd, layer, tm, tf, router=None, final_g=None):
    t = x.shape[0]
    _, ne, _, f = wg.shape
    moe, final = router is not None, final_g is not None
    vec = pl.BlockSpec((1, D_MODEL), lambda i, e, j: (0, 0))
    in_specs = [pl.BlockSpec((tm, D_MODEL), lambda i, e, j: (i, 0)), vec]
    args = [x, g]
    scratch = [pltpu.VMEM((tm, D_MODEL), BF16), pltpu.VMEM((tm, D_MODEL), F32)]
    if moe:
        in_specs.append(pl.BlockSpec((D_MODEL, LANES), lambda i, e, j: (0, 0)))
        args.append(router)
        scratch.append(pltpu.VMEM((tm, LANES), F32))
    if final:
        in_specs.append(vec)
        args.append(final_g)
    in_specs += [pl.BlockSpec((1, 1, D_MODEL, tf), lambda i, e, j: (layer, e, 0, j)),
                 pl.BlockSpec((1, 1, D_MODEL, tf), lambda i, e, j: (layer, e, 0, j)),
                 pl.BlockSpec((1, 1, tf, D_MODEL), lambda i, e, j: (layer, e, j, 0))]
    args += [wg, wu, wd]
    return pl.pallas_call(
        functools.partial(_ffn_kernel, moe, final),
        grid=(t // tm, ne, f // tf),
        in_specs=in_specs,
        out_specs=pl.BlockSpec((tm, D_MODEL), lambda i, e, j: (i, 0)),
        out_shape=jax.ShapeDtypeStruct((t, D_MODEL), F32),
        scratch_shapes=scratch,
        compiler_params=_params("parallel", "arbitrary", "arbitrary"),
        name="ffn_moe" if moe else "ffn_dense",
    )(*args)


def _mixer_tile(b, i, tl, x_ref, o_ref, u_ref, w_ref, wp_ref, scale_ref, nb_ref, wb_ref, carry_ref, yp_ref):
    @pl.when((b == 0) & (i == 0))
    def _():
        wb_ref[...] = w_ref[0].astype(BF16)

    _pool_tile(i, tl, u_ref[...], carry_ref, wp_ref, scale_ref, yp_ref, nb_ref)
    y = jnp.dot(o_ref[...], wb_ref[0:HG_WIDTH, :], preferred_element_type=F32)
    y = y + jnp.dot(yp_ref[...], wb_ref[HG_WIDTH:, :], preferred_element_type=F32)
    return x_ref[...] + y


def _mixer_ffn_kernel(tl, x_ref, o_ref, u_ref, w_ref, wp_ref, scale_ref, g_ref, wg_ref, wu_ref, wd_ref,
                      y_ref, nb_ref, wb_ref, carry_ref, yp_ref, x1_ref, h_ref, acc_ref):
    b, i, j = pl.program_id(0), pl.program_id(1), pl.program_id(2)

    @pl.when(j == 0)
    def _():
        x1 = _mixer_tile(b, i, tl, x_ref, o_ref, u_ref, w_ref, wp_ref, scale_ref, nb_ref, wb_ref, carry_ref, yp_ref)
        x1_ref[...] = x1
        h_ref[...] = _rmsnorm(x1, g_ref[...]).astype(BF16)
        acc_ref[...] = jnp.zeros_like(acc_ref)

    h = h_ref[...]
    gate = jnp.dot(h, wg_ref[0, 0], preferred_element_type=F32)
    up = jnp.dot(h, wu_ref[0, 0], preferred_element_type=F32)
    acc_ref[...] += jnp.dot((_silu(gate) * up).astype(BF16), wd_ref[0, 0], preferred_element_type=F32)

    @pl.when(j == pl.num_programs(2) - 1)
    def _():
        y_ref[...] = x1_ref[...] + acc_ref[...]


def _mixer_specs(nl, tl, layer, tok, fixed):
    ins = [pl.BlockSpec((tl, D_MODEL), tok(0)),
           pl.BlockSpec((tl, HG_WIDTH), tok(0)),
           pl.BlockSpec((tl, POOL_WIDTH), tok(ACT_POOL_PART)),
           pl.BlockSpec((1, D_MODEL, D_MODEL), fixed(layer, 0, 0)),
           pl.BlockSpec((len(POOL_WINDOWS), POOL_GROUP_W, POOL_GROUP_W), fixed(0, 0, 0)),
           pl.BlockSpec((1, POOL_WIDTH), fixed(0, 0))]
    scratch = [pltpu.VMEM((D_MODEL, D_MODEL), BF16), pltpu.VMEM((POOL_HIST, POOL_WIDTH), F32),
               pltpu.VMEM((tl, POOL_WIDTH), BF16)]
    return ins, scratch


def _mixer_ffn(x, og, act, w_o, layer, pool_w, pool_scale, g, wg, wu, wd, wset, batch, seq, tl=512, tf=1408):
    nl = seq // tl
    f = wg.shape[3]
    tok = lambda cols: (lambda b, i, j: (b * nl + i, cols))
    fixed = lambda *idx: (lambda b, i, j: idx)
    ins, scratch = _mixer_specs(nl, tl, layer, tok, fixed)
    ins += [pl.BlockSpec((1, D_MODEL), fixed(0, 0)),
            pl.BlockSpec((1, 1, D_MODEL, tf), lambda b, i, j: (wset, 0, 0, j)),
            pl.BlockSpec((1, 1, D_MODEL, tf), lambda b, i, j: (wset, 0, 0, j)),
            pl.BlockSpec((1, 1, tf, D_MODEL), lambda b, i, j: (wset, 0, j, 0))]
    scratch += [pltpu.VMEM((tl, D_MODEL), F32), pltpu.VMEM((tl, D_MODEL), BF16), pltpu.VMEM((tl, D_MODEL), F32)]
    return pl.pallas_call(
        functools.partial(_mixer_ffn_kernel, tl),
        grid=(batch, nl, f // tf),
        in_specs=ins,
        out_specs=[pl.BlockSpec((tl, D_MODEL), tok(0)),
                   pl.BlockSpec((1, POOL_BUF, POOL_WIDTH), lambda b, i, j: (b, 0, 0))],
        out_shape=[jax.ShapeDtypeStruct((batch * seq, D_MODEL), F32),
                   jax.ShapeDtypeStruct((batch, POOL_BUF, POOL_WIDTH), F32)],
        scratch_shapes=scratch,
        compiler_params=_params("arbitrary", "arbitrary", "arbitrary"),
        name="mixer_ffn",
    )(x, og, act, w_o, pool_w, pool_scale, g, wg, wu, wd)


MOE_TILE = 512
ROUTE_ROWS = 8
NT_DIMS = (((1,), (1,)), ((), ()))


def _mixer_route_kernel(tm, x_ref, o_ref, u_ref, w_ref, wp_ref, scale_ref, g_ref, rt_ref,
                        x1_ref, nb_ref, route_ref, gate_ref, cnt_ref,
                        wb_ref, pool_carry_ref, yp_ref, carry_ref, earlier_ref):
    b, i = pl.program_id(0), pl.program_id(1)

    @pl.when((b == 0) & (i == 0))
    def _():
        carry_ref[...] = jnp.zeros_like(carry_ref)
        t0 = lax.broadcasted_iota(jnp.int32, (tm, tm), 0)
        t1 = lax.broadcasted_iota(jnp.int32, (tm, tm), 1)
        earlier_ref[...] = jnp.where(t0 < t1, 1.0, 0.0).astype(BF16)

    x1 = _mixer_tile(b, i, tm, x_ref, o_ref, u_ref, w_ref, wp_ref, scale_ref, nb_ref, wb_ref, pool_carry_ref, yp_ref)
    x1_ref[...] = x1
    h = _rmsnorm(x1, g_ref[...])
    lt = lax.dot_general(rt_ref[...], h, NT_DIMS, precision=lax.Precision.HIGHEST,
                         preferred_element_type=F32)
    ex = lax.broadcasted_iota(jnp.int32, lt.shape, 0).astype(F32)
    neg = jnp.float32(-jnp.inf)
    m1 = jnp.max(lt, axis=0, keepdims=True)
    i1 = jnp.min(jnp.where(lt == m1, ex, float(N_EXPERTS)), axis=0, keepdims=True)
    l2 = jnp.where(ex == i1, neg, lt)
    m2 = jnp.max(l2, axis=0, keepdims=True)
    i2 = jnp.min(jnp.where(l2 == m2, ex, float(N_EXPERTS)), axis=0, keepdims=True)
    e2 = jnp.exp(m2 - m1)
    g1 = 1.0 / (1.0 + e2)
    g2 = e2 / (1.0 + e2)
    sel1, sel2 = ex == i1, ex == i2
    member = jnp.where(sel1 | sel2, 1.0, 0.0)
    rank = jnp.dot(member.astype(BF16), earlier_ref[...], preferred_element_type=F32) + carry_ref[:, 0:1]
    route_ref[...] = jnp.zeros_like(route_ref)
    route_ref[0:1, :] = i1.astype(jnp.int32)
    route_ref[1:2, :] = jnp.sum(jnp.where(sel1, rank, 0.0), axis=0, keepdims=True).astype(jnp.int32)
    route_ref[2:3, :] = i2.astype(jnp.int32)
    route_ref[3:4, :] = jnp.sum(jnp.where(sel2, rank, 0.0), axis=0, keepdims=True).astype(jnp.int32)
    carry_ref[...] += jnp.sum(member, axis=1, keepdims=True)
    cnt_ref[...] = carry_ref[...]
    row = lax.broadcasted_iota(jnp.int32, (LANES, tm), 0)
    gate_ref[...] = jnp.where(row == 0, g1, jnp.where(row == 1, g2, 0.0)).T


def _mixer_route(x, og, act, w_o, layer, pool_w, pool_scale, g, router_t, batch, seq, tm=512):
    nl = seq // tm
    t = batch * seq
    tok = lambda cols: (lambda b, i: (b * nl + i, cols))
    fixed = lambda *idx: (lambda b, i: idx)
    ins, scratch = _mixer_specs(nl, tm, layer, tok, fixed)
    ins += [pl.BlockSpec((1, D_MODEL), fixed(0, 0)),
            pl.BlockSpec((N_EXPERTS, D_MODEL), fixed(0, 0))]
    scratch += [pltpu.VMEM((N_EXPERTS, LANES), F32), pltpu.VMEM((tm, tm), BF16)]
    return pl.pallas_call(
        functools.partial(_mixer_route_kernel, tm),
        grid=(batch, nl),
        in_specs=ins,
        out_specs=[pl.BlockSpec((tm, D_MODEL), tok(0)),
                   pl.BlockSpec((1, POOL_BUF, POOL_WIDTH), lambda b, i: (b, 0, 0)),
                   pl.BlockSpec((ROUTE_ROWS, tm), lambda b, i: (0, b * nl + i)),
                   pl.BlockSpec((tm, LANES), tok(0)),
                   pl.BlockSpec((N_EXPERTS, LANES), fixed(0, 0))],
        out_shape=[jax.ShapeDtypeStruct((t, D_MODEL), F32),
                   jax.ShapeDtypeStruct((batch, POOL_BUF, POOL_WIDTH), F32),
                   jax.ShapeDtypeStruct((ROUTE_ROWS, t), jnp.int32),
                   jax.ShapeDtypeStruct((t, LANES), F32),
                   jax.ShapeDtypeStruct((N_EXPERTS, LANES), F32)],
        scratch_shapes=scratch,
        compiler_params=_params("arbitrary", "arbitrary"),
        name="mixer_route",
    )(x, og, act, w_o, pool_w, pool_scale, g, router_t)


def _moe_slots_kernel(base_ref, route_ref, slot_ref):
    r = route_ref[...]
    slot_ref[...] = jnp.zeros_like(slot_ref)
    for k in range(2):
        e, rank = r[2 * k:2 * k + 1, :], r[2 * k + 1:2 * k + 2, :]
        start = jnp.zeros_like(e)
        for j in range(N_EXPERTS):
            start = jnp.where(e == j, base_ref[j], start)
        slot_ref[k:k + 1, :] = start + rank


def _moe_slots(route, base, tm=2048):
    t = route.shape[1]
    return pl.pallas_call(
        _moe_slots_kernel,
        grid_spec=pltpu.PrefetchScalarGridSpec(
            num_scalar_prefetch=1,
            grid=(t // tm,),
            in_specs=[pl.BlockSpec((ROUTE_ROWS, tm), lambda i, *_: (0, i))],
            out_specs=pl.BlockSpec((ROUTE_ROWS, tm), lambda i, *_: (0, i))),
        out_shape=jax.ShapeDtypeStruct((ROUTE_ROWS, t), jnp.int32),
        compiler_params=_params("parallel"),
        name="moe_slots",
    )(base, route)


COMBINE_ROWS = 128


def _rows_done(hbm, rows, sem):
    pltpu.make_async_copy(hbm.at[pl.ds(0, rows)], hbm.at[pl.ds(0, rows)], sem).wait()


def _moe_dispatch_kernel(tm, fs_ref, fe_ref, s0_ref, s1_ref, x_ref, xs_hbm, sem):
    i = pl.program_id(0)

    def row_copy(group, sub, dst_row):
        return pltpu.make_async_copy(x_ref.at[group, pl.ds(sub, 1), :], xs_hbm.at[pl.ds(dst_row, 1)], sem)

    def issue(c, carry):
        for u in range(SUBLANES):
            for k, s_ref in enumerate((s0_ref, s1_ref)):
                row_copy(c, u, s_ref[c * SUBLANES + u]).start(priority=k)
        return carry

    lax.fori_loop(0, tm // SUBLANES, issue, 0)
    _rows_done(xs_hbm, 2 * tm, sem)

    @pl.when(i == pl.num_programs(0) - 1)
    def _():
        for e in range(N_EXPERTS + 1):
            def fill(p, c):
                row_copy(0, 0, p).start()
                return c

            def drain(p, c):
                row_copy(0, 0, p).wait()
                return c

            lax.fori_loop(fs_ref[e], fe_ref[e], fill, 0)
            lax.fori_loop(fs_ref[e], fe_ref[e], drain, 0)


def _moe_dispatch(rows, slots, fill_start, fill_end, n_slots, tm=1024):
    t, width = rows.shape
    slot_spec = pl.BlockSpec((tm,), lambda i, *_: (i,), memory_space=pltpu.SMEM)
    return pl.pallas_call(
        functools.partial(_moe_dispatch_kernel, tm),
        grid_spec=pltpu.PrefetchScalarGridSpec(
            num_scalar_prefetch=2,
            grid=(t // tm,),
            in_specs=[slot_spec, slot_spec,
                      pl.BlockSpec((tm // SUBLANES, SUBLANES, width), lambda i, *_: (i, 0, 0))],
            out_specs=pl.BlockSpec(memory_space=pl.ANY),
            scratch_shapes=[pltpu.SemaphoreType.DMA(())]),
        out_shape=jax.ShapeDtypeStruct((n_slots, width), rows.dtype),
        compiler_params=_params("arbitrary"),
        name="moe_dispatch",
    )(fill_start, fill_end, slots[0], slots[1], rows.reshape(t // SUBLANES, SUBLANES, width))


def _ffn_grouped_kernel(te_ref, nv_ref, x_ref, g_ref, wg_ref, wu_ref, wd_ref, y_ref):
    i = pl.program_id(0)

    @pl.when(i < nv_ref[0])
    def _():
        h = _rmsnorm(x_ref[...], g_ref[...]).astype(BF16)
        gate = jnp.dot(h, wg_ref[0, 0], preferred_element_type=F32)
        up = jnp.dot(h, wu_ref[0, 0], preferred_element_type=F32)
        y_ref[...] = jnp.dot((_silu(gate) * up).astype(BF16), wd_ref[0, 0], preferred_element_type=F32)

    @pl.when(i >= nv_ref[0])
    def _():
        y_ref[...] = jnp.zeros_like(y_ref)


def _ffn_grouped(xs, g, wg, wu, wd, layer, tile_expert, n_valid):
    n_slots = xs.shape[0]
    f = wg.shape[3]
    rows = pl.BlockSpec((MOE_TILE, D_MODEL), lambda i, te, nv: (i, 0))
    return pl.pallas_call(
        _ffn_grouped_kernel,
        grid_spec=pltpu.PrefetchScalarGridSpec(
            num_scalar_prefetch=2,
            grid=(n_slots // MOE_TILE,),
            in_specs=[rows,
                      pl.BlockSpec((1, D_MODEL), lambda i, te, nv: (0, 0)),
                      pl.BlockSpec((1, 1, D_MODEL, f), lambda i, te, nv: (layer, te[i], 0, 0)),
                      pl.BlockSpec((1, 1, D_MODEL, f), lambda i, te, nv: (layer, te[i], 0, 0)),
                      pl.BlockSpec((1, 1, f, D_MODEL), lambda i, te, nv: (layer, te[i], 0, 0))],
            out_specs=rows),
        out_shape=jax.ShapeDtypeStruct((n_slots, D_MODEL), F32),
        compiler_params=_params("arbitrary"),
        name="ffn_grouped",
    )(tile_expert, n_valid, xs, g, wg, wu, wd)


def _moe_combine_kernel(tm, final, s0_ref, s1_ref, x_ref, gate_ref, *rest):
    if final:
        gf_ref, rest = rest[0], rest[1:]
    ys_hbm, o_ref, y1_ref, y2_ref, sems = rest

    def issue(c, carry):
        for u in range(SUBLANES):
            for k, (s_ref, buf) in enumerate(((s0_ref, y1_ref), (s1_ref, y2_ref))):
                pltpu.make_async_copy(ys_hbm.at[pl.ds(s_ref[c * SUBLANES + u], 1)],
                                      buf.at[c, pl.ds(u, 1), :], sems.at[k]).start(priority=k)
        return carry

    lax.fori_loop(0, tm // SUBLANES, issue, 0)
    for k in range(2):
        _rows_done(ys_hbm, tm, sems.at[k])
    rc = COMBINE_ROWS

    def rows(ci, c):
        r = pl.ds(pl.multiple_of(ci * rc, rc), rc)
        rg = pl.ds(pl.multiple_of(ci * (rc // SUBLANES), rc // SUBLANES), rc // SUBLANES)
        g1 = jnp.broadcast_to(gate_ref[r, 0:1], (rc, LANES))
        g2 = jnp.broadcast_to(gate_ref[r, 1:2], (rc, LANES))
        ssq = jnp.zeros((rc, LANES), F32)
        for j in range(D_MODEL // LANES):
            cols = slice(j * LANES, (j + 1) * LANES)
            y1 = y1_ref[rg, :, cols].reshape(rc, LANES)
            y2 = y2_ref[rg, :, cols].reshape(rc, LANES)
            out = x_ref[r, cols] + g1 * y1 + g2 * y2
            o_ref[r, cols] = out
            ssq = ssq + out * out
        if final:
            scale = lax.rsqrt(jnp.sum(ssq, axis=-1, keepdims=True) * (1.0 / D_MODEL) + EPS)
            for j in range(D_MODEL // LANES):
                cols = slice(j * LANES, (j + 1) * LANES)
                o_ref[r, cols] = o_ref[r, cols] * scale * gf_ref[:, cols]
        return c

    lax.fori_loop(0, tm // rc, rows, 0)


def _moe_combine(x, slots, gates, ys, final_g=None, tm=512):
    t = x.shape[0]
    final = final_g is not None
    slot_spec = pl.BlockSpec((tm,), lambda i: (i,), memory_space=pltpu.SMEM)
    in_specs = [slot_spec, slot_spec,
                pl.BlockSpec((tm, D_MODEL), lambda i: (i, 0)),
                pl.BlockSpec((tm, LANES), lambda i: (i, 0))]
    args = [slots[0], slots[1], x, gates]
    if final:
        in_specs.append(pl.BlockSpec((1, D_MODEL), lambda i: (0, 0)))
        args.append(final_g)
    in_specs.append(pl.BlockSpec(memory_space=pl.ANY))
    args.append(ys)
    return pl.pallas_call(
        functools.partial(_moe_combine_kernel, tm, final),
        grid=(t // tm,),
        in_specs=in_specs,
        out_specs=pl.BlockSpec((tm, D_MODEL), lambda i: (i, 0)),
        out_shape=jax.ShapeDtypeStruct((t, D_MODEL), F32),
        scratch_shapes=[pltpu.VMEM((tm // SUBLANES, SUBLANES, D_MODEL), F32),
                        pltpu.VMEM((tm // SUBLANES, SUBLANES, D_MODEL), F32),
                        pltpu.SemaphoreType.DMA((2,))],
        compiler_params=_params("arbitrary"),
        name="moe_combine",
    )(*args)


def _moe_routed(x, route, gates, counts, g, wg, wu, wd, layer, final_g=None):
    t = x.shape[0]
    n_tiles = 2 * t // MOE_TILE + N_EXPERTS
    cnt = counts[:, 0].astype(jnp.int32)
    caps = (cnt + MOE_TILE - 1) // MOE_TILE
    cum = jnp.cumsum(caps)
    base = (cum - caps) * MOE_TILE
    n_valid = cum[-1:]
    tile_expert = jnp.minimum(
        jnp.sum((cum[None, :] <= jnp.arange(n_tiles, dtype=jnp.int32)[:, None]).astype(jnp.int32), axis=1),
        N_EXPERTS - 1)
    fill_start = jnp.concatenate([base + cnt, n_valid * MOE_TILE])
    fill_end = jnp.concatenate([base + caps * MOE_TILE, jnp.full((1,), n_tiles * MOE_TILE, jnp.int32)])
    slots = _moe_slots(route, base)
    xs = _moe_dispatch(x, slots, fill_start, fill_end, n_tiles * MOE_TILE)
    ys = _ffn_grouped(xs, g, wg, wu, wd, layer, tile_expert, n_valid)
    return _moe_combine(x, slots, gates, ys, final_g)


def _trunk(x, state_hgrn, state_pool, start_pos, w, seq, tm):
    batch = x.shape[0] // seq
    s_out, b_out = [], []
    for l in range(DEPTH):
        act = _mixer_in(x, w["norm_mix"][l], w["w_in"], w["lb_logits"], l, tm)
        j = l // 2
        dense = l % 2 == 0
        final_g = w["norm_final"] if l == DEPTH - 1 else None
        g_ffn = w["norm_ffn"][l]
        if seq > 1:
            og, s_new = _hgrn_prompt(act, w["hg_norm"][l], batch, seq)
            mixer = (x, og, act, w["w_o"], l, w["pool_w"][l], w["pool_scale"][l])
            if dense:
                assert final_g is None
                x, b_new = _mixer_ffn(*mixer, g_ffn, w["ffn_w_gate"], w["ffn_w_up"], w["ffn_w_down"], j,
                                      batch, seq)
            else:
                x, b_new, route, gates, counts = _mixer_route(*mixer, g_ffn, w["router_t"][j], batch, seq)
                x = _moe_routed(x, route, gates, counts, g_ffn, w["moe_w_gate"], w["moe_w_up"],
                                w["moe_w_down"], j, final_g=final_g)
        else:
            og, s_new = _hgrn_step(act, w["hg_norm"][l], state_hgrn, l)
            yp, b_new = _pool_step(act, state_pool, l, w["pool_w"][l], w["pool_scale"][l], start_pos)
            x = _mixer_out(x, og, yp, w["w_o"], l, tm)
            if dense:
                x = _ffn(x, g_ffn, w["ffn_w_gate"], w["ffn_w_up"], w["ffn_w_down"], j, tm, 1408, final_g=final_g)
            else:
                x = _ffn(x, g_ffn, w["moe_w_gate"], w["moe_w_up"], w["moe_w_down"], j, tm, 1408,
                         router=w["router"][j], final_g=final_g)
        s_out.append(s_new)
        b_out.append(b_new)
    return x, jnp.stack(s_out), jnp.stack(b_out)


def kernel(x_prompt, x_sample, state_hgrn, state_pool, lb_logits, norm_mix, w_in, w_o, hg_norm, pool_w,
           pool_scale, norm_ffn, ffn_w_gate, ffn_w_up, ffn_w_down, router, moe_w_gate, moe_w_up, moe_w_down,
           norm_final):
    batch, seq, _ = x_prompt.shape
    dec_batch, dec_seq, _ = x_sample.shape
    assert dec_seq == 1
    past_len = 16384
    w = dict(
        lb_logits=lb_logits,
        norm_mix=norm_mix.reshape(DEPTH, 1, D_MODEL),
        w_in=w_in,
        w_o=w_o,
        hg_norm=hg_norm.reshape(DEPTH, 1, HG_WIDTH),
        pool_w=pool_w.astype(BF16),
        pool_scale=pool_scale.reshape(DEPTH, 1, POOL_WIDTH),
        norm_ffn=norm_ffn.reshape(DEPTH, 1, D_MODEL),
        ffn_w_gate=ffn_w_gate.astype(BF16)[:, None],
        ffn_w_up=ffn_w_up.astype(BF16)[:, None],
        ffn_w_down=ffn_w_down.astype(BF16)[:, None],
        router=jnp.pad(router, ((0, 0), (0, 0), (0, LANES - N_EXPERTS))),
        router_t=jnp.swapaxes(router, 1, 2),
        moe_w_gate=moe_w_gate.astype(BF16),
        moe_w_up=moe_w_up.astype(BF16),
        moe_w_down=moe_w_down.astype(BF16),
        norm_final=norm_final.reshape(1, D_MODEL),
    )
    yp, sp, bp = _trunk(x_prompt.reshape(batch * seq, D_MODEL), None, None, 0, w, seq, 512)
    ys, ss, bs = _trunk(x_sample.reshape(dec_batch, D_MODEL), state_hgrn, state_pool, past_len, w, 1, 128)
    return (yp.reshape(batch, seq, D_MODEL), ys.reshape(dec_batch, 1, D_MODEL), sp, ss, bp, bs)
```

```python
import functools

import jax
import jax.numpy as jnp
from jax import lax
from jax.experimental import pallas as pl
from jax.experimental.pallas import tpu as pltpu

F32 = jnp.float32
BF16 = jnp.bfloat16

D_MODEL = 1024
DEPTH = 4
HG_WIDTH = 512
HG_HEADS = 4
HG_D = 128
POOL_WIDTH = 512
POOL_WINDOWS = (2, 4, 8, 16)
POOL_GROUP_W = 128
POOL_BUF = 15
IN_WIDTH = 4 * HG_WIDTH + POOL_WIDTH
ACT_WIDTH = 4 * HG_WIDTH + POOL_WIDTH
ACT_POOL_PART = 4
N_EXPERTS = 8
EPS = 1e-6
LANES = 128
SUBLANES = 8
VMEM_LIMIT = 56 * 1024 * 1024
HGRN_CHUNK = 128


def _params(*sem):
    return pltpu.CompilerParams(dimension_semantics=sem, vmem_limit_bytes=VMEM_LIMIT)


def _rmsnorm(x, g):
    return x * lax.rsqrt(jnp.mean(x * x, axis=-1, keepdims=True) + EPS) * g


def _silu(x):
    return x * jax.nn.sigmoid(x)


def _mixer_in_kernel(layer, x_ref, g_ref, w_ref, lbl_ref, o_ref, wb_ref):
    @pl.when(pl.program_id(0) == 0)
    def _():
        wb_ref[...] = w_ref[0].astype(BF16)

    h = _rmsnorm(x_ref[...], g_ref[...])
    p = jnp.dot(h.astype(BF16), wb_ref[...], preferred_element_type=F32)
    lg = lbl_ref[...]
    e = jnp.exp(lg - jnp.max(lg, axis=0, keepdims=True))
    pr = e / jnp.sum(e, axis=0, keepdims=True)
    cum = pr[0:1]
    for j in range(1, layer + 1):
        cum = cum + pr[j:j + 1]
    lb = cum - pr[0:1]
    w = HG_WIDTH
    q, fx, ix, g, u = p[:, :w], p[:, w:2 * w], p[:, 2 * w:3 * w], p[:, 3 * w:4 * w], p[:, 4 * w:]
    f = lb + (1.0 - lb) * jax.nn.sigmoid(fx)
    o_ref[:, 0:w] = _silu(q)
    o_ref[:, w:2 * w] = jnp.log(f)
    o_ref[:, 2 * w:3 * w] = ix
    o_ref[:, 3 * w:4 * w] = _silu(g)
    o_ref[:, 4 * w:] = u


def _mixer_in(x, g, w_in, lb_logits, layer, tm):
    t = x.shape[0]
    return pl.pallas_call(
        functools.partial(_mixer_in_kernel, layer),
        grid=(t // tm,),
        in_specs=[
            pl.BlockSpec((tm, D_MODEL), lambda i: (i, 0)),
            pl.BlockSpec((1, D_MODEL), lambda i: (0, 0)),
            pl.BlockSpec((1, D_MODEL, IN_WIDTH), lambda i: (layer, 0, 0)),
            pl.BlockSpec((DEPTH, HG_WIDTH), lambda i: (0, 0)),
        ],
        out_specs=pl.BlockSpec((tm, ACT_WIDTH), lambda i: (i, 0)),
        out_shape=jax.ShapeDtypeStruct((t, ACT_WIDTH), F32),
        scratch_shapes=[pltpu.VMEM((D_MODEL, IN_WIDTH), BF16)],
        compiler_params=_params("arbitrary"),
        name="mixer_in",
    )(x, g, w_in, lb_logits)


SMALL_LEVELS = (2, 4, 8)


def _level_sum_matrix(n, ti, si):
    half = n // 2
    mid = (ti & (-n)) + (half - 1)
    lo = jnp.where((ti & half) != 0, mid, ti)
    hi = jnp.where((ti & half) != 0, ti, mid)
    return (si > lo) & (si <= hi)


def _level_factors(q, k, b, n, row, arg=None):
    c = b.shape[0]
    half = n // 2
    if arg is None:
        zero = jnp.zeros((half, b.shape[1]), F32)
        qs, ks = [], []
        for r0 in range(0, c, n):
            mid = b[r0 + half - 1:r0 + half, :]
            ks += [k[r0:r0 + half] * jnp.exp(mid - b[r0:r0 + half]), zero]
            qs += [q[r0 + half:r0 + n] * jnp.exp(b[r0 + half:r0 + n] - mid)]
        return jnp.concatenate(qs, axis=0), jnp.concatenate(ks, axis=0)
    right = (row & half) != 0
    qk = jnp.where(right, q, k) * jnp.exp(arg)
    return jnp.where(right, qk, 0.0), jnp.where(right, 0.0, qk)


def _split3(x):
    hi = x.astype(BF16)
    r1 = x - hi.astype(F32)
    mid = r1.astype(BF16)
    lo = (r1 - mid.astype(F32)).astype(BF16)
    return jnp.concatenate([hi, mid, lo], axis=0)


HEAD_PAIRS = HG_HEADS // 2
PAIR_W = 2 * HG_D


def _pair_blocks(a):
    zero = jnp.zeros((a.shape[0], HG_D), a.dtype)
    return jnp.concatenate([jnp.concatenate([a[:, :HG_D], zero], axis=1),
                            jnp.concatenate([zero, a[:, HG_D:]], axis=1)], axis=0)


def _per_head(fn, a):
    return jnp.concatenate([fn(a[:, :HG_D]), fn(a[:, HG_D:])], axis=1)


def _hgrn_prompt_kernel(tl, q_ref, lf_ref, v_ref, g_ref, nrm_ref, o_ref, s_ref, st_ref):
    c = HGRN_CHUNK
    li = pl.program_id(1)

    @pl.when(li == 0)
    def _():
        st_ref[...] = jnp.zeros_like(st_ref)

    row = lax.broadcasted_iota(jnp.int32, (c, PAIR_W), 0)
    ti = lax.broadcasted_iota(jnp.int32, (c, c), 0)
    si = lax.broadcasted_iota(jnp.int32, (c, c), 1)
    ones = lambda m: jnp.where(m, 1.0, 0.0).astype(BF16)
    cum3 = jnp.concatenate([ones(si <= ti)] * 3, axis=1)
    lvl2 = jnp.concatenate([jnp.concatenate([jnp.where(_level_sum_matrix(n, ti, si), 1.0, 0.0)
                                             for n in SMALL_LEVELS], axis=0).astype(BF16)] * 2, axis=1)
    apart = jnp.concatenate([ti ^ si] * 2, axis=1)
    pi = lax.broadcasted_iota(jnp.int32, (PAIR_W, PAIR_W), 0)
    pj = lax.broadcasted_iota(jnp.int32, (PAIR_W, PAIR_W), 1)
    own = (pi < HG_D) == (pj < HG_D)
    nt = (((1,), (1,)), ((), ()))
    tn = (((0,), (0,)), ((), ()))

    def chunk(pairs, ci, carry):
        rows = pl.ds(pl.multiple_of(ci * c, c), c)
        for p in pairs:
            cols = slice(p * PAIR_W, (p + 1) * PAIR_W)
            q, lf, v = q_ref[rows, cols], lf_ref[rows, cols], v_ref[rows, cols]
            k = 1.0 - jnp.exp(lf)
            lf3 = _split3(lf)
            b = jnp.dot(cum3, lf3, preferred_element_type=F32)
            s1 = jnp.dot(lvl2, lf3[:2 * c], preferred_element_type=F32)
            args = {n: s1[i * c:(i + 1) * c] for i, n in enumerate(SMALL_LEVELS)}
            vbd = _pair_blocks(v.astype(BF16))
            scores = None
            n = c
            while n >= 2:
                qt, kt = _level_factors(q, k, b, n, row, args.get(n))
                s_n = lax.dot_general(qt.astype(BF16), _pair_blocks(kt.astype(BF16)), nt,
                                      preferred_element_type=F32)
                if n in args:
                    scores = jnp.where(apart < n, s_n, scores)
                else:
                    half = n // 2
                    parts = []
                    for j, r0 in enumerate(range(0, c, n)):
                        new = s_n[j * half:(j + 1) * half]
                        if scores is None:
                            parts += [jnp.zeros_like(new), new]
                        else:
                            keep = scores[r0 + half:r0 + n]
                            parts += [scores[r0:r0 + half], jnp.where(apart[r0 + half:r0 + n] < n, new, keep)]
                    scores = jnp.concatenate(parts, axis=0)
                n //= 2
            st = st_ref[p]
            qs = (q * jnp.exp(b)).astype(BF16)
            o = jnp.dot(scores.astype(BF16), vbd, preferred_element_type=F32)
            o = o + lax.dot_general(qs, st.astype(BF16), nt, preferred_element_type=F32)
            o = o + _per_head(lambda a: jnp.broadcast_to(jnp.sum(a, axis=-1, keepdims=True), a.shape), q * k) * v
            bc = b[c - 1:c, :]
            kd = (k * jnp.exp(bc - b)).astype(BF16)
            upd = lax.dot_general(v.astype(BF16), kd, tn, preferred_element_type=F32)
            st_ref[p] = st * jnp.exp(bc) + jnp.where(own, upd, 0.0)
            ms = _per_head(lambda a: jnp.broadcast_to(jnp.mean(a, axis=-1, keepdims=True), a.shape), o * o)
            o = o * lax.rsqrt(ms + EPS) * nrm_ref[:, cols]
            o_ref[rows, cols] = (o * g_ref[rows, cols]).astype(o_ref.dtype)
        return carry

    lax.fori_loop(0, tl // c, functools.partial(chunk, range(HEAD_PAIRS)), 0, unroll=2)

    @pl.when(li == pl.num_programs(1) - 1)
    def _():
        for p in range(HEAD_PAIRS):
            st = st_ref[p]
            s_ref[0, 2 * p] = st[:HG_D, :HG_D].T
            s_ref[0, 2 * p + 1] = st[HG_D:, HG_D:].T


def _hgrn_prompt(act, hg_norm, batch, seq, tl=2048):
    nl = seq // tl
    blk = lambda part: pl.BlockSpec((tl, HG_WIDTH), lambda b, i: (b * nl + i, part))
    return pl.pallas_call(
        functools.partial(_hgrn_prompt_kernel, tl),
        grid=(batch, nl),
        in_specs=[blk(0), blk(1), blk(2), blk(3),
                  pl.BlockSpec((1, HG_WIDTH), lambda b, i: (0, 0))],
        out_specs=[pl.BlockSpec((tl, HG_WIDTH), lambda b, i: (b * nl + i, 0)),
                   pl.BlockSpec((1, HG_HEADS, HG_D, HG_D), lambda b, i: (b, 0, 0, 0))],
        out_shape=[jax.ShapeDtypeStruct((batch * seq, HG_WIDTH), BF16),
                   jax.ShapeDtypeStruct((batch, HG_HEADS, HG_D, HG_D), F32)],
        scratch_shapes=[pltpu.VMEM((HEAD_PAIRS, PAIR_W, PAIR_W), F32)],
        compiler_params=_params("parallel", "arbitrary"),
        name="hgrn_prompt",
    )(act, act, act, act, hg_norm)


HGRN_STEP_BT = 32


def _hgrn_step_kernel(q_ref, lf_ref, v_ref, g_ref, nrm_ref, s_ref, _, o_ref, so_ref):
    bt = HGRN_STEP_BT
    q = q_ref[...]
    v = v_ref[...]
    f = jnp.exp(lf_ref[...])
    ri = lax.broadcasted_iota(jnp.int32, (3 * bt, PAIR_W), 0) % bt
    half = lax.broadcasted_iota(jnp.int32, (3 * bt, PAIR_W), 1) // HG_D
    tn = (((0,), (0,)), ((), ()))
    pieces = [_split3(a) for a in (f, 1.0 - f, q)]
    rows = []
    for s in range(0, bt, 2):
        pick = jnp.where(ri == s + half, 1.0, 0.0).astype(BF16)
        fc, kc, qc = [lax.dot_general(p, pick, tn, preferred_element_type=F32) for p in pieces]
        for d in range(2):
            cols = slice(d * HG_D, (d + 1) * HG_D)
            sn = fc[:, cols] * s_ref[0, s + d, 0] + kc[:, cols] * v[s + d:s + d + 1, :]
            so_ref[0, s + d, 0] = sn
            rows.append(jnp.sum(sn * qc[:, cols], axis=0, keepdims=True))
    o = jnp.concatenate(rows, axis=0)
    o = o * lax.rsqrt(jnp.mean(o * o, axis=-1, keepdims=True) + EPS) * nrm_ref[...]
    o_ref[...] = (o * g_ref[...]).astype(o_ref.dtype)


def _hgrn_step(act, hg_norm, states, layer, new_states):
    batch = act.shape[0]
    bt = HGRN_STEP_BT
    blk = lambda part: pl.BlockSpec((bt, HG_D), lambda i, h: (i, part * HG_HEADS + h))
    sblk = pl.BlockSpec((1, bt, 1, HG_D, HG_D), lambda i, h: (layer, i, h, 0, 0))
    return pl.pallas_call(
        _hgrn_step_kernel,
        grid=(batch // bt, HG_HEADS),
        in_specs=[blk(0), blk(1), blk(2), blk(3),
                  pl.BlockSpec((1, HG_D), lambda i, h: (0, h)), sblk,
                  pl.BlockSpec(memory_space=pl.ANY)],
        out_specs=[pl.BlockSpec((bt, HG_D), lambda i, h: (i, h)), sblk],
        out_shape=[jax.ShapeDtypeStruct((batch, HG_WIDTH), BF16),
                   jax.ShapeDtypeStruct(new_states.shape, F32)],
        input_output_aliases={6: 1},
        compiler_params=_params("parallel", "parallel"),
        name="hgrn_step",
    )(act, act, act, act, hg_norm, states, new_states)


def _pool_map(d_groups, wp_ref, scale_ref, o_ref):
    for gi, d in enumerate(d_groups):
        sl = slice(gi * POOL_GROUP_W, (gi + 1) * POOL_GROUP_W)
        y = jnp.dot(d.astype(BF16), wp_ref[gi], preferred_element_type=F32)
        o_ref[:, sl] = (y * scale_ref[:, sl]).astype(o_ref.dtype)


POOL_HIST = 16


def _pool_tile(i, tl, z, carry_ref, wp_ref, scale_ref, o_ref, nb_ref):
    @pl.when(i == 0)
    def _():
        carry_ref[...] = jnp.zeros_like(carry_ref)

    ext = jnp.concatenate([carry_ref[...], z], axis=0)
    sums = {1: ext}
    w = 1
    while w < max(POOL_WINDOWS):
        sums[2 * w] = sums[w] + pltpu.roll(sums[w], w, 0)
        w *= 2
    pos = i * tl + lax.broadcasted_iota(jnp.int32, (tl, POOL_GROUP_W), 0)
    ds = []
    for gi, w in enumerate(POOL_WINDOWS):
        sl = slice(gi * POOL_GROUP_W, (gi + 1) * POOL_GROUP_W)
        cnt = jnp.minimum(pos + 1, w).astype(F32)
        ds.append(sums[w][POOL_HIST:, sl] / cnt - z[:, sl])
    _pool_map(ds, wp_ref, scale_ref, o_ref)
    carry_ref[...] = z[tl - POOL_HIST:, :]

    @pl.when(i == pl.num_programs(1) - 1)
    def _():
        nb_ref[0] = z[tl - POOL_BUF:, :]


POOL_STEP_BT = 16


def _pool_step_kernel(start_pos, u_ref, buf_ref, wp_ref, scale_ref, o_ref, nb_ref):
    u = u_ref[...]
    buf = buf_ref[0]
    ds = []
    for gi, w in enumerate(POOL_WINDOWS):
        sl = slice(gi * POOL_GROUP_W, (gi + 1) * POOL_GROUP_W)
        tot = u[:, sl] + jnp.sum(buf[:, POOL_BUF - (w - 1):, sl], axis=1)
        ds.append(tot / float(min(start_pos + 1, w)) - u[:, sl])
    _pool_map(ds, wp_ref, scale_ref, o_ref)
    nb_ref[:, 0:POOL_BUF - 1, :] = buf[:, 1:, :]
    nb_ref[:, POOL_BUF - 1:, :] = u[:, None, :]


def _pool_step(act, bufs, layer, pool_w, pool_scale, start_pos):
    batch = act.shape[0]
    bt = POOL_STEP_BT
    return pl.pallas_call(
        functools.partial(_pool_step_kernel, start_pos),
        grid=(batch // bt,),
        in_specs=[pl.BlockSpec((bt, POOL_WIDTH), lambda i: (i, ACT_POOL_PART)),
                  pl.BlockSpec((1, bt, POOL_BUF, POOL_WIDTH), lambda i: (layer, i, 0, 0)),
                  pl.BlockSpec((len(POOL_WINDOWS), POOL_GROUP_W, POOL_GROUP_W), lambda i: (0, 0, 0)),
                  pl.BlockSpec((1, POOL_WIDTH), lambda i: (0, 0))],
        out_specs=[pl.BlockSpec((bt, POOL_WIDTH), lambda i: (i, 0)),
                   pl.BlockSpec((bt, POOL_BUF, POOL_WIDTH), lambda i: (i, 0, 0))],
        out_shape=[jax.ShapeDtypeStruct((batch, POOL_WIDTH), BF16),
                   jax.ShapeDtypeStruct(bufs.shape[1:], F32)],
        compiler_params=_params("parallel"),
        name="pool_step",
    )(act, bufs, pool_w, pool_scale)


def _mixer_out_kernel(x_ref, o_ref, p_ref, w_ref, y_ref, wb_ref):
    @pl.when(pl.program_id(0) == 0)
    def _():
        wb_ref[...] = w_ref[0].astype(BF16)

    y = jnp.dot(o_ref[...], wb_ref[0:HG_WIDTH, :], preferred_element_type=F32)
    y = y + jnp.dot(p_ref[...], wb_ref[HG_WIDTH:, :], preferred_element_type=F32)
    y_ref[...] = x_ref[...] + y


def _mixer_out(x, og, yp, w_o, layer, tm):
    t = x.shape[0]
    return pl.pallas_call(
        _mixer_out_kernel,
        grid=(t // tm,),
        in_specs=[pl.BlockSpec((tm, D_MODEL), lambda i: (i, 0)),
                  pl.BlockSpec((tm, HG_WIDTH), lambda i: (i, 0)),
                  pl.BlockSpec((tm, POOL_WIDTH), lambda i: (i, 0)),
                  pl.BlockSpec((1, D_MODEL, D_MODEL), lambda i: (layer, 0, 0))],
        out_specs=pl.BlockSpec((tm, D_MODEL), lambda i: (i, 0)),
        out_shape=jax.ShapeDtypeStruct((t, D_MODEL), F32),
        scratch_shapes=[pltpu.VMEM((D_MODEL, D_MODEL), BF16)],
        compiler_params=_params("arbitrary"),
        name="mixer_out",
    )(x, og, yp, w_o)


def _top2_combine(logits):
    lane = lax.broadcasted_iota(jnp.int32, logits.shape, 1).astype(F32)
    neg = jnp.float32(-jnp.inf)
    lg = jnp.where(lane < N_EXPERTS, logits, neg)
    m1 = jnp.max(lg, axis=-1, keepdims=True)
    i1 = jnp.min(jnp.where(lg == m1, lane, float(LANES)), axis=-1, keepdims=True)
    lg2 = jnp.where(lane == i1, neg, lg)
    m2 = jnp.max(lg2, axis=-1, keepdims=True)
    i2 = jnp.min(jnp.where(lg2 == m2, lane, float(LANES)), axis=-1, keepdims=True)
    e2 = jnp.exp(m2 - m1)
    g1 = 1.0 / (1.0 + e2)
    g2 = e2 / (1.0 + e2)
    return jnp.where(lane == i1, g1, 0.0) + jnp.where(lane == i2, g2, 0.0)


def _ffn_kernel(moe, final, x_ref, g_ref, *rest):
    if moe:
        r_ref, rest = rest[0], rest[1:]
    if final:
        gf_ref, rest = rest[0], rest[1:]
    wg_ref, wu_ref, wd_ref, y_ref, h_ref, acc_ref = rest[:6]
    e, j = pl.program_id(1), pl.program_id(2)
    first = (e == 0) & (j == 0)
    last = (e == pl.num_programs(1) - 1) & (j == pl.num_programs(2) - 1)

    @pl.when(first)
    def _():
        h = _rmsnorm(x_ref[...], g_ref[...])
        h_ref[...] = h.astype(BF16)
        acc_ref[...] = jnp.zeros_like(acc_ref)
        if moe:
            logits = jnp.dot(h, r_ref[...], precision=lax.Precision.HIGHEST, preferred_element_type=F32)
            rest[6][...] = _top2_combine(logits)

    h = h_ref[...]
    gate = jnp.dot(h, wg_ref[0, 0], preferred_element_type=F32)
    up = jnp.dot(h, wu_ref[0, 0], preferred_element_type=F32)
    y = jnp.dot((_silu(gate) * up).astype(BF16), wd_ref[0, 0], preferred_element_type=F32)
    if moe:
        comb = rest[6][...]
        lane = lax.broadcasted_iota(jnp.int32, comb.shape, 1)
        y = jnp.sum(jnp.where(lane == e, comb, 0.0), axis=-1, keepdims=True) * y
    acc_ref[...] += y

    @pl.when(last)
    def _():
        out = x_ref[...] + acc_ref[...]
        if final:
            out = _rmsnorm(out, gf_ref[...])
        y_ref[...] = out


def _ffn(x, g, wg, wu, wd, layer, tm, tf, router=None, final_g=None):
    t = x.shape[0]
    _, ne, _, f = wg.shape
    moe, final = router is not None, final_g is not None
    vec = pl.BlockSpec((1, D_MODEL), lambda i, e, j: (0, 0))
    in_specs = [pl.BlockSpec((tm, D_MODEL), lambda i, e, j: (i, 0)), vec]
    args = [x, g]
    scratch = [pltpu.VMEM((tm, D_MODEL), BF16), pltpu.VMEM((tm, D_MODEL), F32)]
    if moe:
        in_specs.append(pl.BlockSpec((D_MODEL, LANES), lambda i, e, j: (0, 0)))
        args.append(router)
        scratch.append(pltpu.VMEM((tm, LANES), F32))
    if final:
        in_specs.append(vec)
        args.append(final_g)
    in_specs += [pl.BlockSpec((1, 1, D_MODEL, tf), lambda i, e, j: (layer, e, 0, j)),
                 pl.BlockSpec((1, 1, D_MODEL, tf), lambda i, e, j: (layer, e, 0, j)),
                 pl.BlockSpec((1, 1, tf, D_MODEL), lambda i, e, j: (layer, e, j, 0))]
    args += [wg, wu, wd]
    return pl.pallas_call(
        functools.partial(_ffn_kernel, moe, final),
        grid=(t // tm, ne, f // tf),
        in_specs=in_specs,
        out_specs=pl.BlockSpec((tm, D_MODEL), lambda i, e, j: (i, 0)),
        out_shape=jax.ShapeDtypeStruct((t, D_MODEL), F32),
        scratch_shapes=scratch,
        compiler_params=_params("parallel", "arbitrary", "arbitrary"),
        name="ffn_moe" if moe else "ffn_dense",
    )(*args)


def _mixer_tile(b, i, tl, x_ref, o_ref, u_ref, w_ref, wp_ref, scale_ref, nb_ref, wb_ref, carry_ref, yp_ref):
    @pl.when((b == 0) & (i == 0))
    def _():
        wb_ref[...] = w_ref[0].astype(BF16)

    _pool_tile(i, tl, u_ref[...], carry_ref, wp_ref, scale_ref, yp_ref, nb_ref)
    y = jnp.dot(o_ref[...], wb_ref[0:HG_WIDTH, :], preferred_element_type=F32)
    y = y + jnp.dot(yp_ref[...], wb_ref[HG_WIDTH:, :], preferred_element_type=F32)
    return x_ref[...] + y


def _mixer_ffn_kernel(tl, x_ref, o_ref, u_ref, w_ref, wp_ref, scale_ref, g_ref, wg_ref, wu_ref, wd_ref,
                      y_ref, nb_ref, wb_ref, carry_ref, yp_ref, x1_ref, h_ref, acc_ref):
    b, i, j = pl.program_id(0), pl.program_id(1), pl.program_id(2)

    @pl.when(j == 0)
    def _():
        x1 = _mixer_tile(b, i, tl, x_ref, o_ref, u_ref, w_ref, wp_ref, scale_ref, nb_ref, wb_ref, carry_ref, yp_ref)
        x1_ref[...] = x1
        h_ref[...] = _rmsnorm(x1, g_ref[...]).astype(BF16)
        acc_ref[...] = jnp.zeros_like(acc_ref)

    h = h_ref[...]
    gate = jnp.dot(h, wg_ref[0, 0], preferred_element_type=F32)
    up = jnp.dot(h, wu_ref[0, 0], preferred_element_type=F32)
    acc_ref[...] += jnp.dot((_silu(gate) * up).astype(BF16), wd_ref[0, 0], preferred_element_type=F32)

    @pl.when(j == pl.num_programs(2) - 1)
    def _():
        y_ref[...] = x1_ref[...] + acc_ref[...]


def _mixer_specs(nl, tl, layer, tok, fixed):
    ins = [pl.BlockSpec((tl, D_MODEL), tok(0)),
           pl.BlockSpec((tl, HG_WIDTH), tok(0)),
           pl.BlockSpec((tl, POOL_WIDTH), tok(ACT_POOL_PART)),
           pl.BlockSpec((1, D_MODEL, D_MODEL), fixed(layer, 0, 0)),
           pl.BlockSpec((len(POOL_WINDOWS), POOL_GROUP_W, POOL_GROUP_W), fixed(0, 0, 0)),
           pl.BlockSpec((1, POOL_WIDTH), fixed(0, 0))]
    scratch = [pltpu.VMEM((D_MODEL, D_MODEL), BF16), pltpu.VMEM((POOL_HIST, POOL_WIDTH), F32),
               pltpu.VMEM((tl, POOL_WIDTH), BF16)]
    return ins, scratch


def _mixer_ffn(x, og, act, w_o, layer, pool_w, pool_scale, g, wg, wu, wd, wset, batch, seq, tl=512, tf=1408):
    nl = seq // tl
    f = wg.shape[3]
    tok = lambda cols: (lambda b, i, j: (b * nl + i, cols))
    fixed = lambda *idx: (lambda b, i, j: idx)
    ins, scratch = _mixer_specs(nl, tl, layer, tok, fixed)
    ins += [pl.BlockSpec((1, D_MODEL), fixed(0, 0)),
            pl.BlockSpec((1, 1, D_MODEL, tf), lambda b, i, j: (wset, 0, 0, j)),
            pl.BlockSpec((1, 1, D_MODEL, tf), lambda b, i, j: (wset, 0, 0, j)),
            pl.BlockSpec((1, 1, tf, D_MODEL), lambda b, i, j: (wset, 0, j, 0))]
    scratch += [pltpu.VMEM((tl, D_MODEL), F32), pltpu.VMEM((tl, D_MODEL), BF16), pltpu.VMEM((tl, D_MODEL), F32)]
    return pl.pallas_call(
        functools.partial(_mixer_ffn_kernel, tl),
        grid=(batch, nl, f // tf),
        in_specs=ins,
        out_specs=[pl.BlockSpec((tl, D_MODEL), tok(0)),
                   pl.BlockSpec((1, POOL_BUF, POOL_WIDTH), lambda b, i, j: (b, 0, 0))],
        out_shape=[jax.ShapeDtypeStruct((batch * seq, D_MODEL), F32),
                   jax.ShapeDtypeStruct((batch, POOL_BUF, POOL_WIDTH), F32)],
        scratch_shapes=scratch,
        compiler_params=_params("arbitrary", "arbitrary", "arbitrary"),
        name="mixer_ffn",
    )(x, og, act, w_o, pool_w, pool_scale, g, wg, wu, wd)


MOE_TILE = 512
ROUTE_ROWS = 8
NT_DIMS = (((1,), (1,)), ((), ()))


def _mixer_route_kernel(tm, x_ref, o_ref, u_ref, w_ref, wp_ref, scale_ref, g_ref, rt_ref,
                        x1_ref, nb_ref, route_ref, gate_ref, cnt_ref,
                        wb_ref, pool_carry_ref, yp_ref, carry_ref, earlier_ref):
    b, i = pl.program_id(0), pl.program_id(1)

    @pl.when((b == 0) & (i == 0))
    def _():
        carry_ref[...] = jnp.zeros_like(carry_ref)
        t0 = lax.broadcasted_iota(jnp.int32, (tm, tm), 0)
        t1 = lax.broadcasted_iota(jnp.int32, (tm, tm), 1)
        earlier_ref[...] = jnp.where(t0 < t1, 1.0, 0.0).astype(BF16)

    x1 = _mixer_tile(b, i, tm, x_ref, o_ref, u_ref, w_ref, wp_ref, scale_ref, nb_ref, wb_ref, pool_carry_ref, yp_ref)
    x1_ref[...] = x1
    h = _rmsnorm(x1, g_ref[...])
    lt = lax.dot_general(rt_ref[...], h, NT_DIMS, precision=lax.Precision.HIGHEST,
                         preferred_element_type=F32)
    ex = lax.broadcasted_iota(jnp.int32, lt.shape, 0).astype(F32)
    neg = jnp.float32(-jnp.inf)
    m1 = jnp.max(lt, axis=0, keepdims=True)
    i1 = jnp.min(jnp.where(lt == m1, ex, float(N_EXPERTS)), axis=0, keepdims=True)
    l2 = jnp.where(ex == i1, neg, lt)
    m2 = jnp.max(l2, axis=0, keepdims=True)
    i2 = jnp.min(jnp.where(l2 == m2, ex, float(N_EXPERTS)), axis=0, keepdims=True)
    e2 = jnp.exp(m2 - m1)
    g1 = 1.0 / (1.0 + e2)
    g2 = e2 / (1.0 + e2)
    sel1, sel2 = ex == i1, ex == i2
    member = jnp.where(sel1 | sel2, 1.0, 0.0)
    rank = jnp.dot(member.astype(BF16), earlier_ref[...], preferred_element_type=F32) + carry_ref[:, 0:1]
    route_ref[...] = jnp.zeros_like(route_ref)
    route_ref[0:1, :] = i1.astype(jnp.int32)
    route_ref[1:2, :] = jnp.sum(jnp.where(sel1, rank, 0.0), axis=0, keepdims=True).astype(jnp.int32)
    route_ref[2:3, :] = i2.astype(jnp.int32)
    route_ref[3:4, :] = jnp.sum(jnp.where(sel2, rank, 0.0), axis=0, keepdims=True).astype(jnp.int32)
    carry_ref[...] += jnp.sum(member, axis=1, keepdims=True)
    cnt_ref[...] = carry_ref[...]
    row = lax.broadcasted_iota(jnp.int32, (LANES, tm), 0)
    gate_ref[...] = jnp.where(row == 0, g1, jnp.where(row == 1, g2, 0.0)).T


def _mixer_route(x, og, act, w_o, layer, pool_w, pool_scale, g, router_t, batch, seq, tm=512):
    nl = seq // tm
    t = batch * seq
    tok = lambda cols: (lambda b, i: (b * nl + i, cols))
    fixed = lambda *idx: (lambda b, i: idx)
    ins, scratch = _mixer_specs(nl, tm, layer, tok, fixed)
    ins += [pl.BlockSpec((1, D_MODEL), fixed(0, 0)),
            pl.BlockSpec((N_EXPERTS, D_MODEL), fixed(0, 0))]
    scratch += [pltpu.VMEM((N_EXPERTS, LANES), F32), pltpu.VMEM((tm, tm), BF16)]
    return pl.pallas_call(
        functools.partial(_mixer_route_kernel, tm),
        grid=(batch, nl),
        in_specs=ins,
        out_specs=[pl.BlockSpec((tm, D_MODEL), tok(0)),
                   pl.BlockSpec((1, POOL_BUF, POOL_WIDTH), lambda b, i: (b, 0, 0)),
                   pl.BlockSpec((ROUTE_ROWS, tm), lambda b, i: (0, b * nl + i)),
                   pl.BlockSpec((tm, LANES), tok(0)),
                   pl.BlockSpec((N_EXPERTS, LANES), fixed(0, 0))],
        out_shape=[jax.ShapeDtypeStruct((t, D_MODEL), F32),
                   jax.ShapeDtypeStruct((batch, POOL_BUF, POOL_WIDTH), F32),
                   jax.ShapeDtypeStruct((ROUTE_ROWS, t), jnp.int32),
                   jax.ShapeDtypeStruct((t, LANES), F32),
                   jax.ShapeDtypeStruct((N_EXPERTS, LANES), F32)],
        scratch_shapes=scratch,
        compiler_params=_params("arbitrary", "arbitrary"),
        name="mixer_route",
    )(x, og, act, w_o, pool_w, pool_scale, g, router_t)


def _moe_slots_kernel(base_ref, route_ref, slot_ref):
    r = route_ref[...]
    slot_ref[...] = jnp.zeros_like(slot_ref)
    for k in range(2):
        e, rank = r[2 * k:2 * k + 1, :], r[2 * k + 1:2 * k + 2, :]
        start = jnp.zeros_like(e)
        for j in range(N_EXPERTS):
            start = jnp.where(e == j, base_ref[j], start)
        slot_ref[k:k + 1, :] = start + rank


def _moe_slots(route, base, tm=2048):
    t = route.shape[1]
    return pl.pallas_call(
        _moe_slots_kernel,
        grid_spec=pltpu.PrefetchScalarGridSpec(
            num_scalar_prefetch=1,
            grid=(t // tm,),
            in_specs=[pl.BlockSpec((ROUTE_ROWS, tm), lambda i, *_: (0, i))],
            out_specs=pl.BlockSpec((ROUTE_ROWS, tm), lambda i, *_: (0, i))),
        out_shape=jax.ShapeDtypeStruct((ROUTE_ROWS, t), jnp.int32),
        compiler_params=_params("parallel"),
        name="moe_slots",
    )(base, route)


COMBINE_ROWS = 128


def _rows_done(hbm, rows, sem):
    pltpu.make_async_copy(hbm.at[pl.ds(0, rows)], hbm.at[pl.ds(0, rows)], sem).wait()


def _moe_dispatch_kernel(tm, fs_ref, fe_ref, s0_ref, s1_ref, x_ref, xs_hbm, sem):
    i = pl.program_id(0)

    def row_copy(group, sub, dst_row):
        return pltpu.make_async_copy(x_ref.at[group, pl.ds(sub, 1), :], xs_hbm.at[pl.ds(dst_row, 1)], sem)

    def issue(c, carry):
        for u in range(SUBLANES):
            for k, s_ref in enumerate((s0_ref, s1_ref)):
                row_copy(c, u, s_ref[c * SUBLANES + u]).start(priority=k)
        return carry

    lax.fori_loop(0, tm // SUBLANES, issue, 0)
    _rows_done(xs_hbm, 2 * tm, sem)

    @pl.when(i == pl.num_programs(0) - 1)
    def _():
        for e in range(N_EXPERTS + 1):
            def fill(p, c):
                row_copy(0, 0, p).start()
                return c

            def drain(p, c):
                row_copy(0, 0, p).wait()
                return c

            lax.fori_loop(fs_ref[e], fe_ref[e], fill, 0)
            lax.fori_loop(fs_ref[e], fe_ref[e], drain, 0)


def _moe_dispatch(rows, slots, fill_start, fill_end, n_slots, tm=1024):
    t, width = rows.shape
    slot_spec = pl.BlockSpec((tm,), lambda i, *_: (i,), memory_space=pltpu.SMEM)
    return pl.pallas_call(
        functools.partial(_moe_dispatch_kernel, tm),
        grid_spec=pltpu.PrefetchScalarGridSpec(
            num_scalar_prefetch=2,
            grid=(t // tm,),
            in_specs=[slot_spec, slot_spec,
                      pl.BlockSpec((tm // SUBLANES, SUBLANES, width), lambda i, *_: (i, 0, 0))],
            out_specs=pl.BlockSpec(memory_space=pl.ANY),
            scratch_shapes=[pltpu.SemaphoreType.DMA(())]),
        out_shape=jax.ShapeDtypeStruct((n_slots, width), rows.dtype),
        compiler_params=_params("arbitrary"),
        name="moe_dispatch",
    )(fill_start, fill_end, slots[0], slots[1], rows.reshape(t // SUBLANES, SUBLANES, width))


def _ffn_grouped_kernel(te_ref, nv_ref, x_ref, g_ref, wg_ref, wu_ref, wd_ref, y_ref):
    i = pl.program_id(0)

    @pl.when(i < nv_ref[0])
    def _():
        h = _rmsnorm(x_ref[...], g_ref[...]).astype(BF16)
        gate = jnp.dot(h, wg_ref[0, 0], preferred_element_type=F32)
        up = jnp.dot(h, wu_ref[0, 0], preferred_element_type=F32)
        y_ref[...] = jnp.dot((_silu(gate) * up).astype(BF16), wd_ref[0, 0], preferred_element_type=F32)

    @pl.when(i >= nv_ref[0])
    def _():
        y_ref[...] = jnp.zeros_like(y_ref)


def _ffn_grouped(xs, g, wg, wu, wd, layer, tile_expert, n_valid):
    n_slots = xs.shape[0]
    f = wg.shape[3]
    rows = pl.BlockSpec((MOE_TILE, D_MODEL), lambda i, te, nv: (i, 0))
    return pl.pallas_call(
        _ffn_grouped_kernel,
        grid_spec=pltpu.PrefetchScalarGridSpec(
            num_scalar_prefetch=2,
            grid=(n_slots // MOE_TILE,),
            in_specs=[rows,
                      pl.BlockSpec((1, D_MODEL), lambda i, te, nv: (0, 0)),
                      pl.BlockSpec((1, 1, D_MODEL, f), lambda i, te, nv: (layer, te[i], 0, 0)),
                      pl.BlockSpec((1, 1, D_MODEL, f), lambda i, te, nv: (layer, te[i], 0, 0)),
                      pl.BlockSpec((1, 1, f, D_MODEL), lambda i, te, nv: (layer, te[i], 0, 0))],
            out_specs=rows),
        out_shape=jax.ShapeDtypeStruct((n_slots, D_MODEL), F32),
        compiler_params=_params("arbitrary"),
        name="ffn_grouped",
    )(tile_expert, n_valid, xs, g, wg, wu, wd)


def _moe_combine_kernel(tm, final, s0_ref, s1_ref, x_ref, gate_ref, *rest):
    if final:
        gf_ref, rest = rest[0], rest[1:]
    ys_hbm, o_ref, y1_ref, y2_ref, sems = rest

    def issue(c, carry):
        for u in range(SUBLANES):
            for k, (s_ref, buf) in enumerate(((s0_ref, y1_ref), (s1_ref, y2_ref))):
                pltpu.make_async_copy(ys_hbm.at[pl.ds(s_ref[c * SUBLANES + u], 1)],
                                      buf.at[c, pl.ds(u, 1), :], sems.at[k]).start(priority=k)
        return carry

    lax.fori_loop(0, tm // SUBLANES, issue, 0)
    for k in range(2):
        _rows_done(ys_hbm, tm, sems.at[k])
    rc = COMBINE_ROWS

    def rows(ci, c):
        r = pl.ds(pl.multiple_of(ci * rc, rc), rc)
        rg = pl.ds(pl.multiple_of(ci * (rc // SUBLANES), rc // SUBLANES), rc // SUBLANES)
        g1 = jnp.broadcast_to(gate_ref[r, 0:1], (rc, LANES))
        g2 = jnp.broadcast_to(gate_ref[r, 1:2], (rc, LANES))
        ssq = jnp.zeros((rc, LANES), F32)
        for j in range(D_MODEL // LANES):
            cols = slice(j * LANES, (j + 1) * LANES)
            y1 = y1_ref[rg, :, cols].reshape(rc, LANES)
            y2 = y2_ref[rg, :, cols].reshape(rc, LANES)
            out = x_ref[r, cols] + g1 * y1 + g2 * y2
            o_ref[r, cols] = out
            ssq = ssq + out * out
        if final:
            scale = lax.rsqrt(jnp.sum(ssq, axis=-1, keepdims=True) * (1.0 / D_MODEL) + EPS)
            for j in range(D_MODEL // LANES):
                cols = slice(j * LANES, (j + 1) * LANES)
                o_ref[r, cols] = o_ref[r, cols] * scale * gf_ref[:, cols]
        return c

    lax.fori_loop(0, tm // rc, rows, 0)


def _moe_combine(x, slots, gates, ys, final_g=None, tm=512):
    t = x.shape[0]
    final = final_g is not None
    slot_spec = pl.BlockSpec((tm,), lambda i: (i,), memory_space=pltpu.SMEM)
    in_specs = [slot_spec, slot_spec,
                pl.BlockSpec((tm, D_MODEL), lambda i: (i, 0)),
                pl.BlockSpec((tm, LANES), lambda i: (i, 0))]
    args = [slots[0], slots[1], x, gates]
    if final:
        in_specs.append(pl.BlockSpec((1, D_MODEL), lambda i: (0, 0)))
        args.append(final_g)
    in_specs.append(pl.BlockSpec(memory_space=pl.ANY))
    args.append(ys)
    return pl.pallas_call(
        functools.partial(_moe_combine_kernel, tm, final),
        grid=(t // tm,),
        in_specs=in_specs,
        out_specs=pl.BlockSpec((tm, D_MODEL), lambda i: (i, 0)),
        out_shape=jax.ShapeDtypeStruct((t, D_MODEL), F32),
        scratch_shapes=[pltpu.VMEM((tm // SUBLANES, SUBLANES, D_MODEL), F32),
                        pltpu.VMEM((tm // SUBLANES, SUBLANES, D_MODEL), F32),
                        pltpu.SemaphoreType.DMA((2,))],
        compiler_params=_params("arbitrary"),
        name="moe_combine",
    )(*args)


def _moe_routed(x, route, gates, counts, g, wg, wu, wd, layer, final_g=None):
    t = x.shape[0]
    n_tiles = 2 * t // MOE_TILE + N_EXPERTS
    cnt = counts[:, 0].astype(jnp.int32)
    caps = (cnt + MOE_TILE - 1) // MOE_TILE
    cum = jnp.cumsum(caps)
    base = (cum - caps) * MOE_TILE
    n_valid = cum[-1:]
    tile_expert = jnp.minimum(
        jnp.sum((cum[None, :] <= jnp.arange(n_tiles, dtype=jnp.int32)[:, None]).astype(jnp.int32), axis=1),
        N_EXPERTS - 1)
    fill_start = jnp.concatenate([base + cnt, n_valid * MOE_TILE])
    fill_end = jnp.concatenate([base + caps * MOE_TILE, jnp.full((1,), n_tiles * MOE_TILE, jnp.int32)])
    slots = _moe_slots(route, base)
    xs = _moe_dispatch(x, slots, fill_start, fill_end, n_tiles * MOE_TILE)
    ys = _ffn_grouped(xs, g, wg, wu, wd, layer, tile_expert, n_valid)
    return _moe_combine(x, slots, gates, ys, final_g)


def _trunk(x, state_hgrn, state_pool, start_pos, w, seq, tm):
    batch = x.shape[0] // seq
    s_out, b_out = [], []
    for l in range(DEPTH):
        act = _mixer_in(x, w["norm_mix"][l], w["w_in"], w["lb_logits"], l, tm)
        j = l // 2
        dense = l % 2 == 0
        final_g = w["norm_final"] if l == DEPTH - 1 else None
        g_ffn = w["norm_ffn"][l]
        if seq > 1:
            og, s_new = _hgrn_prompt(act, w["hg_norm"][l], batch, seq)
            mixer = (x, og, act, w["w_o"], l, w["pool_w"][l], w["pool_scale"][l])
            if dense:
                assert final_g is None
                x, b_new = _mixer_ffn(*mixer, g_ffn, w["ffn_w_gate"], w["ffn_w_up"], w["ffn_w_down"], j,
                                      batch, seq)
            else:
                x, b_new, route, gates, counts = _mixer_route(*mixer, g_ffn, w["router_t"][j], batch, seq)
                x = _moe_routed(x, route, gates, counts, g_ffn, w["moe_w_gate"], w["moe_w_up"],
                                w["moe_w_down"], j, final_g=final_g)
        else:
            new_states = s_out[-1] if s_out else jnp.zeros_like(state_hgrn)
            og, s_new = _hgrn_step(act, w["hg_norm"][l], state_hgrn, l, new_states)
            yp, b_new = _pool_step(act, state_pool, l, w["pool_w"][l], w["pool_scale"][l], start_pos)
            x = _mixer_out(x, og, yp, w["w_o"], l, tm)
            if dense:
                x = _ffn(x, g_ffn, w["ffn_w_gate"], w["ffn_w_up"], w["ffn_w_down"], j, tm, 1408, final_g=final_g)
            else:
                x = _ffn(x, g_ffn, w["moe_w_gate"], w["moe_w_up"], w["moe_w_down"], j, tm, 1408,
                         router=w["router"][j], final_g=final_g)
        s_out.append(s_new)
        b_out.append(b_new)
    return x, (jnp.stack(s_out) if seq > 1 else s_out[-1]), jnp.stack(b_out)


def kernel(x_prompt, x_sample, state_hgrn, state_pool, lb_logits, norm_mix, w_in, w_o, hg_norm, pool_w,
           pool_scale, norm_ffn, ffn_w_gate, ffn_w_up, ffn_w_down, router, moe_w_gate, moe_w_up, moe_w_down,
           norm_final):
    batch, seq, _ = x_prompt.shape
    dec_batch, dec_seq, _ = x_sample.shape
    assert dec_seq == 1
    past_len = 16384
    w = dict(
        lb_logits=lb_logits,
        norm_mix=norm_mix.reshape(DEPTH, 1, D_MODEL),
        w_in=w_in,
        w_o=w_o,
        hg_norm=hg_norm.reshape(DEPTH, 1, HG_WIDTH),
        pool_w=pool_w.astype(BF16),
        pool_scale=pool_scale.reshape(DEPTH, 1, POOL_WIDTH),
        norm_ffn=norm_ffn.reshape(DEPTH, 1, D_MODEL),
        ffn_w_gate=ffn_w_gate.astype(BF16)[:, None],
        ffn_w_up=ffn_w_up.astype(BF16)[:, None],
        ffn_w_down=ffn_w_down.astype(BF16)[:, None],
        router=jnp.pad(router, ((0, 0), (0, 0), (0, LANES - N_EXPERTS))),
        router_t=jnp.swapaxes(router, 1, 2),
        moe_w_gate=moe_w_gate.astype(BF16),
        moe_w_up=moe_w_up.astype(BF16),
        moe_w_down=moe_w_down.astype(BF16),
        norm_final=norm_final.reshape(1, D_MODEL),
    )
    yp, sp, bp = _trunk(x_prompt.reshape(batch * seq, D_MODEL), None, None, 0, w, seq, 512)
    ys, ss, bs = _trunk(x_sample.reshape(dec_batch, D_MODEL), state_hgrn, state_pool, past_len, w, 1, 128)
    return (yp.reshape(batch, seq, D_MODEL), ys.reshape(dec_batch, 1, D_MODEL), sp, ss, bp, bs)
```

```python
import functools

import jax
import jax.numpy as jnp
from jax import lax
from jax.experimental import pallas as pl
from jax.experimental.pallas import tpu as pltpu

F32 = jnp.float32
BF16 = jnp.bfloat16

D_MODEL = 1024
DEPTH = 4
HG_WIDTH = 512
HG_HEADS = 4
HG_D = 128
POOL_WIDTH = 512
POOL_WINDOWS = (2, 4, 8, 16)
POOL_GROUP_W = 128
POOL_BUF = 15
IN_WIDTH = 4 * HG_WIDTH + POOL_WIDTH
ACT_WIDTH = 4 * HG_WIDTH + POOL_WIDTH
ACT_POOL_PART = 4
N_EXPERTS = 8
EPS = 1e-6
LANES = 128
SUBLANES = 8
VMEM_LIMIT = 56 * 1024 * 1024
HGRN_CHUNK = 128


def _params(*sem):
    return pltpu.CompilerParams(dimension_semantics=sem, vmem_limit_bytes=VMEM_LIMIT)


def _rmsnorm(x, g):
    return x * lax.rsqrt(jnp.mean(x * x, axis=-1, keepdims=True) + EPS) * g


def _silu(x):
    return x * jax.nn.sigmoid(x)


def _mixer_in_kernel(layer, x_ref, g_ref, w_ref, lbl_ref, o_ref, wb_ref):
    @pl.when(pl.program_id(0) == 0)
    def _():
        wb_ref[...] = w_ref[0].astype(BF16)

    h = _rmsnorm(x_ref[...], g_ref[...])
    p = jnp.dot(h.astype(BF16), wb_ref[...], preferred_element_type=F32)
    lg = lbl_ref[...]
    e = jnp.exp(lg - jnp.max(lg, axis=0, keepdims=True))
    pr = e / jnp.sum(e, axis=0, keepdims=True)
    cum = pr[0:1]
    for j in range(1, layer + 1):
        cum = cum + pr[j:j + 1]
    lb = cum - pr[0:1]
    w = HG_WIDTH
    q, fx, ix, g, u = p[:, :w], p[:, w:2 * w], p[:, 2 * w:3 * w], p[:, 3 * w:4 * w], p[:, 4 * w:]
    f = lb + (1.0 - lb) * jax.nn.sigmoid(fx)
    o_ref[:, 0:w] = _silu(q)
    o_ref[:, w:2 * w] = jnp.log(f)
    o_ref[:, 2 * w:3 * w] = ix
    o_ref[:, 3 * w:4 * w] = _silu(g)
    o_ref[:, 4 * w:] = u


def _mixer_in(x, g, w_in, lb_logits, layer, tm):
    t = x.shape[0]
    return pl.pallas_call(
        functools.partial(_mixer_in_kernel, layer),
        grid=(t // tm,),
        in_specs=[
            pl.BlockSpec((tm, D_MODEL), lambda i: (i, 0)),
            pl.BlockSpec((1, D_MODEL), lambda i: (0, 0)),
            pl.BlockSpec((1, D_MODEL, IN_WIDTH), lambda i: (layer, 0, 0)),
            pl.BlockSpec((DEPTH, HG_WIDTH), lambda i: (0, 0)),
        ],
        out_specs=pl.BlockSpec((tm, ACT_WIDTH), lambda i: (i, 0)),
        out_shape=jax.ShapeDtypeStruct((t, ACT_WIDTH), F32),
        scratch_shapes=[pltpu.VMEM((D_MODEL, IN_WIDTH), BF16)],
        compiler_params=_params("arbitrary"),
        name="mixer_in",
    )(x, g, w_in, lb_logits)


SMALL_LEVELS = (2, 4, 8)


def _level_sum_matrix(n, ti, si):
    half = n // 2
    mid = (ti & (-n)) + (half - 1)
    lo = jnp.where((ti & half) != 0, mid, ti)
    hi = jnp.where((ti & half) != 0, ti, mid)
    return (si > lo) & (si <= hi)


def _level_factors(q, k, b, n, row, arg=None):
    c = b.shape[0]
    half = n // 2
    if arg is None:
        zero = jnp.zeros((half, b.shape[1]), F32)
        qs, ks = [], []
        for r0 in range(0, c, n):
            mid = b[r0 + half - 1:r0 + half, :]
            ks += [k[r0:r0 + half] * jnp.exp(mid - b[r0:r0 + half]), zero]
            qs += [q[r0 + half:r0 + n] * jnp.exp(b[r0 + half:r0 + n] - mid)]
        return jnp.concatenate(qs, axis=0), jnp.concatenate(ks, axis=0)
    right = (row & half) != 0
    qk = jnp.where(right, q, k) * jnp.exp(arg)
    return jnp.where(right, qk, 0.0), jnp.where(right, 0.0, qk)


def _split3(x):
    hi = x.astype(BF16)
    r1 = x - hi.astype(F32)
    mid = r1.astype(BF16)
    lo = (r1 - mid.astype(F32)).astype(BF16)
    return jnp.concatenate([hi, mid, lo], axis=0)


HEAD_PAIRS = HG_HEADS // 2
PAIR_W = 2 * HG_D


def _pair_blocks(a):
    zero = jnp.zeros((a.shape[0], HG_D), a.dtype)
    return jnp.concatenate([jnp.concatenate([a[:, :HG_D], zero], axis=1),
                            jnp.concatenate([zero, a[:, HG_D:]], axis=1)], axis=0)


def _per_head(fn, a):
    return jnp.concatenate([fn(a[:, :HG_D]), fn(a[:, HG_D:])], axis=1)


def _hgrn_prompt_kernel(tl, q_ref, lf_ref, v_ref, g_ref, nrm_ref, o_ref, s_ref, st_ref):
    c = HGRN_CHUNK
    li = pl.program_id(1)

    @pl.when(li == 0)
    def _():
        st_ref[...] = jnp.zeros_like(st_ref)

    row = lax.broadcasted_iota(jnp.int32, (c, PAIR_W), 0)
    ti = lax.broadcasted_iota(jnp.int32, (c, c), 0)
    si = lax.broadcasted_iota(jnp.int32, (c, c), 1)
    ones = lambda m: jnp.where(m, 1.0, 0.0).astype(BF16)
    cum3 = jnp.concatenate([ones(si <= ti)] * 3, axis=1)
    lvl2 = jnp.concatenate([jnp.concatenate([jnp.where(_level_sum_matrix(n, ti, si), 1.0, 0.0)
                                             for n in SMALL_LEVELS], axis=0).astype(BF16)] * 2, axis=1)
    apart = jnp.concatenate([ti ^ si] * 2, axis=1)
    pi = lax.broadcasted_iota(jnp.int32, (PAIR_W, PAIR_W), 0)
    pj = lax.broadcasted_iota(jnp.int32, (PAIR_W, PAIR_W), 1)
    own = (pi < HG_D) == (pj < HG_D)
    nt = (((1,), (1,)), ((), ()))
    tn = (((0,), (0,)), ((), ()))

    def chunk(pairs, ci, carry):
        rows = pl.ds(pl.multiple_of(ci * c, c), c)
        for p in pairs:
            cols = slice(p * PAIR_W, (p + 1) * PAIR_W)
            q, lf, v = q_ref[rows, cols], lf_ref[rows, cols], v_ref[rows, cols]
            k = 1.0 - jnp.exp(lf)
            lf3 = _split3(lf)
            b = jnp.dot(cum3, lf3, preferred_element_type=F32)
            s1 = jnp.dot(lvl2, lf3[:2 * c], preferred_element_type=F32)
            args = {n: s1[i * c:(i + 1) * c] for i, n in enumerate(SMALL_LEVELS)}
            vbd = _pair_blocks(v.astype(BF16))
            scores = None
            n = c
            while n >= 2:
                qt, kt = _level_factors(q, k, b, n, row, args.get(n))
                s_n = lax.dot_general(qt.astype(BF16), _pair_blocks(kt.astype(BF16)), nt,
                                      preferred_element_type=F32)
                if n in args:
                    scores = jnp.where(apart < n, s_n, scores)
                else:
                    half = n // 2
                    parts = []
                    for j, r0 in enumerate(range(0, c, n)):
                        new = s_n[j * half:(j + 1) * half]
                        if scores is None:
                            parts += [jnp.zeros_like(new), new]
                        else:
                            keep = scores[r0 + half:r0 + n]
                            parts += [scores[r0:r0 + half], jnp.where(apart[r0 + half:r0 + n] < n, new, keep)]
                    scores = jnp.concatenate(parts, axis=0)
                n //= 2
            st = st_ref[p]
            qs = (q * jnp.exp(b)).astype(BF16)
            o = jnp.dot(scores.astype(BF16), vbd, preferred_element_type=F32)
            o = o + lax.dot_general(qs, st.astype(BF16), nt, preferred_element_type=F32)
            o = o + _per_head(lambda a: jnp.broadcast_to(jnp.sum(a, axis=-1, keepdims=True), a.shape), q * k) * v
            bc = b[c - 1:c, :]
            kd = (k * jnp.exp(bc - b)).astype(BF16)
            upd = lax.dot_general(v.astype(BF16), kd, tn, preferred_element_type=F32)
            st_ref[p] = st * jnp.exp(bc) + jnp.where(own, upd, 0.0)
            ms = _per_head(lambda a: jnp.broadcast_to(jnp.mean(a, axis=-1, keepdims=True), a.shape), o * o)
            o = o * lax.rsqrt(ms + EPS) * nrm_ref[:, cols]
            o_ref[rows, cols] = (o * g_ref[rows, cols]).astype(o_ref.dtype)
        return carry

    lax.fori_loop(0, tl // c, functools.partial(chunk, range(HEAD_PAIRS)), 0, unroll=4)

    @pl.when(li == pl.num_programs(1) - 1)
    def _():
        for p in range(HEAD_PAIRS):
            st = st_ref[p]
            s_ref[0, 2 * p] = st[:HG_D, :HG_D].T
            s_ref[0, 2 * p + 1] = st[HG_D:, HG_D:].T


def _hgrn_prompt(act, hg_norm, batch, seq, tl=2048):
    nl = seq // tl
    blk = lambda part: pl.BlockSpec((tl, HG_WIDTH), lambda b, i: (b * nl + i, part))
    return pl.pallas_call(
        functools.partial(_hgrn_prompt_kernel, tl),
        grid=(batch, nl),
        in_specs=[blk(0), blk(1), blk(2), blk(3),
                  pl.BlockSpec((1, HG_WIDTH), lambda b, i: (0, 0))],
        out_specs=[pl.BlockSpec((tl, HG_WIDTH), lambda b, i: (b * nl + i, 0)),
                   pl.BlockSpec((1, HG_HEADS, HG_D, HG_D), lambda b, i: (b, 0, 0, 0))],
        out_shape=[jax.ShapeDtypeStruct((batch * seq, HG_WIDTH), BF16),
                   jax.ShapeDtypeStruct((batch, HG_HEADS, HG_D, HG_D), F32)],
        scratch_shapes=[pltpu.VMEM((HEAD_PAIRS, PAIR_W, PAIR_W), F32)],
        compiler_params=_params("parallel", "arbitrary"),
        name="hgrn_prompt",
    )(act, act, act, act, hg_norm)


HGRN_STEP_BT = 32


def _hgrn_step_kernel(q_ref, lf_ref, v_ref, g_ref, nrm_ref, s_ref, _, o_ref, so_ref):
    bt = HGRN_STEP_BT
    q = q_ref[...]
    v = v_ref[...]
    f = jnp.exp(lf_ref[...])
    ri = lax.broadcasted_iota(jnp.int32, (3 * bt, PAIR_W), 0) % bt
    half = lax.broadcasted_iota(jnp.int32, (3 * bt, PAIR_W), 1) // HG_D
    tn = (((0,), (0,)), ((), ()))
    pieces = [_split3(a) for a in (f, 1.0 - f, q)]
    rows = []
    for s in range(0, bt, 2):
        pick = jnp.where(ri == s + half, 1.0, 0.0).astype(BF16)
        fc, kc, qc = [lax.dot_general(p, pick, tn, preferred_element_type=F32) for p in pieces]
        for d in range(2):
            cols = slice(d * HG_D, (d + 1) * HG_D)
            sn = fc[:, cols] * s_ref[0, s + d, 0] + kc[:, cols] * v[s + d:s + d + 1, :]
            so_ref[0, s + d, 0] = sn
            rows.append(jnp.sum(sn * qc[:, cols], axis=0, keepdims=True))
    o = jnp.concatenate(rows, axis=0)
    o = o * lax.rsqrt(jnp.mean(o * o, axis=-1, keepdims=True) + EPS) * nrm_ref[...]
    o_ref[...] = (o * g_ref[...]).astype(o_ref.dtype)


def _hgrn_step(act, hg_norm, states, layer, new_states):
    batch = act.shape[0]
    bt = HGRN_STEP_BT
    blk = lambda part: pl.BlockSpec((bt, HG_D), lambda i, h: (i, part * HG_HEADS + h))
    sblk = pl.BlockSpec((1, bt, 1, HG_D, HG_D), lambda i, h: (layer, i, h, 0, 0))
    return pl.pallas_call(
        _hgrn_step_kernel,
        grid=(batch // bt, HG_HEADS),
        in_specs=[blk(0), blk(1), blk(2), blk(3),
                  pl.BlockSpec((1, HG_D), lambda i, h: (0, h)), sblk,
                  pl.BlockSpec(memory_space=pl.ANY)],
        out_specs=[pl.BlockSpec((bt, HG_D), lambda i, h: (i, h)), sblk],
        out_shape=[jax.ShapeDtypeStruct((batch, HG_WIDTH), BF16),
                   jax.ShapeDtypeStruct(new_states.shape, F32)],
        input_output_aliases={6: 1},
        compiler_params=_params("parallel", "parallel"),
        name="hgrn_step",
    )(act, act, act, act, hg_norm, states, new_states)


def _pool_map(d_groups, wp_ref, scale_ref, o_ref):
    for gi, d in enumerate(d_groups):
        sl = slice(gi * POOL_GROUP_W, (gi + 1) * POOL_GROUP_W)
        y = jnp.dot(d.astype(BF16), wp_ref[gi], preferred_element_type=F32)
        o_ref[:, sl] = (y * scale_ref[:, sl]).astype(o_ref.dtype)


POOL_HIST = 16


def _pool_tile(i, tl, z, carry_ref, wp_ref, scale_ref, o_ref, nb_ref):
    @pl.when(i == 0)
    def _():
        carry_ref[...] = jnp.zeros_like(carry_ref)

    ext = jnp.concatenate([carry_ref[...], z], axis=0)
    sums = {1: ext}
    w = 1
    while w < max(POOL_WINDOWS):
        sums[2 * w] = sums[w] + pltpu.roll(sums[w], w, 0)
        w *= 2
    pos = i * tl + lax.broadcasted_iota(jnp.int32, (tl, POOL_GROUP_W), 0)
    ds = []
    for gi, w in enumerate(POOL_WINDOWS):
        sl = slice(gi * POOL_GROUP_W, (gi + 1) * POOL_GROUP_W)
        cnt = jnp.minimum(pos + 1, w).astype(F32)
        ds.append(sums[w][POOL_HIST:, sl] / cnt - z[:, sl])
    _pool_map(ds, wp_ref, scale_ref, o_ref)
    carry_ref[...] = z[tl - POOL_HIST:, :]

    @pl.when(i == pl.num_programs(1) - 1)
    def _():
        nb_ref[0] = z[tl - POOL_BUF:, :]


POOL_STEP_BT = 16


def _pool_step_kernel(start_pos, u_ref, buf_ref, wp_ref, scale_ref, o_ref, nb_ref):
    u = u_ref[...]
    buf = buf_ref[0]
    ds = []
    for gi, w in enumerate(POOL_WINDOWS):
        sl = slice(gi * POOL_GROUP_W, (gi + 1) * POOL_GROUP_W)
        tot = u[:, sl] + jnp.sum(buf[:, POOL_BUF - (w - 1):, sl], axis=1)
        ds.append(tot / float(min(start_pos + 1, w)) - u[:, sl])
    _pool_map(ds, wp_ref, scale_ref, o_ref)
    nb_ref[:, 0:POOL_BUF - 1, :] = buf[:, 1:, :]
    nb_ref[:, POOL_BUF - 1:, :] = u[:, None, :]


def _pool_step(act, bufs, layer, pool_w, pool_scale, start_pos):
    batch = act.shape[0]
    bt = POOL_STEP_BT
    return pl.pallas_call(
        functools.partial(_pool_step_kernel, start_pos),
        grid=(batch // bt,),
        in_specs=[pl.BlockSpec((bt, POOL_WIDTH), lambda i: (i, ACT_POOL_PART)),
                  pl.BlockSpec((1, bt, POOL_BUF, POOL_WIDTH), lambda i: (layer, i, 0, 0)),
                  pl.BlockSpec((len(POOL_WINDOWS), POOL_GROUP_W, POOL_GROUP_W), lambda i: (0, 0, 0)),
                  pl.BlockSpec((1, POOL_WIDTH), lambda i: (0, 0))],
        out_specs=[pl.BlockSpec((bt, POOL_WIDTH), lambda i: (i, 0)),
                   pl.BlockSpec((bt, POOL_BUF, POOL_WIDTH), lambda i: (i, 0, 0))],
        out_shape=[jax.ShapeDtypeStruct((batch, POOL_WIDTH), BF16),
                   jax.ShapeDtypeStruct(bufs.shape[1:], F32)],
        compiler_params=_params("parallel"),
        name="pool_step",
    )(act, bufs, pool_w, pool_scale)


def _mixer_out_kernel(x_ref, o_ref, p_ref, w_ref, y_ref, wb_ref):
    @pl.when(pl.program_id(0) == 0)
    def _():
        wb_ref[...] = w_ref[0].astype(BF16)

    y = jnp.dot(o_ref[...], wb_ref[0:HG_WIDTH, :], preferred_element_type=F32)
    y = y + jnp.dot(p_ref[...], wb_ref[HG_WIDTH:, :], preferred_element_type=F32)
    y_ref[...] = x_ref[...] + y


def _mixer_out(x, og, yp, w_o, layer, tm):
    t = x.shape[0]
    return pl.pallas_call(
        _mixer_out_kernel,
        grid=(t // tm,),
        in_specs=[pl.BlockSpec((tm, D_MODEL), lambda i: (i, 0)),
                  pl.BlockSpec((tm, HG_WIDTH), lambda i: (i, 0)),
                  pl.BlockSpec((tm, POOL_WIDTH), lambda i: (i, 0)),
                  pl.BlockSpec((1, D_MODEL, D_MODEL), lambda i: (layer, 0, 0))],
        out_specs=pl.BlockSpec((tm, D_MODEL), lambda i: (i, 0)),
        out_shape=jax.ShapeDtypeStruct((t, D_MODEL), F32),
        scratch_shapes=[pltpu.VMEM((D_MODEL, D_MODEL), BF16)],
        compiler_params=_params("arbitrary"),
        name="mixer_out",
    )(x, og, yp, w_o)


def _top2_combine(logits):
    lane = lax.broadcasted_iota(jnp.int32, logits.shape, 1).astype(F32)
    neg = jnp.float32(-jnp.inf)
    lg = jnp.where(lane < N_EXPERTS, logits, neg)
    m1 = jnp.max(lg, axis=-1, keepdims=True)
    i1 = jnp.min(jnp.where(lg == m1, lane, float(LANES)), axis=-1, keepdims=True)
    lg2 = jnp.where(lane == i1, neg, lg)
    m2 = jnp.max(lg2, axis=-1, keepdims=True)
    i2 = jnp.min(jnp.where(lg2 == m2, lane, float(LANES)), axis=-1, keepdims=True)
    e2 = jnp.exp(m2 - m1)
    g1 = 1.0 / (1.0 + e2)
    g2 = e2 / (1.0 + e2)
    return jnp.where(lane == i1, g1, 0.0) + jnp.where(lane == i2, g2, 0.0)


def _ffn_kernel(moe, final, x_ref, g_ref, *rest):
    if moe:
        r_ref, rest = rest[0], rest[1:]
    if final:
        gf_ref, rest = rest[0], rest[1:]
    wg_ref, wu_ref, wd_ref, y_ref, h_ref, acc_ref = rest[:6]
    e, j = pl.program_id(1), pl.program_id(2)
    first = (e == 0) & (j == 0)
    last = (e == pl.num_programs(1) - 1) & (j == pl.num_programs(2) - 1)

    @pl.when(first)
    def _():
        h = _rmsnorm(x_ref[...], g_ref[...])
        h_ref[...] = h.astype(BF16)
        acc_ref[...] = jnp.zeros_like(acc_ref)
        if moe:
            logits = jnp.dot(h, r_ref[...], precision=lax.Precision.HIGHEST, preferred_element_type=F32)
            rest[6][...] = _top2_combine(logits)

    h = h_ref[...]
    gate = jnp.dot(h, wg_ref[0, 0], preferred_element_type=F32)
    up = jnp.dot(h, wu_ref[0, 0], preferred_element_type=F32)
    y = jnp.dot((_silu(gate) * up).astype(BF16), wd_ref[0, 0], preferred_element_type=F32)
    if moe:
        comb = rest[6][...]
        lane = lax.broadcasted_iota(jnp.int32, comb.shape, 1)
        y = jnp.sum(jnp.where(lane == e, comb, 0.0), axis=-1, keepdims=True) * y
    acc_ref[...] += y

    @pl.when(last)
    def _():
        out = x_ref[...] + acc_ref[...]
        if final:
            out = _rmsnorm(out, gf_ref[...])
        y_ref[...] = out


def _ffn(x, g, wg, wu, wd, layer, tm, tf, router=None, final_g=None):
    t = x.shape[0]
    _, ne, _, f = wg.shape
    moe, final = router is not None, final_g is not None
    vec = pl.BlockSpec((1, D_MODEL), lambda i, e, j: (0, 0))
    in_specs = [pl.BlockSpec((tm, D_MODEL), lambda i, e, j: (i, 0)), vec]
    args = [x, g]
    scratch = [pltpu.VMEM((tm, D_MODEL), BF16), pltpu.VMEM((tm, D_MODEL), F32)]
    if moe:
        in_specs.append(pl.BlockSpec((D_MODEL, LANES), lambda i, e, j: (0, 0)))
        args.append(router)
        scratch.append(pltpu.VMEM((tm, LANES), F32))
    if final:
        in_specs.append(vec)
        args.append(final_g)
    in_specs += [pl.BlockSpec((1, 1, D_MODEL, tf), lambda i, e, j: (layer, e, 0, j)),
                 pl.BlockSpec((1, 1, D_MODEL, tf), lambda i, e, j: (layer, e, 0, j)),
                 pl.BlockSpec((1, 1, tf, D_MODEL), lambda i, e, j: (layer, e, j, 0))]
    args += [wg, wu, wd]
    return pl.pallas_call(
        functools.partial(_ffn_kernel, moe, final),
        grid=(t // tm, ne, f // tf),
        in_specs=in_specs,
        out_specs=pl.BlockSpec((tm, D_MODEL), lambda i, e, j: (i, 0)),
        out_shape=jax.ShapeDtypeStruct((t, D_MODEL), F32),
        scratch_shapes=scratch,
        compiler_params=_params("parallel", "arbitrary", "arbitrary"),
        name="ffn_moe" if moe else "ffn_dense",
    )(*args)


def _mixer_tile(b, i, tl, x_ref, o_ref, u_ref, w_ref, wp_ref, scale_ref, nb_ref, wb_ref, carry_ref, yp_ref):
    @pl.when((b == 0) & (i == 0))
    def _():
        wb_ref[...] = w_ref[0].astype(BF16)

    _pool_tile(i, tl, u_ref[...], carry_ref, wp_ref, scale_ref, yp_ref, nb_ref)
    y = jnp.dot(o_ref[...], wb_ref[0:HG_WIDTH, :], preferred_element_type=F32)
    y = y + jnp.dot(yp_ref[...], wb_ref[HG_WIDTH:, :], preferred_element_type=F32)
    return x_ref[...] + y


def _mixer_ffn_kernel(tl, x_ref, o_ref, u_ref, w_ref, wp_ref, scale_ref, g_ref, wg_ref, wu_ref, wd_ref,
                      y_ref, nb_ref, wb_ref, carry_ref, yp_ref, x1_ref, h_ref, acc_ref):
    b, i, j = pl.program_id(0), pl.program_id(1), pl.program_id(2)

    @pl.when(j == 0)
    def _():
        x1 = _mixer_tile(b, i, tl, x_ref, o_ref, u_ref, w_ref, wp_ref, scale_ref, nb_ref, wb_ref, carry_ref, yp_ref)
        x1_ref[...] = x1
        h_ref[...] = _rmsnorm(x1, g_ref[...]).astype(BF16)
        acc_ref[...] = jnp.zeros_like(acc_ref)

    h = h_ref[...]
    gate = jnp.dot(h, wg_ref[0, 0], preferred_element_type=F32)
    up = jnp.dot(h, wu_ref[0, 0], preferred_element_type=F32)
    acc_ref[...] += jnp.dot((_silu(gate) * up).astype(BF16), wd_ref[0, 0], preferred_element_type=F32)

    @pl.when(j == pl.num_programs(2) - 1)
    def _():
        y_ref[...] = x1_ref[...] + acc_ref[...]


def _mixer_specs(nl, tl, layer, tok, fixed):
    ins = [pl.BlockSpec((tl, D_MODEL), tok(0)),
           pl.BlockSpec((tl, HG_WIDTH), tok(0)),
           pl.BlockSpec((tl, POOL_WIDTH), tok(ACT_POOL_PART)),
           pl.BlockSpec((1, D_MODEL, D_MODEL), fixed(layer, 0, 0)),
           pl.BlockSpec((len(POOL_WINDOWS), POOL_GROUP_W, POOL_GROUP_W), fixed(0, 0, 0)),
           pl.BlockSpec((1, POOL_WIDTH), fixed(0, 0))]
    scratch = [pltpu.VMEM((D_MODEL, D_MODEL), BF16), pltpu.VMEM((POOL_HIST, POOL_WIDTH), F32),
               pltpu.VMEM((tl, POOL_WIDTH), BF16)]
    return ins, scratch


def _mixer_ffn(x, og, act, w_o, layer, pool_w, pool_scale, g, wg, wu, wd, wset, batch, seq, tl=512, tf=1408):
    nl = seq // tl
    f = wg.shape[3]
    tok = lambda cols: (lambda b, i, j: (b * nl + i, cols))
    fixed = lambda *idx: (lambda b, i, j: idx)
    ins, scratch = _mixer_specs(nl, tl, layer, tok, fixed)
    ins += [pl.BlockSpec((1, D_MODEL), fixed(0, 0)),
            pl.BlockSpec((1, 1, D_MODEL, tf), lambda b, i, j: (wset, 0, 0, j)),
            pl.BlockSpec((1, 1, D_MODEL, tf), lambda b, i, j: (wset, 0, 0, j)),
            pl.BlockSpec((1, 1, tf, D_MODEL), lambda b, i, j: (wset, 0, j, 0))]
    scratch += [pltpu.VMEM((tl, D_MODEL), F32), pltpu.VMEM((tl, D_MODEL), BF16), pltpu.VMEM((tl, D_MODEL), F32)]
    return pl.pallas_call(
        functools.partial(_mixer_ffn_kernel, tl),
        grid=(batch, nl, f // tf),
        in_specs=ins,
        out_specs=[pl.BlockSpec((tl, D_MODEL), tok(0)),
                   pl.BlockSpec((1, POOL_BUF, POOL_WIDTH), lambda b, i, j: (b, 0, 0))],
        out_shape=[jax.ShapeDtypeStruct((batch * seq, D_MODEL), F32),
                   jax.ShapeDtypeStruct((batch, POOL_BUF, POOL_WIDTH), F32)],
        scratch_shapes=scratch,
        compiler_params=_params("arbitrary", "arbitrary", "arbitrary"),
        name="mixer_ffn",
    )(x, og, act, w_o, pool_w, pool_scale, g, wg, wu, wd)


MOE_TILE = 512
ROUTE_ROWS = 8
NT_DIMS = (((1,), (1,)), ((), ()))


def _mixer_route_kernel(tm, x_ref, o_ref, u_ref, w_ref, wp_ref, scale_ref, g_ref, rt_ref,
                        x1_ref, nb_ref, route_ref, gate_ref, cnt_ref,
                        wb_ref, pool_carry_ref, yp_ref, carry_ref, earlier_ref):
    b, i = pl.program_id(0), pl.program_id(1)

    @pl.when((b == 0) & (i == 0))
    def _():
        carry_ref[...] = jnp.zeros_like(carry_ref)
        t0 = lax.broadcasted_iota(jnp.int32, (tm, tm), 0)
        t1 = lax.broadcasted_iota(jnp.int32, (tm, tm), 1)
        earlier_ref[...] = jnp.where(t0 < t1, 1.0, 0.0).astype(BF16)

    x1 = _mixer_tile(b, i, tm, x_ref, o_ref, u_ref, w_ref, wp_ref, scale_ref, nb_ref, wb_ref, pool_carry_ref, yp_ref)
    x1_ref[...] = x1
    h = _rmsnorm(x1, g_ref[...])
    lt = lax.dot_general(rt_ref[...], h, NT_DIMS, precision=lax.Precision.HIGHEST,
                         preferred_element_type=F32)
    ex = lax.broadcasted_iota(jnp.int32, lt.shape, 0).astype(F32)
    neg = jnp.float32(-jnp.inf)
    m1 = jnp.max(lt, axis=0, keepdims=True)
    i1 = jnp.min(jnp.where(lt == m1, ex, float(N_EXPERTS)), axis=0, keepdims=True)
    l2 = jnp.where(ex == i1, neg, lt)
    m2 = jnp.max(l2, axis=0, keepdims=True)
    i2 = jnp.min(jnp.where(l2 == m2, ex, float(N_EXPERTS)), axis=0, keepdims=True)
    e2 = jnp.exp(m2 - m1)
    g1 = 1.0 / (1.0 + e2)
    g2 = e2 / (1.0 + e2)
    sel1, sel2 = ex == i1, ex == i2
    member = jnp.where(sel1 | sel2, 1.0, 0.0)
    rank = jnp.dot(member.astype(BF16), earlier_ref[...], preferred_element_type=F32) + carry_ref[:, 0:1]
    route_ref[...] = jnp.zeros_like(route_ref)
    route_ref[0:1, :] = i1.astype(jnp.int32)
    route_ref[1:2, :] = jnp.sum(jnp.where(sel1, rank, 0.0), axis=0, keepdims=True).astype(jnp.int32)
    route_ref[2:3, :] = i2.astype(jnp.int32)
    route_ref[3:4, :] = jnp.sum(jnp.where(sel2, rank, 0.0), axis=0, keepdims=True).astype(jnp.int32)
    carry_ref[...] += jnp.sum(member, axis=1, keepdims=True)
    cnt_ref[...] = carry_ref[...]
    row = lax.broadcasted_iota(jnp.int32, (LANES, tm), 0)
    gate_ref[...] = jnp.where(row == 0, g1, jnp.where(row == 1, g2, 0.0)).T


def _mixer_route(x, og, act, w_o, layer, pool_w, pool_scale, g, router_t, batch, seq, tm=512):
    nl = seq // tm
    t = batch * seq
    tok = lambda cols: (lambda b, i: (b * nl + i, cols))
    fixed = lambda *idx: (lambda b, i: idx)
    ins, scratch = _mixer_specs(nl, tm, layer, tok, fixed)
    ins += [pl.BlockSpec((1, D_MODEL), fixed(0, 0)),
            pl.BlockSpec((N_EXPERTS, D_MODEL), fixed(0, 0))]
    scratch += [pltpu.VMEM((N_EXPERTS, LANES), F32), pltpu.VMEM((tm, tm), BF16)]
    return pl.pallas_call(
        functools.partial(_mixer_route_kernel, tm),
        grid=(batch, nl),
        in_specs=ins,
        out_specs=[pl.BlockSpec((tm, D_MODEL), tok(0)),
                   pl.BlockSpec((1, POOL_BUF, POOL_WIDTH), lambda b, i: (b, 0, 0)),
                   pl.BlockSpec((ROUTE_ROWS, tm), lambda b, i: (0, b * nl + i)),
                   pl.BlockSpec((tm, LANES), tok(0)),
                   pl.BlockSpec((N_EXPERTS, LANES), fixed(0, 0))],
        out_shape=[jax.ShapeDtypeStruct((t, D_MODEL), F32),
                   jax.ShapeDtypeStruct((batch, POOL_BUF, POOL_WIDTH), F32),
                   jax.ShapeDtypeStruct((ROUTE_ROWS, t), jnp.int32),
                   jax.ShapeDtypeStruct((t, LANES), F32),
                   jax.ShapeDtypeStruct((N_EXPERTS, LANES), F32)],
        scratch_shapes=scratch,
        compiler_params=_params("arbitrary", "arbitrary"),
        name="mixer_route",
    )(x, og, act, w_o, pool_w, pool_scale, g, router_t)


def _moe_slots_kernel(base_ref, route_ref, slot_ref):
    r = route_ref[...]
    slot_ref[...] = jnp.zeros_like(slot_ref)
    for k in range(2):
        e, rank = r[2 * k:2 * k + 1, :], r[2 * k + 1:2 * k + 2, :]
        start = jnp.zeros_like(e)
        for j in range(N_EXPERTS):
            start = jnp.where(e == j, base_ref[j], start)
        slot_ref[k:k + 1, :] = start + rank


def _moe_slots(route, base, tm=2048):
    t = route.shape[1]
    return pl.pallas_call(
        _moe_slots_kernel,
        grid_spec=pltpu.PrefetchScalarGridSpec(
            num_scalar_prefetch=1,
            grid=(t // tm,),
            in_specs=[pl.BlockSpec((ROUTE_ROWS, tm), lambda i, *_: (0, i))],
            out_specs=pl.BlockSpec((ROUTE_ROWS, tm), lambda i, *_: (0, i))),
        out_shape=jax.ShapeDtypeStruct((ROUTE_ROWS, t), jnp.int32),
        compiler_params=_params("parallel"),
        name="moe_slots",
    )(base, route)


COMBINE_ROWS = 128


def _rows_done(hbm, rows, sem):
    pltpu.make_async_copy(hbm.at[pl.ds(0, rows)], hbm.at[pl.ds(0, rows)], sem).wait()


def _moe_dispatch_kernel(tm, fs_ref, fe_ref, s0_ref, s1_ref, x_ref, xs_hbm, sem):
    i = pl.program_id(0)

    def row_copy(group, sub, dst_row):
        return pltpu.make_async_copy(x_ref.at[group, pl.ds(sub, 1), :], xs_hbm.at[pl.ds(dst_row, 1)], sem)

    def issue(c, carry):
        for u in range(SUBLANES):
            for k, s_ref in enumerate((s0_ref, s1_ref)):
                row_copy(c, u, s_ref[c * SUBLANES + u]).start(priority=k)
        return carry

    lax.fori_loop(0, tm // SUBLANES, issue, 0)
    _rows_done(xs_hbm, 2 * tm, sem)

    @pl.when(i == pl.num_programs(0) - 1)
    def _():
        for e in range(N_EXPERTS + 1):
            def fill(p, c):
                row_copy(0, 0, p).start()
                return c

            def drain(p, c):
                row_copy(0, 0, p).wait()
                return c

            lax.fori_loop(fs_ref[e], fe_ref[e], fill, 0)
            lax.fori_loop(fs_ref[e], fe_ref[e], drain, 0)


def _moe_dispatch(rows, slots, fill_start, fill_end, n_slots, tm=1024):
    t, width = rows.shape
    slot_spec = pl.BlockSpec((tm,), lambda i, *_: (i,), memory_space=pltpu.SMEM)
    return pl.pallas_call(
        functools.partial(_moe_dispatch_kernel, tm),
        grid_spec=pltpu.PrefetchScalarGridSpec(
            num_scalar_prefetch=2,
            grid=(t // tm,),
            in_specs=[slot_spec, slot_spec,
                      pl.BlockSpec((tm // SUBLANES, SUBLANES, width), lambda i, *_: (i, 0, 0))],
            out_specs=pl.BlockSpec(memory_space=pl.ANY),
            scratch_shapes=[pltpu.SemaphoreType.DMA(())]),
        out_shape=jax.ShapeDtypeStruct((n_slots, width), rows.dtype),
        compiler_params=_params("arbitrary"),
        name="moe_dispatch",
    )(fill_start, fill_end, slots[0], slots[1], rows.reshape(t // SUBLANES, SUBLANES, width))


def _ffn_grouped_kernel(te_ref, nv_ref, x_ref, g_ref, wg_ref, wu_ref, wd_ref, y_ref):
    i = pl.program_id(0)

    @pl.when(i < nv_ref[0])
    def _():
        h = _rmsnorm(x_ref[...], g_ref[...]).astype(BF16)
        gate = jnp.dot(h, wg_ref[0, 0], preferred_element_type=F32)
        up = jnp.dot(h, wu_ref[0, 0], preferred_element_type=F32)
        y_ref[...] = jnp.dot((_silu(gate) * up).astype(BF16), wd_ref[0, 0], preferred_element_type=F32)

    @pl.when(i >= nv_ref[0])
    def _():
        y_ref[...] = jnp.zeros_like(y_ref)


def _ffn_grouped(xs, g, wg, wu, wd, layer, tile_expert, n_valid):
    n_slots = xs.shape[0]
    f = wg.shape[3]
    rows = pl.BlockSpec((MOE_TILE, D_MODEL), lambda i, te, nv: (i, 0))
    return pl.pallas_call(
        _ffn_grouped_kernel,
        grid_spec=pltpu.PrefetchScalarGridSpec(
            num_scalar_prefetch=2,
            grid=(n_slots // MOE_TILE,),
            in_specs=[rows,
                      pl.BlockSpec((1, D_MODEL), lambda i, te, nv: (0, 0)),
                      pl.BlockSpec((1, 1, D_MODEL, f), lambda i, te, nv: (layer, te[i], 0, 0)),
                      pl.BlockSpec((1, 1, D_MODEL, f), lambda i, te, nv: (layer, te[i], 0, 0)),
                      pl.BlockSpec((1, 1, f, D_MODEL), lambda i, te, nv: (layer, te[i], 0, 0))],
            out_specs=rows),
        out_shape=jax.ShapeDtypeStruct((n_slots, D_MODEL), F32),
        compiler_params=_params("arbitrary"),
        name="ffn_grouped",
    )(tile_expert, n_valid, xs, g, wg, wu, wd)


def _moe_combine_kernel(tm, final, s0_ref, s1_ref, x_ref, gate_ref, *rest):
    if final:
        gf_ref, rest = rest[0], rest[1:]
    ys_hbm, o_ref, y1_ref, y2_ref, sems = rest

    def issue(c, carry):
        for u in range(SUBLANES):
            for k, (s_ref, buf) in enumerate(((s0_ref, y1_ref), (s1_ref, y2_ref))):
                pltpu.make_async_copy(ys_hbm.at[pl.ds(s_ref[c * SUBLANES + u], 1)],
                                      buf.at[c, pl.ds(u, 1), :], sems.at[k]).start(priority=k)
        return carry

    lax.fori_loop(0, tm // SUBLANES, issue, 0)
    for k in range(2):
        _rows_done(ys_hbm, tm, sems.at[k])
    rc = COMBINE_ROWS

    def rows(ci, c):
        r = pl.ds(pl.multiple_of(ci * rc, rc), rc)
        rg = pl.ds(pl.multiple_of(ci * (rc // SUBLANES), rc // SUBLANES), rc // SUBLANES)
        g1 = jnp.broadcast_to(gate_ref[r, 0:1], (rc, LANES))
        g2 = jnp.broadcast_to(gate_ref[r, 1:2], (rc, LANES))
        ssq = jnp.zeros((rc, LANES), F32)
        for j in range(D_MODEL // LANES):
            cols = slice(j * LANES, (j + 1) * LANES)
            y1 = y1_ref[rg, :, cols].reshape(rc, LANES)
            y2 = y2_ref[rg, :, cols].reshape(rc, LANES)
            out = x_ref[r, cols] + g1 * y1 + g2 * y2
            o_ref[r, cols] = out
            ssq = ssq + out * out
        if final:
            scale = lax.rsqrt(jnp.sum(ssq, axis=-1, keepdims=True) * (1.0 / D_MODEL) + EPS)
            for j in range(D_MODEL // LANES):
                cols = slice(j * LANES, (j + 1) * LANES)
                o_ref[r, cols] = o_ref[r, cols] * scale * gf_ref[:, cols]
        return c

    lax.fori_loop(0, tm // rc, rows, 0)


def _moe_combine(x, slots, gates, ys, final_g=None, tm=512):
    t = x.shape[0]
    final = final_g is not None
    slot_spec = pl.BlockSpec((tm,), lambda i: (i,), memory_space=pltpu.SMEM)
    in_specs = [slot_spec, slot_spec,
                pl.BlockSpec((tm, D_MODEL), lambda i: (i, 0)),
                pl.BlockSpec((tm, LANES), lambda i: (i, 0))]
    args = [slots[0], slots[1], x, gates]
    if final:
        in_specs.append(pl.BlockSpec((1, D_MODEL), lambda i: (0, 0)))
        args.append(final_g)
    in_specs.append(pl.BlockSpec(memory_space=pl.ANY))
    args.append(ys)
    return pl.pallas_call(
        functools.partial(_moe_combine_kernel, tm, final),
        grid=(t // tm,),
        in_specs=in_specs,
        out_specs=pl.BlockSpec((tm, D_MODEL), lambda i: (i, 0)),
        out_shape=jax.ShapeDtypeStruct((t, D_MODEL), F32),
        scratch_shapes=[pltpu.VMEM((tm // SUBLANES, SUBLANES, D_MODEL), F32),
                        pltpu.VMEM((tm // SUBLANES, SUBLANES, D_MODEL), F32),
                        pltpu.SemaphoreType.DMA((2,))],
        compiler_params=_params("arbitrary"),
        name="moe_combine",
    )(*args)


def _moe_routed(x, route, gates, counts, g, wg, wu, wd, layer, final_g=None):
    t = x.shape[0]
    n_tiles = 2 * t // MOE_TILE + N_EXPERTS
    cnt = counts[:, 0].astype(jnp.int32)
    caps = (cnt + MOE_TILE - 1) // MOE_TILE
    cum = jnp.cumsum(caps)
    base = (cum - caps) * MOE_TILE
    n_valid = cum[-1:]
    tile_expert = jnp.minimum(
        jnp.sum((cum[None, :] <= jnp.arange(n_tiles, dtype=jnp.int32)[:, None]).astype(jnp.int32), axis=1),
        N_EXPERTS - 1)
    fill_start = jnp.concatenate([base + cnt, n_valid * MOE_TILE])
    fill_end = jnp.concatenate([base + caps * MOE_TILE, jnp.full((1,), n_tiles * MOE_TILE, jnp.int32)])
    slots = _moe_slots(route, base)
    xs = _moe_dispatch(x, slots, fill_start, fill_end, n_tiles * MOE_TILE)
    ys = _ffn_grouped(xs, g, wg, wu, wd, layer, tile_expert, n_valid)
    return _moe_combine(x, slots, gates, ys, final_g)


def _trunk(x, state_hgrn, state_pool, start_pos, w, seq, tm):
    batch = x.shape[0] // seq
    s_out, b_out = [], []
    for l in range(DEPTH):
        act = _mixer_in(x, w["norm_mix"][l], w["w_in"], w["lb_logits"], l, tm)
        j = l // 2
        dense = l % 2 == 0
        final_g = w["norm_final"] if l == DEPTH - 1 else None
        g_ffn = w["norm_ffn"][l]
        if seq > 1:
            og, s_new = _hgrn_prompt(act, w["hg_norm"][l], batch, seq)
            mixer = (x, og, act, w["w_o"], l, w["pool_w"][l], w["pool_scale"][l])
            if dense:
                assert final_g is None
                x, b_new = _mixer_ffn(*mixer, g_ffn, w["ffn_w_gate"], w["ffn_w_up"], w["ffn_w_down"], j,
                                      batch, seq)
            else:
                x, b_new, route, gates, counts = _mixer_route(*mixer, g_ffn, w["router_t"][j], batch, seq)
                x = _moe_routed(x, route, gates, counts, g_ffn, w["moe_w_gate"], w["moe_w_up"],
                                w["moe_w_down"], j, final_g=final_g)
        else:
            new_states = s_out[-1] if s_out else jnp.zeros_like(state_hgrn)
            og, s_new = _hgrn_step(act, w["hg_norm"][l], state_hgrn, l, new_states)
            yp, b_new = _pool_step(act, state_pool, l, w["pool_w"][l], w["pool_scale"][l], start_pos)
            x = _mixer_out(x, og, yp, w["w_o"], l, tm)
            if dense:
                x = _ffn(x, g_ffn, w["ffn_w_gate"], w["ffn_w_up"], w["ffn_w_down"], j, tm, 1408, final_g=final_g)
            else:
                x = _ffn(x, g_ffn, w["moe_w_gate"], w["moe_w_up"], w["moe_w_down"], j, tm, 1408,
                         router=w["router"][j], final_g=final_g)
        s_out.append(s_new)
        b_out.append(b_new)
    return x, (jnp.stack(s_out) if seq > 1 else s_out[-1]), jnp.stack(b_out)


def kernel(x_prompt, x_sample, state_hgrn, state_pool, lb_logits, norm_mix, w_in, w_o, hg_norm, pool_w,
           pool_scale, norm_ffn, ffn_w_gate, ffn_w_up, ffn_w_down, router, moe_w_gate, moe_w_up, moe_w_down,
           norm_final):
    batch, seq, _ = x_prompt.shape
    dec_batch, dec_seq, _ = x_sample.shape
    assert dec_seq == 1
    past_len = 16384
    w = dict(
        lb_logits=lb_logits,
        norm_mix=norm_mix.reshape(DEPTH, 1, D_MODEL),
        w_in=w_in,
        w_o=w_o,
        hg_norm=hg_norm.reshape(DEPTH, 1, HG_WIDTH),
        pool_w=pool_w.astype(BF16),
        pool_scale=pool_scale.reshape(DEPTH, 1, POOL_WIDTH),
        norm_ffn=norm_ffn.reshape(DEPTH, 1, D_MODEL),
        ffn_w_gate=ffn_w_gate.astype(BF16)[:, None],
        ffn_w_up=ffn_w_up.astype(BF16)[:, None],
        ffn_w_down=ffn_w_down.astype(BF16)[:, None],
        router=jnp.pad(router, ((0, 0), (0, 0), (0, LANES - N_EXPERTS))),
        router_t=jnp.swapaxes(router, 1, 2),
        moe_w_gate=moe_w_gate.astype(BF16),
        moe_w_up=moe_w_up.astype(BF16),
        moe_w_down=moe_w_down.astype(BF16),
        norm_final=norm_final.reshape(1, D_MODEL),
    )
    yp, sp, bp = _trunk(x_prompt.reshape(batch * seq, D_MODEL), None, None, 0, w, seq, 512)
    ys, ss, bs = _trunk(x_sample.reshape(dec_batch, D_MODEL), state_hgrn, state_pool, past_len, w, 1, 128)
    return (yp.reshape(batch, seq, D_MODEL), ys.reshape(dec_batch, 1, D_MODEL), sp, ss, bp, bs)
```

```python
import functools

import jax
import jax.numpy as jnp
from jax import lax
from jax.experimental import pallas as pl
from jax.experimental.pallas import tpu as pltpu

F32 = jnp.float32
BF16 = jnp.bfloat16

D_MODEL = 1024
DEPTH = 4
HG_WIDTH = 512
HG_HEADS = 4
HG_D = 128
POOL_WIDTH = 512
POOL_WINDOWS = (2, 4, 8, 16)
POOL_GROUP_W = 128
POOL_BUF = 15
IN_WIDTH = 4 * HG_WIDTH + POOL_WIDTH
ACT_WIDTH = 4 * HG_WIDTH + POOL_WIDTH
ACT_POOL_PART = 4
N_EXPERTS = 8
EPS = 1e-6
LANES = 128
SUBLANES = 8
VMEM_LIMIT = 56 * 1024 * 1024
HGRN_CHUNK = 128


def _params(*sem):
    return pltpu.CompilerParams(dimension_semantics=sem, vmem_limit_bytes=VMEM_LIMIT)


def _rmsnorm(x, g):
    return x * lax.rsqrt(jnp.mean(x * x, axis=-1, keepdims=True) + EPS) * g


def _silu(x):
    return x * jax.nn.sigmoid(x)


def _mixer_in_kernel(layer, x_ref, g_ref, w_ref, lbl_ref, o_ref, wb_ref):
    @pl.when(pl.program_id(0) == 0)
    def _():
        wb_ref[...] = w_ref[0].astype(BF16)

    h = _rmsnorm(x_ref[...], g_ref[...])
    p = jnp.dot(h.astype(BF16), wb_ref[...], preferred_element_type=F32)
    lg = lbl_ref[...]
    e = jnp.exp(lg - jnp.max(lg, axis=0, keepdims=True))
    pr = e / jnp.sum(e, axis=0, keepdims=True)
    cum = pr[0:1]
    for j in range(1, layer + 1):
        cum = cum + pr[j:j + 1]
    lb = cum - pr[0:1]
    w = HG_WIDTH
    q, fx, ix, g, u = p[:, :w], p[:, w:2 * w], p[:, 2 * w:3 * w], p[:, 3 * w:4 * w], p[:, 4 * w:]
    f = lb + (1.0 - lb) * jax.nn.sigmoid(fx)
    o_ref[:, 0:w] = _silu(q)
    o_ref[:, w:2 * w] = jnp.log(f)
    o_ref[:, 2 * w:3 * w] = ix
    o_ref[:, 3 * w:4 * w] = _silu(g)
    o_ref[:, 4 * w:] = u


def _mixer_in(x, g, w_in, lb_logits, layer, tm):
    t = x.shape[0]
    return pl.pallas_call(
        functools.partial(_mixer_in_kernel, layer),
        grid=(t // tm,),
        in_specs=[
            pl.BlockSpec((tm, D_MODEL), lambda i: (i, 0)),
            pl.BlockSpec((1, D_MODEL), lambda i: (0, 0)),
            pl.BlockSpec((1, D_MODEL, IN_WIDTH), lambda i: (layer, 0, 0)),
            pl.BlockSpec((DEPTH, HG_WIDTH), lambda i: (0, 0)),
        ],
        out_specs=pl.BlockSpec((tm, ACT_WIDTH), lambda i: (i, 0)),
        out_shape=jax.ShapeDtypeStruct((t, ACT_WIDTH), F32),
        scratch_shapes=[pltpu.VMEM((D_MODEL, IN_WIDTH), BF16)],
        compiler_params=_params("arbitrary"),
        name="mixer_in",
    )(x, g, w_in, lb_logits)


SMALL_LEVELS = (2, 4, 8)


def _level_sum_matrix(n, ti, si):
    half = n // 2
    mid = (ti & (-n)) + (half - 1)
    lo = jnp.where((ti & half) != 0, mid, ti)
    hi = jnp.where((ti & half) != 0, ti, mid)
    return (si > lo) & (si <= hi)


def _level_factors(q, k, b, n, row, arg=None):
    c = b.shape[0]
    half = n // 2
    if arg is None:
        zero = jnp.zeros((half, b.shape[1]), F32)
        qs, ks = [], []
        for r0 in range(0, c, n):
            mid = b[r0 + half - 1:r0 + half, :]
            ks += [k[r0:r0 + half] * jnp.exp(mid - b[r0:r0 + half]), zero]
            qs += [q[r0 + half:r0 + n] * jnp.exp(b[r0 + half:r0 + n] - mid)]
        return jnp.concatenate(qs, axis=0), jnp.concatenate(ks, axis=0)
    right = (row & half) != 0
    qk = jnp.where(right, q, k) * jnp.exp(arg)
    return jnp.where(right, qk, 0.0), jnp.where(right, 0.0, qk)


def _split3(x):
    hi = x.astype(BF16)
    r1 = x - hi.astype(F32)
    mid = r1.astype(BF16)
    lo = (r1 - mid.astype(F32)).astype(BF16)
    return jnp.concatenate([hi, mid, lo], axis=0)


HEAD_PAIRS = HG_HEADS // 2
PAIR_W = 2 * HG_D


def _pair_blocks(a):
    zero = jnp.zeros((a.shape[0], HG_D), a.dtype)
    return jnp.concatenate([jnp.concatenate([a[:, :HG_D], zero], axis=1),
                            jnp.concatenate([zero, a[:, HG_D:]], axis=1)], axis=0)


def _per_head(fn, a):
    return jnp.concatenate([fn(a[:, :HG_D]), fn(a[:, HG_D:])], axis=1)


def _hgrn_prompt_kernel(tl, q_ref, lf_ref, v_ref, g_ref, nrm_ref, o_ref, s_ref, st_ref):
    c = HGRN_CHUNK
    li = pl.program_id(1)

    @pl.when(li == 0)
    def _():
        st_ref[...] = jnp.zeros_like(st_ref)

    row = lax.broadcasted_iota(jnp.int32, (c, PAIR_W), 0)
    ti = lax.broadcasted_iota(jnp.int32, (c, c), 0)
    si = lax.broadcasted_iota(jnp.int32, (c, c), 1)
    ones = lambda m: jnp.where(m, 1.0, 0.0).astype(BF16)
    cum3 = jnp.concatenate([ones(si <= ti)] * 3, axis=1)
    lvl2 = jnp.concatenate([jnp.concatenate([jnp.where(_level_sum_matrix(n, ti, si), 1.0, 0.0)
                                             for n in SMALL_LEVELS], axis=0).astype(BF16)] * 2, axis=1)
    apart = jnp.concatenate([ti ^ si] * 2, axis=1)
    pi = lax.broadcasted_iota(jnp.int32, (PAIR_W, PAIR_W), 0)
    pj = lax.broadcasted_iota(jnp.int32, (PAIR_W, PAIR_W), 1)
    own = (pi < HG_D) == (pj < HG_D)
    nt = (((1,), (1,)), ((), ()))
    tn = (((0,), (0,)), ((), ()))

    def chunk(pairs, ci, carry):
        rows = pl.ds(pl.multiple_of(ci * c, c), c)
        for p in pairs:
            cols = slice(p * PAIR_W, (p + 1) * PAIR_W)
            q, lf, v = q_ref[rows, cols], lf_ref[rows, cols], v_ref[rows, cols]
            k = 1.0 - jnp.exp(lf)
            lf3 = _split3(lf)
            b = jnp.dot(cum3, lf3, preferred_element_type=F32)
            s1 = jnp.dot(lvl2, lf3[:2 * c], preferred_element_type=F32)
            args = {n: s1[i * c:(i + 1) * c] for i, n in enumerate(SMALL_LEVELS)}
            vbd = _pair_blocks(v.astype(BF16))
            scores = None
            n = c
            while n >= 2:
                qt, kt = _level_factors(q, k, b, n, row, args.get(n))
                s_n = lax.dot_general(qt.astype(BF16), _pair_blocks(kt.astype(BF16)), nt,
                                      preferred_element_type=F32)
                if n in args:
                    scores = jnp.where(apart < n, s_n, scores)
                else:
                    half = n // 2
                    parts = []
                    for j, r0 in enumerate(range(0, c, n)):
                        new = s_n[j * half:(j + 1) * half]
                        if scores is None:
                            parts += [jnp.zeros_like(new), new]
                        else:
                            keep = scores[r0 + half:r0 + n]
                            parts += [scores[r0:r0 + half], jnp.where(apart[r0 + half:r0 + n] < n, new, keep)]
                    scores = jnp.concatenate(parts, axis=0)
                n //= 2
            st = st_ref[p]
            qs = (q * jnp.exp(b)).astype(BF16)
            o = jnp.dot(scores.astype(BF16), vbd, preferred_element_type=F32)
            o = o + lax.dot_general(qs, st.astype(BF16), nt, preferred_element_type=F32)
            o = o + _per_head(lambda a: jnp.broadcast_to(jnp.sum(a, axis=-1, keepdims=True), a.shape), q * k) * v
            bc = b[c - 1:c, :]
            kd = (k * jnp.exp(bc - b)).astype(BF16)
            upd = lax.dot_general(v.astype(BF16), kd, tn, preferred_element_type=F32)
            st_ref[p] = st * jnp.exp(bc) + jnp.where(own, upd, 0.0)
            ms = _per_head(lambda a: jnp.broadcast_to(jnp.mean(a, axis=-1, keepdims=True), a.shape), o * o)
            o = o * lax.rsqrt(ms + EPS) * nrm_ref[:, cols]
            o_ref[rows, cols] = (o * g_ref[rows, cols]).astype(o_ref.dtype)
        return carry

    lax.fori_loop(0, tl // c, functools.partial(chunk, range(HEAD_PAIRS)), 0, unroll=4)

    @pl.when(li == pl.num_programs(1) - 1)
    def _():
        for p in range(HEAD_PAIRS):
            st = st_ref[p]
            s_ref[0, 2 * p] = st[:HG_D, :HG_D].T
            s_ref[0, 2 * p + 1] = st[HG_D:, HG_D:].T


def _hgrn_prompt(act, hg_norm, batch, seq, tl=2048):
    nl = seq // tl
    blk = lambda part: pl.BlockSpec((tl, HG_WIDTH), lambda b, i: (b * nl + i, part))
    return pl.pallas_call(
        functools.partial(_hgrn_prompt_kernel, tl),
        grid=(batch, nl),
        in_specs=[blk(0), blk(1), blk(2), blk(3),
                  pl.BlockSpec((1, HG_WIDTH), lambda b, i: (0, 0))],
        out_specs=[pl.BlockSpec((tl, HG_WIDTH), lambda b, i: (b * nl + i, 0)),
                   pl.BlockSpec((1, HG_HEADS, HG_D, HG_D), lambda b, i: (b, 0, 0, 0))],
        out_shape=[jax.ShapeDtypeStruct((batch * seq, HG_WIDTH), BF16),
                   jax.ShapeDtypeStruct((batch, HG_HEADS, HG_D, HG_D), F32)],
        scratch_shapes=[pltpu.VMEM((HEAD_PAIRS, PAIR_W, PAIR_W), F32)],
        compiler_params=_params("parallel", "arbitrary"),
        name="hgrn_prompt",
    )(act, act, act, act, hg_norm)


HGRN_STEP_BT = 32


def _hgrn_step_kernel(q_ref, lf_ref, v_ref, g_ref, nrm_ref, s_ref, _, o_ref, so_ref):
    bt = HGRN_STEP_BT
    q = q_ref[...]
    v = v_ref[...]
    f = jnp.exp(lf_ref[...])
    ri = lax.broadcasted_iota(jnp.int32, (3 * bt, PAIR_W), 0) % bt
    half = lax.broadcasted_iota(jnp.int32, (3 * bt, PAIR_W), 1) // HG_D
    tn = (((0,), (0,)), ((), ()))
    pieces = [_split3(a) for a in (f, 1.0 - f, q)]
    rows = []
    for s in range(0, bt, 2):
        pick = jnp.where(ri == s + half, 1.0, 0.0).astype(BF16)
        fc, kc, qc = [lax.dot_general(p, pick, tn, preferred_element_type=F32) for p in pieces]
        for d in range(2):
            cols = slice(d * HG_D, (d + 1) * HG_D)
            sn = fc[:, cols] * s_ref[0, s + d, 0] + kc[:, cols] * v[s + d:s + d + 1, :]
            so_ref[0, s + d, 0] = sn
            rows.append(jnp.sum(sn * qc[:, cols], axis=0, keepdims=True))
    o = jnp.concatenate(rows, axis=0)
    o = o * lax.rsqrt(jnp.mean(o * o, axis=-1, keepdims=True) + EPS) * nrm_ref[...]
    o_ref[...] = (o * g_ref[...]).astype(o_ref.dtype)


def _hgrn_step(act, hg_norm, states, layer, new_states):
    batch = act.shape[0]
    bt = HGRN_STEP_BT
    blk = lambda part: pl.BlockSpec((bt, HG_D), lambda i, h: (i, part * HG_HEADS + h))
    sblk = pl.BlockSpec((1, bt, 1, HG_D, HG_D), lambda i, h: (layer, i, h, 0, 0))
    return pl.pallas_call(
        _hgrn_step_kernel,
        grid=(batch // bt, HG_HEADS),
        in_specs=[blk(0), blk(1), blk(2), blk(3),
                  pl.BlockSpec((1, HG_D), lambda i, h: (0, h)), sblk,
                  pl.BlockSpec(memory_space=pl.ANY)],
        out_specs=[pl.BlockSpec((bt, HG_D), lambda i, h: (i, h)), sblk],
        out_shape=[jax.ShapeDtypeStruct((batch, HG_WIDTH), BF16),
                   jax.ShapeDtypeStruct(new_states.shape, F32)],
        input_output_aliases={6: 1},
        compiler_params=_params("parallel", "parallel"),
        name="hgrn_step",
    )(act, act, act, act, hg_norm, states, new_states)


def _pool_map(d_groups, wp_ref, scale_ref, o_ref):
    for gi, d in enumerate(d_groups):
        sl = slice(gi * POOL_GROUP_W, (gi + 1) * POOL_GROUP_W)
        y = jnp.dot(d.astype(BF16), wp_ref[gi], preferred_element_type=F32)
        o_ref[:, sl] = (y * scale_ref[:, sl]).astype(o_ref.dtype)


POOL_HIST = 16


def _pool_tile(i, tl, z, carry_ref, wp_ref, scale_ref, o_ref, nb_ref):
    @pl.when(i == 0)
    def _():
        carry_ref[...] = jnp.zeros_like(carry_ref)

    ext = jnp.concatenate([carry_ref[...], z], axis=0)
    sums = {1: ext}
    w = 1
    while w < max(POOL_WINDOWS):
        sums[2 * w] = sums[w] + pltpu.roll(sums[w], w, 0)
        w *= 2
    pos = i * tl + lax.broadcasted_iota(jnp.int32, (tl, POOL_GROUP_W), 0)
    ds = []
    for gi, w in enumerate(POOL_WINDOWS):
        sl = slice(gi * POOL_GROUP_W, (gi + 1) * POOL_GROUP_W)
        cnt = jnp.minimum(pos + 1, w).astype(F32)
        ds.append(sums[w][POOL_HIST:, sl] / cnt - z[:, sl])
    _pool_map(ds, wp_ref, scale_ref, o_ref)
    carry_ref[...] = z[tl - POOL_HIST:, :]

    @pl.when(i == pl.num_programs(1) - 1)
    def _():
        nb_ref[0] = z[tl - POOL_BUF:, :]


POOL_STEP_BT = 16


def _pool_step_kernel(start_pos, u_ref, buf_ref, wp_ref, scale_ref, o_ref, nb_ref):
    u = u_ref[...]
    buf = buf_ref[0]
    ds = []
    for gi, w in enumerate(POOL_WINDOWS):
        sl = slice(gi * POOL_GROUP_W, (gi + 1) * POOL_GROUP_W)
        tot = u[:, sl] + jnp.sum(buf[:, POOL_BUF - (w - 1):, sl], axis=1)
        ds.append(tot / float(min(start_pos + 1, w)) - u[:, sl])
    _pool_map(ds, wp_ref, scale_ref, o_ref)
    nb_ref[:, 0:POOL_BUF - 1, :] = buf[:, 1:, :]
    nb_ref[:, POOL_BUF - 1:, :] = u[:, None, :]


def _pool_step(act, bufs, layer, pool_w, pool_scale, start_pos):
    batch = act.shape[0]
    bt = POOL_STEP_BT
    return pl.pallas_call(
        functools.partial(_pool_step_kernel, start_pos),
        grid=(batch // bt,),
        in_specs=[pl.BlockSpec((bt, POOL_WIDTH), lambda i: (i, ACT_POOL_PART)),
                  pl.BlockSpec((1, bt, POOL_BUF, POOL_WIDTH), lambda i: (layer, i, 0, 0)),
                  pl.BlockSpec((len(POOL_WINDOWS), POOL_GROUP_W, POOL_GROUP_W), lambda i: (0, 0, 0)),
                  pl.BlockSpec((1, POOL_WIDTH), lambda i: (0, 0))],
        out_specs=[pl.BlockSpec((bt, POOL_WIDTH), lambda i: (i, 0)),
                   pl.BlockSpec((bt, POOL_BUF, POOL_WIDTH), lambda i: (i, 0, 0))],
        out_shape=[jax.ShapeDtypeStruct((batch, POOL_WIDTH), BF16),
                   jax.ShapeDtypeStruct(bufs.shape[1:], F32)],
        compiler_params=_params("parallel"),
        name="pool_step",
    )(act, bufs, pool_w, pool_scale)


def _mixer_out_kernel(x_ref, o_ref, p_ref, w_ref, y_ref, wb_ref):
    @pl.when(pl.program_id(0) == 0)
    def _():
        wb_ref[...] = w_ref[0].astype(BF16)

    y = jnp.dot(o_ref[...], wb_ref[0:HG_WIDTH, :], preferred_element_type=F32)
    y = y + jnp.dot(p_ref[...], wb_ref[HG_WIDTH:, :], preferred_element_type=F32)
    y_ref[...] = x_ref[...] + y


def _mixer_out(x, og, yp, w_o, layer, tm):
    t = x.shape[0]
    return pl.pallas_call(
        _mixer_out_kernel,
        grid=(t // tm,),
        in_specs=[pl.BlockSpec((tm, D_MODEL), lambda i: (i, 0)),
                  pl.BlockSpec((tm, HG_WIDTH), lambda i: (i, 0)),
                  pl.BlockSpec((tm, POOL_WIDTH), lambda i: (i, 0)),
                  pl.BlockSpec((1, D_MODEL, D_MODEL), lambda i: (layer, 0, 0))],
        out_specs=pl.BlockSpec((tm, D_MODEL), lambda i: (i, 0)),
        out_shape=jax.ShapeDtypeStruct((t, D_MODEL), F32),
        scratch_shapes=[pltpu.VMEM((D_MODEL, D_MODEL), BF16)],
        compiler_params=_params("arbitrary"),
        name="mixer_out",
    )(x, og, yp, w_o)


def _top2_combine(logits):
    lane = lax.broadcasted_iota(jnp.int32, logits.shape, 1).astype(F32)
    neg = jnp.float32(-jnp.inf)
    lg = jnp.where(lane < N_EXPERTS, logits, neg)
    m1 = jnp.max(lg, axis=-1, keepdims=True)
    i1 = jnp.min(jnp.where(lg == m1, lane, float(LANES)), axis=-1, keepdims=True)
    lg2 = jnp.where(lane == i1, neg, lg)
    m2 = jnp.max(lg2, axis=-1, keepdims=True)
    i2 = jnp.min(jnp.where(lg2 == m2, lane, float(LANES)), axis=-1, keepdims=True)
    e2 = jnp.exp(m2 - m1)
    g1 = 1.0 / (1.0 + e2)
    g2 = e2 / (1.0 + e2)
    return jnp.where(lane == i1, g1, 0.0) + jnp.where(lane == i2, g2, 0.0)


def _ffn_kernel(moe, final, x_ref, g_ref, *rest):
    if moe:
        r_ref, rest = rest[0], rest[1:]
    if final:
        gf_ref, rest = rest[0], rest[1:]
    wg_ref, wu_ref, wd_ref, y_ref, h_ref, acc_ref = rest[:6]
    e, j = pl.program_id(1), pl.program_id(2)
    first = (e == 0) & (j == 0)
    last = (e == pl.num_programs(1) - 1) & (j == pl.num_programs(2) - 1)

    @pl.when(first)
    def _():
        h = _rmsnorm(x_ref[...], g_ref[...])
        h_ref[...] = h.astype(BF16)
        acc_ref[...] = jnp.zeros_like(acc_ref)
        if moe:
            logits = jnp.dot(h, r_ref[...], precision=lax.Precision.HIGHEST, preferred_element_type=F32)
            rest[6][...] = _top2_combine(logits)

    h = h_ref[...]
    gate = jnp.dot(h, wg_ref[0, 0], preferred_element_type=F32)
    up = jnp.dot(h, wu_ref[0, 0], preferred_element_type=F32)
    y = jnp.dot((_silu(gate) * up).astype(BF16), wd_ref[0, 0], preferred_element_type=F32)
    if moe:
        comb = rest[6][...]
        lane = lax.broadcasted_iota(jnp.int32, comb.shape, 1)
        y = jnp.sum(jnp.where(lane == e, comb, 0.0), axis=-1, keepdims=True) * y
    acc_ref[...] += y

    @pl.when(last)
    def _():
        out = x_ref[...] + acc_ref[...]
        if final:
            out = _rmsnorm(out, gf_ref[...])
        y_ref[...] = out


def _ffn(x, g, wg, wu, wd, layer, tm, tf, router=None, final_g=None):
    t = x.shape[0]
    _, ne, _, f = wg.shape
    moe, final = router is not None, final_g is not None
    vec = pl.BlockSpec((1, D_MODEL), lambda i, e, j: (0, 0))
    in_specs = [pl.BlockSpec((tm, D_MODEL), lambda i, e, j: (i, 0)), vec]
    args = [x, g]
    scratch = [pltpu.VMEM((tm, D_MODEL), BF16), pltpu.VMEM((tm, D_MODEL), F32)]
    if moe:
        in_specs.append(pl.BlockSpec((D_MODEL, LANES), lambda i, e, j: (0, 0)))
        args.append(router)
        scratch.append(pltpu.VMEM((tm, LANES), F32))
    if final:
        in_specs.append(vec)
        args.append(final_g)
    in_specs += [pl.BlockSpec((1, 1, D_MODEL, tf), lambda i, e, j: (layer, e, 0, j)),
                 pl.BlockSpec((1, 1, D_MODEL, tf), lambda i, e, j: (layer, e, 0, j)),
                 pl.BlockSpec((1, 1, tf, D_MODEL), lambda i, e, j: (layer, e, j, 0))]
    args += [wg, wu, wd]
    return pl.pallas_call(
        functools.partial(_ffn_kernel, moe, final),
        grid=(t // tm, ne, f // tf),
        in_specs=in_specs,
        out_specs=pl.BlockSpec((tm, D_MODEL), lambda i, e, j: (i, 0)),
        out_shape=jax.ShapeDtypeStruct((t, D_MODEL), F32),
        scratch_shapes=scratch,
        compiler_params=_params("parallel", "arbitrary", "arbitrary"),
        name="ffn_moe" if moe else "ffn_dense",
    )(*args)


def _mixer_tile(b, i, tl, x_ref, o_ref, u_ref, w_ref, wp_ref, scale_ref, nb_ref, wb_ref, carry_ref, yp_ref):
    @pl.when((b == 0) & (i == 0))
    def _():
        wb_ref[...] = w_ref[0].astype(BF16)

    _pool_tile(i, tl, u_ref[...], carry_ref, wp_ref, scale_ref, yp_ref, nb_ref)
    y = jnp.dot(o_ref[...], wb_ref[0:HG_WIDTH, :], preferred_element_type=F32)
    y = y + jnp.dot(yp_ref[...], wb_ref[HG_WIDTH:, :], preferred_element_type=F32)
    return x_ref[...] + y


def _mixer_ffn_kernel(tl, x_ref, o_ref, u_ref, w_ref, wp_ref, scale_ref, g_ref, wg_ref, wu_ref, wd_ref,
                      y_ref, nb_ref, wb_ref, carry_ref, yp_ref, x1_ref, h_ref, acc_ref):
    b, i, j = pl.program_id(0), pl.program_id(1), pl.program_id(2)

    @pl.when(j == 0)
    def _():
        x1 = _mixer_tile(b, i, tl, x_ref, o_ref, u_ref, w_ref, wp_ref, scale_ref, nb_ref, wb_ref, carry_ref, yp_ref)
        x1_ref[...] = x1
        h_ref[...] = _rmsnorm(x1, g_ref[...]).astype(BF16)
        acc_ref[...] = jnp.zeros_like(acc_ref)

    h = h_ref[...]
    gate = jnp.dot(h, wg_ref[0, 0], preferred_element_type=F32)
    up = jnp.dot(h, wu_ref[0, 0], preferred_element_type=F32)
    acc_ref[...] += jnp.dot((_silu(gate) * up).astype(BF16), wd_ref[0, 0], preferred_element_type=F32)

    @pl.when(j == pl.num_programs(2) - 1)
    def _():
        y_ref[...] = x1_ref[...] + acc_ref[...]


def _mixer_specs(nl, tl, layer, tok, fixed):
    ins = [pl.BlockSpec((tl, D_MODEL), tok(0)),
           pl.BlockSpec((tl, HG_WIDTH), tok(0)),
           pl.BlockSpec((tl, POOL_WIDTH), tok(ACT_POOL_PART)),
           pl.BlockSpec((1, D_MODEL, D_MODEL), fixed(layer, 0, 0)),
           pl.BlockSpec((len(POOL_WINDOWS), POOL_GROUP_W, POOL_GROUP_W), fixed(0, 0, 0)),
           pl.BlockSpec((1, POOL_WIDTH), fixed(0, 0))]
    scratch = [pltpu.VMEM((D_MODEL, D_MODEL), BF16), pltpu.VMEM((POOL_HIST, POOL_WIDTH), F32),
               pltpu.VMEM((tl, POOL_WIDTH), BF16)]
    return ins, scratch


def _mixer_ffn(x, og, act, w_o, layer, pool_w, pool_scale, g, wg, wu, wd, wset, batch, seq, tl=512, tf=1408):
    nl = seq // tl
    f = wg.shape[3]
    tok = lambda cols: (lambda b, i, j: (b * nl + i, cols))
    fixed = lambda *idx: (lambda b, i, j: idx)
    ins, scratch = _mixer_specs(nl, tl, layer, tok, fixed)
    ins += [pl.BlockSpec((1, D_MODEL), fixed(0, 0)),
            pl.BlockSpec((1, 1, D_MODEL, tf), lambda b, i, j: (wset, 0, 0, j)),
            pl.BlockSpec((1, 1, D_MODEL, tf), lambda b, i, j: (wset, 0, 0, j)),
            pl.BlockSpec((1, 1, tf, D_MODEL), lambda b, i, j: (wset, 0, j, 0))]
    scratch += [pltpu.VMEM((tl, D_MODEL), F32), pltpu.VMEM((tl, D_MODEL), BF16), pltpu.VMEM((tl, D_MODEL), F32)]
    return pl.pallas_call(
        functools.partial(_mixer_ffn_kernel, tl),
        grid=(batch, nl, f // tf),
        in_specs=ins,
        out_specs=[pl.BlockSpec((tl, D_MODEL), tok(0)),
                   pl.BlockSpec((1, POOL_BUF, POOL_WIDTH), lambda b, i, j: (b, 0, 0))],
        out_shape=[jax.ShapeDtypeStruct((batch * seq, D_MODEL), F32),
                   jax.ShapeDtypeStruct((batch, POOL_BUF, POOL_WIDTH), F32)],
        scratch_shapes=scratch,
        compiler_params=_params("arbitrary", "arbitrary", "arbitrary"),
        name="mixer_ffn",
    )(x, og, act, w_o, pool_w, pool_scale, g, wg, wu, wd)


MOE_TILE = 512
ROUTE_ROWS = 8
NT_DIMS = (((1,), (1,)), ((), ()))


def _mixer_route_kernel(tm, x_ref, o_ref, u_ref, w_ref, wp_ref, scale_ref, g_ref, rt_ref,
                        x1_ref, nb_ref, route_ref, gate_ref, cnt_ref,
                        wb_ref, pool_carry_ref, yp_ref, carry_ref, earlier_ref):
    b, i = pl.program_id(0), pl.program_id(1)

    @pl.when((b == 0) & (i == 0))
    def _():
        carry_ref[...] = jnp.zeros_like(carry_ref)
        t0 = lax.broadcasted_iota(jnp.int32, (tm, tm), 0)
        t1 = lax.broadcasted_iota(jnp.int32, (tm, tm), 1)
        earlier_ref[...] = jnp.where(t0 < t1, 1.0, 0.0).astype(BF16)

    x1 = _mixer_tile(b, i, tm, x_ref, o_ref, u_ref, w_ref, wp_ref, scale_ref, nb_ref, wb_ref, pool_carry_ref, yp_ref)
    x1_ref[...] = x1
    h = _rmsnorm(x1, g_ref[...])
    lt = lax.dot_general(rt_ref[...], h, NT_DIMS, precision=lax.Precision.HIGHEST,
                         preferred_element_type=F32)
    ex = lax.broadcasted_iota(jnp.int32, lt.shape, 0).astype(F32)
    neg = jnp.float32(-jnp.inf)
    m1 = jnp.max(lt, axis=0, keepdims=True)
    i1 = jnp.min(jnp.where(lt == m1, ex, float(N_EXPERTS)), axis=0, keepdims=True)
    l2 = jnp.where(ex == i1, neg, lt)
    m2 = jnp.max(l2, axis=0, keepdims=True)
    i2 = jnp.min(jnp.where(l2 == m2, ex, float(N_EXPERTS)), axis=0, keepdims=True)
    e2 = jnp.exp(m2 - m1)
    g1 = 1.0 / (1.0 + e2)
    g2 = e2 / (1.0 + e2)
    sel1, sel2 = ex == i1, ex == i2
    member = jnp.where(sel1 | sel2, 1.0, 0.0)
    rank = jnp.dot(member.astype(BF16), earlier_ref[...], preferred_element_type=F32) + carry_ref[:, 0:1]
    route_ref[...] = jnp.zeros_like(route_ref)
    route_ref[0:1, :] = i1.astype(jnp.int32)
    route_ref[1:2, :] = jnp.sum(jnp.where(sel1, rank, 0.0), axis=0, keepdims=True).astype(jnp.int32)
    route_ref[2:3, :] = i2.astype(jnp.int32)
    route_ref[3:4, :] = jnp.sum(jnp.where(sel2, rank, 0.0), axis=0, keepdims=True).astype(jnp.int32)
    carry_ref[...] += jnp.sum(member, axis=1, keepdims=True)
    cnt_ref[...] = carry_ref[...]
    row = lax.broadcasted_iota(jnp.int32, (LANES, tm), 0)
    gate_ref[...] = jnp.where(row == 0, g1, jnp.where(row == 1, g2, 0.0)).T


def _mixer_route(x, og, act, w_o, layer, pool_w, pool_scale, g, router_t, batch, seq, tm=512):
    nl = seq // tm
    t = batch * seq
    tok = lambda cols: (lambda b, i: (b * nl + i, cols))
    fixed = lambda *idx: (lambda b, i: idx)
    ins, scratch = _mixer_specs(nl, tm, layer, tok, fixed)
    ins += [pl.BlockSpec((1, D_MODEL), fixed(0, 0)),
            pl.BlockSpec((N_EXPERTS, D_MODEL), fixed(0, 0))]
    scratch += [pltpu.VMEM((N_EXPERTS, LANES), F32), pltpu.VMEM((tm, tm), BF16)]
    return pl.pallas_call(
        functools.partial(_mixer_route_kernel, tm),
        grid=(batch, nl),
        in_specs=ins,
        out_specs=[pl.BlockSpec((tm, D_MODEL), tok(0)),
                   pl.BlockSpec((1, POOL_BUF, POOL_WIDTH), lambda b, i: (b, 0, 0)),
                   pl.BlockSpec((ROUTE_ROWS, tm), lambda b, i: (0, b * nl + i)),
                   pl.BlockSpec((tm, LANES), tok(0)),
                   pl.BlockSpec((N_EXPERTS, LANES), fixed(0, 0))],
        out_shape=[jax.ShapeDtypeStruct((t, D_MODEL), F32),
                   jax.ShapeDtypeStruct((batch, POOL_BUF, POOL_WIDTH), F32),
                   jax.ShapeDtypeStruct((ROUTE_ROWS, t), jnp.int32),
                   jax.ShapeDtypeStruct((t, LANES), F32),
                   jax.ShapeDtypeStruct((N_EXPERTS, LANES), F32)],
        scratch_shapes=scratch,
        compiler_params=_params("arbitrary", "arbitrary"),
        name="mixer_route",
    )(x, og, act, w_o, pool_w, pool_scale, g, router_t)


def _moe_slots_kernel(base_ref, route_ref, slot_ref):
    r = route_ref[...]
    slot_ref[...] = jnp.zeros_like(slot_ref)
    for k in range(2):
        e, rank = r[2 * k:2 * k + 1, :], r[2 * k + 1:2 * k + 2, :]
        start = jnp.zeros_like(e)
        for j in range(N_EXPERTS):
            start = jnp.where(e == j, base_ref[j], start)
        slot_ref[k:k + 1, :] = start + rank


def _moe_slots(route, base, tm=2048):
    t = route.shape[1]
    return pl.pallas_call(
        _moe_slots_kernel,
        grid_spec=pltpu.PrefetchScalarGridSpec(
            num_scalar_prefetch=1,
            grid=(t // tm,),
            in_specs=[pl.BlockSpec((ROUTE_ROWS, tm), lambda i, *_: (0, i))],
            out_specs=pl.BlockSpec((ROUTE_ROWS, tm), lambda i, *_: (0, i))),
        out_shape=jax.ShapeDtypeStruct((ROUTE_ROWS, t), jnp.int32),
        compiler_params=_params("parallel"),
        name="moe_slots",
    )(base, route)


COMBINE_ROWS = 128


def _rows_done(hbm, rows, sem):
    pltpu.make_async_copy(hbm.at[pl.ds(0, rows)], hbm.at[pl.ds(0, rows)], sem).wait()


def _moe_dispatch_kernel(tm, fs_ref, fe_ref, s0_ref, s1_ref, x_ref, xs_hbm, sem):
    i = pl.program_id(0)

    def row_copy(group, sub, dst_row):
        return pltpu.make_async_copy(x_ref.at[group, pl.ds(sub, 1), :], xs_hbm.at[pl.ds(dst_row, 1)], sem)

    def issue(c, carry):
        for u in range(SUBLANES):
            for k, s_ref in enumerate((s0_ref, s1_ref)):
                row_copy(c, u, s_ref[c * SUBLANES + u]).start(priority=k)
        return carry

    lax.fori_loop(0, tm // SUBLANES, issue, 0)
    _rows_done(xs_hbm, 2 * tm, sem)

    @pl.when(i == pl.num_programs(0) - 1)
    def _():
        for e in range(N_EXPERTS + 1):
            def fill(p, c):
                row_copy(0, 0, p).start()
                return c

            def drain(p, c):
                row_copy(0, 0, p).wait()
                return c

            lax.fori_loop(fs_ref[e], fe_ref[e], fill, 0)
            lax.fori_loop(fs_ref[e], fe_ref[e], drain, 0)


def _moe_dispatch(rows, slots, fill_start, fill_end, n_slots, tm=2048):
    t, width = rows.shape
    slot_spec = pl.BlockSpec((tm,), lambda i, *_: (i,), memory_space=pltpu.SMEM)
    return pl.pallas_call(
        functools.partial(_moe_dispatch_kernel, tm),
        grid_spec=pltpu.PrefetchScalarGridSpec(
            num_scalar_prefetch=2,
            grid=(t // tm,),
            in_specs=[slot_spec, slot_spec,
                      pl.BlockSpec((tm // SUBLANES, SUBLANES, width), lambda i, *_: (i, 0, 0))],
            out_specs=pl.BlockSpec(memory_space=pl.ANY),
            scratch_shapes=[pltpu.SemaphoreType.DMA(())]),
        out_shape=jax.ShapeDtypeStruct((n_slots, width), rows.dtype),
        compiler_params=_params("arbitrary"),
        name="moe_dispatch",
    )(fill_start, fill_end, slots[0], slots[1], rows.reshape(t // SUBLANES, SUBLANES, width))


def _ffn_grouped_kernel(te_ref, nv_ref, x_ref, g_ref, wg_ref, wu_ref, wd_ref, y_ref):
    i = pl.program_id(0)

    @pl.when(i < nv_ref[0])
    def _():
        h = _rmsnorm(x_ref[...], g_ref[...]).astype(BF16)
        gate = jnp.dot(h, wg_ref[0, 0], preferred_element_type=F32)
        up = jnp.dot(h, wu_ref[0, 0], preferred_element_type=F32)
        y_ref[...] = jnp.dot((_silu(gate) * up).astype(BF16), wd_ref[0, 0], preferred_element_type=F32)

    @pl.when(i >= nv_ref[0])
    def _():
        y_ref[...] = jnp.zeros_like(y_ref)


def _ffn_grouped(xs, g, wg, wu, wd, layer, tile_expert, n_valid):
    n_slots = xs.shape[0]
    f = wg.shape[3]
    rows = pl.BlockSpec((MOE_TILE, D_MODEL), lambda i, te, nv: (i, 0))
    return pl.pallas_call(
        _ffn_grouped_kernel,
        grid_spec=pltpu.PrefetchScalarGridSpec(
            num_scalar_prefetch=2,
            grid=(n_slots // MOE_TILE,),
            in_specs=[rows,
                      pl.BlockSpec((1, D_MODEL), lambda i, te, nv: (0, 0)),
                      pl.BlockSpec((1, 1, D_MODEL, f), lambda i, te, nv: (layer, te[i], 0, 0)),
                      pl.BlockSpec((1, 1, D_MODEL, f), lambda i, te, nv: (layer, te[i], 0, 0)),
                      pl.BlockSpec((1, 1, f, D_MODEL), lambda i, te, nv: (layer, te[i], 0, 0))],
            out_specs=rows),
        out_shape=jax.ShapeDtypeStruct((n_slots, D_MODEL), F32),
        compiler_params=_params("arbitrary"),
        name="ffn_grouped",
    )(tile_expert, n_valid, xs, g, wg, wu, wd)


def _moe_combine_kernel(tm, final, s0_ref, s1_ref, x_ref, gate_ref, *rest):
    if final:
        gf_ref, rest = rest[0], rest[1:]
    ys_hbm, o_ref, y1_ref, y2_ref, sems = rest

    def issue(c, carry):
        for u in range(SUBLANES):
            for k, (s_ref, buf) in enumerate(((s0_ref, y1_ref), (s1_ref, y2_ref))):
                pltpu.make_async_copy(ys_hbm.at[pl.ds(s_ref[c * SUBLANES + u], 1)],
                                      buf.at[c, pl.ds(u, 1), :], sems.at[k]).start(priority=k)
        return carry

    lax.fori_loop(0, tm // SUBLANES, issue, 0)
    for k in range(2):
        _rows_done(ys_hbm, tm, sems.at[k])
    rc = COMBINE_ROWS

    def rows(ci, c):
        r = pl.ds(pl.multiple_of(ci * rc, rc), rc)
        rg = pl.ds(pl.multiple_of(ci * (rc // SUBLANES), rc // SUBLANES), rc // SUBLANES)
        g1 = jnp.broadcast_to(gate_ref[r, 0:1], (rc, LANES))
        g2 = jnp.broadcast_to(gate_ref[r, 1:2], (rc, LANES))
        ssq = jnp.zeros((rc, LANES), F32)
        for j in range(D_MODEL // LANES):
            cols = slice(j * LANES, (j + 1) * LANES)
            y1 = y1_ref[rg, :, cols].reshape(rc, LANES)
            y2 = y2_ref[rg, :, cols].reshape(rc, LANES)
            out = x_ref[r, cols] + g1 * y1 + g2 * y2
            o_ref[r, cols] = out
            ssq = ssq + out * out
        if final:
            scale = lax.rsqrt(jnp.sum(ssq, axis=-1, keepdims=True) * (1.0 / D_MODEL) + EPS)
            for j in range(D_MODEL // LANES):
                cols = slice(j * LANES, (j + 1) * LANES)
                o_ref[r, cols] = o_ref[r, cols] * scale * gf_ref[:, cols]
        return c

    lax.fori_loop(0, tm // rc, rows, 0)


def _moe_combine(x, slots, gates, ys, final_g=None, tm=1024):
    t = x.shape[0]
    final = final_g is not None
    slot_spec = pl.BlockSpec((tm,), lambda i: (i,), memory_space=pltpu.SMEM)
    in_specs = [slot_spec, slot_spec,
                pl.BlockSpec((tm, D_MODEL), lambda i: (i, 0)),
                pl.BlockSpec((tm, LANES), lambda i: (i, 0))]
    args = [slots[0], slots[1], x, gates]
    if final:
        in_specs.append(pl.BlockSpec((1, D_MODEL), lambda i: (0, 0)))
        args.append(final_g)
    in_specs.append(pl.BlockSpec(memory_space=pl.ANY))
    args.append(ys)
    return pl.pallas_call(
        functools.partial(_moe_combine_kernel, tm, final),
        grid=(t // tm,),
        in_specs=in_specs,
        out_specs=pl.BlockSpec((tm, D_MODEL), lambda i: (i, 0)),
        out_shape=jax.ShapeDtypeStruct((t, D_MODEL), F32),
        scratch_shapes=[pltpu.VMEM((tm // SUBLANES, SUBLANES, D_MODEL), F32),
                        pltpu.VMEM((tm // SUBLANES, SUBLANES, D_MODEL), F32),
                        pltpu.SemaphoreType.DMA((2,))],
        compiler_params=_params("arbitrary"),
        name="moe_combine",
    )(*args)


def _moe_routed(x, route, gates, counts, g, wg, wu, wd, layer, final_g=None):
    t = x.shape[0]
    n_tiles = 2 * t // MOE_TILE + N_EXPERTS
    cnt = counts[:, 0].astype(jnp.int32)
    caps = (cnt + MOE_TILE - 1) // MOE_TILE
    cum = jnp.cumsum(caps)
    base = (cum - caps) * MOE_TILE
    n_valid = cum[-1:]
    tile_expert = jnp.minimum(
        jnp.sum((cum[None, :] <= jnp.arange(n_tiles, dtype=jnp.int32)[:, None]).astype(jnp.int32), axis=1),
        N_EXPERTS - 1)
    fill_start = jnp.concatenate([base + cnt, n_valid * MOE_TILE])
    fill_end = jnp.concatenate([base + caps * MOE_TILE, jnp.full((1,), n_tiles * MOE_TILE, jnp.int32)])
    slots = _moe_slots(route, base)
    xs = _moe_dispatch(x, slots, fill_start, fill_end, n_tiles * MOE_TILE)
    ys = _ffn_grouped(xs, g, wg, wu, wd, layer, tile_expert, n_valid)
    return _moe_combine(x, slots, gates, ys, final_g)


def _trunk(x, state_hgrn, state_pool, start_pos, w, seq, tm):
    batch = x.shape[0] // seq
    s_out, b_out = [], []
    for l in range(DEPTH):
        act = _mixer_in(x, w["norm_mix"][l], w["w_in"], w["lb_logits"], l, tm)
        j = l // 2
        dense = l % 2 == 0
        final_g = w["norm_final"] if l == DEPTH - 1 else None
        g_ffn = w["norm_ffn"][l]
        if seq > 1:
            og, s_new = _hgrn_prompt(act, w["hg_norm"][l], batch, seq)
            mixer = (x, og, act, w["w_o"], l, w["pool_w"][l], w["pool_scale"][l])
            if dense:
                assert final_g is None
                x, b_new = _mixer_ffn(*mixer, g_ffn, w["ffn_w_gate"], w["ffn_w_up"], w["ffn_w_down"], j,
                                      batch, seq)
            else:
                x, b_new, route, gates, counts = _mixer_route(*mixer, g_ffn, w["router_t"][j], batch, seq)
                x = _moe_routed(x, route, gates, counts, g_ffn, w["moe_w_gate"], w["moe_w_up"],
                                w["moe_w_down"], j, final_g=final_g)
        else:
            new_states = s_out[-1] if s_out else jnp.zeros_like(state_hgrn)
            og, s_new = _hgrn_step(act, w["hg_norm"][l], state_hgrn, l, new_states)
            yp, b_new = _pool_step(act, state_pool, l, w["pool_w"][l], w["pool_scale"][l], start_pos)
            x = _mixer_out(x, og, yp, w["w_o"], l, tm)
            if dense:
                x = _ffn(x, g_ffn, w["ffn_w_gate"], w["ffn_w_up"], w["ffn_w_down"], j, tm, 1408, final_g=final_g)
            else:
                x = _ffn(x, g_ffn, w["moe_w_gate"], w["moe_w_up"], w["moe_w_down"], j, tm, 1408,
                         router=w["router"][j], final_g=final_g)
        s_out.append(s_new)
        b_out.append(b_new)
    return x, (jnp.stack(s_out) if seq > 1 else s_out[-1]), jnp.stack(b_out)


def kernel(x_prompt, x_sample, state_hgrn, state_pool, lb_logits, norm_mix, w_in, w_o, hg_norm, pool_w,
           pool_scale, norm_ffn, ffn_w_gate, ffn_w_up, ffn_w_down, router, moe_w_gate, moe_w_up, moe_w_down,
           norm_final):
    batch, seq, _ = x_prompt.shape
    dec_batch, dec_seq, _ = x_sample.shape
    assert dec_seq == 1
    past_len = 16384
    w = dict(
        lb_logits=lb_logits,
        norm_mix=norm_mix.reshape(DEPTH, 1, D_MODEL),
        w_in=w_in,
        w_o=w_o,
        hg_norm=hg_norm.reshape(DEPTH, 1, HG_WIDTH),
        pool_w=pool_w.astype(BF16),
        pool_scale=pool_scale.reshape(DEPTH, 1, POOL_WIDTH),
        norm_ffn=norm_ffn.reshape(DEPTH, 1, D_MODEL),
        ffn_w_gate=ffn_w_gate.astype(BF16)[:, None],
        ffn_w_up=ffn_w_up.astype(BF16)[:, None],
        ffn_w_down=ffn_w_down.astype(BF16)[:, None],
        router=jnp.pad(router, ((0, 0), (0, 0), (0, LANES - N_EXPERTS))),
        router_t=jnp.swapaxes(router, 1, 2),
        moe_w_gate=moe_w_gate.astype(BF16),
        moe_w_up=moe_w_up.astype(BF16),
        moe_w_down=moe_w_down.astype(BF16),
        norm_final=norm_final.reshape(1, D_MODEL),
    )
    yp, sp, bp = _trunk(x_prompt.reshape(batch * seq, D_MODEL), None, None, 0, w, seq, 512)
    ys, ss, bs = _trunk(x_sample.reshape(dec_batch, D_MODEL), state_hgrn, state_pool, past_len, w, 1, 128)
    return (yp.reshape(batch, seq, D_MODEL), ys.reshape(dec_batch, 1, D_MODEL), sp, ss, bp, bs)
```

```python
import functools

import jax
import jax.numpy as jnp
from jax import lax
from jax.experimental import pallas as pl
from jax.experimental.pallas import tpu as pltpu

F32 = jnp.float32
BF16 = jnp.bfloat16

D_MODEL = 1024
DEPTH = 4
HG_WIDTH = 512
HG_HEADS = 4
HG_D = 128
POOL_WIDTH = 512
POOL_WINDOWS = (2, 4, 8, 16)
POOL_GROUP_W = 128
POOL_BUF = 15
IN_WIDTH = 4 * HG_WIDTH + POOL_WIDTH
ACT_WIDTH = 4 * HG_WIDTH + POOL_WIDTH
ACT_POOL_PART = 4
N_EXPERTS = 8
EPS = 1e-6
LANES = 128
SUBLANES = 8
VMEM_LIMIT = 56 * 1024 * 1024
HGRN_CHUNK = 128


def _params(*sem):
    return pltpu.CompilerParams(dimension_semantics=sem, vmem_limit_bytes=VMEM_LIMIT)


def _rmsnorm(x, g):
    return x * lax.rsqrt(jnp.mean(x * x, axis=-1, keepdims=True) + EPS) * g


def _silu(x):
    return x * jax.nn.sigmoid(x)


def _mixer_in_kernel(layer, x_ref, g_ref, w_ref, lbl_ref, o_ref, wb_ref):
    @pl.when(pl.program_id(0) == 0)
    def _():
        wb_ref[...] = w_ref[0].astype(BF16)

    h = _rmsnorm(x_ref[...], g_ref[...])
    p = jnp.dot(h.astype(BF16), wb_ref[...], preferred_element_type=F32)
    lg = lbl_ref[...]
    e = jnp.exp(lg - jnp.max(lg, axis=0, keepdims=True))
    pr = e / jnp.sum(e, axis=0, keepdims=True)
    cum = pr[0:1]
    for j in range(1, layer + 1):
        cum = cum + pr[j:j + 1]
    lb = cum - pr[0:1]
    w = HG_WIDTH
    q, fx, ix, g, u = p[:, :w], p[:, w:2 * w], p[:, 2 * w:3 * w], p[:, 3 * w:4 * w], p[:, 4 * w:]
    f = lb + (1.0 - lb) * jax.nn.sigmoid(fx)
    o_ref[:, 0:w] = _silu(q)
    o_ref[:, w:2 * w] = jnp.log(f)
    o_ref[:, 2 * w:3 * w] = ix
    o_ref[:, 3 * w:4 * w] = _silu(g)
    o_ref[:, 4 * w:] = u


def _mixer_in(x, g, w_in, lb_logits, layer, tm):
    t = x.shape[0]
    return pl.pallas_call(
        functools.partial(_mixer_in_kernel, layer),
        grid=(t // tm,),
        in_specs=[
            pl.BlockSpec((tm, D_MODEL), lambda i: (i, 0)),
            pl.BlockSpec((1, D_MODEL), lambda i: (0, 0)),
            pl.BlockSpec((1, D_MODEL, IN_WIDTH), lambda i: (layer, 0, 0)),
            pl.BlockSpec((DEPTH, HG_WIDTH), lambda i: (0, 0)),
        ],
        out_specs=pl.BlockSpec((tm, ACT_WIDTH), lambda i: (i, 0)),
        out_shape=jax.ShapeDtypeStruct((t, ACT_WIDTH), F32),
        scratch_shapes=[pltpu.VMEM((D_MODEL, IN_WIDTH), BF16)],
        compiler_params=_params("arbitrary"),
        name="mixer_in",
    )(x, g, w_in, lb_logits)


SMALL_LEVELS = (2, 4, 8)


def _level_sum_matrix(n, ti, si):
    half = n // 2
    mid = (ti & (-n)) + (half - 1)
    lo = jnp.where((ti & half) != 0, mid, ti)
    hi = jnp.where((ti & half) != 0, ti, mid)
    return (si > lo) & (si <= hi)


def _level_factors(q, k, b, n, row, arg=None):
    c = b.shape[0]
    half = n // 2
    if arg is None:
        zero = jnp.zeros((half, b.shape[1]), F32)
        qs, ks = [], []
        for r0 in range(0, c, n):
            mid = b[r0 + half - 1:r0 + half, :]
            ks += [k[r0:r0 + half] * jnp.exp(mid - b[r0:r0 + half]), zero]
            qs += [q[r0 + half:r0 + n] * jnp.exp(b[r0 + half:r0 + n] - mid)]
        return jnp.concatenate(qs, axis=0), jnp.concatenate(ks, axis=0)
    right = (row & half) != 0
    qk = jnp.where(right, q, k) * jnp.exp(arg)
    return jnp.where(right, qk, 0.0), jnp.where(right, 0.0, qk)


def _split3(x):
    hi = x.astype(BF16)
    r1 = x - hi.astype(F32)
    mid = r1.astype(BF16)
    lo = (r1 - mid.astype(F32)).astype(BF16)
    return jnp.concatenate([hi, mid, lo], axis=0)


HEAD_PAIRS = HG_HEADS // 2
PAIR_W = 2 * HG_D


def _pair_blocks(a):
    zero = jnp.zeros((a.shape[0], HG_D), a.dtype)
    return jnp.concatenate([jnp.concatenate([a[:, :HG_D], zero], axis=1),
                            jnp.concatenate([zero, a[:, HG_D:]], axis=1)], axis=0)


def _per_head(fn, a):
    return jnp.concatenate([fn(a[:, :HG_D]), fn(a[:, HG_D:])], axis=1)


def _hgrn_prompt_kernel(tl, q_ref, lf_ref, v_ref, g_ref, nrm_ref, o_ref, s_ref, st_ref):
    c = HGRN_CHUNK
    li = pl.program_id(1)

    @pl.when(li == 0)
    def _():
        st_ref[...] = jnp.zeros_like(st_ref)

    row = lax.broadcasted_iota(jnp.int32, (c, PAIR_W), 0)
    ti = lax.broadcasted_iota(jnp.int32, (c, c), 0)
    si = lax.broadcasted_iota(jnp.int32, (c, c), 1)
    ones = lambda m: jnp.where(m, 1.0, 0.0).astype(BF16)
    cum3 = jnp.concatenate([ones(si <= ti)] * 3, axis=1)
    lvl2 = jnp.concatenate([jnp.concatenate([jnp.where(_level_sum_matrix(n, ti, si), 1.0, 0.0)
                                             for n in SMALL_LEVELS], axis=0).astype(BF16)] * 2, axis=1)
    apart = jnp.concatenate([ti ^ si] * 2, axis=1)
    pi = lax.broadcasted_iota(jnp.int32, (PAIR_W, PAIR_W), 0)
    pj = lax.broadcasted_iota(jnp.int32, (PAIR_W, PAIR_W), 1)
    own = (pi < HG_D) == (pj < HG_D)
    nt = (((1,), (1,)), ((), ()))
    tn = (((0,), (0,)), ((), ()))

    def chunk(pairs, ci, carry):
        rows = pl.ds(pl.multiple_of(ci * c, c), c)
        for p in pairs:
            cols = slice(p * PAIR_W, (p + 1) * PAIR_W)
            q, lf, v = q_ref[rows, cols], lf_ref[rows, cols], v_ref[rows, cols]
            k = 1.0 - jnp.exp(lf)
            lf3 = _split3(lf)
            b = jnp.dot(cum3, lf3, preferred_element_type=F32)
            s1 = jnp.dot(lvl2, lf3[:2 * c], preferred_element_type=F32)
            args = {n: s1[i * c:(i + 1) * c] for i, n in enumerate(SMALL_LEVELS)}
            vbd = _pair_blocks(v.astype(BF16))
            scores = None
            n = c
            while n >= 2:
                qt, kt = _level_factors(q, k, b, n, row, args.get(n))
                s_n = lax.dot_general(qt.astype(BF16), _pair_blocks(kt.astype(BF16)), nt,
                                      preferred_element_type=F32)
                if n in args:
                    scores = jnp.where(apart < n, s_n, scores)
                else:
                    half = n // 2
                    parts = []
                    for j, r0 in enumerate(range(0, c, n)):
                        new = s_n[j * half:(j + 1) * half]
                        if scores is None:
                            parts += [jnp.zeros_like(new), new]
                        else:
                            keep = scores[r0 + half:r0 + n]
                            parts += [scores[r0:r0 + half], jnp.where(apart[r0 + half:r0 + n] < n, new, keep)]
                    scores = jnp.concatenate(parts, axis=0)
                n //= 2
            st = st_ref[p]
            qs = (q * jnp.exp(b)).astype(BF16)
            o = jnp.dot(scores.astype(BF16), vbd, preferred_element_type=F32)
            o = o + lax.dot_general(qs, st.astype(BF16), nt, preferred_element_type=F32)
            o = o + _per_head(lambda a: jnp.broadcast_to(jnp.sum(a, axis=-1, keepdims=True), a.shape), q * k) * v
            bc = b[c - 1:c, :]
            kd = (k * jnp.exp(bc - b)).astype(BF16)
            upd = lax.dot_general(v.astype(BF16), kd, tn, preferred_element_type=F32)
            st_ref[p] = st * jnp.exp(bc) + jnp.where(own, upd, 0.0)
            ms = _per_head(lambda a: jnp.broadcast_to(jnp.mean(a, axis=-1, keepdims=True), a.shape), o * o)
            o = o * lax.rsqrt(ms + EPS) * nrm_ref[:, cols]
            o_ref[rows, cols] = (o * g_ref[rows, cols]).astype(o_ref.dtype)
        return carry

    lax.fori_loop(0, tl // c, functools.partial(chunk, range(HEAD_PAIRS)), 0, unroll=4)

    @pl.when(li == pl.num_programs(1) - 1)
    def _():
        for p in range(HEAD_PAIRS):
            st = st_ref[p]
            s_ref[0, 2 * p] = st[:HG_D, :HG_D].T
            s_ref[0, 2 * p + 1] = st[HG_D:, HG_D:].T


def _hgrn_prompt(act, hg_norm, batch, seq, tl=2048):
    nl = seq // tl
    blk = lambda part: pl.BlockSpec((tl, HG_WIDTH), lambda b, i: (b * nl + i, part))
    return pl.pallas_call(
        functools.partial(_hgrn_prompt_kernel, tl),
        grid=(batch, nl),
        in_specs=[blk(0), blk(1), blk(2), blk(3),
                  pl.BlockSpec((1, HG_WIDTH), lambda b, i: (0, 0))],
        out_specs=[pl.BlockSpec((tl, HG_WIDTH), lambda b, i: (b * nl + i, 0)),
                   pl.BlockSpec((1, HG_HEADS, HG_D, HG_D), lambda b, i: (b, 0, 0, 0))],
        out_shape=[jax.ShapeDtypeStruct((batch * seq, HG_WIDTH), BF16),
                   jax.ShapeDtypeStruct((batch, HG_HEADS, HG_D, HG_D), F32)],
        scratch_shapes=[pltpu.VMEM((HEAD_PAIRS, PAIR_W, PAIR_W), F32)],
        compiler_params=_params("parallel", "arbitrary"),
        name="hgrn_prompt",
    )(act, act, act, act, hg_norm)


HGRN_STEP_BT = 32


def _hgrn_step_kernel(q_ref, lf_ref, v_ref, g_ref, nrm_ref, s_ref, _, o_ref, so_ref):
    bt = HGRN_STEP_BT
    q = q_ref[...]
    v = v_ref[...]
    f = jnp.exp(lf_ref[...])
    ri = lax.broadcasted_iota(jnp.int32, (3 * bt, PAIR_W), 0) % bt
    half = lax.broadcasted_iota(jnp.int32, (3 * bt, PAIR_W), 1) // HG_D
    tn = (((0,), (0,)), ((), ()))
    pieces = [_split3(a) for a in (f, 1.0 - f, q)]
    rows = []
    for s in range(0, bt, 2):
        pick = jnp.where(ri == s + half, 1.0, 0.0).astype(BF16)
        fc, kc, qc = [lax.dot_general(p, pick, tn, preferred_element_type=F32) for p in pieces]
        for d in range(2):
            cols = slice(d * HG_D, (d + 1) * HG_D)
            sn = fc[:, cols] * s_ref[0, s + d, 0] + kc[:, cols] * v[s + d:s + d + 1, :]
            so_ref[0, s + d, 0] = sn
            rows.append(jnp.sum(sn * qc[:, cols], axis=0, keepdims=True))
    o = jnp.concatenate(rows, axis=0)
    o = o * lax.rsqrt(jnp.mean(o * o, axis=-1, keepdims=True) + EPS) * nrm_ref[...]
    o_ref[...] = (o * g_ref[...]).astype(o_ref.dtype)


def _hgrn_step(act, hg_norm, states, layer, new_states):
    batch = act.shape[0]
    bt = HGRN_STEP_BT
    blk = lambda part: pl.BlockSpec((bt, HG_D), lambda i, h: (i, part * HG_HEADS + h))
    sblk = pl.BlockSpec((1, bt, 1, HG_D, HG_D), lambda i, h: (layer, i, h, 0, 0))
    return pl.pallas_call(
        _hgrn_step_kernel,
        grid=(batch // bt, HG_HEADS),
        in_specs=[blk(0), blk(1), blk(2), blk(3),
                  pl.BlockSpec((1, HG_D), lambda i, h: (0, h)), sblk,
                  pl.BlockSpec(memory_space=pl.ANY)],
        out_specs=[pl.BlockSpec((bt, HG_D), lambda i, h: (i, h)), sblk],
        out_shape=[jax.ShapeDtypeStruct((batch, HG_WIDTH), BF16),
                   jax.ShapeDtypeStruct(new_states.shape, F32)],
        input_output_aliases={6: 1},
        compiler_params=_params("parallel", "parallel"),
        name="hgrn_step",
    )(act, act, act, act, hg_norm, states, new_states)


def _pool_map(d_groups, wp_ref, scale_ref, o_ref):
    for gi, d in enumerate(d_groups):
        sl = slice(gi * POOL_GROUP_W, (gi + 1) * POOL_GROUP_W)
        y = jnp.dot(d.astype(BF16), wp_ref[gi], preferred_element_type=F32)
        o_ref[:, sl] = (y * scale_ref[:, sl]).astype(o_ref.dtype)


POOL_HIST = 16


def _pool_tile(i, tl, z, carry_ref, wp_ref, scale_ref, o_ref, nb_ref):
    @pl.when(i == 0)
    def _():
        carry_ref[...] = jnp.zeros_like(carry_ref)

    ext = jnp.concatenate([carry_ref[...], z], axis=0)
    sums = {1: ext}
    w = 1
    while w < max(POOL_WINDOWS):
        sums[2 * w] = sums[w] + pltpu.roll(sums[w], w, 0)
        w *= 2
    pos = i * tl + lax.broadcasted_iota(jnp.int32, (tl, POOL_GROUP_W), 0)
    ds = []
    for gi, w in enumerate(POOL_WINDOWS):
        sl = slice(gi * POOL_GROUP_W, (gi + 1) * POOL_GROUP_W)
        cnt = jnp.minimum(pos + 1, w).astype(F32)
        ds.append(sums[w][POOL_HIST:, sl] / cnt - z[:, sl])
    _pool_map(ds, wp_ref, scale_ref, o_ref)
    carry_ref[...] = z[tl - POOL_HIST:, :]

    @pl.when(i == pl.num_programs(1) - 1)
    def _():
        nb_ref[0] = z[tl - POOL_BUF:, :]


POOL_STEP_BT = 16


def _pool_step_kernel(start_pos, u_ref, buf_ref, wp_ref, scale_ref, o_ref, nb_ref):
    u = u_ref[...]
    buf = buf_ref[0]
    ds = []
    for gi, w in enumerate(POOL_WINDOWS):
        sl = slice(gi * POOL_GROUP_W, (gi + 1) * POOL_GROUP_W)
        tot = u[:, sl] + jnp.sum(buf[:, POOL_BUF - (w - 1):, sl], axis=1)
        ds.append(tot / float(min(start_pos + 1, w)) - u[:, sl])
    _pool_map(ds, wp_ref, scale_ref, o_ref)
    nb_ref[:, 0:POOL_BUF - 1, :] = buf[:, 1:, :]
    nb_ref[:, POOL_BUF - 1:, :] = u[:, None, :]


def _pool_step(act, bufs, layer, pool_w, pool_scale, start_pos):
    batch = act.shape[0]
    bt = POOL_STEP_BT
    return pl.pallas_call(
        functools.partial(_pool_step_kernel, start_pos),
        grid=(batch // bt,),
        in_specs=[pl.BlockSpec((bt, POOL_WIDTH), lambda i: (i, ACT_POOL_PART)),
                  pl.BlockSpec((1, bt, POOL_BUF, POOL_WIDTH), lambda i: (layer, i, 0, 0)),
                  pl.BlockSpec((len(POOL_WINDOWS), POOL_GROUP_W, POOL_GROUP_W), lambda i: (0, 0, 0)),
                  pl.BlockSpec((1, POOL_WIDTH), lambda i: (0, 0))],
        out_specs=[pl.BlockSpec((bt, POOL_WIDTH), lambda i: (i, 0)),
                   pl.BlockSpec((bt, POOL_BUF, POOL_WIDTH), lambda i: (i, 0, 0))],
        out_shape=[jax.ShapeDtypeStruct((batch, POOL_WIDTH), BF16),
                   jax.ShapeDtypeStruct(bufs.shape[1:], F32)],
        compiler_params=_params("parallel"),
        name="pool_step",
    )(act, bufs, pool_w, pool_scale)


def _mixer_out_kernel(x_ref, o_ref, p_ref, w_ref, y_ref, wb_ref):
    @pl.when(pl.program_id(0) == 0)
    def _():
        wb_ref[...] = w_ref[0].astype(BF16)

    y = jnp.dot(o_ref[...], wb_ref[0:HG_WIDTH, :], preferred_element_type=F32)
    y = y + jnp.dot(p_ref[...], wb_ref[HG_WIDTH:, :], preferred_element_type=F32)
    y_ref[...] = x_ref[...] + y


def _mixer_out(x, og, yp, w_o, layer, tm):
    t = x.shape[0]
    return pl.pallas_call(
        _mixer_out_kernel,
        grid=(t // tm,),
        in_specs=[pl.BlockSpec((tm, D_MODEL), lambda i: (i, 0)),
                  pl.BlockSpec((tm, HG_WIDTH), lambda i: (i, 0)),
                  pl.BlockSpec((tm, POOL_WIDTH), lambda i: (i, 0)),
                  pl.BlockSpec((1, D_MODEL, D_MODEL), lambda i: (layer, 0, 0))],
        out_specs=pl.BlockSpec((tm, D_MODEL), lambda i: (i, 0)),
        out_shape=jax.ShapeDtypeStruct((t, D_MODEL), F32),
        scratch_shapes=[pltpu.VMEM((D_MODEL, D_MODEL), BF16)],
        compiler_params=_params("arbitrary"),
        name="mixer_out",
    )(x, og, yp, w_o)


def _top2_combine(logits):
    lane = lax.broadcasted_iota(jnp.int32, logits.shape, 1).astype(F32)
    neg = jnp.float32(-jnp.inf)
    lg = jnp.where(lane < N_EXPERTS, logits, neg)
    m1 = jnp.max(lg, axis=-1, keepdims=True)
    i1 = jnp.min(jnp.where(lg == m1, lane, float(LANES)), axis=-1, keepdims=True)
    lg2 = jnp.where(lane == i1, neg, lg)
    m2 = jnp.max(lg2, axis=-1, keepdims=True)
    i2 = jnp.min(jnp.where(lg2 == m2, lane, float(LANES)), axis=-1, keepdims=True)
    e2 = jnp.exp(m2 - m1)
    g1 = 1.0 / (1.0 + e2)
    g2 = e2 / (1.0 + e2)
    return jnp.where(lane == i1, g1, 0.0) + jnp.where(lane == i2, g2, 0.0)


def _ffn_kernel(moe, final, x_ref, g_ref, *rest):
    if moe:
        r_ref, rest = rest[0], rest[1:]
    if final:
        gf_ref, rest = rest[0], rest[1:]
    wg_ref, wu_ref, wd_ref, y_ref, h_ref, acc_ref = rest[:6]
    e, j = pl.program_id(1), pl.program_id(2)
    first = (e == 0) & (j == 0)
    last = (e == pl.num_programs(1) - 1) & (j == pl.num_programs(2) - 1)

    @pl.when(first)
    def _():
        h = _rmsnorm(x_ref[...], g_ref[...])
        h_ref[...] = h.astype(BF16)
        acc_ref[...] = jnp.zeros_like(acc_ref)
        if moe:
            logits = jnp.dot(h, r_ref[...], precision=lax.Precision.HIGHEST, preferred_element_type=F32)
            rest[6][...] = _top2_combine(logits)

    h = h_ref[...]
    gate = jnp.dot(h, wg_ref[0, 0], preferred_element_type=F32)
    up = jnp.dot(h, wu_ref[0, 0], preferred_element_type=F32)
    y = jnp.dot((_silu(gate) * up).astype(BF16), wd_ref[0, 0], preferred_element_type=F32)
    if moe:
        comb = rest[6][...]
        lane = lax.broadcasted_iota(jnp.int32, comb.shape, 1)
        y = jnp.sum(jnp.where(lane == e, comb, 0.0), axis=-1, keepdims=True) * y
    acc_ref[...] += y

    @pl.when(last)
    def _():
        out = x_ref[...] + acc_ref[...]
        if final:
            out = _rmsnorm(out, gf_ref[...])
        y_ref[...] = out


def _ffn(x, g, wg, wu, wd, layer, tm, tf, router=None, final_g=None):
    t = x.shape[0]
    _, ne, _, f = wg.shape
    moe, final = router is not None, final_g is not None
    vec = pl.BlockSpec((1, D_MODEL), lambda i, e, j: (0, 0))
    in_specs = [pl.BlockSpec((tm, D_MODEL), lambda i, e, j: (i, 0)), vec]
    args = [x, g]
    scratch = [pltpu.VMEM((tm, D_MODEL), BF16), pltpu.VMEM((tm, D_MODEL), F32)]
    if moe:
        in_specs.append(pl.BlockSpec((D_MODEL, LANES), lambda i, e, j: (0, 0)))
        args.append(router)
        scratch.append(pltpu.VMEM((tm, LANES), F32))
    if final:
        in_specs.append(vec)
        args.append(final_g)
    in_specs += [pl.BlockSpec((1, 1, D_MODEL, tf), lambda i, e, j: (layer, e, 0, j)),
                 pl.BlockSpec((1, 1, D_MODEL, tf), lambda i, e, j: (layer, e, 0, j)),
                 pl.BlockSpec((1, 1, tf, D_MODEL), lambda i, e, j: (layer, e, j, 0))]
    args += [wg, wu, wd]
    return pl.pallas_call(
        functools.partial(_ffn_kernel, moe, final),
        grid=(t // tm, ne, f // tf),
        in_specs=in_specs,
        out_specs=pl.BlockSpec((tm, D_MODEL), lambda i, e, j: (i, 0)),
        out_shape=jax.ShapeDtypeStruct((t, D_MODEL), F32),
        scratch_shapes=scratch,
        compiler_params=_params("parallel", "arbitrary", "arbitrary"),
        name="ffn_moe" if moe else "ffn_dense",
    )(*args)


def _mixer_tile(b, i, tl, x_ref, o_ref, u_ref, w_ref, wp_ref, scale_ref, nb_ref, wb_ref, carry_ref, yp_ref):
    @pl.when((b == 0) & (i == 0))
    def _():
        wb_ref[...] = w_ref[0].astype(BF16)

    _pool_tile(i, tl, u_ref[...], carry_ref, wp_ref, scale_ref, yp_ref, nb_ref)
    y = jnp.dot(o_ref[...], wb_ref[0:HG_WIDTH, :], preferred_element_type=F32)
    y = y + jnp.dot(yp_ref[...], wb_ref[HG_WIDTH:, :], preferred_element_type=F32)
    return x_ref[...] + y


def _mixer_ffn_kernel(tl, x_ref, o_ref, u_ref, w_ref, wp_ref, scale_ref, g_ref, wg_ref, wu_ref, wd_ref,
                      y_ref, nb_ref, wb_ref, carry_ref, yp_ref, x1_ref, h_ref, acc_ref):
    b, i, j = pl.program_id(0), pl.program_id(1), pl.program_id(2)

    @pl.when(j == 0)
    def _():
        x1 = _mixer_tile(b, i, tl, x_ref, o_ref, u_ref, w_ref, wp_ref, scale_ref, nb_ref, wb_ref, carry_ref, yp_ref)
        x1_ref[...] = x1
        h_ref[...] = _rmsnorm(x1, g_ref[...]).astype(BF16)
        acc_ref[...] = jnp.zeros_like(acc_ref)

    h = h_ref[...]
    gate = jnp.dot(h, wg_ref[0, 0], preferred_element_type=F32)
    up = jnp.dot(h, wu_ref[0, 0], preferred_element_type=F32)
    acc_ref[...] += jnp.dot((_silu(gate) * up).astype(BF16), wd_ref[0, 0], preferred_element_type=F32)

    @pl.when(j == pl.num_programs(2) - 1)
    def _():
        y_ref[...] = x1_ref[...] + acc_ref[...]


def _mixer_specs(nl, tl, layer, tok, fixed):
    ins = [pl.BlockSpec((tl, D_MODEL), tok(0)),
           pl.BlockSpec((tl, HG_WIDTH), tok(0)),
           pl.BlockSpec((tl, POOL_WIDTH), tok(ACT_POOL_PART)),
           pl.BlockSpec((1, D_MODEL, D_MODEL), fixed(layer, 0, 0)),
           pl.BlockSpec((len(POOL_WINDOWS), POOL_GROUP_W, POOL_GROUP_W), fixed(0, 0, 0)),
           pl.BlockSpec((1, POOL_WIDTH), fixed(0, 0))]
    scratch = [pltpu.VMEM((D_MODEL, D_MODEL), BF16), pltpu.VMEM((POOL_HIST, POOL_WIDTH), F32),
               pltpu.VMEM((tl, POOL_WIDTH), BF16)]
    return ins, scratch


def _mixer_ffn(x, og, act, w_o, layer, pool_w, pool_scale, g, wg, wu, wd, wset, batch, seq, tl=512, tf=1408):
    nl = seq // tl
    f = wg.shape[3]
    tok = lambda cols: (lambda b, i, j: (b * nl + i, cols))
    fixed = lambda *idx: (lambda b, i, j: idx)
    ins, scratch = _mixer_specs(nl, tl, layer, tok, fixed)
    ins += [pl.BlockSpec((1, D_MODEL), fixed(0, 0)),
            pl.BlockSpec((1, 1, D_MODEL, tf), lambda b, i, j: (wset, 0, 0, j)),
            pl.BlockSpec((1, 1, D_MODEL, tf), lambda b, i, j: (wset, 0, 0, j)),
            pl.BlockSpec((1, 1, tf, D_MODEL), lambda b, i, j: (wset, 0, j, 0))]
    scratch += [pltpu.VMEM((tl, D_MODEL), F32), pltpu.VMEM((tl, D_MODEL), BF16), pltpu.VMEM((tl, D_MODEL), F32)]
    return pl.pallas_call(
        functools.partial(_mixer_ffn_kernel, tl),
        grid=(batch, nl, f // tf),
        in_specs=ins,
        out_specs=[pl.BlockSpec((tl, D_MODEL), tok(0)),
                   pl.BlockSpec((1, POOL_BUF, POOL_WIDTH), lambda b, i, j: (b, 0, 0))],
        out_shape=[jax.ShapeDtypeStruct((batch * seq, D_MODEL), F32),
                   jax.ShapeDtypeStruct((batch, POOL_BUF, POOL_WIDTH), F32)],
        scratch_shapes=scratch,
        compiler_params=_params("arbitrary", "arbitrary", "arbitrary"),
        name="mixer_ffn",
    )(x, og, act, w_o, pool_w, pool_scale, g, wg, wu, wd)


MOE_TILE = 512
ROUTE_ROWS = 8
NT_DIMS = (((1,), (1,)), ((), ()))


def _mixer_route_kernel(tm, x_ref, o_ref, u_ref, w_ref, wp_ref, scale_ref, g_ref, rt_ref,
                        x1_ref, nb_ref, route_ref, gate_ref, cnt_ref,
                        wb_ref, pool_carry_ref, yp_ref, carry_ref, earlier_ref):
    b, i = pl.program_id(0), pl.program_id(1)

    @pl.when((b == 0) & (i == 0))
    def _():
        carry_ref[...] = jnp.zeros_like(carry_ref)
        t0 = lax.broadcasted_iota(jnp.int32, (tm, tm), 0)
        t1 = lax.broadcasted_iota(jnp.int32, (tm, tm), 1)
        earlier_ref[...] = jnp.where(t0 < t1, 1.0, 0.0).astype(BF16)

    x1 = _mixer_tile(b, i, tm, x_ref, o_ref, u_ref, w_ref, wp_ref, scale_ref, nb_ref, wb_ref, pool_carry_ref, yp_ref)
    x1_ref[...] = x1
    h = _rmsnorm(x1, g_ref[...])
    lt = lax.dot_general(rt_ref[...], h, NT_DIMS, precision=lax.Precision.HIGHEST,
                         preferred_element_type=F32)
    ex = lax.broadcasted_iota(jnp.int32, lt.shape, 0).astype(F32)
    neg = jnp.float32(-jnp.inf)
    m1 = jnp.max(lt, axis=0, keepdims=True)
    i1 = jnp.min(jnp.where(lt == m1, ex, float(N_EXPERTS)), axis=0, keepdims=True)
    l2 = jnp.where(ex == i1, neg, lt)
    m2 = jnp.max(l2, axis=0, keepdims=True)
    i2 = jnp.min(jnp.where(l2 == m2, ex, float(N_EXPERTS)), axis=0, keepdims=True)
    e2 = jnp.exp(m2 - m1)
    g1 = 1.0 / (1.0 + e2)
    g2 = e2 / (1.0 + e2)
    sel1, sel2 = ex == i1, ex == i2
    member = jnp.where(sel1 | sel2, 1.0, 0.0)
    rank = jnp.dot(member.astype(BF16), earlier_ref[...], preferred_element_type=F32) + carry_ref[:, 0:1]
    route_ref[...] = jnp.zeros_like(route_ref)
    route_ref[0:1, :] = i1.astype(jnp.int32)
    route_ref[1:2, :] = jnp.sum(jnp.where(sel1, rank, 0.0), axis=0, keepdims=True).astype(jnp.int32)
    route_ref[2:3, :] = i2.astype(jnp.int32)
    route_ref[3:4, :] = jnp.sum(jnp.where(sel2, rank, 0.0), axis=0, keepdims=True).astype(jnp.int32)
    carry_ref[...] += jnp.sum(member, axis=1, keepdims=True)
    cnt_ref[...] = carry_ref[...]
    row = lax.broadcasted_iota(jnp.int32, (LANES, tm), 0)
    gate_ref[...] = jnp.where(row == 0, g1, jnp.where(row == 1, g2, 0.0)).T


def _mixer_route(x, og, act, w_o, layer, pool_w, pool_scale, g, router_t, batch, seq, tm=512):
    nl = seq // tm
    t = batch * seq
    tok = lambda cols: (lambda b, i: (b * nl + i, cols))
    fixed = lambda *idx: (lambda b, i: idx)
    ins, scratch = _mixer_specs(nl, tm, layer, tok, fixed)
    ins += [pl.BlockSpec((1, D_MODEL), fixed(0, 0)),
            pl.BlockSpec((N_EXPERTS, D_MODEL), fixed(0, 0))]
    scratch += [pltpu.VMEM((N_EXPERTS, LANES), F32), pltpu.VMEM((tm, tm), BF16)]
    return pl.pallas_call(
        functools.partial(_mixer_route_kernel, tm),
        grid=(batch, nl),
        in_specs=ins,
        out_specs=[pl.BlockSpec((tm, D_MODEL), tok(0)),
                   pl.BlockSpec((1, POOL_BUF, POOL_WIDTH), lambda b, i: (b, 0, 0)),
                   pl.BlockSpec((ROUTE_ROWS, tm), lambda b, i: (0, b * nl + i)),
                   pl.BlockSpec((tm, LANES), tok(0)),
                   pl.BlockSpec((N_EXPERTS, LANES), fixed(0, 0))],
        out_shape=[jax.ShapeDtypeStruct((t, D_MODEL), F32),
                   jax.ShapeDtypeStruct((batch, POOL_BUF, POOL_WIDTH), F32),
                   jax.ShapeDtypeStruct((ROUTE_ROWS, t), jnp.int32),
                   jax.ShapeDtypeStruct((t, LANES), F32),
                   jax.ShapeDtypeStruct((N_EXPERTS, LANES), F32)],
        scratch_shapes=scratch,
        compiler_params=_params("arbitrary", "arbitrary"),
        name="mixer_route",
    )(x, og, act, w_o, pool_w, pool_scale, g, router_t)


def _moe_slots_kernel(base_ref, route_ref, slot_ref):
    r = route_ref[...]
    slot_ref[...] = jnp.zeros_like(slot_ref)
    for k in range(2):
        e, rank = r[2 * k:2 * k + 1, :], r[2 * k + 1:2 * k + 2, :]
        start = jnp.zeros_like(e)
        for j in range(N_EXPERTS):
            start = jnp.where(e == j, base_ref[j], start)
        slot_ref[k:k + 1, :] = start + rank


def _moe_slots(route, base, tm=2048):
    t = route.shape[1]
    return pl.pallas_call(
        _moe_slots_kernel,
        grid_spec=pltpu.PrefetchScalarGridSpec(
            num_scalar_prefetch=1,
            grid=(t // tm,),
            in_specs=[pl.BlockSpec((ROUTE_ROWS, tm), lambda i, *_: (0, i))],
            out_specs=pl.BlockSpec((ROUTE_ROWS, tm), lambda i, *_: (0, i))),
        out_shape=jax.ShapeDtypeStruct((ROUTE_ROWS, t), jnp.int32),
        compiler_params=_params("parallel"),
        name="moe_slots",
    )(base, route)


COMBINE_ROWS = 128


def _rows_done(hbm, rows, sem):
    pltpu.make_async_copy(hbm.at[pl.ds(0, rows)], hbm.at[pl.ds(0, rows)], sem).wait()


def _moe_dispatch_kernel(tm, fs_ref, fe_ref, s0_ref, s1_ref, x_ref, xs_hbm, sem):
    i = pl.program_id(0)

    def row_copy(group, sub, dst_row):
        return pltpu.make_async_copy(x_ref.at[group, pl.ds(sub, 1), :], xs_hbm.at[pl.ds(dst_row, 1)], sem)

    def issue(c, carry):
        for u in range(SUBLANES):
            for k, s_ref in enumerate((s0_ref, s1_ref)):
                row_copy(c, u, s_ref[c * SUBLANES + u]).start(priority=k)
        return carry

    lax.fori_loop(0, tm // SUBLANES, issue, 0)
    _rows_done(xs_hbm, 2 * tm, sem)

    @pl.when(i == pl.num_programs(0) - 1)
    def _():
        for e in range(N_EXPERTS + 1):
            def fill(p, c):
                row_copy(0, 0, p).start()
                return c

            def drain(p, c):
                row_copy(0, 0, p).wait()
                return c

            lax.fori_loop(fs_ref[e], fe_ref[e], fill, 0)
            lax.fori_loop(fs_ref[e], fe_ref[e], drain, 0)


def _moe_dispatch(rows, slots, fill_start, fill_end, n_slots, tm=2048):
    t, width = rows.shape
    slot_spec = pl.BlockSpec((tm,), lambda i, *_: (i,), memory_space=pltpu.SMEM)
    return pl.pallas_call(
        functools.partial(_moe_dispatch_kernel, tm),
        grid_spec=pltpu.PrefetchScalarGridSpec(
            num_scalar_prefetch=2,
            grid=(t // tm,),
            in_specs=[slot_spec, slot_spec,
                      pl.BlockSpec((tm // SUBLANES, SUBLANES, width), lambda i, *_: (i, 0, 0))],
            out_specs=pl.BlockSpec(memory_space=pl.ANY),
            scratch_shapes=[pltpu.SemaphoreType.DMA(())]),
        out_shape=jax.ShapeDtypeStruct((n_slots, width), rows.dtype),
        compiler_params=_params("arbitrary"),
        name="moe_dispatch",
    )(fill_start, fill_end, slots[0], slots[1], rows.reshape(t // SUBLANES, SUBLANES, width))


def _ffn_grouped_kernel(te_ref, nv_ref, x_ref, g_ref, wg_ref, wu_ref, wd_ref, y_ref):
    i = pl.program_id(0)

    @pl.when(i < nv_ref[0])
    def _():
        h = _rmsnorm(x_ref[...], g_ref[...]).astype(BF16)
        gate = jnp.dot(h, wg_ref[0, 0], preferred_element_type=F32)
        up = jnp.dot(h, wu_ref[0, 0], preferred_element_type=F32)
        y_ref[...] = jnp.dot((_silu(gate) * up).astype(BF16), wd_ref[0, 0], preferred_element_type=F32)

    @pl.when(i >= nv_ref[0])
    def _():
        y_ref[...] = jnp.zeros_like(y_ref)


def _ffn_grouped(xs, g, wg, wu, wd, layer, tile_expert, n_valid):
    n_slots = xs.shape[0]
    f = wg.shape[3]
    rows = pl.BlockSpec((MOE_TILE, D_MODEL), lambda i, te, nv: (i, 0))
    return pl.pallas_call(
        _ffn_grouped_kernel,
        grid_spec=pltpu.PrefetchScalarGridSpec(
            num_scalar_prefetch=2,
            grid=(n_slots // MOE_TILE,),
            in_specs=[rows,
                      pl.BlockSpec((1, D_MODEL), lambda i, te, nv: (0, 0)),
                      pl.BlockSpec((1, 1, D_MODEL, f), lambda i, te, nv: (layer, te[i], 0, 0)),
                      pl.BlockSpec((1, 1, D_MODEL, f), lambda i, te, nv: (layer, te[i], 0, 0)),
                      pl.BlockSpec((1, 1, f, D_MODEL), lambda i, te, nv: (layer, te[i], 0, 0))],
            out_specs=rows),
        out_shape=jax.ShapeDtypeStruct((n_slots, D_MODEL), F32),
        compiler_params=_params("arbitrary"),
        name="ffn_grouped",
    )(tile_expert, n_valid, xs, g, wg, wu, wd)


def _moe_combine_kernel(tm, final, s0_ref, s1_ref, s0n_ref, s1n_ref, x_ref, gate_ref, *rest):
    if final:
        gf_ref, rest = rest[0], rest[1:]
    ys_hbm, o_ref, y1_ref, y2_ref, sems = rest
    i = pl.program_id(0)
    cur = i % 2

    def request(sa_ref, sb_ref, buf_set):
        def issue(c, carry):
            for u in range(SUBLANES):
                for k, (s_ref, buf) in enumerate(((sa_ref, y1_ref), (sb_ref, y2_ref))):
                    pltpu.make_async_copy(ys_hbm.at[pl.ds(s_ref[c * SUBLANES + u], 1)],
                                          buf.at[buf_set, c, pl.ds(u, 1), :],
                                          sems.at[buf_set, k]).start(priority=k)
            return carry

        lax.fori_loop(0, tm // SUBLANES, issue, 0)

    @pl.when(i == 0)
    def _():
        request(s0_ref, s1_ref, 0)

    @pl.when(i + 1 < pl.num_programs(0))
    def _():
        request(s0n_ref, s1n_ref, 1 - cur)

    for k in range(2):
        _rows_done(ys_hbm, tm, sems.at[cur, k])
    rc = COMBINE_ROWS

    def rows(ci, c):
        r = pl.ds(pl.multiple_of(ci * rc, rc), rc)
        rg = pl.ds(pl.multiple_of(ci * (rc // SUBLANES), rc // SUBLANES), rc // SUBLANES)
        g1 = jnp.broadcast_to(gate_ref[r, 0:1], (rc, LANES))
        g2 = jnp.broadcast_to(gate_ref[r, 1:2], (rc, LANES))
        ssq = jnp.zeros((rc, LANES), F32)
        for j in range(D_MODEL // LANES):
            cols = slice(j * LANES, (j + 1) * LANES)
            y1 = y1_ref[cur, rg, :, cols].reshape(rc, LANES)
            y2 = y2_ref[cur, rg, :, cols].reshape(rc, LANES)
            out = x_ref[r, cols] + g1 * y1 + g2 * y2
            o_ref[r, cols] = out
            ssq = ssq + out * out
        if final:
            scale = lax.rsqrt(jnp.sum(ssq, axis=-1, keepdims=True) * (1.0 / D_MODEL) + EPS)
            for j in range(D_MODEL // LANES):
                cols = slice(j * LANES, (j + 1) * LANES)
                o_ref[r, cols] = o_ref[r, cols] * scale * gf_ref[:, cols]
        return c

    lax.fori_loop(0, tm // rc, rows, 0)


def _moe_combine(x, slots, gates, ys, final_g=None, tm=1024):
    t = x.shape[0]
    final = final_g is not None
    last = t // tm - 1
    slot_spec = pl.BlockSpec((tm,), lambda i: (i,), memory_space=pltpu.SMEM)
    next_spec = pl.BlockSpec((tm,), lambda i: (jnp.minimum(i + 1, last),), memory_space=pltpu.SMEM)
    in_specs = [slot_spec, slot_spec, next_spec, next_spec,
                pl.BlockSpec((tm, D_MODEL), lambda i: (i, 0)),
                pl.BlockSpec((tm, LANES), lambda i: (i, 0))]
    args = [slots[0], slots[1], slots[0], slots[1], x, gates]
    if final:
        in_specs.append(pl.BlockSpec((1, D_MODEL), lambda i: (0, 0)))
        args.append(final_g)
    in_specs.append(pl.BlockSpec(memory_space=pl.ANY))
    args.append(ys)
    return pl.pallas_call(
        functools.partial(_moe_combine_kernel, tm, final),
        grid=(t // tm,),
        in_specs=in_specs,
        out_specs=pl.BlockSpec((tm, D_MODEL), lambda i: (i, 0)),
        out_shape=jax.ShapeDtypeStruct((t, D_MODEL), F32),
        scratch_shapes=[pltpu.VMEM((2, tm // SUBLANES, SUBLANES, D_MODEL), F32),
                        pltpu.VMEM((2, tm // SUBLANES, SUBLANES, D_MODEL), F32),
                        pltpu.SemaphoreType.DMA((2, 2))],
        compiler_params=_params("arbitrary"),
        name="moe_combine",
    )(*args)


def _moe_routed(x, route, gates, counts, g, wg, wu, wd, layer, final_g=None):
    t = x.shape[0]
    n_tiles = 2 * t // MOE_TILE + N_EXPERTS
    cnt = counts[:, 0].astype(jnp.int32)
    caps = (cnt + MOE_TILE - 1) // MOE_TILE
    cum = jnp.cumsum(caps)
    base = (cum - caps) * MOE_TILE
    n_valid = cum[-1:]
    tile_expert = jnp.minimum(
        jnp.sum((cum[None, :] <= jnp.arange(n_tiles, dtype=jnp.int32)[:, None]).astype(jnp.int32), axis=1),
        N_EXPERTS - 1)
    fill_start = jnp.concatenate([base + cnt, n_valid * MOE_TILE])
    fill_end = jnp.concatenate([base + caps * MOE_TILE, jnp.full((1,), n_tiles * MOE_TILE, jnp.int32)])
    slots = _moe_slots(route, base)
    xs = _moe_dispatch(x, slots, fill_start, fill_end, n_tiles * MOE_TILE)
    ys = _ffn_grouped(xs, g, wg, wu, wd, layer, tile_expert, n_valid)
    return _moe_combine(x, slots, gates, ys, final_g)


def _trunk(x, state_hgrn, state_pool, start_pos, w, seq, tm):
    batch = x.shape[0] // seq
    s_out, b_out = [], []
    for l in range(DEPTH):
        act = _mixer_in(x, w["norm_mix"][l], w["w_in"], w["lb_logits"], l, tm)
        j = l // 2
        dense = l % 2 == 0
        final_g = w["norm_final"] if l == DEPTH - 1 else None
        g_ffn = w["norm_ffn"][l]
        if seq > 1:
            og, s_new = _hgrn_prompt(act, w["hg_norm"][l], batch, seq)
            mixer = (x, og, act, w["w_o"], l, w["pool_w"][l], w["pool_scale"][l])
            if dense:
                assert final_g is None
                x, b_new = _mixer_ffn(*mixer, g_ffn, w["ffn_w_gate"], w["ffn_w_up"], w["ffn_w_down"], j,
                                      batch, seq)
            else:
                x, b_new, route, gates, counts = _mixer_route(*mixer, g_ffn, w["router_t"][j], batch, seq)
                x = _moe_routed(x, route, gates, counts, g_ffn, w["moe_w_gate"], w["moe_w_up"],
                                w["moe_w_down"], j, final_g=final_g)
        else:
            new_states = s_out[-1] if s_out else jnp.zeros_like(state_hgrn)
            og, s_new = _hgrn_step(act, w["hg_norm"][l], state_hgrn, l, new_states)
            yp, b_new = _pool_step(act, state_pool, l, w["pool_w"][l], w["pool_scale"][l], start_pos)
            x = _mixer_out(x, og, yp, w["w_o"], l, tm)
            if dense:
                x = _ffn(x, g_ffn, w["ffn_w_gate"], w["ffn_w_up"], w["ffn_w_down"], j, tm, 1408, final_g=final_g)
            else:
                x = _ffn(x, g_ffn, w["moe_w_gate"], w["moe_w_up"], w["moe_w_down"], j, tm, 1408,
                         router=w["router"][j], final_g=final_g)
        s_out.append(s_new)
        b_out.append(b_new)
    return x, (jnp.stack(s_out) if seq > 1 else s_out[-1]), jnp.stack(b_out)


def kernel(x_prompt, x_sample, state_hgrn, state_pool, lb_logits, norm_mix, w_in, w_o, hg_norm, pool_w,
           pool_scale, norm_ffn, ffn_w_gate, ffn_w_up, ffn_w_down, router, moe_w_gate, moe_w_up, moe_w_down,
           norm_final):
    batch, seq, _ = x_prompt.shape
    dec_batch, dec_seq, _ = x_sample.shape
    assert dec_seq == 1
    past_len = 16384
    w = dict(
        lb_logits=lb_logits,
        norm_mix=norm_mix.reshape(DEPTH, 1, D_MODEL),
        w_in=w_in,
        w_o=w_o,
        hg_norm=hg_norm.reshape(DEPTH, 1, HG_WIDTH),
        pool_w=pool_w.astype(BF16),
        pool_scale=pool_scale.reshape(DEPTH, 1, POOL_WIDTH),
        norm_ffn=norm_ffn.reshape(DEPTH, 1, D_MODEL),
        ffn_w_gate=ffn_w_gate.astype(BF16)[:, None],
        ffn_w_up=ffn_w_up.astype(BF16)[:, None],
        ffn_w_down=ffn_w_down.astype(BF16)[:, None],
        router=jnp.pad(router, ((0, 0), (0, 0), (0, LANES - N_EXPERTS))),
        router_t=jnp.swapaxes(router, 1, 2),
        moe_w_gate=moe_w_gate.astype(BF16),
        moe_w_up=moe_w_up.astype(BF16),
        moe_w_down=moe_w_down.astype(BF16),
        norm_final=norm_final.reshape(1, D_MODEL),
    )
    yp, sp, bp = _trunk(x_prompt.reshape(batch * seq, D_MODEL), None, None, 0, w, seq, 512)
    ys, ss, bs = _trunk(x_sample.reshape(dec_batch, D_MODEL), state_hgrn, state_pool, past_len, w, 1, 128)
    return (yp.reshape(batch, seq, D_MODEL), ys.reshape(dec_batch, 1, D_MODEL), sp, ss, bp, bs)
```
